```python
import math
import jax, jax.numpy as jnp
from jax import lax
import numpy as np

D_MODEL = 2048
BATCH = 4
SEQ = 2048
DEPTH = 1
DEC_BATCH = 128
DEC_SEQ = 4
PAST_LEN = 16384
PAGE_SIZE = 128

D_MIX = D_MODEL
D_CONV = D_MIX // 2
D_SSM = D_MIX - D_CONV
CONV_WIDTH = 31
SSM_HEAD_DIM = 64
SSM_HEADS = D_SSM // SSM_HEAD_DIM
SSM_GROUPS = 2
SSM_STATE = 128
SSM_CONV_WIDTH = 4
SSM_CHUNK = 128
D_XBC = D_SSM + 2 * SSM_GROUPS * SSM_STATE
D_IN = 2 * D_CONV + D_SSM + D_XBC + SSM_HEADS
D_FF = ((8 * D_MODEL // 3 + 127) // 128) * 128
N_MEM = 256
X_HEADS = 4
X_HEAD_DIM = D_MODEL // X_HEADS
EPS = 1e-6

kernel_name = 'hymba_conformer_ssd_macaron_decoder_step'


def rms_norm(x, g):
    xf = x.astype(jnp.float32)
    y = xf * lax.rsqrt(jnp.mean(xf * xf, axis=-1, keepdims=True) + EPS)
    return (y * g.astype(jnp.float32)).astype(x.dtype)


def layer_norm(x, g, b):
    xf = x.astype(jnp.float32)
    xc = xf - jnp.mean(xf, axis=-1, keepdims=True)
    y = xc * lax.rsqrt(jnp.mean(xc * xc, axis=-1, keepdims=True) + EPS)
    return (y * g.astype(jnp.float32) + b.astype(jnp.float32)).astype(x.dtype)


def swiglu(x, w_gate, w_up, w_down):
    return (jax.nn.silu(x @ w_gate) * (x @ w_up)) @ w_down


def causal_dwconv(buf, x, w, b):
    xe = jnp.concatenate([buf, x], axis=1)
    y = lax.conv_general_dilated(xe, w[:, None, :], window_strides=(1,), padding='VALID',
                                 dimension_numbers=('NWC', 'WIO', 'NWC'),
                                 feature_group_count=x.shape[-1])
    return y + b, xe[:, -(w.shape[0] - 1):]


def ssd_scan(x, dt, A, B, C, h0, chunk):
    b, L, H, P = x.shape
    G, N = B.shape[2], B.shape[3]
    Hg = H // G
    c = L // chunk
    x = x.reshape(b, c, chunk, G, Hg, P)
    dt = dt.reshape(b, c, chunk, G, Hg)
    B = B.reshape(b, c, chunk, G, N)
    C = C.reshape(b, c, chunk, G, N)
    a_cum = jnp.cumsum(dt * A.reshape(G, Hg), axis=2)
    seg = a_cum[:, :, :, None] - a_cum[:, :, None]
    causal = jnp.tril(jnp.ones((chunk, chunk), dtype=bool))[:, :, None, None]
    decay = jnp.exp(jnp.where(causal, seg, -jnp.inf))
    cb = jnp.einsum('bclgn,bcsgn->bclsg', C, B)
    w_intra = cb[..., None] * decay * dt[:, :, None]
    y_diag = jnp.einsum('bclsgh,bcsghp->bclghp', w_intra, x)
    decay_to_end = jnp.exp(a_cum[:, :, -1:] - a_cum)
    states = jnp.einsum('bclgn,bclgh,bclghp->bcghpn', B, decay_to_end * dt, x)
    chunk_decay = jnp.exp(a_cum[:, :, -1])

    def step(h, inp):
        s, d = inp
        return h * d[..., None, None] + s, h

    h_final, h_start = lax.scan(step, h0.reshape(b, G, Hg, P, N),
                                (jnp.moveaxis(states, 1, 0), jnp.moveaxis(chunk_decay, 1, 0)))
    h_start = jnp.moveaxis(h_start, 0, 1)
    y_off = jnp.einsum('bclgn,bcghpn,bclgh->bclghp', C, h_start, jnp.exp(a_cum))
    y = (y_diag + y_off).reshape(b, L, H, P)
    return y, h_final.reshape(b, H, P, N)


def mixer(h, conv_buf, ssm_conv_buf, ssm_state, chunk, p):
    bsz, T, _ = h.shape
    f32 = jnp.float32
    proj = h @ p['w_in']
    s1 = D_CONV
    s2 = 2 * D_CONV
    s3 = s2 + D_SSM
    s4 = s3 + D_XBC
    conv_val, conv_gate, z, xbc, dt_raw = jnp.split(proj, [s1, s2, s3, s4], axis=-1)
    u = conv_val * jax.nn.sigmoid(conv_gate)
    u, new_conv_buf = causal_dwconv(conv_buf, u, p['conv_w'], p['conv_b'])
    a_out = jax.nn.silu(layer_norm(u, p['conv_ln_g'], p['conv_ln_b']))
    xbc, new_ssm_conv_buf = causal_dwconv(ssm_conv_buf, xbc, p['ssm_conv_w'], p['ssm_conv_b'])
    xbc = jax.nn.silu(xbc)
    xs, Bm, Cm = jnp.split(xbc, [D_SSM, D_SSM + SSM_GROUPS * SSM_STATE], axis=-1)
    xs = xs.reshape(bsz, T, SSM_HEADS, SSM_HEAD_DIM).astype(f32)
    Bm = Bm.reshape(bsz, T, SSM_GROUPS, SSM_STATE).astype(f32)
    Cm = Cm.reshape(bsz, T, SSM_GROUPS, SSM_STATE).astype(f32)
    dt = jax.nn.softplus(dt_raw.astype(f32) + p['dt_bias'].astype(f32))
    A = -jnp.exp(p['a_log'].astype(f32))
    y, new_state = ssd_scan(xs, dt, A, Bm, Cm, ssm_state.astype(f32), chunk)
    y = y + p['d_skip'].astype(f32)[:, None] * xs
    y = y.reshape(bsz, T, D_SSM) * jax.nn.silu(z.astype(f32))
    y = rms_norm(y.reshape(bsz, T, SSM_GROUPS, D_SSM // SSM_GROUPS),
                 p['ssm_norm_g'].reshape(SSM_GROUPS, D_SSM // SSM_GROUPS))
    y = y.reshape(bsz, T, D_SSM).astype(h.dtype)
    out = jnp.concatenate([a_out, y], axis=-1) @ p['w_out']
    return out, new_conv_buf, new_ssm_conv_buf, new_state.astype(ssm_state.dtype)


def memory_kv(mem, p):
    bsz = mem.shape[0]
    m = rms_norm(mem, p['mem_norm_g'])
    k = (m @ p['w_xk']).reshape(bsz, N_MEM, X_HEADS, X_HEAD_DIM)
    v = (m @ p['w_xv']).reshape(bsz, N_MEM, X_HEADS, X_HEAD_DIM)
    return k, v


def cross_attend(h, k, v, p):
    bsz, T, _ = h.shape
    q = (h @ p['w_xq']).reshape(bsz, T, X_HEADS, X_HEAD_DIM)
    s = jnp.einsum('bthd,bmhd->bhtm', q, k, preferred_element_type=jnp.float32) * (X_HEAD_DIM ** -0.5)
    w = jax.nn.softmax(s, axis=-1).astype(v.dtype)
    o = jnp.einsum('bhtm,bmhd->bthd', w, v).reshape(bsz, T, D_MODEL)
    return o @ p['w_xo']


def decoder_layer(x, mem_k, mem_v, conv_buf, ssm_conv_buf, ssm_state, chunk, p):
    f1 = swiglu(rms_norm(x, p['ffn1_pre_g']), p['ffn1_w_gate'], p['ffn1_w_up'], p['ffn1_w_down'])
    x = x + 0.5 * rms_norm(f1, p['ffn1_post_g'])
    m, cb, scb, st = mixer(rms_norm(x, p['mix_pre_g']), conv_buf, ssm_conv_buf, ssm_state, chunk, p)
    x = x + rms_norm(m, p['mix_post_g'])
    a = cross_attend(rms_norm(x, p['xattn_pre_g']), mem_k, mem_v, p)
    x = x + rms_norm(a, p['xattn_post_g'])
    f2 = swiglu(rms_norm(x, p['ffn2_pre_g']), p['ffn2_w_gate'], p['ffn2_w_up'], p['ffn2_w_down'])
    x = x + 0.5 * rms_norm(f2, p['ffn2_post_g'])
    return x, cb, scb, st


def setup_inputs(seed: int = 0) -> dict:
    key = jax.random.key(seed)
    ks = iter(jax.random.split(key, 64))
    f32 = jnp.float32

    def nrm(shape, scale):
        return jax.random.normal(next(ks), shape, f32) * scale

    def gain(n):
        return 1.0 + nrm((DEPTH, n), 0.02)

    def lin(fi, fo):
        return nrm((DEPTH, fi, fo), fi ** -0.5)

    inp = {}
    inp['x_prompt'] = nrm((BATCH, SEQ, D_MODEL), 1.0)
    inp['x_sample'] = nrm((DEC_BATCH, DEC_SEQ, D_MODEL), 1.0)
    inp['mem_prompt'] = nrm((BATCH, N_MEM, D_MODEL), 1.0)
    inp['cache_mem_k'] = nrm((DEPTH, DEC_BATCH, N_MEM, X_HEADS, X_HEAD_DIM), 1.0)
    inp['cache_mem_v'] = nrm((DEPTH, DEC_BATCH, N_MEM, X_HEADS, X_HEAD_DIM), 1.0)
    inp['state_conv'] = nrm((DEPTH, DEC_BATCH, CONV_WIDTH - 1, D_CONV), 0.5)
    inp['state_ssm_conv'] = nrm((DEPTH, DEC_BATCH, SSM_CONV_WIDTH - 1, D_XBC), 1.0)
    inp['state_ssm'] = nrm((DEPTH, DEC_BATCH, SSM_HEADS, SSM_HEAD_DIM, SSM_STATE), 0.5)
    inp['ffn1_pre_g'] = gain(D_MODEL)
    inp['ffn1_w_gate'] = lin(D_MODEL, D_FF)
    inp['ffn1_w_up'] = lin(D_MODEL, D_FF)
    inp['ffn1_w_down'] = lin(D_FF, D_MODEL)
    inp['ffn1_post_g'] = gain(D_MODEL)
    inp['mix_pre_g'] = gain(D_MODEL)
    inp['w_in'] = lin(D_MODEL, D_IN)
    inp['conv_w'] = nrm((DEPTH, CONV_WIDTH, D_CONV), CONV_WIDTH ** -0.5)
    inp['conv_b'] = nrm((DEPTH, D_CONV), 0.02)
    inp['conv_ln_g'] = gain(D_CONV)
    inp['conv_ln_b'] = nrm((DEPTH, D_CONV), 0.02)
    inp['ssm_conv_w'] = nrm((DEPTH, SSM_CONV_WIDTH, D_XBC), SSM_CONV_WIDTH ** -0.5)
    inp['ssm_conv_b'] = nrm((DEPTH, D_XBC), 0.02)
    dt0 = jnp.exp(jax.random.uniform(next(ks), (DEPTH, SSM_HEADS), f32, math.log(1e-3), math.log(1e-1)))
    inp['dt_bias'] = dt0 + jnp.log(-jnp.expm1(-dt0))
    inp['a_log'] = jnp.log(jax.random.uniform(next(ks), (DEPTH, SSM_HEADS), f32, 1.0, 16.0))
    inp['d_skip'] = gain(SSM_HEADS)
    inp['ssm_norm_g'] = gain(D_SSM)
    inp['w_out'] = lin(D_MIX, D_MODEL)
    inp['mix_post_g'] = gain(D_MODEL)
    inp['xattn_pre_g'] = gain(D_MODEL)
    inp['mem_norm_g'] = gain(D_MODEL)
    inp['w_xq'] = lin(D_MODEL, D_MODEL)
    inp['w_xk'] = lin(D_MODEL, D_MODEL)
    inp['w_xv'] = lin(D_MODEL, D_MODEL)
    inp['w_xo'] = lin(D_MODEL, D_MODEL)
    inp['xattn_post_g'] = gain(D_MODEL)
    inp['ffn2_pre_g'] = gain(D_MODEL)
    inp['ffn2_w_gate'] = lin(D_MODEL, D_FF)
    inp['ffn2_w_up'] = lin(D_MODEL, D_FF)
    inp['ffn2_w_down'] = lin(D_FF, D_MODEL)
    inp['ffn2_post_g'] = gain(D_MODEL)
    return inp


def reference(x_prompt, x_sample, mem_prompt, cache_mem_k, cache_mem_v, state_conv, state_ssm_conv, state_ssm,
              ffn1_pre_g, ffn1_w_gate, ffn1_w_up, ffn1_w_down, ffn1_post_g,
              mix_pre_g, w_in, conv_w, conv_b, conv_ln_g, conv_ln_b, ssm_conv_w, ssm_conv_b,
              dt_bias, a_log, d_skip, ssm_norm_g, w_out, mix_post_g,
              xattn_pre_g, mem_norm_g, w_xq, w_xk, w_xv, w_xo, xattn_post_g,
              ffn2_pre_g, ffn2_w_gate, ffn2_w_up, ffn2_w_down, ffn2_post_g):
    params = dict(ffn1_pre_g=ffn1_pre_g, ffn1_w_gate=ffn1_w_gate, ffn1_w_up=ffn1_w_up, ffn1_w_down=ffn1_w_down,
                  ffn1_post_g=ffn1_post_g, mix_pre_g=mix_pre_g, w_in=w_in, conv_w=conv_w, conv_b=conv_b,
                  conv_ln_g=conv_ln_g, conv_ln_b=conv_ln_b, ssm_conv_w=ssm_conv_w, ssm_conv_b=ssm_conv_b,
                  dt_bias=dt_bias, a_log=a_log, d_skip=d_skip, ssm_norm_g=ssm_norm_g, w_out=w_out,
                  mix_post_g=mix_post_g, xattn_pre_g=xattn_pre_g, mem_norm_g=mem_norm_g, w_xq=w_xq,
                  w_xk=w_xk, w_xv=w_xv, w_xo=w_xo, xattn_post_g=xattn_post_g, ffn2_pre_g=ffn2_pre_g,
                  ffn2_w_gate=ffn2_w_gate, ffn2_w_up=ffn2_w_up, ffn2_w_down=ffn2_w_down, ffn2_post_g=ffn2_post_g)
    bp = x_prompt.shape[0]
    dt_ = x_prompt.dtype
    yp, ys = x_prompt, x_sample
    pk, pv, pc, psc, pst, sc, ssc, sst = [], [], [], [], [], [], [], []
    for layer in range(DEPTH):
        pl = {name: w[layer] for name, w in params.items()}
        mk, mv = memory_kv(mem_prompt, pl)
        yp, c1, c2, c3 = decoder_layer(
            yp, mk, mv,
            jnp.zeros((bp, CONV_WIDTH - 1, D_CONV), dt_),
            jnp.zeros((bp, SSM_CONV_WIDTH - 1, D_XBC), dt_),
            jnp.zeros((bp, SSM_HEADS, SSM_HEAD_DIM, SSM_STATE), dt_),
            SSM_CHUNK, pl)
        ys, d1, d2, d3 = decoder_layer(
            ys, cache_mem_k[layer], cache_mem_v[layer],
            state_conv[layer], state_ssm_conv[layer], state_ssm[layer],
            x_sample.shape[1], pl)
        pk.append(mk)
        pv.append(mv)
        pc.append(c1)
        psc.append(c2)
        pst.append(c3)
        sc.append(d1)
        ssc.append(d2)
        sst.append(d3)
    return (yp, ys, jnp.stack(pk), jnp.stack(pv), jnp.stack(pc), jnp.stack(psc), jnp.stack(pst),
            jnp.stack(sc), jnp.stack(ssc), jnp.stack(sst))
```

```python
import functools

import jax
import jax.numpy as jnp
from jax import lax
from jax.experimental import pallas as pl
from jax.experimental.pallas import tpu as pltpu

F32 = jnp.float32
BF16 = jnp.bfloat16

D = 2048
FF = 5504
DC = 1024
DS = 1024
KC = 31
NH = 16
HP = 64
NG = 2
NS = 128
KS = 4
DX = DS + 2 * NG * NS
CHUNK = 128
NM = 256
XH = 4
XD = D // XH
EPS = 1e-6

LANE = 128
SUB = 8
HIST = 32
HALO = 8

FF_TILE = 512
FFP = ((FF + FF_TILE - 1) // FF_TILE) * FF_TILE

VMEM_LIMIT = 56 * 1024 * 1024


def _cparams(sem):
    return pltpu.CompilerParams(dimension_semantics=sem, vmem_limit_bytes=VMEM_LIMIT)


def _rms(x, g):
    return x * lax.rsqrt(jnp.mean(x * x, axis=-1, keepdims=True) + EPS) * g


def _silu(x):
    return x * jax.nn.sigmoid(x)


def _resident(shape):
    return pl.BlockSpec(shape, lambda *_: (0,) * len(shape), pipeline_mode=pl.Buffered(1))


def _ffn_kernel(x_ref, pg_ref, wg_ref, wu_ref, wd_ref, qg_ref, o_ref, xn_ref, acc_ref):
    f = pl.program_id(1)

    @pl.when(f == 0)
    def _():
        xn_ref[...] = _rms(x_ref[...], pg_ref[...]).astype(BF16)
        acc_ref[...] = jnp.zeros_like(acc_ref)

    xn = xn_ref[...]
    h = jnp.dot(xn, wg_ref[...], preferred_element_type=F32)
    u = jnp.dot(xn, wu_ref[...], preferred_element_type=F32)
    a = (_silu(h) * u).astype(BF16)
    acc_ref[...] += jnp.dot(a, wd_ref[...], preferred_element_type=F32)

    @pl.when(f == pl.num_programs(1) - 1)
    def _():
        o_ref[...] = x_ref[...] + 0.5 * _rms(acc_ref[...], qg_ref[...])


def _ffn(x, pre_g, wg, wu, wd, post_g, *, tm=512):
    nt = x.shape[0]
    return pl.pallas_call(
        _ffn_kernel,
        grid=(nt // tm, FFP // FF_TILE),
        in_specs=[
            pl.BlockSpec((tm, D), lambda i, f: (i, 0)),
            pl.BlockSpec((1, D), lambda i, f: (0, 0)),
            pl.BlockSpec((D, FF_TILE), lambda i, f: (0, f)),
            pl.BlockSpec((D, FF_TILE), lambda i, f: (0, f)),
            pl.BlockSpec((FF_TILE, D), lambda i, f: (f, 0)),
            pl.BlockSpec((1, D), lambda i, f: (0, 0)),
        ],
        out_specs=pl.BlockSpec((tm, D), lambda i, f: (i, 0)),
        out_shape=jax.ShapeDtypeStruct((nt, D), F32),
        scratch_shapes=[pltpu.VMEM((tm, D), BF16), pltpu.VMEM((tm, D), F32)],
        compiler_params=_cparams(("parallel", "arbitrary")),
        name="ffn",
    )(x, pre_g, wg, wu, wd, post_g)


def _mix_in_kernel(x_ref, g_ref, wv_ref, wg_ref, wz_ref, wx_ref, wt_ref, u_ref, z_ref, xbc_ref, dt_ref, *, tn):
    hn = _rms(x_ref[...], g_ref[...]).astype(BF16)
    for c in range(DC // tn):
        sl = slice(c * tn, (c + 1) * tn)
        v = jnp.dot(hn, wv_ref[:, sl], preferred_element_type=F32)
        gt = jnp.dot(hn, wg_ref[:, sl], preferred_element_type=F32)
        u_ref[:, sl] = v * jax.nn.sigmoid(gt)
    for c in range(DS // tn):
        sl = slice(c * tn, (c + 1) * tn)
        z_ref[:, sl] = jnp.dot(hn, wz_ref[:, sl], preferred_element_type=F32)
    for c in range(DX // tn):
        sl = slice(c * tn, (c + 1) * tn)
        xbc_ref[:, sl] = jnp.dot(hn, wx_ref[:, sl], preferred_element_type=F32)
    dt_ref[...] = jnp.dot(hn, wt_ref[...], preferred_element_type=F32)


def _mix_in(x, g, wv, wg, wz, wx, wt, *, tm=256, tn=512):
    nt = x.shape[0]
    row = lambda n: pl.BlockSpec((tm, n), lambda i: (i, 0))
    return pl.pallas_call(
        functools.partial(_mix_in_kernel, tn=tn),
        grid=(nt // tm,),
        in_specs=[row(D), _resident((1, D)), _resident((D, DC)), _resident((D, DC)), _resident((D, DS)),
                  _resident((D, DX)), _resident((D, LANE))],
        out_specs=[row(DC), row(DS), row(DX), row(LANE)],
        out_shape=[jax.ShapeDtypeStruct((nt, n), F32) for n in (DC, DS, DX, LANE)],
        compiler_params=_cparams(("parallel",)),
        name="mix_in",
    )(x, g, wv, wg, wz, wx, wt)


def _conv_body(xe_ref, y_ref, w_ref, b_ref, lg_ref, lb_ref, tt):
    off = HIST - (KC - 1)
    for j in range(DC // LANE):
        sl = slice(j * LANE, (j + 1) * LANE)
        acc = jnp.broadcast_to(b_ref[:, sl], (tt, LANE))
        for k in range(KC):
            acc = acc + w_ref[k:k + 1, sl] * xe_ref[off + k:off + k + tt, sl]
        y_ref[:, sl] = acc
    y = y_ref[...]
    yc = y - jnp.mean(y, axis=-1, keepdims=True)
    a = yc * lax.rsqrt(jnp.mean(yc * yc, axis=-1, keepdims=True) + EPS) * lg_ref[...] + lb_ref[...]
    return _silu(a)


def _conv_prompt_kernel(u_ref, w_ref, b_ref, lg_ref, lb_ref, a_ref, nb_ref, xe_ref, y_ref, *, tt):
    t = pl.program_id(1)

    @pl.when(t == 0)
    def _():
        xe_ref[0:HIST, :] = jnp.zeros((HIST, DC), F32)

    xe_ref[HIST:HIST + tt, :] = u_ref[...]
    a_ref[...] = _conv_body(xe_ref, y_ref, w_ref, b_ref, lg_ref, lb_ref, tt)
    nb_ref[...] = xe_ref[HIST + tt - (KC - 1):HIST + tt, :]
    xe_ref[0:HIST, :] = xe_ref[tt:tt + HIST, :]


def _conv_prompt(u, w, b, lg, lb, *, nb, seq, tt=256):
    nt = u.shape[0]
    nper = seq // tt
    par = lambda r: _resident((r, DC))
    return pl.pallas_call(
        functools.partial(_conv_prompt_kernel, tt=tt),
        grid=(nb, nper),
        in_specs=[pl.BlockSpec((tt, DC), lambda s, t: (s * nper + t, 0)), par(KC), par(1), par(1), par(1)],
        out_specs=[pl.BlockSpec((tt, DC), lambda s, t: (s * nper + t, 0)),
                   pl.BlockSpec((None, KC - 1, DC), lambda s, t: (s, 0, 0))],
        out_shape=[jax.ShapeDtypeStruct((nt, DC), F32), jax.ShapeDtypeStruct((nb, KC - 1, DC), F32)],
        scratch_shapes=[pltpu.VMEM((HIST + tt, DC), F32), pltpu.VMEM((tt, DC), F32)],
        compiler_params=_cparams(("parallel", "arbitrary")),
        name="conv_prompt",
    )(u, w, b, lg, lb)


def _conv_sample_kernel(u_ref, hist_ref, w_ref, b_ref, lg_ref, lb_ref, prev_ref, a_ref, nb_ref, xe_ref, y_ref,
                        *, nseq, tv):
    del prev_ref
    for j in range(nseq):
        xe_ref[HIST - (KC - 1):HIST, :] = hist_ref[j]
        xe_ref[HIST:HIST + tv, :] = u_ref[j * tv:(j + 1) * tv, :]
        xe_ref[HIST + tv:HIST + SUB, :] = jnp.zeros((SUB - tv, DC), F32)
        a = _conv_body(xe_ref, y_ref, w_ref, b_ref, lg_ref, lb_ref, SUB)
        a_ref[j * tv:(j + 1) * tv, :] = a[0:tv, :]
        nb_ref[j] = xe_ref[HIST + tv - (KC - 1):HIST + tv, :]


def _conv_sample(u, hist, w, b, lg, lb, a_prev, *, row0, tv, nseq=2):
    nb = hist.shape[0]
    rows = nseq * tv
    par = lambda r: _resident((r, DC))
    blk0 = row0 // rows
    return pl.pallas_call(
        functools.partial(_conv_sample_kernel, nseq=nseq, tv=tv),
        grid=(nb // nseq,),
        in_specs=[pl.BlockSpec((rows, DC), lambda i: (blk0 + i, 0)),
                  pl.BlockSpec((nseq, KC - 1, DC), lambda i: (i, 0, 0)),
                  par(KC), par(1), par(1), par(1),
                  pl.BlockSpec(memory_space=pl.ANY)],
        out_specs=[pl.BlockSpec((rows, DC), lambda i: (blk0 + i, 0)),
                   pl.BlockSpec((nseq, KC - 1, DC), lambda i: (i, 0, 0))],
        out_shape=[jax.ShapeDtypeStruct(a_prev.shape, F32), jax.ShapeDtypeStruct((nb, KC - 1, DC), F32)],
        input_output_aliases={6: 0},
        scratch_shapes=[pltpu.VMEM((HIST + SUB, DC), F32), pltpu.VMEM((SUB, DC), F32)],
        compiler_params=_cparams(("arbitrary",)),
        name="conv_sample",
    )(u, hist, w, b, lg, lb, a_prev)


def _ssd_chunk(xe_ref, z, dt_raw, st_ref, cw_ref, cb_ref, dtb_ref, alog_ref, dsk_ref, ng_ref, *, L, tv):
    hi = lax.Precision.HIGHEST
    off = HALO - (KS - 1)
    xc = jnp.broadcast_to(cb_ref[...], (L, DX))
    for k in range(KS):
        xc = xc + cw_ref[k:k + 1, :] * xe_ref[off + k:off + k + L, :]
    xc = _silu(xc)
    xs = xc[:, 0:DS]

    lane = lax.broadcasted_iota(jnp.int32, (L, LANE), 1)
    rowi = lax.broadcasted_iota(jnp.int32, (L, LANE), 0)
    xdt = dt_raw + dtb_ref[...]
    dt = jnp.maximum(xdt, 0.0) + jnp.log1p(jnp.exp(-jnp.abs(xdt)))
    dt = jnp.where((lane < NH) & (rowi < tv), dt, 0.0)
    da = dt * (-jnp.exp(alog_ref[...]))

    r2 = lax.broadcasted_iota(jnp.int32, (L, L), 0)
    c2 = lax.broadcasted_iota(jnp.int32, (L, L), 1)
    causal = r2 >= c2
    a_cum = jnp.dot(causal.astype(F32), da, precision=hi, preferred_element_type=F32)
    a_last = a_cum[L - 1:L, :]

    er = lax.broadcasted_iota(jnp.int32, (LANE, DS), 0)
    ec = lax.broadcasted_iota(jnp.int32, (LANE, DS), 1)
    expand = (ec // HP == er).astype(F32)
    stack = jnp.concatenate([jnp.exp(a_cum), jnp.exp(a_last - a_cum) * dt,
                             jnp.broadcast_to(jnp.exp(a_last), (SUB, LANE))], axis=0)
    stack_x = jnp.dot(stack, expand, precision=hi, preferred_element_type=F32)
    ea_x = stack_x[0:L]
    wend_x = stack_x[L:2 * L]
    cd_x = stack_x[2 * L:2 * L + 1]

    ir = lax.broadcasted_iota(jnp.int32, (LANE, LANE), 0)
    ic = lax.broadcasted_iota(jnp.int32, (LANE, LANE), 1)
    ident = (ir == ic).astype(F32)
    tr = lax.dot_general(ident, jnp.concatenate([dt, a_cum], axis=0), (((1,), (1,)), ((), ())),
                         precision=hi, preferred_element_type=F32)
    dt_t = tr[:, 0:L]
    acum_t = tr[:, L:2 * L]

    lane_x = lax.broadcasted_iota(jnp.int32, (L, LANE), 1)
    hpg = NH // NG
    y_diag = []
    y_off = []
    for g in range(NG):
        bm = xc[:, DS + g * NS:DS + (g + 1) * NS]
        cm = xc[:, DS + NG * NS + g * NS:DS + NG * NS + (g + 1) * NS]
        cbm = lax.dot_general(cm, bm, (((1,), (1,)), ((), ())), preferred_element_type=F32)
        gs = slice(g * (DS // NG), (g + 1) * (DS // NG))
        st_g = st_ref[:, gs]
        y_off.append(jnp.dot(cm, st_g, preferred_element_type=F32) * ea_x[:, gs])
        xw = xs[:, gs] * wend_x[:, gs]
        st_ref[:, gs] = st_g * cd_x[:, gs] + jnp.dot(bm.T, xw, preferred_element_type=F32)
        for hp2 in range(hpg // 2):
            ws = []
            for h in (g * hpg + 2 * hp2, g * hpg + 2 * hp2 + 1):
                seg = a_cum[:, h:h + 1] - acum_t[h:h + 1, :]
                dec = jnp.exp(jnp.where(causal, seg, -jnp.inf))
                ws.append(cbm * dec * dt_t[h:h + 1, :])
            xp = xs[:, (g * hpg + 2 * hp2) * HP:(g * hpg + 2 * hp2 + 2) * HP]
            rhs = jnp.concatenate([jnp.where(lane_x < HP, xp, 0.0), jnp.where(lane_x >= HP, xp, 0.0)], axis=0)
            y_diag.append(jnp.dot(jnp.concatenate(ws, axis=1), rhs, preferred_element_type=F32))
    y = jnp.concatenate(y_diag, axis=1) + jnp.concatenate(y_off, axis=1) + dsk_ref[...] * xs
    y = y * _silu(z)
    outs = []
    for g in range(NG):
        gs = slice(g * (DS // NG), (g + 1) * (DS // NG))
        outs.append(_rms(y[:, gs], ng_ref[:, gs]))
    return jnp.concatenate(outs, axis=1)


def _state_out(st_ref, h_ref):
    for j in range(DS // LANE):
        blk = st_ref[:, j * LANE:(j + 1) * LANE].T
        for q in range(LANE // HP):
            h_ref[j * (LANE // HP) + q] = blk[q * HP:(q + 1) * HP, :]


def _ssd_prompt_kernel(xbc_ref, z_ref, dt_ref, cw_ref, cb_ref, dtb_ref, alog_ref, dsk_ref, ng_ref,
                       y_ref, nb_ref, h_ref, xe_ref, st_ref, *, L):
    c = pl.program_id(1)

    @pl.when(c == 0)
    def _():
        xe_ref[0:HALO, :] = jnp.zeros((HALO, DX), F32)
        st_ref[...] = jnp.zeros_like(st_ref)

    xe_ref[HALO:HALO + L, :] = xbc_ref[...]
    y_ref[...] = _ssd_chunk(xe_ref, z_ref[...], dt_ref[...], st_ref, cw_ref, cb_ref, dtb_ref, alog_ref, dsk_ref,
                            ng_ref, L=L, tv=L)
    xe_ref[0:HALO, :] = xe_ref[L:L + HALO, :]

    @pl.when(c == pl.num_programs(1) - 1)
    def _():
        nb_ref[...] = xe_ref[HALO - (KS - 1):HALO, :]
        _state_out(st_ref, h_ref)


def _ssd_params_specs():
    return [_resident((KS, DX)), _resident((1, DX)), _resident((1, LANE)), _resident((1, LANE)),
            _resident((1, DS)), _resident((1, DS))]


def _ssd_prompt(xbc, z, dt, params, *, nb, seq):
    nt = xbc.shape[0]
    L = CHUNK
    nper = seq // L
    row = lambda n: pl.BlockSpec((L, n), lambda s, c: (s * nper + c, 0))
    return pl.pallas_call(
        functools.partial(_ssd_prompt_kernel, L=L),
        grid=(nb, nper),
        in_specs=[row(DX), row(DS), row(LANE)] + _ssd_params_specs(),
        out_specs=[row(DS),
                   pl.BlockSpec((None, KS - 1, DX), lambda s, c: (s, 0, 0)),
                   pl.BlockSpec((None, NH, HP, NS), lambda s, c: (s, 0, 0, 0))],
        out_shape=[jax.ShapeDtypeStruct((nt, DS), F32), jax.ShapeDtypeStruct((nb, KS - 1, DX), F32),
                   jax.ShapeDtypeStruct((nb, NH, HP, NS), F32)],
        scratch_shapes=[pltpu.VMEM((HALO + L, DX), F32), pltpu.VMEM((NS, DS), F32)],
        compiler_params=_cparams(("parallel", "arbitrary")),
        name="ssd_prompt",
    )(xbc, z, dt, *params)


def _ssd_sample_kernel(xbc_ref, z_ref, dt_ref, hist_ref, h0_ref, cw_ref, cb_ref, dtb_ref, alog_ref, dsk_ref,
                       ng_ref, prev_ref, y_ref, nb_ref, h_ref, xe_ref, zb_ref, dtp_ref, st_ref, *, L, tv, nseq):
    del prev_ref
    for j in range(nseq):
        rows = slice(j * tv, (j + 1) * tv)
        xe_ref[...] = jnp.zeros_like(xe_ref)
        xe_ref[HALO - (KS - 1):HALO, :] = hist_ref[j]
        xe_ref[HALO:HALO + tv, :] = xbc_ref[rows, :]
        zb_ref[...] = jnp.zeros_like(zb_ref)
        zb_ref[0:tv, :] = z_ref[rows, :]
        dtp_ref[...] = jnp.zeros_like(dtp_ref)
        dtp_ref[0:tv, :] = dt_ref[rows, :]
        for q in range(NH * HP // LANE):
            blk = jnp.concatenate([h0_ref[j, q * (LANE // HP) + r] for r in range(LANE // HP)], axis=0)
            st_ref[:, q * LANE:(q + 1) * LANE] = blk.T
        y = _ssd_chunk(xe_ref, zb_ref[...], dtp_ref[...], st_ref, cw_ref, cb_ref, dtb_ref, alog_ref, dsk_ref,
                       ng_ref, L=L, tv=tv)
        y_ref[rows, :] = y[0:tv, :]
        nb_ref[j] = xe_ref[HALO + tv - (KS - 1):HALO + tv, :]
        _state_out(st_ref, h_ref.at[j])


def _ssd_sample(xbc, z, dt, hist, h0, params, y_prev, *, row0, tv, nseq=2):
    nb = hist.shape[0]
    L = CHUNK
    rows = nseq * tv
    blk0 = row0 // rows
    row = lambda n: pl.BlockSpec((rows, n), lambda i: (blk0 + i, 0))
    return pl.pallas_call(
        functools.partial(_ssd_sample_kernel, L=L, tv=tv, nseq=nseq),
        grid=(nb // nseq,),
        in_specs=[row(DX), row(DS), row(LANE),
                  pl.BlockSpec((nseq, KS - 1, DX), lambda i: (i, 0, 0)),
                  pl.BlockSpec((nseq, NH, HP, NS), lambda i: (i, 0, 0, 0))] + _ssd_params_specs()
                 + [pl.BlockSpec(memory_space=pl.ANY)],
        out_specs=[row(DS),
                   pl.BlockSpec((nseq, KS - 1, DX), lambda i: (i, 0, 0)),
                   pl.BlockSpec((nseq, NH, HP, NS), lambda i: (i, 0, 0, 0))],
        out_shape=[jax.ShapeDtypeStruct(y_prev.shape, F32), jax.ShapeDtypeStruct((nb, KS - 1, DX), F32),
                   jax.ShapeDtypeStruct((nb, NH, HP, NS), F32)],
        input_output_aliases={11: 0},
        scratch_shapes=[pltpu.VMEM((HALO + L, DX), F32), pltpu.VMEM((L, DS), F32), pltpu.VMEM((L, LANE), F32),
                        pltpu.VMEM((NS, DS), F32)],
        compiler_params=_cparams(("arbitrary",)),
        name="ssd_sample",
    )(xbc, z, dt, hist, h0, *params, y_prev)


def _proj_out_kernel(*refs, n):
    lhs = refs[0:n]
    ws = refs[n:2 * n]
    x_ref, g_ref, o_ref = refs[2 * n:2 * n + 3]
    m = None
    for a_ref, w_ref in zip(lhs, ws):
        p = jnp.dot(a_ref[...].astype(BF16), w_ref[...], preferred_element_type=F32)
        m = p if m is None else m + p
    o_ref[...] = x_ref[...] + _rms(m, g_ref[...])


def _proj_out(lhs, ws, x, g, *, tm=512):
    nt = x.shape[0]
    n = len(lhs)
    return pl.pallas_call(
        functools.partial(_proj_out_kernel, n=n),
        grid=(nt // tm,),
        in_specs=[pl.BlockSpec((tm, a.shape[1]), lambda i: (i, 0)) for a in lhs]
                 + [_resident(w.shape) for w in ws]
                 + [pl.BlockSpec((tm, D), lambda i: (i, 0)), _resident((1, D))],
        out_specs=pl.BlockSpec((tm, D), lambda i: (i, 0)),
        out_shape=jax.ShapeDtypeStruct((nt, D), F32),
        compiler_params=_cparams(("parallel",)),
        name="proj_out",
    )(*lhs, *ws, x, g)


def _norm_proj_kernel(*refs, n, tn):
    x_ref, g_ref = refs[0:2]
    ws = refs[2:2 + n]
    outs = refs[2 + n:2 + 2 * n]
    hn = _rms(x_ref[...], g_ref[...]).astype(BF16)
    for w_ref, o_ref in zip(ws, outs):
        for c in range(D // tn):
            sl = slice(c * tn, (c + 1) * tn)
            o_ref[:, sl] = jnp.dot(hn, w_ref[:, sl], preferred_element_type=F32).astype(o_ref.dtype)


def _norm_proj(x, g, ws, out_dtype, *, tm=512, tn=512):
    nt = x.shape[0]
    n = len(ws)
    return pl.pallas_call(
        functools.partial(_norm_proj_kernel, n=n, tn=tn),
        grid=(nt // tm,),
        in_specs=[pl.BlockSpec((tm, D), lambda i: (i, 0)), _resident((1, D))] + [_resident((D, D))] * n,
        out_specs=[pl.BlockSpec((tm, D), lambda i: (i, 0))] * n,
        out_shape=[jax.ShapeDtypeStruct((nt, D), out_dtype)] * n,
        compiler_params=_cparams(("parallel",)),
        name="norm_proj",
    )(x, g, *ws)


def _attn_kernel(*refs, nseq, tv, aliased):
    if aliased:
        q_ref, k_ref, v_ref, _, o_ref = refs
    else:
        q_ref, k_ref, v_ref, o_ref = refs
    tq = q_ref.shape[0]
    scale = XD ** -0.5
    rowi = lax.broadcasted_iota(jnp.int32, (tq, XD), 0)
    for h in range(XH):
        sl = slice(h * XD, (h + 1) * XD)
        qh = q_ref[:, sl]
        acc = None
        for j in range(nseq):
            kh = k_ref[j, :, sl].astype(BF16)
            vh = v_ref[j, :, sl].astype(BF16)
            s = lax.dot_general(qh, kh, (((1,), (1,)), ((), ())), preferred_element_type=F32) * scale
            e = jnp.exp(s - jnp.max(s, axis=-1, keepdims=True))
            p = (e / jnp.sum(e, axis=-1, keepdims=True)).astype(BF16)
            o = jnp.dot(p, vh, preferred_element_type=F32)
            if nseq == 1:
                acc = o
            else:
                mine = (rowi >= j * tv) & (rowi < (j + 1) * tv)
                acc = jnp.where(mine, o, 0.0 if acc is None else acc)
        o_ref[:, sl] = acc.astype(o_ref.dtype)


def _attn_prompt(q, k, v, *, nb, seq, tq=512):
    nt = q.shape[0]
    nper = seq // tq
    return pl.pallas_call(
        functools.partial(_attn_kernel, nseq=1, tv=tq, aliased=False),
        grid=(nb, nper),
        in_specs=[pl.BlockSpec((tq, D), lambda s, t: (s * nper + t, 0)),
                  pl.BlockSpec((1, NM, D), lambda s, t: (s, 0, 0)),
                  pl.BlockSpec((1, NM, D), lambda s, t: (s, 0, 0))],
        out_specs=pl.BlockSpec((tq, D), lambda s, t: (s * nper + t, 0)),
        out_shape=jax.ShapeDtypeStruct((nt, D), BF16),
        compiler_params=_cparams(("parallel", "arbitrary")),
        name="attn_prompt",
    )(q, k, v)


def _attn_sample(q, k, v, o_prev, *, row0, tv, nseq=4):
    nb = k.shape[0]
    rows = nseq * tv
    blk0 = row0 // rows
    return pl.pallas_call(
        functools.partial(_attn_kernel, nseq=nseq, tv=tv, aliased=True),
        grid=(nb // nseq,),
        in_specs=[pl.BlockSpec((rows, D), lambda i: (blk0 + i, 0)),
                  pl.BlockSpec((nseq, NM, D), lambda i: (i, 0, 0)),
                  pl.BlockSpec((nseq, NM, D), lambda i: (i, 0, 0)),
                  pl.BlockSpec(memory_space=pl.ANY)],
        out_specs=pl.BlockSpec((rows, D), lambda i: (blk0 + i, 0)),
        out_shape=jax.ShapeDtypeStruct(o_prev.shape, BF16),
        input_output_aliases={3: 0},
        compiler_params=_cparams(("arbitrary",)),
        name="attn_sample",
    )(q, k, v, o_prev)


def _row(v):
    return v.reshape(1, -1).astype(F32)


def _pad_lanes(v, n):
    return jnp.pad(v.reshape(1, -1).astype(F32), ((0, 0), (0, n - v.size)))


def _layer(x, mem, cache_k, cache_v, st_conv, st_sconv, st_ssm, p, *, nbp, seq, nbs, tv):
    npr = nbp * seq
    bf = lambda w: w.astype(BF16)
    padf = FFP - FF

    def ffn(x, pre, wg, wu, wd, post):
        wg = jnp.pad(bf(wg), ((0, 0), (0, padf)))
        wu = jnp.pad(bf(wu), ((0, 0), (0, padf)))
        wd = jnp.pad(bf(wd), ((0, padf), (0, 0)))
        return _ffn(x, _row(pre), wg, wu, wd, _row(post))

    x = ffn(x, p['ffn1_pre_g'], p['ffn1_w_gate'], p['ffn1_w_up'], p['ffn1_w_down'], p['ffn1_post_g'])

    w_in = p['w_in']
    s1, s2, s3, s4 = DC, 2 * DC, 2 * DC + DS, 2 * DC + DS + DX
    wt = jnp.pad(bf(w_in[:, s4:]), ((0, 0), (0, LANE - NH)))
    u, z, xbc, dtr = _mix_in(x, _row(p['mix_pre_g']), bf(w_in[:, :s1]), bf(w_in[:, s1:s2]), bf(w_in[:, s2:s3]),
                             bf(w_in[:, s3:s4]), wt)

    cpar = (p['conv_w'].astype(F32), _row(p['conv_b']), _row(p['conv_ln_g']), _row(p['conv_ln_b']))
    a_out, conv_p = _conv_prompt(u, *cpar, nb=nbp, seq=seq)
    a_out, conv_s = _conv_sample(u, st_conv, *cpar, a_out, row0=npr, tv=tv)

    spar = (p['ssm_conv_w'].astype(F32), _row(p['ssm_conv_b']), _pad_lanes(p['dt_bias'], LANE),
            _pad_lanes(p['a_log'], LANE), jnp.repeat(p['d_skip'].astype(F32), HP).reshape(1, DS),
            _row(p['ssm_norm_g']))
    y, sconv_p, ssm_p = _ssd_prompt(xbc, z, dtr, spar, nb=nbp, seq=seq)
    y, sconv_s, ssm_s = _ssd_sample(xbc, z, dtr, st_sconv, st_ssm, spar, y, row0=npr, tv=tv)

    w_out = bf(p['w_out'])
    x = _proj_out([a_out, y], [w_out[:DC], w_out[DC:]], x, _row(p['mix_post_g']))

    (q,) = _norm_proj(x, _row(p['xattn_pre_g']), [bf(p['w_xq'])], BF16)
    mk, mv = _norm_proj(mem, _row(p['mem_norm_g']), [bf(p['w_xk']), bf(p['w_xv'])], F32)
    mk = mk.reshape(nbp, NM, D)
    mv = mv.reshape(nbp, NM, D)
    o = _attn_prompt(q, mk, mv, nb=nbp, seq=seq)
    o = _attn_sample(q, cache_k, cache_v, o, row0=npr, tv=tv)
    x = _proj_out([o], [bf(p['w_xo'])], x, _row(p['xattn_post_g']))

    x = ffn(x, p['ffn2_pre_g'], p['ffn2_w_gate'], p['ffn2_w_up'], p['ffn2_w_down'], p['ffn2_post_g'])
    return x, (mk, mv, conv_p, sconv_p, ssm_p, conv_s, sconv_s, ssm_s)


def kernel(x_prompt, x_sample, mem_prompt, cache_mem_k, cache_mem_v, state_conv, state_ssm_conv, state_ssm, ffn1_pre_g, ffn1_w_gate, ffn1_w_up, ffn1_w_down, ffn1_post_g, mix_pre_g, w_in, conv_w, conv_b, conv_ln_g, conv_ln_b, ssm_conv_w, ssm_conv_b, dt_bias, a_log, d_skip, ssm_norm_g, w_out, mix_post_g, xattn_pre_g, mem_norm_g, w_xq, w_xk, w_xv, w_xo, xattn_post_g, ffn2_pre_g, ffn2_w_gate, ffn2_w_up, ffn2_w_down, ffn2_post_g):
    params = dict(ffn1_pre_g=ffn1_pre_g, ffn1_w_gate=ffn1_w_gate, ffn1_w_up=ffn1_w_up, ffn1_w_down=ffn1_w_down,
                  ffn1_post_g=ffn1_post_g, mix_pre_g=mix_pre_g, w_in=w_in, conv_w=conv_w, conv_b=conv_b,
                  conv_ln_g=conv_ln_g, conv_ln_b=conv_ln_b, ssm_conv_w=ssm_conv_w, ssm_conv_b=ssm_conv_b,
                  dt_bias=dt_bias, a_log=a_log, d_skip=d_skip, ssm_norm_g=ssm_norm_g, w_out=w_out,
                  mix_post_g=mix_post_g, xattn_pre_g=xattn_pre_g, mem_norm_g=mem_norm_g, w_xq=w_xq,
                  w_xk=w_xk, w_xv=w_xv, w_xo=w_xo, xattn_post_g=xattn_post_g, ffn2_pre_g=ffn2_pre_g,
                  ffn2_w_gate=ffn2_w_gate, ffn2_w_up=ffn2_w_up, ffn2_w_down=ffn2_w_down, ffn2_post_g=ffn2_post_g)
    depth = ffn1_pre_g.shape[0]
    nbp, seq, _ = x_prompt.shape
    nbs, tv, _ = x_sample.shape
    npr = nbp * seq
    x = jnp.concatenate([x_prompt.reshape(npr, D), x_sample.reshape(nbs * tv, D)], axis=0)
    mem = mem_prompt.reshape(nbp * NM, D)
    per_layer = []
    for layer in range(depth):
        p = {name: w[layer] for name, w in params.items()}
        x, states = _layer(x, mem, cache_mem_k[layer].reshape(nbs, NM, D), cache_mem_v[layer].reshape(nbs, NM, D),
                           state_conv[layer], state_ssm_conv[layer], state_ssm[layer], p,
                           nbp=nbp, seq=seq, nbs=nbs, tv=tv)
        per_layer.append(states)
    mk, mv, conv_p, sconv_p, ssm_p, conv_s, sconv_s, ssm_s = [jnp.stack(t) for t in zip(*per_layer)]
    yp = x[:npr].reshape(nbp, seq, D)
    ys = x[npr:].reshape(nbs, tv, D)
    return (yp, ys, mk.reshape(depth, nbp, NM, XH, XD), mv.reshape(depth, nbp, NM, XH, XD), conv_p, sconv_p, ssm_p,
            conv_s, sconv_s, ssm_s)
```

```python
import functools

import jax
import jax.numpy as jnp
from jax import lax
from jax.experimental import pallas as pl
from jax.experimental.pallas import tpu as pltpu

F32 = jnp.float32
BF16 = jnp.bfloat16

D = 2048
FF = 5504
DC = 1024
DS = 1024
KC = 31
NH = 16
HP = 64
NG = 2
NS = 128
KS = 4
DX = DS + 2 * NG * NS
CHUNK = 128
NM = 256
XH = 4
XD = D // XH
EPS = 1e-6

LANE = 128
SUB = 8
HIST = 32
HALO = 8

FF_TILE = 512
FF_TAIL = FF - (FF // FF_TILE) * FF_TILE

VMEM_LIMIT = 56 * 1024 * 1024


def _cparams(sem):
    return pltpu.CompilerParams(dimension_semantics=sem, vmem_limit_bytes=VMEM_LIMIT)


def _rms(x, g):
    return x * lax.rsqrt(jnp.mean(x * x, axis=-1, keepdims=True) + EPS) * g


def _silu(x):
    return x * jax.nn.sigmoid(x)


def _resident(shape):
    return pl.BlockSpec(shape, lambda *_: (0,) * len(shape), pipeline_mode=pl.Buffered(1))


def _ffn_kernel(*refs, n_in, n_out, n_main):
    x_refs = refs[:n_in]
    pg_ref, wg_ref, wu_ref, wd_ref, qg_ref = refs[n_in:n_in + 5]
    o_refs = refs[n_in + 5:n_in + 5 + n_out]
    xn_ref, acc_ref = refs[n_in + 5 + n_out:]
    i = pl.program_id(0)
    f = pl.program_id(1)
    last = pl.num_programs(1) - 1

    def x_tile():
        if n_in == 1:
            return x_refs[0][...]
        return jnp.where(i < n_main, x_refs[0][...], x_refs[1][...])

    @pl.when(f == 0)
    def _():
        xn_ref[...] = _rms(x_tile(), pg_ref[...]).astype(BF16)
        acc_ref[...] = jnp.zeros_like(acc_ref)

    def hidden_tile(width):
        xn = xn_ref[...]
        h = jnp.dot(xn, wg_ref[:, 0:width], preferred_element_type=F32)
        u = jnp.dot(xn, wu_ref[:, 0:width], preferred_element_type=F32)
        a = (_silu(h) * u).astype(BF16)
        acc_ref[...] += jnp.dot(a, wd_ref[0:width, :], preferred_element_type=F32)

    @pl.when(f < last)
    def _():
        hidden_tile(FF_TILE)

    @pl.when(f == last)
    def _():
        hidden_tile(FF_TAIL)
        res = x_tile() + 0.5 * _rms(acc_ref[...], qg_ref[...])
        if n_out == 1:
            o_refs[0][...] = res
        else:
            @pl.when(i < n_main)
            def _():
                o_refs[0][...] = res

            @pl.when(i >= n_main)
            def _():
                o_refs[1][...] = res


def _ffn(xs, pre_g, wg, wu, wd, post_g, *, split_out, tm=512):
    n_in = len(xs)
    nt = sum(x.shape[0] for x in xs)
    n_main = (nt - tm) // tm
    main = lambda i, f: (jnp.minimum(i, n_main - 1), 0)
    extra = lambda i, f: (0, 0)
    whole = lambda i, f: (i, 0)
    if n_in == 1:
        x_specs = [pl.BlockSpec((tm, D), whole)]
    else:
        x_specs = [pl.BlockSpec((tm, D), main), pl.BlockSpec((tm, D), extra)]
    if split_out:
        out_specs = [pl.BlockSpec((tm, D), main), pl.BlockSpec((tm, D), extra)]
        out_shape = [jax.ShapeDtypeStruct((n_main * tm, D), F32), jax.ShapeDtypeStruct((tm, D), F32)]
    else:
        out_specs = pl.BlockSpec((tm, D), whole)
        out_shape = jax.ShapeDtypeStruct((nt, D), F32)
    return pl.pallas_call(
        functools.partial(_ffn_kernel, n_in=n_in, n_out=2 if split_out else 1, n_main=n_main),
        grid=(nt // tm, pl.cdiv(FF, FF_TILE)),
        in_specs=x_specs + [
            pl.BlockSpec((1, D), lambda i, f: (0, 0)),
            pl.BlockSpec((D, FF_TILE), lambda i, f: (0, f)),
            pl.BlockSpec((D, FF_TILE), lambda i, f: (0, f)),
            pl.BlockSpec((FF_TILE, D), lambda i, f: (f, 0)),
            pl.BlockSpec((1, D), lambda i, f: (0, 0)),
        ],
        out_specs=out_specs,
        out_shape=out_shape,
        scratch_shapes=[pltpu.VMEM((tm, D), BF16), pltpu.VMEM((tm, D), F32)],
        compiler_params=_cparams(("arbitrary", "arbitrary")),
        name="ffn",
    )(*xs, pre_g, wg, wu, wd, post_g)


def _mix_in_kernel(x_ref, g_ref, wv_ref, wg_ref, wz_ref, wx_ref, wt_ref, u_ref, z_ref, xbc_ref, dt_ref, *, tn):
    hn = _rms(x_ref[...], g_ref[...]).astype(BF16)
    for c in range(DC // tn):
        sl = slice(c * tn, (c + 1) * tn)
        v = jnp.dot(hn, wv_ref[:, sl], preferred_element_type=F32)
        gt = jnp.dot(hn, wg_ref[:, sl], preferred_element_type=F32)
        u_ref[:, sl] = v * jax.nn.sigmoid(gt)
    for c in range(DS // tn):
        sl = slice(c * tn, (c + 1) * tn)
        z_ref[:, sl] = jnp.dot(hn, wz_ref[:, sl], preferred_element_type=F32)
    for c in range(DX // tn):
        sl = slice(c * tn, (c + 1) * tn)
        xbc_ref[:, sl] = jnp.dot(hn, wx_ref[:, sl], preferred_element_type=F32)
    dt_ref[...] = jnp.dot(hn, wt_ref[...], preferred_element_type=F32)


def _mix_in(x, g, wv, wg, wz, wx, wt, *, tm=256, tn=512):
    nt = x.shape[0]
    row = lambda n: pl.BlockSpec((tm, n), lambda i: (i, 0))
    return pl.pallas_call(
        functools.partial(_mix_in_kernel, tn=tn),
        grid=(nt // tm,),
        in_specs=[row(D), _resident((1, D)), _resident((D, DC)), _resident((D, DC)), _resident((D, DS)),
                  _resident((D, DX)), _resident((D, LANE))],
        out_specs=[row(DC), row(DS), row(DX), row(LANE)],
        out_shape=[jax.ShapeDtypeStruct((nt, n), F32) for n in (DC, DS, DX, LANE)],
        compiler_params=_cparams(("parallel",)),
        name="mix_in",
    )(x, g, wv, wg, wz, wx, wt)


def _conv_body(xe_ref, y_ref, w_ref, b_ref, lg_ref, lb_ref, tt):
    off = HIST - (KC - 1)
    for j in range(DC // LANE):
        sl = slice(j * LANE, (j + 1) * LANE)
        acc = jnp.broadcast_to(b_ref[:, sl], (tt, LANE))
        for k in range(KC):
            acc = acc + w_ref[k:k + 1, sl] * xe_ref[off + k:off + k + tt, sl]
        y_ref[:, sl] = acc
    y = y_ref[...]
    yc = y - jnp.mean(y, axis=-1, keepdims=True)
    a = yc * lax.rsqrt(jnp.mean(yc * yc, axis=-1, keepdims=True) + EPS) * lg_ref[...] + lb_ref[...]
    return _silu(a)


def _conv_prompt_kernel(u_ref, w_ref, b_ref, lg_ref, lb_ref, a_ref, nb_ref, xe_ref, y_ref, *, tt):
    t = pl.program_id(1)

    @pl.when(t == 0)
    def _():
        xe_ref[0:HIST, :] = jnp.zeros((HIST, DC), F32)

    xe_ref[HIST:HIST + tt, :] = u_ref[...]
    a_ref[...] = _conv_body(xe_ref, y_ref, w_ref, b_ref, lg_ref, lb_ref, tt)
    nb_ref[...] = xe_ref[HIST + tt - (KC - 1):HIST + tt, :]
    xe_ref[0:HIST, :] = xe_ref[tt:tt + HIST, :]


def _conv_prompt(u, w, b, lg, lb, *, nb, seq, tt=256):
    nt = u.shape[0]
    nper = seq // tt
    par = lambda r: _resident((r, DC))
    return pl.pallas_call(
        functools.partial(_conv_prompt_kernel, tt=tt),
        grid=(nb, nper),
        in_specs=[pl.BlockSpec((tt, DC), lambda s, t: (s * nper + t, 0)), par(KC), par(1), par(1), par(1)],
        out_specs=[pl.BlockSpec((tt, DC), lambda s, t: (s * nper + t, 0)),
                   pl.BlockSpec((None, KC - 1, DC), lambda s, t: (s, 0, 0))],
        out_shape=[jax.ShapeDtypeStruct((nt, DC), F32), jax.ShapeDtypeStruct((nb, KC - 1, DC), F32)],
        scratch_shapes=[pltpu.VMEM((HIST + tt, DC), F32), pltpu.VMEM((tt, DC), F32)],
        compiler_params=_cparams(("parallel", "arbitrary")),
        name="conv_prompt",
    )(u, w, b, lg, lb)


def _conv_sample_kernel(u_ref, hist_ref, w_ref, b_ref, lg_ref, lb_ref, prev_ref, a_ref, nb_ref, xe_ref, y_ref,
                        *, nseq, tv):
    del prev_ref
    for j in range(nseq):
        xe_ref[HIST - (KC - 1):HIST, :] = hist_ref[j]
        xe_ref[HIST:HIST + tv, :] = u_ref[j * tv:(j + 1) * tv, :]
        xe_ref[HIST + tv:HIST + SUB, :] = jnp.zeros((SUB - tv, DC), F32)
        a = _conv_body(xe_ref, y_ref, w_ref, b_ref, lg_ref, lb_ref, SUB)
        a_ref[j * tv:(j + 1) * tv, :] = a[0:tv, :]
        nb_ref[j] = xe_ref[HIST + tv - (KC - 1):HIST + tv, :]


def _conv_sample(u, hist, w, b, lg, lb, a_prev, *, row0, tv, nseq=2):
    nb = hist.shape[0]
    rows = nseq * tv
    par = lambda r: _resident((r, DC))
    blk0 = row0 // rows
    return pl.pallas_call(
        functools.partial(_conv_sample_kernel, nseq=nseq, tv=tv),
        grid=(nb // nseq,),
        in_specs=[pl.BlockSpec((rows, DC), lambda i: (blk0 + i, 0)),
                  pl.BlockSpec((nseq, KC - 1, DC), lambda i: (i, 0, 0)),
                  par(KC), par(1), par(1), par(1),
                  pl.BlockSpec(memory_space=pl.ANY)],
        out_specs=[pl.BlockSpec((rows, DC), lambda i: (blk0 + i, 0)),
                   pl.BlockSpec((nseq, KC - 1, DC), lambda i: (i, 0, 0))],
        out_shape=[jax.ShapeDtypeStruct(a_prev.shape, F32), jax.ShapeDtypeStruct((nb, KC - 1, DC), F32)],
        input_output_aliases={6: 0},
        scratch_shapes=[pltpu.VMEM((HIST + SUB, DC), F32), pltpu.VMEM((SUB, DC), F32)],
        compiler_params=_cparams(("arbitrary",)),
        name="conv_sample",
    )(u, hist, w, b, lg, lb, a_prev)


def _ssd_chunk(xe_ref, z, dt_raw, st_ref, cw_ref, cb_ref, dtb_ref, alog_ref, dsk_ref, ng_ref, *, L, tv):
    hi = lax.Precision.HIGHEST
    off = HALO - (KS - 1)
    xc = jnp.broadcast_to(cb_ref[...], (L, DX))
    for k in range(KS):
        xc = xc + cw_ref[k:k + 1, :] * xe_ref[off + k:off + k + L, :]
    xc = _silu(xc)
    xs = xc[:, 0:DS]

    lane = lax.broadcasted_iota(jnp.int32, (L, LANE), 1)
    rowi = lax.broadcasted_iota(jnp.int32, (L, LANE), 0)
    xdt = dt_raw + dtb_ref[...]
    dt = jnp.maximum(xdt, 0.0) + jnp.log1p(jnp.exp(-jnp.abs(xdt)))
    dt = jnp.where((lane < NH) & (rowi < tv), dt, 0.0)
    da = dt * (-jnp.exp(alog_ref[...]))

    r2 = lax.broadcasted_iota(jnp.int32, (L, L), 0)
    c2 = lax.broadcasted_iota(jnp.int32, (L, L), 1)
    causal = r2 >= c2
    a_cum = jnp.dot(causal.astype(F32), da, precision=hi, preferred_element_type=F32)
    a_last = a_cum[L - 1:L, :]

    er = lax.broadcasted_iota(jnp.int32, (LANE, DS), 0)
    ec = lax.broadcasted_iota(jnp.int32, (LANE, DS), 1)
    expand = (ec // HP == er).astype(F32)
    stack = jnp.concatenate([jnp.exp(a_cum), jnp.exp(a_last - a_cum) * dt,
                             jnp.broadcast_to(jnp.exp(a_last), (SUB, LANE))], axis=0)
    stack_x = jnp.dot(stack, expand, precision=hi, preferred_element_type=F32)
    ea_x = stack_x[0:L]
    wend_x = stack_x[L:2 * L]
    cd_x = stack_x[2 * L:2 * L + 1]

    ir = lax.broadcasted_iota(jnp.int32, (LANE, LANE), 0)
    ic = lax.broadcasted_iota(jnp.int32, (LANE, LANE), 1)
    ident = (ir == ic).astype(F32)
    tr = lax.dot_general(ident, jnp.concatenate([dt, a_cum], axis=0), (((1,), (1,)), ((), ())),
                         precision=hi, preferred_element_type=F32)
    dt_t = tr[:, 0:L]
    acum_t = tr[:, L:2 * L]

    lane_x = lax.broadcasted_iota(jnp.int32, (L, LANE), 1)
    hpg = NH // NG
    y_diag = []
    y_off = []
    for g in range(NG):
        bm = xc[:, DS + g * NS:DS + (g + 1) * NS]
        cm = xc[:, DS + NG * NS + g * NS:DS + NG * NS + (g + 1) * NS]
        cbm = lax.dot_general(cm, bm, (((1,), (1,)), ((), ())), preferred_element_type=F32)
        gs = slice(g * (DS // NG), (g + 1) * (DS // NG))
        st_g = st_ref[:, gs]
        y_off.append(jnp.dot(cm, st_g, preferred_element_type=F32) * ea_x[:, gs])
        xw = xs[:, gs] * wend_x[:, gs]
        st_ref[:, gs] = st_g * cd_x[:, gs] + jnp.dot(bm.T, xw, preferred_element_type=F32)
        for hp2 in range(hpg // 2):
            ws = []
            for h in (g * hpg + 2 * hp2, g * hpg + 2 * hp2 + 1):
                seg = a_cum[:, h:h + 1] - acum_t[h:h + 1, :]
                dec = jnp.exp(jnp.where(causal, seg, -jnp.inf))
                ws.append(cbm * dec * dt_t[h:h + 1, :])
            xp = xs[:, (g * hpg + 2 * hp2) * HP:(g * hpg + 2 * hp2 + 2) * HP]
            rhs = jnp.concatenate([jnp.where(lane_x < HP, xp, 0.0), jnp.where(lane_x >= HP, xp, 0.0)], axis=0)
            y_diag.append(jnp.dot(jnp.concatenate(ws, axis=1), rhs, preferred_element_type=F32))
    y = jnp.concatenate(y_diag, axis=1) + jnp.concatenate(y_off, axis=1) + dsk_ref[...] * xs
    y = y * _silu(z)
    outs = []
    for g in range(NG):
        gs = slice(g * (DS // NG), (g + 1) * (DS // NG))
        outs.append(_rms(y[:, gs], ng_ref[:, gs]))
    return jnp.concatenate(outs, axis=1)


def _state_out(st_ref, h_ref):
    for j in range(DS // LANE):
        blk = st_ref[:, j * LANE:(j + 1) * LANE].T
        for q in range(LANE // HP):
            h_ref[j * (LANE // HP) + q] = blk[q * HP:(q + 1) * HP, :]


def _ssd_prompt_kernel(xbc_ref, z_ref, dt_ref, cw_ref, cb_ref, dtb_ref, alog_ref, dsk_ref, ng_ref,
                       y_ref, nb_ref, h_ref, xe_ref, st_ref, *, L):
    c = pl.program_id(1)

    @pl.when(c == 0)
    def _():
        xe_ref[0:HALO, :] = jnp.zeros((HALO, DX), F32)
        st_ref[...] = jnp.zeros_like(st_ref)

    xe_ref[HALO:HALO + L, :] = xbc_ref[...]
    y_ref[...] = _ssd_chunk(xe_ref, z_ref[...], dt_ref[...], st_ref, cw_ref, cb_ref, dtb_ref, alog_ref, dsk_ref,
                            ng_ref, L=L, tv=L)
    xe_ref[0:HALO, :] = xe_ref[L:L + HALO, :]

    @pl.when(c == pl.num_programs(1) - 1)
    def _():
        nb_ref[...] = xe_ref[HALO - (KS - 1):HALO, :]
        _state_out(st_ref, h_ref)


def _ssd_params_specs():
    return [_resident((KS, DX)), _resident((1, DX)), _resident((1, LANE)), _resident((1, LANE)),
            _resident((1, DS)), _resident((1, DS))]


def _ssd_prompt(xbc, z, dt, params, *, nb, seq):
    nt = xbc.shape[0]
    L = CHUNK
    nper = seq // L
    row = lambda n: pl.BlockSpec((L, n), lambda s, c: (s * nper + c, 0))
    return pl.pallas_call(
        functools.partial(_ssd_prompt_kernel, L=L),
        grid=(nb, nper),
        in_specs=[row(DX), row(DS), row(LANE)] + _ssd_params_specs(),
        out_specs=[row(DS),
                   pl.BlockSpec((None, KS - 1, DX), lambda s, c: (s, 0, 0)),
                   pl.BlockSpec((None, NH, HP, NS), lambda s, c: (s, 0, 0, 0))],
        out_shape=[jax.ShapeDtypeStruct((nt, DS), F32), jax.ShapeDtypeStruct((nb, KS - 1, DX), F32),
                   jax.ShapeDtypeStruct((nb, NH, HP, NS), F32)],
        scratch_shapes=[pltpu.VMEM((HALO + L, DX), F32), pltpu.VMEM((NS, DS), F32)],
        compiler_params=_cparams(("parallel", "arbitrary")),
        name="ssd_prompt",
    )(xbc, z, dt, *params)


def _ssd_sample_kernel(xbc_ref, z_ref, dt_ref, hist_ref, h0_ref, cw_ref, cb_ref, dtb_ref, alog_ref, dsk_ref,
                       ng_ref, prev_ref, y_ref, nb_ref, h_ref, xe_ref, zb_ref, dtp_ref, st_ref, *, L, tv, nseq):
    del prev_ref
    for j in range(nseq):
        rows = slice(j * tv, (j + 1) * tv)
        xe_ref[...] = jnp.zeros_like(xe_ref)
        xe_ref[HALO - (KS - 1):HALO, :] = hist_ref[j]
        xe_ref[HALO:HALO + tv, :] = xbc_ref[rows, :]
        zb_ref[...] = jnp.zeros_like(zb_ref)
        zb_ref[0:tv, :] = z_ref[rows, :]
        dtp_ref[...] = jnp.zeros_like(dtp_ref)
        dtp_ref[0:tv, :] = dt_ref[rows, :]
        for q in range(NH * HP // LANE):
            blk = jnp.concatenate([h0_ref[j, q * (LANE // HP) + r] for r in range(LANE // HP)], axis=0)
            st_ref[:, q * LANE:(q + 1) * LANE] = blk.T
        y = _ssd_chunk(xe_ref, zb_ref[...], dtp_ref[...], st_ref, cw_ref, cb_ref, dtb_ref, alog_ref, dsk_ref,
                       ng_ref, L=L, tv=tv)
        y_ref[rows, :] = y[0:tv, :]
        nb_ref[j] = xe_ref[HALO + tv - (KS - 1):HALO + tv, :]
        _state_out(st_ref, h_ref.at[j])


def _ssd_sample(xbc, z, dt, hist, h0, params, y_prev, *, row0, tv, nseq=2):
    nb = hist.shape[0]
    L = CHUNK
    rows = nseq * tv
    blk0 = row0 // rows
    row = lambda n: pl.BlockSpec((rows, n), lambda i: (blk0 + i, 0))
    return pl.pallas_call(
        functools.partial(_ssd_sample_kernel, L=L, tv=tv, nseq=nseq),
        grid=(nb // nseq,),
        in_specs=[row(DX), row(DS), row(LANE),
                  pl.BlockSpec((nseq, KS - 1, DX), lambda i: (i, 0, 0)),
                  pl.BlockSpec((nseq, NH, HP, NS), lambda i: (i, 0, 0, 0))] + _ssd_params_specs()
                 + [pl.BlockSpec(memory_space=pl.ANY)],
        out_specs=[row(DS),
                   pl.BlockSpec((nseq, KS - 1, DX), lambda i: (i, 0, 0)),
                   pl.BlockSpec((nseq, NH, HP, NS), lambda i: (i, 0, 0, 0))],
        out_shape=[jax.ShapeDtypeStruct(y_prev.shape, F32), jax.ShapeDtypeStruct((nb, KS - 1, DX), F32),
                   jax.ShapeDtypeStruct((nb, NH, HP, NS), F32)],
        input_output_aliases={11: 0},
        scratch_shapes=[pltpu.VMEM((HALO + L, DX), F32), pltpu.VMEM((L, DS), F32), pltpu.VMEM((L, LANE), F32),
                        pltpu.VMEM((NS, DS), F32)],
        compiler_params=_cparams(("arbitrary",)),
        name="ssd_sample",
    )(xbc, z, dt, hist, h0, *params, y_prev)


def _proj_out_kernel(*refs, n):
    lhs = refs[0:n]
    ws = refs[n:2 * n]
    x_ref, g_ref, o_ref = refs[2 * n:2 * n + 3]
    m = None
    for a_ref, w_ref in zip(lhs, ws):
        p = jnp.dot(a_ref[...].astype(BF16), w_ref[...], preferred_element_type=F32)
        m = p if m is None else m + p
    o_ref[...] = x_ref[...] + _rms(m, g_ref[...])


def _proj_out(lhs, ws, x, g, *, tm=512):
    nt = x.shape[0]
    n = len(lhs)
    return pl.pallas_call(
        functools.partial(_proj_out_kernel, n=n),
        grid=(nt // tm,),
        in_specs=[pl.BlockSpec((tm, a.shape[1]), lambda i: (i, 0)) for a in lhs]
                 + [_resident(w.shape) for w in ws]
                 + [pl.BlockSpec((tm, D), lambda i: (i, 0)), _resident((1, D))],
        out_specs=pl.BlockSpec((tm, D), lambda i: (i, 0)),
        out_shape=jax.ShapeDtypeStruct((nt, D), F32),
        compiler_params=_cparams(("parallel",)),
        name="proj_out",
    )(*lhs, *ws, x, g)


def _norm_proj_kernel(*refs, n, heads):
    x_ref, g_ref = refs[0:2]
    ws = refs[2:2 + n]
    outs = refs[2 + n:2 + 2 * n]
    hn = _rms(x_ref[...], g_ref[...]).astype(BF16)
    for w_ref, o_ref in zip(ws, outs):
        for h in range(XH):
            sl = slice(h * XD, (h + 1) * XD)
            r = jnp.dot(hn, w_ref[:, sl], preferred_element_type=F32).astype(o_ref.dtype)
            if heads:
                o_ref[:, h, :] = r
            else:
                o_ref[:, sl] = r


def _norm_proj(x, g, ws, out_dtype, *, heads=False, tm=512):
    nt = x.shape[0]
    n = len(ws)
    if heads:
        tm = NM
        out_specs = [pl.BlockSpec((None, NM, XH, XD), lambda i: (i, 0, 0, 0))] * n
        out_shape = [jax.ShapeDtypeStruct((nt // NM, NM, XH, XD), out_dtype)] * n
    else:
        out_specs = [pl.BlockSpec((tm, D), lambda i: (i, 0))] * n
        out_shape = [jax.ShapeDtypeStruct((nt, D), out_dtype)] * n
    return pl.pallas_call(
        functools.partial(_norm_proj_kernel, n=n, heads=heads),
        grid=(nt // tm,),
        in_specs=[pl.BlockSpec((tm, D), lambda i: (i, 0)), _resident((1, D))] + [_resident((D, D))] * n,
        out_specs=out_specs,
        out_shape=out_shape,
        compiler_params=_cparams(("parallel",)),
        name="norm_proj",
    )(x, g, *ws)


def _attn_kernel(*refs, nseq, tv, aliased):
    if aliased:
        q_ref, k_ref, v_ref, _, o_ref = refs
    else:
        q_ref, k_ref, v_ref, o_ref = refs
    tq = q_ref.shape[0]
    scale = XD ** -0.5
    rowi = lax.broadcasted_iota(jnp.int32, (tq, XD), 0)
    for h in range(XH):
        sl = slice(h * XD, (h + 1) * XD)
        qh = q_ref[:, sl]
        acc = None
        for j in range(nseq):
            kh = k_ref[j, :, h, :].astype(BF16)
            vh = v_ref[j, :, h, :].astype(BF16)
            s = lax.dot_general(qh, kh, (((1,), (1,)), ((), ())), preferred_element_type=F32) * scale
            e = jnp.exp(s - jnp.max(s, axis=-1, keepdims=True))
            p = (e / jnp.sum(e, axis=-1, keepdims=True)).astype(BF16)
            o = jnp.dot(p, vh, preferred_element_type=F32)
            if nseq == 1:
                acc = o
            else:
                mine = (rowi >= j * tv) & (rowi < (j + 1) * tv)
                acc = jnp.where(mine, o, 0.0 if acc is None else acc)
        o_ref[:, sl] = acc.astype(o_ref.dtype)


def _attn_prompt(q, k, v, *, nb, seq, tq=512):
    nt = q.shape[0]
    nper = seq // tq
    return pl.pallas_call(
        functools.partial(_attn_kernel, nseq=1, tv=tq, aliased=False),
        grid=(nb, nper),
        in_specs=[pl.BlockSpec((tq, D), lambda s, t: (s * nper + t, 0)),
                  pl.BlockSpec((1, NM, XH, XD), lambda s, t: (s, 0, 0, 0)),
                  pl.BlockSpec((1, NM, XH, XD), lambda s, t: (s, 0, 0, 0))],
        out_specs=pl.BlockSpec((tq, D), lambda s, t: (s * nper + t, 0)),
        out_shape=jax.ShapeDtypeStruct((nt, D), BF16),
        compiler_params=_cparams(("parallel", "arbitrary")),
        name="attn_prompt",
    )(q, k, v)


def _attn_sample(q, k, v, o_prev, *, row0, tv, nseq=4):
    nb = k.shape[0]
    rows = nseq * tv
    blk0 = row0 // rows
    return pl.pallas_call(
        functools.partial(_attn_kernel, nseq=nseq, tv=tv, aliased=True),
        grid=(nb // nseq,),
        in_specs=[pl.BlockSpec((rows, D), lambda i: (blk0 + i, 0)),
                  pl.BlockSpec((nseq, NM, XH, XD), lambda i: (i, 0, 0, 0)),
                  pl.BlockSpec((nseq, NM, XH, XD), lambda i: (i, 0, 0, 0)),
                  pl.BlockSpec(memory_space=pl.ANY)],
        out_specs=pl.BlockSpec((rows, D), lambda i: (blk0 + i, 0)),
        out_shape=jax.ShapeDtypeStruct(o_prev.shape, BF16),
        input_output_aliases={3: 0},
        compiler_params=_cparams(("arbitrary",)),
        name="attn_sample",
    )(q, k, v, o_prev)


def _row(v):
    return v.reshape(1, -1).astype(F32)


def _pad_lanes(v, n):
    return jnp.pad(v.reshape(1, -1).astype(F32), ((0, 0), (0, n - v.size)))


def _layer(xs, mem, cache_k, cache_v, st_conv, st_sconv, st_ssm, p, *, nbp, seq, nbs, tv, split_out):
    npr = nbp * seq
    bf = lambda w: w.astype(BF16)

    def ffn(xs, pre, wg, wu, wd, post, split_out):
        return _ffn(xs, _row(pre), bf(wg), bf(wu), bf(wd), _row(post), split_out=split_out)

    x = ffn(xs, p['ffn1_pre_g'], p['ffn1_w_gate'], p['ffn1_w_up'], p['ffn1_w_down'], p['ffn1_post_g'], False)

    w_in = p['w_in']
    s1, s2, s3, s4 = DC, 2 * DC, 2 * DC + DS, 2 * DC + DS + DX
    wt = jnp.pad(bf(w_in[:, s4:]), ((0, 0), (0, LANE - NH)))
    u, z, xbc, dtr = _mix_in(x, _row(p['mix_pre_g']), bf(w_in[:, :s1]), bf(w_in[:, s1:s2]), bf(w_in[:, s2:s3]),
                             bf(w_in[:, s3:s4]), wt)

    cpar = (p['conv_w'].astype(F32), _row(p['conv_b']), _row(p['conv_ln_g']), _row(p['conv_ln_b']))
    a_out, conv_p = _conv_prompt(u, *cpar, nb=nbp, seq=seq)
    a_out, conv_s = _conv_sample(u, st_conv, *cpar, a_out, row0=npr, tv=tv)

    spar = (p['ssm_conv_w'].astype(F32), _row(p['ssm_conv_b']), _pad_lanes(p['dt_bias'], LANE),
            _pad_lanes(p['a_log'], LANE), jnp.repeat(p['d_skip'].astype(F32), HP).reshape(1, DS),
            _row(p['ssm_norm_g']))
    y, sconv_p, ssm_p = _ssd_prompt(xbc, z, dtr, spar, nb=nbp, seq=seq)
    y, sconv_s, ssm_s = _ssd_sample(xbc, z, dtr, st_sconv, st_ssm, spar, y, row0=npr, tv=tv)

    w_out = bf(p['w_out'])
    x = _proj_out([a_out, y], [w_out[:DC], w_out[DC:]], x, _row(p['mix_post_g']))

    (q,) = _norm_proj(x, _row(p['xattn_pre_g']), [bf(p['w_xq'])], BF16)
    mk, mv = _norm_proj(mem, _row(p['mem_norm_g']), [bf(p['w_xk']), bf(p['w_xv'])], F32, heads=True)
    o = _attn_prompt(q, mk, mv, nb=nbp, seq=seq)
    o = _attn_sample(q, cache_k, cache_v, o, row0=npr, tv=tv)
    x = _proj_out([o], [bf(p['w_xo'])], x, _row(p['xattn_post_g']))

    x = ffn([x], p['ffn2_pre_g'], p['ffn2_w_gate'], p['ffn2_w_up'], p['ffn2_w_down'], p['ffn2_post_g'], split_out)
    return x, (mk, mv, conv_p, sconv_p, ssm_p, conv_s, sconv_s, ssm_s)


def kernel(x_prompt, x_sample, mem_prompt, cache_mem_k, cache_mem_v, state_conv, state_ssm_conv, state_ssm, ffn1_pre_g, ffn1_w_gate, ffn1_w_up, ffn1_w_down, ffn1_post_g, mix_pre_g, w_in, conv_w, conv_b, conv_ln_g, conv_ln_b, ssm_conv_w, ssm_conv_b, dt_bias, a_log, d_skip, ssm_norm_g, w_out, mix_post_g, xattn_pre_g, mem_norm_g, w_xq, w_xk, w_xv, w_xo, xattn_post_g, ffn2_pre_g, ffn2_w_gate, ffn2_w_up, ffn2_w_down, ffn2_post_g):
    params = dict(ffn1_pre_g=ffn1_pre_g, ffn1_w_gate=ffn1_w_gate, ffn1_w_up=ffn1_w_up, ffn1_w_down=ffn1_w_down,
                  ffn1_post_g=ffn1_post_g, mix_pre_g=mix_pre_g, w_in=w_in, conv_w=conv_w, conv_b=conv_b,
                  conv_ln_g=conv_ln_g, conv_ln_b=conv_ln_b, ssm_conv_w=ssm_conv_w, ssm_conv_b=ssm_conv_b,
                  dt_bias=dt_bias, a_log=a_log, d_skip=d_skip, ssm_norm_g=ssm_norm_g, w_out=w_out,
                  mix_post_g=mix_post_g, xattn_pre_g=xattn_pre_g, mem_norm_g=mem_norm_g, w_xq=w_xq,
                  w_xk=w_xk, w_xv=w_xv, w_xo=w_xo, xattn_post_g=xattn_post_g, ffn2_pre_g=ffn2_pre_g,
                  ffn2_w_gate=ffn2_w_gate, ffn2_w_up=ffn2_w_up, ffn2_w_down=ffn2_w_down, ffn2_post_g=ffn2_post_g)
    depth = ffn1_pre_g.shape[0]
    nbp, seq, _ = x_prompt.shape
    nbs, tv, _ = x_sample.shape
    npr = nbp * seq
    xs = [x_prompt.reshape(npr, D), x_sample.reshape(nbs * tv, D)]
    mem = mem_prompt.reshape(nbp * NM, D)
    per_layer = []
    for layer in range(depth):
        p = {name: w[layer] for name, w in params.items()}
        x, states = _layer(xs, mem, cache_mem_k[layer], cache_mem_v[layer],
                           state_conv[layer], state_ssm_conv[layer], state_ssm[layer], p,
                           nbp=nbp, seq=seq, nbs=nbs, tv=tv, split_out=layer == depth - 1)
        xs = [x]
        per_layer.append(states)
    mk, mv, conv_p, sconv_p, ssm_p, conv_s, sconv_s, ssm_s = [jnp.stack(t) for t in zip(*per_layer)]
    yp, ys = x
    return (yp.reshape(nbp, seq, D), ys.reshape(nbs, tv, D), mk, mv, conv_p, sconv_p, ssm_p, conv_s, sconv_s, ssm_s)
```

```python
import functools

import jax
import jax.numpy as jnp
from jax import lax
from jax.experimental import pallas as pl
from jax.experimental.pallas import tpu as pltpu

F32 = jnp.float32
BF16 = jnp.bfloat16

D = 2048
FF = 5504
DC = 1024
DS = 1024
KC = 31
NH = 16
HP = 64
NG = 2
NS = 128
KS = 4
DX = DS + 2 * NG * NS
CHUNK = 128
NM = 256
XH = 4
XD = D // XH
EPS = 1e-6

LANE = 128
SUB = 8
HIST = 32
HALO = 8

FF_TILE = 512
FF_TAIL = FF - (FF // FF_TILE) * FF_TILE

VMEM_LIMIT = 56 * 1024 * 1024


def _cparams(sem):
    return pltpu.CompilerParams(dimension_semantics=sem, vmem_limit_bytes=VMEM_LIMIT)


def _rms(x, g):
    return x * lax.rsqrt(jnp.mean(x * x, axis=-1, keepdims=True) + EPS) * g


def _silu(x):
    return x * jax.nn.sigmoid(x)


def _resident(shape):
    return pl.BlockSpec(shape, lambda *_: (0,) * len(shape), pipeline_mode=pl.Buffered(1))


def _ffn_kernel(*refs, n_in, n_out, n_main):
    x_refs = refs[:n_in]
    pg_ref, wg_ref, wu_ref, wd_ref, qg_ref = refs[n_in:n_in + 5]
    o_refs = refs[n_in + 5:n_in + 5 + n_out]
    xn_ref, acc_ref = refs[n_in + 5 + n_out:]
    i = pl.program_id(0)
    f = pl.program_id(1)
    last = pl.num_programs(1) - 1

    def x_tile():
        if n_in == 1:
            return x_refs[0][...]
        return jnp.where(i < n_main, x_refs[0][...], x_refs[1][...])

    @pl.when(f == 0)
    def _():
        xn_ref[...] = _rms(x_tile(), pg_ref[...]).astype(BF16)
        acc_ref[...] = jnp.zeros_like(acc_ref)

    def hidden_tile(width):
        xn = xn_ref[...]
        h = jnp.dot(xn, wg_ref[:, 0:width], preferred_element_type=F32)
        u = jnp.dot(xn, wu_ref[:, 0:width], preferred_element_type=F32)
        a = (_silu(h) * u).astype(BF16)
        acc_ref[...] += jnp.dot(a, wd_ref[0:width, :], preferred_element_type=F32)

    @pl.when(f < last)
    def _():
        hidden_tile(FF_TILE)

    @pl.when(f == last)
    def _():
        hidden_tile(FF_TAIL)
        res = x_tile() + 0.5 * _rms(acc_ref[...], qg_ref[...])
        if n_out == 1:
            o_refs[0][...] = res
        else:
            @pl.when(i < n_main)
            def _():
                o_refs[0][...] = res

            @pl.when(i >= n_main)
            def _():
                o_refs[1][...] = res


def _ffn(xs, pre_g, wg, wu, wd, post_g, *, split_out, tm=512):
    n_in = len(xs)
    nt = sum(x.shape[0] for x in xs)
    n_main = (nt - tm) // tm
    main = lambda i, f: (jnp.minimum(i, n_main - 1), 0)
    extra = lambda i, f: (0, 0)
    whole = lambda i, f: (i, 0)
    if n_in == 1:
        x_specs = [pl.BlockSpec((tm, D), whole)]
    else:
        x_specs = [pl.BlockSpec((tm, D), main), pl.BlockSpec((tm, D), extra)]
    if split_out:
        out_specs = [pl.BlockSpec((tm, D), main), pl.BlockSpec((tm, D), extra)]
        out_shape = [jax.ShapeDtypeStruct((n_main * tm, D), F32), jax.ShapeDtypeStruct((tm, D), F32)]
    else:
        out_specs = pl.BlockSpec((tm, D), whole)
        out_shape = jax.ShapeDtypeStruct((nt, D), F32)
    return pl.pallas_call(
        functools.partial(_ffn_kernel, n_in=n_in, n_out=2 if split_out else 1, n_main=n_main),
        grid=(nt // tm, pl.cdiv(FF, FF_TILE)),
        in_specs=x_specs + [
            pl.BlockSpec((1, D), lambda i, f: (0, 0)),
            pl.BlockSpec((D, FF_TILE), lambda i, f: (0, f)),
            pl.BlockSpec((D, FF_TILE), lambda i, f: (0, f)),
            pl.BlockSpec((FF_TILE, D), lambda i, f: (f, 0)),
            pl.BlockSpec((1, D), lambda i, f: (0, 0)),
        ],
        out_specs=out_specs,
        out_shape=out_shape,
        scratch_shapes=[pltpu.VMEM((tm, D), BF16), pltpu.VMEM((tm, D), F32)],
        compiler_params=_cparams(("arbitrary", "arbitrary")),
        name="ffn",
    )(*xs, pre_g, wg, wu, wd, post_g)


def _mix_in_kernel(x_ref, g_ref, wv_ref, wg_ref, wz_ref, wx_ref, wt_ref, u_ref, z_ref, xbc_ref, dt_ref, *, tn):
    hn = _rms(x_ref[...], g_ref[...]).astype(BF16)
    for c in range(DC // tn):
        sl = slice(c * tn, (c + 1) * tn)
        v = jnp.dot(hn, wv_ref[:, sl], preferred_element_type=F32)
        gt = jnp.dot(hn, wg_ref[:, sl], preferred_element_type=F32)
        u_ref[:, sl] = v * jax.nn.sigmoid(gt)
    for c in range(DS // tn):
        sl = slice(c * tn, (c + 1) * tn)
        z_ref[:, sl] = jnp.dot(hn, wz_ref[:, sl], preferred_element_type=F32)
    for c in range(DX // tn):
        sl = slice(c * tn, (c + 1) * tn)
        xbc_ref[:, sl] = jnp.dot(hn, wx_ref[:, sl], preferred_element_type=F32)
    dt_ref[...] = jnp.dot(hn, wt_ref[...], preferred_element_type=F32)


def _mix_in(x, g, wv, wg, wz, wx, wt, *, tm=256, tn=512):
    nt = x.shape[0]
    row = lambda n: pl.BlockSpec((tm, n), lambda i: (i, 0))
    return pl.pallas_call(
        functools.partial(_mix_in_kernel, tn=tn),
        grid=(nt // tm,),
        in_specs=[row(D), _resident((1, D)), _resident((D, DC)), _resident((D, DC)), _resident((D, DS)),
                  _resident((D, DX)), _resident((D, LANE))],
        out_specs=[row(DC), row(DS), row(DX), row(LANE)],
        out_shape=[jax.ShapeDtypeStruct((nt, n), F32) for n in (DC, DS, DX, LANE)],
        compiler_params=_cparams(("parallel",)),
        name="mix_in",
    )(x, g, wv, wg, wz, wx, wt)


def _conv_body(xe_ref, y_ref, w_ref, b_ref, lg_ref, lb_ref, tt):
    off = HIST - (KC - 1)
    for j in range(DC // LANE):
        sl = slice(j * LANE, (j + 1) * LANE)
        acc = jnp.broadcast_to(b_ref[:, sl], (tt, LANE))
        for k in range(KC):
            acc = acc + w_ref[k:k + 1, sl] * xe_ref[off + k:off + k + tt, sl]
        y_ref[:, sl] = acc
    y = y_ref[...]
    yc = y - jnp.mean(y, axis=-1, keepdims=True)
    a = yc * lax.rsqrt(jnp.mean(yc * yc, axis=-1, keepdims=True) + EPS) * lg_ref[...] + lb_ref[...]
    return _silu(a)


def _conv_prompt_kernel(u_ref, w_ref, b_ref, lg_ref, lb_ref, a_ref, nb_ref, xe_ref, y_ref, *, tt):
    t = pl.program_id(1)

    @pl.when(t == 0)
    def _():
        xe_ref[0:HIST, :] = jnp.zeros((HIST, DC), F32)

    xe_ref[HIST:HIST + tt, :] = u_ref[...]
    a_ref[...] = _conv_body(xe_ref, y_ref, w_ref, b_ref, lg_ref, lb_ref, tt)
    nb_ref[...] = xe_ref[HIST + tt - (KC - 1):HIST + tt, :]
    xe_ref[0:HIST, :] = xe_ref[tt:tt + HIST, :]


def _conv_prompt(u, w, b, lg, lb, *, nb, seq, tt=256):
    nt = nb * seq
    nper = seq // tt
    par = lambda r: _resident((r, DC))
    return pl.pallas_call(
        functools.partial(_conv_prompt_kernel, tt=tt),
        grid=(nb, nper),
        in_specs=[pl.BlockSpec((tt, DC), lambda s, t: (s * nper + t, 0)), par(KC), par(1), par(1), par(1)],
        out_specs=[pl.BlockSpec((tt, DC), lambda s, t: (s * nper + t, 0)),
                   pl.BlockSpec((None, KC - 1, DC), lambda s, t: (s, 0, 0))],
        out_shape=[jax.ShapeDtypeStruct((nt, DC), F32), jax.ShapeDtypeStruct((nb, KC - 1, DC), F32)],
        scratch_shapes=[pltpu.VMEM((HIST + tt, DC), F32), pltpu.VMEM((tt, DC), F32)],
        compiler_params=_cparams(("parallel", "arbitrary")),
        name="conv_prompt",
    )(u, w, b, lg, lb)


def _conv_sample_kernel(u_ref, hist_ref, w_ref, b_ref, lg_ref, lb_ref, a_ref, nb_ref, xe_ref, y_ref, *, nseq, tv):
    for j in range(nseq):
        xe_ref[HIST - (KC - 1):HIST, :] = hist_ref[j]
        xe_ref[HIST:HIST + tv, :] = u_ref[j * tv:(j + 1) * tv, :]
        xe_ref[HIST + tv:HIST + SUB, :] = jnp.zeros((SUB - tv, DC), F32)
        a = _conv_body(xe_ref, y_ref, w_ref, b_ref, lg_ref, lb_ref, SUB)
        a_ref[j * tv:(j + 1) * tv, :] = a[0:tv, :]
        nb_ref[j] = xe_ref[HIST + tv - (KC - 1):HIST + tv, :]


def _conv_sample(u, hist, w, b, lg, lb, *, row0, tv, nseq=2):
    nb = hist.shape[0]
    rows = nseq * tv
    par = lambda r: _resident((r, DC))
    blk0 = row0 // rows
    return pl.pallas_call(
        functools.partial(_conv_sample_kernel, nseq=nseq, tv=tv),
        grid=(nb // nseq,),
        in_specs=[pl.BlockSpec((rows, DC), lambda i: (blk0 + i, 0)),
                  pl.BlockSpec((nseq, KC - 1, DC), lambda i: (i, 0, 0)),
                  par(KC), par(1), par(1), par(1)],
        out_specs=[pl.BlockSpec((rows, DC), lambda i: (i, 0)),
                   pl.BlockSpec((nseq, KC - 1, DC), lambda i: (i, 0, 0))],
        out_shape=[jax.ShapeDtypeStruct((nb * tv, DC), F32), jax.ShapeDtypeStruct((nb, KC - 1, DC), F32)],
        scratch_shapes=[pltpu.VMEM((HIST + SUB, DC), F32), pltpu.VMEM((SUB, DC), F32)],
        compiler_params=_cparams(("arbitrary",)),
        name="conv_sample",
    )(u, hist, w, b, lg, lb)


def _ssd_chunk(xe_ref, z, dt_raw, st_ref, cw_ref, cb_ref, dtb_ref, alog_ref, dsk_ref, ng_ref, *, L, tv):
    hi = lax.Precision.HIGHEST
    off = HALO - (KS - 1)
    xc = jnp.broadcast_to(cb_ref[...], (L, DX))
    for k in range(KS):
        xc = xc + cw_ref[k:k + 1, :] * xe_ref[off + k:off + k + L, :]
    xc = _silu(xc)
    xs = xc[:, 0:DS]

    lane = lax.broadcasted_iota(jnp.int32, (L, LANE), 1)
    rowi = lax.broadcasted_iota(jnp.int32, (L, LANE), 0)
    xdt = dt_raw + dtb_ref[...]
    dt = jnp.maximum(xdt, 0.0) + jnp.log1p(jnp.exp(-jnp.abs(xdt)))
    dt = jnp.where((lane < NH) & (rowi < tv), dt, 0.0)
    da = dt * (-jnp.exp(alog_ref[...]))

    r2 = lax.broadcasted_iota(jnp.int32, (L, L), 0)
    c2 = lax.broadcasted_iota(jnp.int32, (L, L), 1)
    causal = r2 >= c2
    a_cum = jnp.dot(causal.astype(F32), da, precision=hi, preferred_element_type=F32)
    a_last = a_cum[L - 1:L, :]

    er = lax.broadcasted_iota(jnp.int32, (LANE, DS), 0)
    ec = lax.broadcasted_iota(jnp.int32, (LANE, DS), 1)
    expand = (ec // HP == er).astype(F32)
    stack = jnp.concatenate([jnp.exp(a_cum), jnp.exp(a_last - a_cum) * dt,
                             jnp.broadcast_to(jnp.exp(a_last), (SUB, LANE))], axis=0)
    stack_x = jnp.dot(stack, expand, precision=hi, preferred_element_type=F32)
    ea_x = stack_x[0:L]
    wend_x = stack_x[L:2 * L]
    cd_x = stack_x[2 * L:2 * L + 1]

    ir = lax.broadcasted_iota(jnp.int32, (LANE, LANE), 0)
    ic = lax.broadcasted_iota(jnp.int32, (LANE, LANE), 1)
    ident = (ir == ic).astype(F32)
    tr = lax.dot_general(ident, jnp.concatenate([dt, a_cum], axis=0), (((1,), (1,)), ((), ())),
                         precision=hi, preferred_element_type=F32)
    dt_t = tr[:, 0:L]
    acum_t = tr[:, L:2 * L]

    lane_x = lax.broadcasted_iota(jnp.int32, (L, LANE), 1)
    hpg = NH // NG
    y_diag = []
    y_off = []
    for g in range(NG):
        bm = xc[:, DS + g * NS:DS + (g + 1) * NS]
        cm = xc[:, DS + NG * NS + g * NS:DS + NG * NS + (g + 1) * NS]
        cbm = lax.dot_general(cm, bm, (((1,), (1,)), ((), ())), preferred_element_type=F32)
        gs = slice(g * (DS // NG), (g + 1) * (DS // NG))
        st_g = st_ref[:, gs]
        y_off.append(jnp.dot(cm, st_g, preferred_element_type=F32) * ea_x[:, gs])
        xw = xs[:, gs] * wend_x[:, gs]
        st_ref[:, gs] = st_g * cd_x[:, gs] + jnp.dot(bm.T, xw, preferred_element_type=F32)
        for hp2 in range(hpg // 2):
            ws = []
            for h in (g * hpg + 2 * hp2, g * hpg + 2 * hp2 + 1):
                seg = a_cum[:, h:h + 1] - acum_t[h:h + 1, :]
                dec = jnp.exp(jnp.where(causal, seg, -jnp.inf))
                ws.append(cbm * dec * dt_t[h:h + 1, :])
            xp = xs[:, (g * hpg + 2 * hp2) * HP:(g * hpg + 2 * hp2 + 2) * HP]
            rhs = jnp.concatenate([jnp.where(lane_x < HP, xp, 0.0), jnp.where(lane_x >= HP, xp, 0.0)], axis=0)
            y_diag.append(jnp.dot(jnp.concatenate(ws, axis=1), rhs, preferred_element_type=F32))
    y = jnp.concatenate(y_diag, axis=1) + jnp.concatenate(y_off, axis=1) + dsk_ref[...] * xs
    y = y * _silu(z)
    outs = []
    for g in range(NG):
        gs = slice(g * (DS // NG), (g + 1) * (DS // NG))
        outs.append(_rms(y[:, gs], ng_ref[:, gs]))
    return jnp.concatenate(outs, axis=1)


def _state_out(st_ref, h_ref):
    for j in range(DS // LANE):
        blk = st_ref[:, j * LANE:(j + 1) * LANE].T
        for q in range(LANE // HP):
            h_ref[j * (LANE // HP) + q] = blk[q * HP:(q + 1) * HP, :]


def _ssd_prompt_kernel(xbc_ref, z_ref, dt_ref, cw_ref, cb_ref, dtb_ref, alog_ref, dsk_ref, ng_ref,
                       y_ref, nb_ref, h_ref, xe_ref, st_ref, *, L):
    c = pl.program_id(1)

    @pl.when(c == 0)
    def _():
        xe_ref[0:HALO, :] = jnp.zeros((HALO, DX), F32)
        st_ref[...] = jnp.zeros_like(st_ref)

    xe_ref[HALO:HALO + L, :] = xbc_ref[...]
    y_ref[...] = _ssd_chunk(xe_ref, z_ref[...], dt_ref[...], st_ref, cw_ref, cb_ref, dtb_ref, alog_ref, dsk_ref,
                            ng_ref, L=L, tv=L)
    xe_ref[0:HALO, :] = xe_ref[L:L + HALO, :]

    @pl.when(c == pl.num_programs(1) - 1)
    def _():
        nb_ref[...] = xe_ref[HALO - (KS - 1):HALO, :]
        _state_out(st_ref, h_ref)


def _ssd_params_specs():
    return [_resident((KS, DX)), _resident((1, DX)), _resident((1, LANE)), _resident((1, LANE)),
            _resident((1, DS)), _resident((1, DS))]


def _ssd_prompt(xbc, z, dt, params, *, nb, seq):
    nt = nb * seq
    L = CHUNK
    nper = seq // L
    row = lambda n: pl.BlockSpec((L, n), lambda s, c: (s * nper + c, 0))
    return pl.pallas_call(
        functools.partial(_ssd_prompt_kernel, L=L),
        grid=(nb, nper),
        in_specs=[row(DX), row(DS), row(LANE)] + _ssd_params_specs(),
        out_specs=[row(DS),
                   pl.BlockSpec((None, KS - 1, DX), lambda s, c: (s, 0, 0)),
                   pl.BlockSpec((None, NH, HP, NS), lambda s, c: (s, 0, 0, 0))],
        out_shape=[jax.ShapeDtypeStruct((nt, DS), F32), jax.ShapeDtypeStruct((nb, KS - 1, DX), F32),
                   jax.ShapeDtypeStruct((nb, NH, HP, NS), F32)],
        scratch_shapes=[pltpu.VMEM((HALO + L, DX), F32), pltpu.VMEM((NS, DS), F32)],
        compiler_params=_cparams(("parallel", "arbitrary")),
        name="ssd_prompt",
    )(xbc, z, dt, *params)


def _ssd_sample_kernel(xbc_ref, z_ref, dt_ref, hist_ref, h0_ref, cw_ref, cb_ref, dtb_ref, alog_ref, dsk_ref,
                       ng_ref, y_ref, nb_ref, h_ref, xe_ref, zb_ref, dtp_ref, st_ref, *, L, tv, nseq):
    for j in range(nseq):
        rows = slice(j * tv, (j + 1) * tv)
        xe_ref[...] = jnp.zeros_like(xe_ref)
        xe_ref[HALO - (KS - 1):HALO, :] = hist_ref[j]
        xe_ref[HALO:HALO + tv, :] = xbc_ref[rows, :]
        zb_ref[...] = jnp.zeros_like(zb_ref)
        zb_ref[0:tv, :] = z_ref[rows, :]
        dtp_ref[...] = jnp.zeros_like(dtp_ref)
        dtp_ref[0:tv, :] = dt_ref[rows, :]
        for q in range(NH * HP // LANE):
            blk = jnp.concatenate([h0_ref[j, q * (LANE // HP) + r] for r in range(LANE // HP)], axis=0)
            st_ref[:, q * LANE:(q + 1) * LANE] = blk.T
        y = _ssd_chunk(xe_ref, zb_ref[...], dtp_ref[...], st_ref, cw_ref, cb_ref, dtb_ref, alog_ref, dsk_ref,
                       ng_ref, L=L, tv=tv)
        y_ref[rows, :] = y[0:tv, :]
        nb_ref[j] = xe_ref[HALO + tv - (KS - 1):HALO + tv, :]
        _state_out(st_ref, h_ref.at[j])


def _ssd_sample(xbc, z, dt, hist, h0, params, *, row0, tv, nseq=2, L=CHUNK):
    nb = hist.shape[0]
    rows = nseq * tv
    blk0 = row0 // rows
    row = lambda n: pl.BlockSpec((rows, n), lambda i: (blk0 + i, 0))
    return pl.pallas_call(
        functools.partial(_ssd_sample_kernel, L=L, tv=tv, nseq=nseq),
        grid=(nb // nseq,),
        in_specs=[row(DX), row(DS), row(LANE),
                  pl.BlockSpec((nseq, KS - 1, DX), lambda i: (i, 0, 0)),
                  pl.BlockSpec((nseq, NH, HP, NS), lambda i: (i, 0, 0, 0))] + _ssd_params_specs(),
        out_specs=[pl.BlockSpec((rows, DS), lambda i: (i, 0)),
                   pl.BlockSpec((nseq, KS - 1, DX), lambda i: (i, 0, 0)),
                   pl.BlockSpec((nseq, NH, HP, NS), lambda i: (i, 0, 0, 0))],
        out_shape=[jax.ShapeDtypeStruct((nb * tv, DS), F32), jax.ShapeDtypeStruct((nb, KS - 1, DX), F32),
                   jax.ShapeDtypeStruct((nb, NH, HP, NS), F32)],
        scratch_shapes=[pltpu.VMEM((HALO + L, DX), F32), pltpu.VMEM((L, DS), F32), pltpu.VMEM((L, LANE), F32),
                        pltpu.VMEM((NS, DS), F32)],
        compiler_params=_cparams(("arbitrary",)),
        name="ssd_sample",
    )(xbc, z, dt, hist, h0, *params)


def _proj_out_kernel(*refs, n, n_main):
    lhs = refs[0:2 * n]
    ws = refs[2 * n:3 * n]
    x_ref, g_ref, o_ref = refs[3 * n:3 * n + 3]
    is_main = pl.program_id(0) < n_main
    m = None
    for k, w_ref in enumerate(ws):
        a = jnp.where(is_main, lhs[2 * k][...], lhs[2 * k + 1][...]).astype(BF16)
        p = jnp.dot(a, w_ref[...], preferred_element_type=F32)
        m = p if m is None else m + p
    o_ref[...] = x_ref[...] + _rms(m, g_ref[...])


def _proj_out(lhs_pairs, ws, x, g, *, tm=512):
    nt = x.shape[0]
    n = len(lhs_pairs)
    n_main = nt // tm - 1
    lhs_specs = []
    for a_main, a_extra in lhs_pairs:
        assert a_main.shape[0] == n_main * tm and a_extra.shape[0] == tm
        lhs_specs.append(pl.BlockSpec((tm, a_main.shape[1]), lambda i: (jnp.minimum(i, n_main - 1), 0)))
        lhs_specs.append(pl.BlockSpec((tm, a_extra.shape[1]), lambda i: (0, 0)))
    return pl.pallas_call(
        functools.partial(_proj_out_kernel, n=n, n_main=n_main),
        grid=(nt // tm,),
        in_specs=lhs_specs + [_resident(w.shape) for w in ws]
                 + [pl.BlockSpec((tm, D), lambda i: (i, 0)), _resident((1, D))],
        out_specs=pl.BlockSpec((tm, D), lambda i: (i, 0)),
        out_shape=jax.ShapeDtypeStruct((nt, D), F32),
        compiler_params=_cparams(("arbitrary",)),
        name="proj_out",
    )(*[a for pair in lhs_pairs for a in pair], *ws, x, g)


def _norm_proj_kernel(*refs, n, heads):
    x_ref, g_ref = refs[0:2]
    ws = refs[2:2 + n]
    outs = refs[2 + n:2 + 2 * n]
    hn = _rms(x_ref[...], g_ref[...]).astype(BF16)
    for w_ref, o_ref in zip(ws, outs):
        for h in range(XH):
            sl = slice(h * XD, (h + 1) * XD)
            r = jnp.dot(hn, w_ref[:, sl], preferred_element_type=F32).astype(o_ref.dtype)
            if heads:
                o_ref[:, h, :] = r
            else:
                o_ref[:, sl] = r


def _norm_proj(x, g, ws, out_dtype, *, heads=False, tm=512):
    nt = x.shape[0]
    n = len(ws)
    if heads:
        tm = NM
        out_specs = [pl.BlockSpec((None, NM, XH, XD), lambda i: (i, 0, 0, 0))] * n
        out_shape = [jax.ShapeDtypeStruct((nt // NM, NM, XH, XD), out_dtype)] * n
    else:
        out_specs = [pl.BlockSpec((tm, D), lambda i: (i, 0))] * n
        out_shape = [jax.ShapeDtypeStruct((nt, D), out_dtype)] * n
    return pl.pallas_call(
        functools.partial(_norm_proj_kernel, n=n, heads=heads),
        grid=(nt // tm,),
        in_specs=[pl.BlockSpec((tm, D), lambda i: (i, 0)), _resident((1, D))] + [_resident((D, D))] * n,
        out_specs=out_specs,
        out_shape=out_shape,
        compiler_params=_cparams(("parallel",)),
        name="norm_proj",
    )(x, g, *ws)


NLT = XD // LANE
LT_STRIDE = NLT * XH
KV_ROWS = NM * LT_STRIDE


def _kv_tiles(x):
    nb = x.shape[0]
    return x.reshape(nb, NM, XH, NLT, LANE).transpose(0, 1, 3, 2, 4).reshape(nb, KV_ROWS, LANE)


def _attn_kernel(q_ref, k_ref, v_ref, o_ref, *, nseq, tv):
    tq = q_ref.shape[0]
    scale = XD ** -0.5
    nt_dims = (((1,), (1,)), ((), ()))
    rowi = lax.broadcasted_iota(jnp.int32, (tq, LANE), 0)
    for h in range(XH):
        acc = [None] * NLT
        for j in range(nseq):
            s = None
            for lt in range(NLT):
                kt = k_ref[j, pl.ds(lt * XH + h, NM, stride=LT_STRIDE), :]
                qs = q_ref[:, h * XD + lt * LANE:h * XD + (lt + 1) * LANE].astype(F32)
                part = lax.dot_general(qs, kt, nt_dims, preferred_element_type=F32)
                s = part if s is None else s + part
            s = s * scale
            e = jnp.exp(s - jnp.max(s, axis=-1, keepdims=True))
            p = e / jnp.sum(e, axis=-1, keepdims=True)
            mine = (rowi >= j * tv) & (rowi < (j + 1) * tv)
            for lt in range(NLT):
                vt = v_ref[j, pl.ds(lt * XH + h, NM, stride=LT_STRIDE), :]
                o = jnp.dot(p, vt, preferred_element_type=F32)
                acc[lt] = o if nseq == 1 else jnp.where(mine, o, 0.0 if acc[lt] is None else acc[lt])
        for lt in range(NLT):
            o_ref[:, h * XD + lt * LANE:h * XD + (lt + 1) * LANE] = acc[lt].astype(o_ref.dtype)


def _attn_prompt(q, k, v, *, nb, seq, tq=512):
    nt = nb * seq
    nper = seq // tq
    kv_spec = pl.BlockSpec((1, KV_ROWS, LANE), lambda s, t: (s, 0, 0))
    return pl.pallas_call(
        functools.partial(_attn_kernel, nseq=1, tv=tq),
        grid=(nb, nper),
        in_specs=[pl.BlockSpec((tq, D), lambda s, t: (s * nper + t, 0)), kv_spec, kv_spec],
        out_specs=pl.BlockSpec((tq, D), lambda s, t: (s * nper + t, 0)),
        out_shape=jax.ShapeDtypeStruct((nt, D), BF16),
        compiler_params=_cparams(("parallel", "arbitrary")),
        name="attn_prompt",
    )(q, _kv_tiles(k), _kv_tiles(v))


def _attn_sample(q, k, v, *, row0, tv, nseq=4):
    nb = k.shape[0]
    rows = nseq * tv
    blk0 = row0 // rows
    kv_spec = pl.BlockSpec((nseq, KV_ROWS, LANE), lambda i: (i, 0, 0))
    return pl.pallas_call(
        functools.partial(_attn_kernel, nseq=nseq, tv=tv),
        grid=(nb // nseq,),
        in_specs=[pl.BlockSpec((rows, D), lambda i: (blk0 + i, 0)), kv_spec, kv_spec],
        out_specs=pl.BlockSpec((rows, D), lambda i: (i, 0)),
        out_shape=jax.ShapeDtypeStruct((nb * tv, D), BF16),
        compiler_params=_cparams(("arbitrary",)),
        name="attn_sample",
    )(q, _kv_tiles(k), _kv_tiles(v))


def _row(v):
    return v.reshape(1, -1).astype(F32)


def _pad_lanes(v, n):
    return jnp.pad(v.reshape(1, -1).astype(F32), ((0, 0), (0, n - v.size)))


def _layer(xs, mem, cache_k, cache_v, st_conv, st_sconv, st_ssm, p, *, nbp, seq, nbs, tv, split_out):
    npr = nbp * seq
    bf = lambda w: w.astype(BF16)

    def ffn(xs, pre, wg, wu, wd, post, split_out):
        return _ffn(xs, _row(pre), bf(wg), bf(wu), bf(wd), _row(post), split_out=split_out)

    x = ffn(xs, p['ffn1_pre_g'], p['ffn1_w_gate'], p['ffn1_w_up'], p['ffn1_w_down'], p['ffn1_post_g'], False)

    w_in = p['w_in']
    s1, s2, s3, s4 = DC, 2 * DC, 2 * DC + DS, 2 * DC + DS + DX
    wt = jnp.pad(bf(w_in[:, s4:]), ((0, 0), (0, LANE - NH)))
    u, z, xbc, dtr = _mix_in(x, _row(p['mix_pre_g']), bf(w_in[:, :s1]), bf(w_in[:, s1:s2]), bf(w_in[:, s2:s3]),
                             bf(w_in[:, s3:s4]), wt)

    cpar = (p['conv_w'].astype(F32), _row(p['conv_b']), _row(p['conv_ln_g']), _row(p['conv_ln_b']))
    a_p, conv_p = _conv_prompt(u, *cpar, nb=nbp, seq=seq)
    a_s, conv_s = _conv_sample(u, st_conv, *cpar, row0=npr, tv=tv)

    spar = (p['ssm_conv_w'].astype(F32), _row(p['ssm_conv_b']), _pad_lanes(p['dt_bias'], LANE),
            _pad_lanes(p['a_log'], LANE), jnp.repeat(p['d_skip'].astype(F32), HP).reshape(1, DS),
            _row(p['ssm_norm_g']))
    y_p, sconv_p, ssm_p = _ssd_prompt(xbc, z, dtr, spar, nb=nbp, seq=seq)
    y_s, sconv_s, ssm_s = _ssd_sample(xbc, z, dtr, st_sconv, st_ssm, spar, row0=npr, tv=tv, L=SUB)

    w_out = bf(p['w_out'])
    x = _proj_out([(a_p, a_s), (y_p, y_s)], [w_out[:DC], w_out[DC:]], x, _row(p['mix_post_g']))

    (q,) = _norm_proj(x, _row(p['xattn_pre_g']), [bf(p['w_xq'])], BF16)
    mk, mv = _norm_proj(mem, _row(p['mem_norm_g']), [bf(p['w_xk']), bf(p['w_xv'])], F32, heads=True)
    o_p = _attn_prompt(q, mk, mv, nb=nbp, seq=seq)
    o_s = _attn_sample(q, cache_k, cache_v, row0=npr, tv=tv)
    x = _proj_out([(o_p, o_s)], [bf(p['w_xo'])], x, _row(p['xattn_post_g']))

    x = ffn([x], p['ffn2_pre_g'], p['ffn2_w_gate'], p['ffn2_w_up'], p['ffn2_w_down'], p['ffn2_post_g'], split_out)
    return x, (mk, mv, conv_p, sconv_p, ssm_p, conv_s, sconv_s, ssm_s)


def kernel(x_prompt, x_sample, mem_prompt, cache_mem_k, cache_mem_v, state_conv, state_ssm_conv, state_ssm, ffn1_pre_g, ffn1_w_gate, ffn1_w_up, ffn1_w_down, ffn1_post_g, mix_pre_g, w_in, conv_w, conv_b, conv_ln_g, conv_ln_b, ssm_conv_w, ssm_conv_b, dt_bias, a_log, d_skip, ssm_norm_g, w_out, mix_post_g, xattn_pre_g, mem_norm_g, w_xq, w_xk, w_xv, w_xo, xattn_post_g, ffn2_pre_g, ffn2_w_gate, ffn2_w_up, ffn2_w_down, ffn2_post_g):
    params = dict(ffn1_pre_g=ffn1_pre_g, ffn1_w_gate=ffn1_w_gate, ffn1_w_up=ffn1_w_up, ffn1_w_down=ffn1_w_down,
                  ffn1_post_g=ffn1_post_g, mix_pre_g=mix_pre_g, w_in=w_in, conv_w=conv_w, conv_b=conv_b,
                  conv_ln_g=conv_ln_g, conv_ln_b=conv_ln_b, ssm_conv_w=ssm_conv_w, ssm_conv_b=ssm_conv_b,
                  dt_bias=dt_bias, a_log=a_log, d_skip=d_skip, ssm_norm_g=ssm_norm_g, w_out=w_out,
                  mix_post_g=mix_post_g, xattn_pre_g=xattn_pre_g, mem_norm_g=mem_norm_g, w_xq=w_xq,
                  w_xk=w_xk, w_xv=w_xv, w_xo=w_xo, xattn_post_g=xattn_post_g, ffn2_pre_g=ffn2_pre_g,
                  ffn2_w_gate=ffn2_w_gate, ffn2_w_up=ffn2_w_up, ffn2_w_down=ffn2_w_down, ffn2_post_g=ffn2_post_g)
    depth = ffn1_pre_g.shape[0]
    nbp, seq, _ = x_prompt.shape
    nbs, tv, _ = x_sample.shape
    npr = nbp * seq
    xs = [x_prompt.reshape(npr, D), x_sample.reshape(nbs * tv, D)]
    mem = mem_prompt.reshape(nbp * NM, D)
    per_layer = []
    for layer in range(depth):
        p = {name: w[layer] for name, w in params.items()}
        x, states = _layer(xs, mem, cache_mem_k[layer], cache_mem_v[layer],
                           state_conv[layer], state_ssm_conv[layer], state_ssm[layer], p,
                           nbp=nbp, seq=seq, nbs=nbs, tv=tv, split_out=layer == depth - 1)
        xs = [x]
        per_layer.append(states)
    mk, mv, conv_p, sconv_p, ssm_p, conv_s, sconv_s, ssm_s = [jnp.stack(t) for t in zip(*per_layer)]
    yp, ys = x
    return (yp.reshape(nbp, seq, D), ys.reshape(nbs, tv, D), mk, mv, conv_p, sconv_p, ssm_p, conv_s, sconv_s, ssm_s)
```

```python
import functools

import jax
import jax.numpy as jnp
from jax import lax
from jax.experimental import pallas as pl
from jax.experimental.pallas import tpu as pltpu

F32 = jnp.float32
BF16 = jnp.bfloat16

D = 2048
FF = 5504
DC = 1024
DS = 1024
KC = 31
NH = 16
HP = 64
NG = 2
NS = 128
KS = 4
DX = DS + 2 * NG * NS
CHUNK = 128
NM = 256
XH = 4
XD = D // XH
EPS = 1e-6

LANE = 128
SUB = 8
HIST = 32
HALO = 8

FF_TILE = 1024
FF_TAIL = FF - (FF // FF_TILE) * FF_TILE

VMEM_LIMIT = 56 * 1024 * 1024


def _cparams(sem):
    return pltpu.CompilerParams(dimension_semantics=sem, vmem_limit_bytes=VMEM_LIMIT)


def _rms(x, g):
    return x * lax.rsqrt(jnp.mean(x * x, axis=-1, keepdims=True) + EPS) * g


def _silu(x):
    return x * jax.nn.sigmoid(x)


def _resident(shape):
    return pl.BlockSpec(shape, lambda *_: (0,) * len(shape), pipeline_mode=pl.Buffered(1))


def _ffn_kernel(*refs, n_in, n_out, n_main):
    x_refs = refs[:n_in]
    pg_ref, wg_ref, wu_ref, wd_ref, qg_ref = refs[n_in:n_in + 5]
    o_refs = refs[n_in + 5:n_in + 5 + n_out]
    xn_ref, acc_ref = refs[n_in + 5 + n_out:]
    i = pl.program_id(0)
    f = pl.program_id(1)
    last = pl.num_programs(1) - 1

    def x_tile():
        if n_in == 1:
            return x_refs[0][...]
        return jnp.where(i < n_main, x_refs[0][...], x_refs[1][...])

    @pl.when(f == 0)
    def _():
        xn_ref[...] = _rms(x_tile(), pg_ref[...]).astype(BF16)
        acc_ref[...] = jnp.zeros_like(acc_ref)

    def hidden_tile(width):
        xn = xn_ref[...]
        h = jnp.dot(xn, wg_ref[:, 0:width], preferred_element_type=F32)
        u = jnp.dot(xn, wu_ref[:, 0:width], preferred_element_type=F32)
        a = (_silu(h) * u).astype(BF16)
        acc_ref[...] += jnp.dot(a, wd_ref[0:width, :], preferred_element_type=F32)

    @pl.when(f < last)
    def _():
        hidden_tile(FF_TILE)

    @pl.when(f == last)
    def _():
        hidden_tile(FF_TAIL)
        res = x_tile() + 0.5 * _rms(acc_ref[...], qg_ref[...])
        if n_out == 1:
            o_refs[0][...] = res
        else:
            @pl.when(i < n_main)
            def _():
                o_refs[0][...] = res

            @pl.when(i >= n_main)
            def _():
                o_refs[1][...] = res


def _ffn(xs, pre_g, wg, wu, wd, post_g, *, split_out, tm=512):
    n_in = len(xs)
    nt = sum(x.shape[0] for x in xs)
    n_main = (nt - tm) // tm
    main = lambda i, f: (jnp.minimum(i, n_main - 1), 0)
    extra = lambda i, f: (0, 0)
    whole = lambda i, f: (i, 0)
    if n_in == 1:
        x_specs = [pl.BlockSpec((tm, D), whole)]
    else:
        x_specs = [pl.BlockSpec((tm, D), main), pl.BlockSpec((tm, D), extra)]
    if split_out:
        out_specs = [pl.BlockSpec((tm, D), main), pl.BlockSpec((tm, D), extra)]
        out_shape = [jax.ShapeDtypeStruct((n_main * tm, D), F32), jax.ShapeDtypeStruct((tm, D), F32)]
    else:
        out_specs = pl.BlockSpec((tm, D), whole)
        out_shape = jax.ShapeDtypeStruct((nt, D), F32)
    return pl.pallas_call(
        functools.partial(_ffn_kernel, n_in=n_in, n_out=2 if split_out else 1, n_main=n_main),
        grid=(nt // tm, pl.cdiv(FF, FF_TILE)),
        in_specs=x_specs + [
            pl.BlockSpec((1, D), lambda i, f: (0, 0)),
            pl.BlockSpec((D, FF_TILE), lambda i, f: (0, f)),
            pl.BlockSpec((D, FF_TILE), lambda i, f: (0, f)),
            pl.BlockSpec((FF_TILE, D), lambda i, f: (f, 0)),
            pl.BlockSpec((1, D), lambda i, f: (0, 0)),
        ],
        out_specs=out_specs,
        out_shape=out_shape,
        scratch_shapes=[pltpu.VMEM((tm, D), BF16), pltpu.VMEM((tm, D), F32)],
        compiler_params=_cparams(("arbitrary", "arbitrary")),
        name="ffn",
    )(*xs, pre_g, wg, wu, wd, post_g)


def _mix_in_kernel(x_ref, g_ref, w_ref, wt_ref, u_ref, z_ref, xbc_ref, dt_ref, *, tn):
    hn = _rms(x_ref[...], g_ref[...]).astype(BF16)

    def cols(start, c):
        return jnp.dot(hn, w_ref[:, start + c * tn:start + (c + 1) * tn], preferred_element_type=F32)

    for c in range(DC // tn):
        u_ref[:, c * tn:(c + 1) * tn] = cols(0, c) * jax.nn.sigmoid(cols(DC, c))
    for c in range(DS // tn):
        z_ref[:, c * tn:(c + 1) * tn] = cols(2 * DC, c)
    for c in range(DX // tn):
        xbc_ref[:, c * tn:(c + 1) * tn] = cols(2 * DC + DS, c)
    dt_ref[...] = jnp.dot(hn, wt_ref[...], preferred_element_type=F32)


def _mix_in(x, g, w, wt, *, tm=256, tn=512):
    nt = x.shape[0]
    row = lambda n: pl.BlockSpec((tm, n), lambda i: (i, 0))
    return pl.pallas_call(
        functools.partial(_mix_in_kernel, tn=tn),
        grid=(nt // tm,),
        in_specs=[row(D), _resident((1, D)), _resident(w.shape), _resident((D, LANE))],
        out_specs=[row(DC), row(DS), row(DX), row(LANE)],
        out_shape=[jax.ShapeDtypeStruct((nt, n), F32) for n in (DC, DS, DX, LANE)],
        compiler_params=_cparams(("parallel",)),
        name="mix_in",
    )(x, g, w, wt)


def _conv_body(xe_ref, y_ref, w_ref, b_ref, lg_ref, lb_ref, tt, xs_ref=None):
    off = HIST - (KC - 1)
    if xs_ref is not None:
        span = tt + HIST - SUB
        for ph in range(1, SUB):
            xs_ref[ph - 1, 0:span, :] = xe_ref[ph:ph + span, :]

    def window(k, sl):
        a = off + k
        ph = a % SUB
        if xs_ref is None or ph == 0:
            return xe_ref[a:a + tt, sl]
        return xs_ref[ph - 1, a - ph:a - ph + tt, sl]

    for j in range(DC // LANE):
        sl = slice(j * LANE, (j + 1) * LANE)
        acc = jnp.broadcast_to(b_ref[:, sl], (tt, LANE))
        for k in range(KC):
            acc = acc + w_ref[k:k + 1, sl] * window(k, sl)
        y_ref[:, sl] = acc
    y = y_ref[...]
    yc = y - jnp.mean(y, axis=-1, keepdims=True)
    a = yc * lax.rsqrt(jnp.mean(yc * yc, axis=-1, keepdims=True) + EPS) * lg_ref[...] + lb_ref[...]
    return _silu(a)


def _conv_prompt_kernel(u_ref, w_ref, b_ref, lg_ref, lb_ref, a_ref, nb_ref, xe_ref, y_ref, xs_ref, *, tt):
    t = pl.program_id(1)

    @pl.when(t == 0)
    def _():
        xe_ref[0:HIST, :] = jnp.zeros((HIST, DC), F32)

    xe_ref[HIST:HIST + tt, :] = u_ref[...]
    a_ref[...] = _conv_body(xe_ref, y_ref, w_ref, b_ref, lg_ref, lb_ref, tt, xs_ref)
    nb_ref[...] = xe_ref[HIST + tt - (KC - 1):HIST + tt, :]
    xe_ref[0:HIST, :] = xe_ref[tt:tt + HIST, :]


def _conv_prompt(u, w, b, lg, lb, *, nb, seq, tt=256):
    nt = nb * seq
    nper = seq // tt
    par = lambda r: _resident((r, DC))
    return pl.pallas_call(
        functools.partial(_conv_prompt_kernel, tt=tt),
        grid=(nb, nper),
        in_specs=[pl.BlockSpec((tt, DC), lambda s, t: (s * nper + t, 0)), par(KC), par(1), par(1), par(1)],
        out_specs=[pl.BlockSpec((tt, DC), lambda s, t: (s * nper + t, 0)),
                   pl.BlockSpec((None, KC - 1, DC), lambda s, t: (s, 0, 0))],
        out_shape=[jax.ShapeDtypeStruct((nt, DC), F32), jax.ShapeDtypeStruct((nb, KC - 1, DC), F32)],
        scratch_shapes=[pltpu.VMEM((HIST + tt, DC), F32), pltpu.VMEM((tt, DC), F32),
                        pltpu.VMEM((SUB - 1, HIST + tt - SUB, DC), F32)],
        compiler_params=_cparams(("parallel", "arbitrary")),
        name="conv_prompt",
    )(u, w, b, lg, lb)


def _conv_sample_kernel(u_ref, hist_ref, w_ref, b_ref, lg_ref, lb_ref, a_ref, nb_ref, xe_ref, y_ref, *, nseq, tv):
    for j in range(nseq):
        xe_ref[HIST - (KC - 1):HIST, :] = hist_ref[j]
        xe_ref[HIST:HIST + tv, :] = u_ref[j * tv:(j + 1) * tv, :]
        xe_ref[HIST + tv:HIST + SUB, :] = jnp.zeros((SUB - tv, DC), F32)
        a = _conv_body(xe_ref, y_ref, w_ref, b_ref, lg_ref, lb_ref, SUB)
        a_ref[j * tv:(j + 1) * tv, :] = a[0:tv, :]
        nb_ref[j] = xe_ref[HIST + tv - (KC - 1):HIST + tv, :]


def _conv_sample(u, hist, w, b, lg, lb, *, row0, tv, nseq=2):
    nb = hist.shape[0]
    rows = nseq * tv
    par = lambda r: _resident((r, DC))
    blk0 = row0 // rows
    return pl.pallas_call(
        functools.partial(_conv_sample_kernel, nseq=nseq, tv=tv),
        grid=(nb // nseq,),
        in_specs=[pl.BlockSpec((rows, DC), lambda i: (blk0 + i, 0)),
                  pl.BlockSpec((nseq, KC - 1, DC), lambda i: (i, 0, 0)),
                  par(KC), par(1), par(1), par(1)],
        out_specs=[pl.BlockSpec((rows, DC), lambda i: (i, 0)),
                   pl.BlockSpec((nseq, KC - 1, DC), lambda i: (i, 0, 0))],
        out_shape=[jax.ShapeDtypeStruct((nb * tv, DC), F32), jax.ShapeDtypeStruct((nb, KC - 1, DC), F32)],
        scratch_shapes=[pltpu.VMEM((HIST + SUB, DC), F32), pltpu.VMEM((SUB, DC), F32)],
        compiler_params=_cparams(("arbitrary",)),
        name="conv_sample",
    )(u, hist, w, b, lg, lb)


def _ssd_chunk(xe_ref, z, dt_raw, st_ref, cw_ref, cb_ref, dtb_ref, alog_ref, dsk_ref, ng_ref, *, L, tv):
    hi = lax.Precision.HIGHEST
    off = HALO - (KS - 1)
    xc = jnp.broadcast_to(cb_ref[...], (L, DX))
    for k in range(KS):
        xc = xc + cw_ref[k:k + 1, :] * xe_ref[off + k:off + k + L, :]
    xc = _silu(xc)
    xs = xc[:, 0:DS]

    lane = lax.broadcasted_iota(jnp.int32, (L, LANE), 1)
    rowi = lax.broadcasted_iota(jnp.int32, (L, LANE), 0)
    xdt = dt_raw + dtb_ref[...]
    dt = jnp.maximum(xdt, 0.0) + jnp.log1p(jnp.exp(-jnp.abs(xdt)))
    dt = jnp.where((lane < NH) & (rowi < tv), dt, 0.0)
    da = dt * (-jnp.exp(alog_ref[...]))

    r2 = lax.broadcasted_iota(jnp.int32, (L, L), 0)
    c2 = lax.broadcasted_iota(jnp.int32, (L, L), 1)
    causal = r2 >= c2
    a_cum = jnp.dot(causal.astype(F32), da, precision=hi, preferred_element_type=F32)
    a_last = a_cum[L - 1:L, :]

    er = lax.broadcasted_iota(jnp.int32, (LANE, DS), 0)
    ec = lax.broadcasted_iota(jnp.int32, (LANE, DS), 1)
    expand = (ec // HP == er).astype(F32)
    stack = jnp.concatenate([jnp.exp(a_cum), jnp.exp(a_last - a_cum) * dt,
                             jnp.broadcast_to(jnp.exp(a_last), (SUB, LANE))], axis=0)
    stack_x = jnp.dot(stack, expand, precision=hi, preferred_element_type=F32)
    ea_x = stack_x[0:L]
    wend_x = stack_x[L:2 * L]
    cd_x = stack_x[2 * L:2 * L + 1]

    ir = lax.broadcasted_iota(jnp.int32, (LANE, LANE), 0)
    ic = lax.broadcasted_iota(jnp.int32, (LANE, LANE), 1)
    ident = (ir == ic).astype(F32)
    tr = lax.dot_general(ident, jnp.concatenate([dt, a_cum], axis=0), (((1,), (1,)), ((), ())),
                         precision=hi, preferred_element_type=F32)
    dt_t = tr[:, 0:L]
    acum_t = tr[:, L:2 * L]

    lane_x = lax.broadcasted_iota(jnp.int32, (L, LANE), 1)
    hpg = NH // NG
    bms = [xc[:, DS + g * NS:DS + (g + 1) * NS] for g in range(NG)]
    cms = [xc[:, DS + NG * NS + g * NS:DS + NG * NS + (g + 1) * NS] for g in range(NG)]
    gss = [slice(g * (DS // NG), (g + 1) * (DS // NG)) for g in range(NG)]
    cbms = [lax.dot_general(cms[g], bms[g], (((1,), (1,)), ((), ())), preferred_element_type=F32)
            for g in range(NG)]
    st_old = [st_ref[:, gss[g]] for g in range(NG)]
    y_off = [jnp.dot(cms[g], st_old[g], preferred_element_type=F32) * ea_x[:, gss[g]] for g in range(NG)]
    st_new = [jnp.dot(bms[g].T, xs[:, gss[g]] * wend_x[:, gss[g]], preferred_element_type=F32) for g in range(NG)]
    for g in range(NG):
        st_ref[:, gss[g]] = st_old[g] * cd_x[:, gss[g]] + st_new[g]
    segs = [a_cum[:, h:h + 1] - acum_t[h:h + 1, :] for h in range(NH)]
    decs = [jnp.exp(jnp.where(causal, segs[h], -jnp.inf)) for h in range(NH)]
    ws = [cbms[h // hpg] * decs[h] * dt_t[h:h + 1, :] for h in range(NH)]
    y_diag = []
    for pr in range(NH // 2):
        xp = xs[:, 2 * pr * HP:(2 * pr + 2) * HP]
        rhs = jnp.concatenate([jnp.where(lane_x < HP, xp, 0.0), jnp.where(lane_x >= HP, xp, 0.0)], axis=0)
        y_diag.append(jnp.dot(jnp.concatenate([ws[2 * pr], ws[2 * pr + 1]], axis=1), rhs,
                              preferred_element_type=F32))
    y = jnp.concatenate(y_diag, axis=1) + jnp.concatenate(y_off, axis=1) + dsk_ref[...] * xs
    y = y * _silu(z)
    outs = []
    for g in range(NG):
        gs = slice(g * (DS // NG), (g + 1) * (DS // NG))
        outs.append(_rms(y[:, gs], ng_ref[:, gs]))
    return jnp.concatenate(outs, axis=1)


def _state_out(st_ref, h_ref):
    for j in range(DS // LANE):
        blk = st_ref[:, j * LANE:(j + 1) * LANE].T
        for q in range(LANE // HP):
            h_ref[j * (LANE // HP) + q] = blk[q * HP:(q + 1) * HP, :]


def _ssd_prompt_kernel(xbc_ref, z_ref, dt_ref, cw_ref, cb_ref, dtb_ref, alog_ref, dsk_ref, ng_ref,
                       y_ref, nb_ref, h_ref, xe_ref, st_ref, *, L):
    c = pl.program_id(1)

    @pl.when(c == 0)
    def _():
        xe_ref[0:HALO, :] = jnp.zeros((HALO, DX), F32)
        st_ref[...] = jnp.zeros_like(st_ref)

    xe_ref[HALO:HALO + L, :] = xbc_ref[...]
    y_ref[...] = _ssd_chunk(xe_ref, z_ref[...], dt_ref[...], st_ref, cw_ref, cb_ref, dtb_ref, alog_ref, dsk_ref,
                            ng_ref, L=L, tv=L)
    xe_ref[0:HALO, :] = xe_ref[L:L + HALO, :]

    @pl.when(c == pl.num_programs(1) - 1)
    def _():
        nb_ref[...] = xe_ref[HALO - (KS - 1):HALO, :]
        _state_out(st_ref, h_ref)


def _ssd_params_specs():
    return [_resident((KS, DX)), _resident((1, DX)), _resident((1, LANE)), _resident((1, LANE)),
            _resident((1, DS)), _resident((1, DS))]


def _ssd_prompt(xbc, z, dt, params, *, nb, seq):
    nt = nb * seq
    L = CHUNK
    nper = seq // L
    row = lambda n: pl.BlockSpec((L, n), lambda s, c: (s * nper + c, 0))
    return pl.pallas_call(
        functools.partial(_ssd_prompt_kernel, L=L),
        grid=(nb, nper),
        in_specs=[row(DX), row(DS), row(LANE)] + _ssd_params_specs(),
        out_specs=[row(DS),
                   pl.BlockSpec((None, KS - 1, DX), lambda s, c: (s, 0, 0)),
                   pl.BlockSpec((None, NH, HP, NS), lambda s, c: (s, 0, 0, 0))],
        out_shape=[jax.ShapeDtypeStruct((nt, DS), F32), jax.ShapeDtypeStruct((nb, KS - 1, DX), F32),
                   jax.ShapeDtypeStruct((nb, NH, HP, NS), F32)],
        scratch_shapes=[pltpu.VMEM((HALO + L, DX), F32), pltpu.VMEM((NS, DS), F32)],
        compiler_params=_cparams(("parallel", "arbitrary")),
        name="ssd_prompt",
    )(xbc, z, dt, *params)


def _ssd_sample_kernel(xbc_ref, z_ref, dt_ref, hist_ref, h0_ref, cw_ref, cb_ref, dtb_ref, alog_ref, dsk_ref,
                       ng_ref, y_ref, nb_ref, h_ref, xe_ref, zb_ref, dtp_ref, st_ref, *, L, tv, nseq):
    for j in range(nseq):
        rows = slice(j * tv, (j + 1) * tv)
        xe_ref[...] = jnp.zeros_like(xe_ref)
        xe_ref[HALO - (KS - 1):HALO, :] = hist_ref[j]
        xe_ref[HALO:HALO + tv, :] = xbc_ref[rows, :]
        zb_ref[...] = jnp.zeros_like(zb_ref)
        zb_ref[0:tv, :] = z_ref[rows, :]
        dtp_ref[...] = jnp.zeros_like(dtp_ref)
        dtp_ref[0:tv, :] = dt_ref[rows, :]
        for q in range(NH * HP // LANE):
            blk = jnp.concatenate([h0_ref[j, q * (LANE // HP) + r] for r in range(LANE // HP)], axis=0)
            st_ref[:, q * LANE:(q + 1) * LANE] = blk.T
        y = _ssd_chunk(xe_ref, zb_ref[...], dtp_ref[...], st_ref, cw_ref, cb_ref, dtb_ref, alog_ref, dsk_ref,
                       ng_ref, L=L, tv=tv)
        y_ref[rows, :] = y[0:tv, :]
        nb_ref[j] = xe_ref[HALO + tv - (KS - 1):HALO + tv, :]
        _state_out(st_ref, h_ref.at[j])


def _ssd_sample(xbc, z, dt, hist, h0, params, *, row0, tv, nseq=2, L=CHUNK):
    nb = hist.shape[0]
    rows = nseq * tv
    blk0 = row0 // rows
    row = lambda n: pl.BlockSpec((rows, n), lambda i: (blk0 + i, 0))
    return pl.pallas_call(
        functools.partial(_ssd_sample_kernel, L=L, tv=tv, nseq=nseq),
        grid=(nb // nseq,),
        in_specs=[row(DX), row(DS), row(LANE),
                  pl.BlockSpec((nseq, KS - 1, DX), lambda i: (i, 0, 0)),
                  pl.BlockSpec((nseq, NH, HP, NS), lambda i: (i, 0, 0, 0))] + _ssd_params_specs(),
        out_specs=[pl.BlockSpec((rows, DS), lambda i: (i, 0)),
                   pl.BlockSpec((nseq, KS - 1, DX), lambda i: (i, 0, 0)),
                   pl.BlockSpec((nseq, NH, HP, NS), lambda i: (i, 0, 0, 0))],
        out_shape=[jax.ShapeDtypeStruct((nb * tv, DS), F32), jax.ShapeDtypeStruct((nb, KS - 1, DX), F32),
                   jax.ShapeDtypeStruct((nb, NH, HP, NS), F32)],
        scratch_shapes=[pltpu.VMEM((HALO + L, DX), F32), pltpu.VMEM((L, DS), F32), pltpu.VMEM((L, LANE), F32),
                        pltpu.VMEM((NS, DS), F32)],
        compiler_params=_cparams(("arbitrary",)),
        name="ssd_sample",
    )(xbc, z, dt, hist, h0, *params)


def _proj_out_kernel(*refs, n, n_main):
    lhs = refs[0:2 * n]
    ws = refs[2 * n:3 * n]
    x_ref, g_ref, o_ref = refs[3 * n:3 * n + 3]
    is_main = pl.program_id(0) < n_main
    m = None
    for k, w_ref in enumerate(ws):
        a = jnp.where(is_main, lhs[2 * k][...], lhs[2 * k + 1][...]).astype(BF16)
        p = jnp.dot(a, w_ref[...], preferred_element_type=F32)
        m = p if m is None else m + p
    o_ref[...] = x_ref[...] + _rms(m, g_ref[...])


def _proj_out(lhs_pairs, ws, x, g, *, tm=512):
    nt = x.shape[0]
    n = len(lhs_pairs)
    n_main = nt // tm - 1
    lhs_specs = []
    for a_main, a_extra in lhs_pairs:
        assert a_main.shape[0] == n_main * tm and a_extra.shape[0] == tm
        lhs_specs.append(pl.BlockSpec((tm, a_main.shape[1]), lambda i: (jnp.minimum(i, n_main - 1), 0)))
        lhs_specs.append(pl.BlockSpec((tm, a_extra.shape[1]), lambda i: (0, 0)))
    return pl.pallas_call(
        functools.partial(_proj_out_kernel, n=n, n_main=n_main),
        grid=(nt // tm,),
        in_specs=lhs_specs + [_resident(w.shape) for w in ws]
                 + [pl.BlockSpec((tm, D), lambda i: (i, 0)), _resident((1, D))],
        out_specs=pl.BlockSpec((tm, D), lambda i: (i, 0)),
        out_shape=jax.ShapeDtypeStruct((nt, D), F32),
        compiler_params=_cparams(("arbitrary",)),
        name="proj_out",
    )(*[a for pair in lhs_pairs for a in pair], *ws, x, g)


def _norm_proj_kernel(*refs, n, heads):
    x_ref, g_ref = refs[0:2]
    ws = refs[2:2 + n]
    outs = refs[2 + n:2 + 2 * n]
    hn = _rms(x_ref[...], g_ref[...]).astype(BF16)
    for w_ref, o_ref in zip(ws, outs):
        for h in range(XH):
            sl = slice(h * XD, (h + 1) * XD)
            r = jnp.dot(hn, w_ref[:, sl], preferred_element_type=F32).astype(o_ref.dtype)
            if heads:
                o_ref[:, h, :] = r
            else:
                o_ref[:, sl] = r


def _norm_proj(x, g, ws, out_dtype, *, heads=False, tm=512):
    nt = x.shape[0]
    n = len(ws)
    if heads:
        tm = NM
        out_specs = [pl.BlockSpec((None, NM, XH, XD), lambda i: (i, 0, 0, 0))] * n
        out_shape = [jax.ShapeDtypeStruct((nt // NM, NM, XH, XD), out_dtype)] * n
    else:
        out_specs = [pl.BlockSpec((tm, D), lambda i: (i, 0))] * n
        out_shape = [jax.ShapeDtypeStruct((nt, D), out_dtype)] * n
    return pl.pallas_call(
        functools.partial(_norm_proj_kernel, n=n, heads=heads),
        grid=(nt // tm,),
        in_specs=[pl.BlockSpec((tm, D), lambda i: (i, 0)), _resident((1, D))] + [_resident((D, D))] * n,
        out_specs=out_specs,
        out_shape=out_shape,
        compiler_params=_cparams(("parallel",)),
        name="norm_proj",
    )(x, g, *ws)


NLT = XD // LANE
LT_STRIDE = NLT * XH
KV_ROWS = NM * LT_STRIDE


def _kv_tiles(x):
    nb = x.shape[0]
    return x.reshape(nb, NM, XH, NLT, LANE).transpose(0, 1, 3, 2, 4).reshape(nb, KV_ROWS, LANE)


def _attn_kernel(q_ref, k_ref, v_ref, o_ref, *, nseq, tv):
    tq = q_ref.shape[0]
    scale = XD ** -0.5
    nt_dims = (((1,), (1,)), ((), ()))
    rowi = lax.broadcasted_iota(jnp.int32, (tq, LANE), 0)
    for h in range(XH):
        acc = [None] * NLT
        for j in range(nseq):
            s = None
            for lt in range(NLT):
                kt = k_ref[j, pl.ds(lt * XH + h, NM, stride=LT_STRIDE), :]
                qs = q_ref[:, h * XD + lt * LANE:h * XD + (lt + 1) * LANE].astype(F32)
                part = lax.dot_general(qs, kt, nt_dims, preferred_element_type=F32)
                s = part if s is None else s + part
            s = s * scale
            e = jnp.exp(s - jnp.max(s, axis=-1, keepdims=True))
            p = e / jnp.sum(e, axis=-1, keepdims=True)
            mine = (rowi >= j * tv) & (rowi < (j + 1) * tv)
            for lt in range(NLT):
                vt = v_ref[j, pl.ds(lt * XH + h, NM, stride=LT_STRIDE), :]
                o = jnp.dot(p, vt, preferred_element_type=F32)
                acc[lt] = o if nseq == 1 else jnp.where(mine, o, 0.0 if acc[lt] is None else acc[lt])
        for lt in range(NLT):
            o_ref[:, h * XD + lt * LANE:h * XD + (lt + 1) * LANE] = acc[lt].astype(o_ref.dtype)


def _group_sum(x, col_lt):
    n = x.shape[1]
    a = x + jnp.where(col_lt % 2 == 0, pltpu.roll(x, n - 1, axis=1), pltpu.roll(x, 1, axis=1))
    return a + jnp.where(col_lt < 2, pltpu.roll(a, n - 2, axis=1), pltpu.roll(a, 2, axis=1))


def _attn_sample_kernel(q_ref, k_ref, v_ref, o_ref, *, nseq, tv):
    assert NLT == 4
    tq = q_ref.shape[0]
    ncol = NM * NLT
    scale = XD ** -0.5
    nt_dims = (((1,), (1,)), ((), ()))
    col_lt = lax.broadcasted_iota(jnp.int32, (tq, ncol), 1) % NLT
    rowi = lax.broadcasted_iota(jnp.int32, (tq, LANE), 0)
    for h in range(XH):
        qp = jnp.concatenate([q_ref[:, h * XD + lt * LANE:h * XD + (lt + 1) * LANE].astype(F32)
                              for lt in range(NLT)], axis=0)
        seqs = range(nseq)
        g = [lax.dot_general(qp, k_ref[j, pl.ds(h, ncol, stride=XH), :], nt_dims, preferred_element_type=F32)
             for j in seqs]
        s4 = [sum(jnp.where(col_lt == lt, g[j][lt * tq:(lt + 1) * tq], 0.0) for lt in range(NLT)) for j in seqs]
        s = [_group_sum(s4[j], col_lt) * scale for j in seqs]
        e = [jnp.exp(s[j] - jnp.max(s[j], axis=-1, keepdims=True)) for j in seqs]
        p = [e[j] / (jnp.sum(e[j], axis=-1, keepdims=True) * (1.0 / NLT)) for j in seqs]
        o = [jnp.dot(jnp.concatenate([jnp.where(col_lt == lt, p[j], 0.0) for lt in range(NLT)], axis=0),
                     v_ref[j, pl.ds(h, ncol, stride=XH), :], preferred_element_type=F32) for j in seqs]
        acc = [None] * NLT
        for j in seqs:
            mine = (rowi >= j * tv) & (rowi < (j + 1) * tv)
            for lt in range(NLT):
                acc[lt] = jnp.where(mine, o[j][lt * tq:(lt + 1) * tq], 0.0 if acc[lt] is None else acc[lt])
        for lt in range(NLT):
            o_ref[:, h * XD + lt * LANE:h * XD + (lt + 1) * LANE] = acc[lt].astype(o_ref.dtype)


def _attn_prompt(q, k, v, *, nb, seq, tq=512):
    nt = nb * seq
    nper = seq // tq
    kv_spec = pl.BlockSpec((1, KV_ROWS, LANE), lambda s, t: (s, 0, 0))
    return pl.pallas_call(
        functools.partial(_attn_kernel, nseq=1, tv=tq),
        grid=(nb, nper),
        in_specs=[pl.BlockSpec((tq, D), lambda s, t: (s * nper + t, 0)), kv_spec, kv_spec],
        out_specs=pl.BlockSpec((tq, D), lambda s, t: (s * nper + t, 0)),
        out_shape=jax.ShapeDtypeStruct((nt, D), BF16),
        compiler_params=_cparams(("parallel", "arbitrary")),
        name="attn_prompt",
    )(q, _kv_tiles(k), _kv_tiles(v))


def _attn_sample(q, k, v, *, row0, tv, nseq=4):
    nb = k.shape[0]
    rows = nseq * tv
    blk0 = row0 // rows
    kv_spec = pl.BlockSpec((nseq, KV_ROWS, LANE), lambda i: (i, 0, 0))
    return pl.pallas_call(
        functools.partial(_attn_sample_kernel, nseq=nseq, tv=tv),
        grid=(nb // nseq,),
        in_specs=[pl.BlockSpec((rows, D), lambda i: (blk0 + i, 0)), kv_spec, kv_spec],
        out_specs=pl.BlockSpec((rows, D), lambda i: (i, 0)),
        out_shape=jax.ShapeDtypeStruct((nb * tv, D), BF16),
        compiler_params=_cparams(("arbitrary",)),
        name="attn_sample",
    )(q, _kv_tiles(k), _kv_tiles(v))


def _row(v):
    return v.reshape(1, -1).astype(F32)


def _pad_lanes(v, n):
    return jnp.pad(v.reshape(1, -1).astype(F32), ((0, 0), (0, n - v.size)))


def _layer(xs, mem, cache_k, cache_v, st_conv, st_sconv, st_ssm, p, *, nbp, seq, nbs, tv, split_out):
    npr = nbp * seq
    bf = lambda w: w.astype(BF16)

    def ffn(xs, pre, wg, wu, wd, post, split_out):
        return _ffn(xs, _row(pre), bf(wg), bf(wu), bf(wd), _row(post), split_out=split_out)

    x = ffn(xs, p['ffn1_pre_g'], p['ffn1_w_gate'], p['ffn1_w_up'], p['ffn1_w_down'], p['ffn1_post_g'], False)

    w_in = bf(p['w_in'])
    wt = jnp.pad(w_in[:, 2 * DC + DS + DX:], ((0, 0), (0, LANE - NH)))
    u, z, xbc, dtr = _mix_in(x, _row(p['mix_pre_g']), w_in, wt)

    cpar = (p['conv_w'].astype(F32), _row(p['conv_b']), _row(p['conv_ln_g']), _row(p['conv_ln_b']))
    a_p, conv_p = _conv_prompt(u, *cpar, nb=nbp, seq=seq)
    a_s, conv_s = _conv_sample(u, st_conv, *cpar, row0=npr, tv=tv)

    spar = (p['ssm_conv_w'].astype(F32), _row(p['ssm_conv_b']), _pad_lanes(p['dt_bias'], LANE),
            _pad_lanes(p['a_log'], LANE), jnp.repeat(p['d_skip'].astype(F32), HP).reshape(1, DS),
            _row(p['ssm_norm_g']))
    y_p, sconv_p, ssm_p = _ssd_prompt(xbc, z, dtr, spar, nb=nbp, seq=seq)
    y_s, sconv_s, ssm_s = _ssd_sample(xbc, z, dtr, st_sconv, st_ssm, spar, row0=npr, tv=tv, L=SUB)

    w_out = bf(p['w_out'])
    x = _proj_out([(a_p, a_s), (y_p, y_s)], [w_out[:DC], w_out[DC:]], x, _row(p['mix_post_g']))

    (q,) = _norm_proj(x, _row(p['xattn_pre_g']), [bf(p['w_xq'])], BF16)
    mk, mv = _norm_proj(mem, _row(p['mem_norm_g']), [bf(p['w_xk']), bf(p['w_xv'])], F32, heads=True)
    o_p = _attn_prompt(q, mk, mv, nb=nbp, seq=seq)
    o_s = _attn_sample(q, cache_k, cache_v, row0=npr, tv=tv)
    x = _proj_out([(o_p, o_s)], [bf(p['w_xo'])], x, _row(p['xattn_post_g']))

    x = ffn([x], p['ffn2_pre_g'], p['ffn2_w_gate'], p['ffn2_w_up'], p['ffn2_w_down'], p['ffn2_post_g'], split_out)
    return x, (mk, mv, conv_p, sconv_p, ssm_p, conv_s, sconv_s, ssm_s)


def kernel(x_prompt, x_sample, mem_prompt, cache_mem_k, cache_mem_v, state_conv, state_ssm_conv, state_ssm, ffn1_pre_g, ffn1_w_gate, ffn1_w_up, ffn1_w_down, ffn1_post_g, mix_pre_g, w_in, conv_w, conv_b, conv_ln_g, conv_ln_b, ssm_conv_w, ssm_conv_b, dt_bias, a_log, d_skip, ssm_norm_g, w_out, mix_post_g, xattn_pre_g, mem_norm_g, w_xq, w_xk, w_xv, w_xo, xattn_post_g, ffn2_pre_g, ffn2_w_gate, ffn2_w_up, ffn2_w_down, ffn2_post_g):
    params = dict(ffn1_pre_g=ffn1_pre_g, ffn1_w_gate=ffn1_w_gate, ffn1_w_up=ffn1_w_up, ffn1_w_down=ffn1_w_down,
                  ffn1_post_g=ffn1_post_g, mix_pre_g=mix_pre_g, w_in=w_in, conv_w=conv_w, conv_b=conv_b,
                  conv_ln_g=conv_ln_g, conv_ln_b=conv_ln_b, ssm_conv_w=ssm_conv_w, ssm_conv_b=ssm_conv_b,
                  dt_bias=dt_bias, a_log=a_log, d_skip=d_skip, ssm_norm_g=ssm_norm_g, w_out=w_out,
                  mix_post_g=mix_post_g, xattn_pre_g=xattn_pre_g, mem_norm_g=mem_norm_g, w_xq=w_xq,
                  w_xk=w_xk, w_xv=w_xv, w_xo=w_xo, xattn_post_g=xattn_post_g, ffn2_pre_g=ffn2_pre_g,
                  ffn2_w_gate=ffn2_w_gate, ffn2_w_up=ffn2_w_up, ffn2_w_down=ffn2_w_down, ffn2_post_g=ffn2_post_g)
    depth = ffn1_pre_g.shape[0]
    nbp, seq, _ = x_prompt.shape
    nbs, tv, _ = x_sample.shape
    npr = nbp * seq
    xs = [x_prompt.reshape(npr, D), x_sample.reshape(nbs * tv, D)]
    mem = mem_prompt.reshape(nbp * NM, D)
    per_layer = []
    for layer in range(depth):
        p = {name: w[layer] for name, w in params.items()}
        x, states = _layer(xs, mem, cache_mem_k[layer], cache_mem_v[layer],
                           state_conv[layer], state_ssm_conv[layer], state_ssm[layer], p,
                           nbp=nbp, seq=seq, nbs=nbs, tv=tv, split_out=layer == depth - 1)
        xs = [x]
        per_layer.append(states)
    mk, mv, conv_p, sconv_p, ssm_p, conv_s, sconv_s, ssm_s = [jnp.stack(t) for t in zip(*per_layer)]
    yp, ys = x
    return (yp.reshape(nbp, seq, D), ys.reshape(nbs, tv, D), mk, mv, conv_p, sconv_p, ssm_p, conv_s, sconv_s, ssm_s)
```

```python
import functools

import jax
import jax.numpy as jnp
from jax import lax
from jax.experimental import pallas as pl
from jax.experimental.pallas import tpu as pltpu

F32 = jnp.float32
BF16 = jnp.bfloat16

D = 2048
FF = 5504
DC = 1024
DS = 1024
KC = 31
NH = 16
HP = 64
NG = 2
NS = 128
KS = 4
DX = DS + 2 * NG * NS
CHUNK = 128
NM = 256
XH = 4
XD = D // XH
EPS = 1e-6

LANE = 128
SUB = 8
HIST = 32
HALO = 8

FF_TILE = 1024
FF_TAIL = FF - (FF // FF_TILE) * FF_TILE

VMEM_LIMIT = 56 * 1024 * 1024


def _cparams(sem):
    return pltpu.CompilerParams(dimension_semantics=sem, vmem_limit_bytes=VMEM_LIMIT)


def _rms(x, g):
    return x * lax.rsqrt(jnp.mean(x * x, axis=-1, keepdims=True) + EPS) * g


def _silu(x):
    return x * jax.nn.sigmoid(x)


def _resident(shape):
    return pl.BlockSpec(shape, lambda *_: (0,) * len(shape), pipeline_mode=pl.Buffered(1))


def _ffn_kernel(*refs, n_in, n_out, n_main):
    x_refs = refs[:n_in]
    pg_ref, wg_ref, wu_ref, wd_ref, qg_ref = refs[n_in:n_in + 5]
    o_refs = refs[n_in + 5:n_in + 5 + n_out]
    xn_ref, acc_ref = refs[n_in + 5 + n_out:]
    i = pl.program_id(0)
    f = pl.program_id(1)
    last = pl.num_programs(1) - 1

    def x_tile():
        if n_in == 1:
            return x_refs[0][...]
        return jnp.where(i < n_main, x_refs[0][...], x_refs[1][...])

    @pl.when(f == 0)
    def _():
        xn_ref[...] = _rms(x_tile(), pg_ref[...]).astype(BF16)
        acc_ref[...] = jnp.zeros_like(acc_ref)

    def hidden_tile(width):
        xn = xn_ref[...]
        h = jnp.dot(xn, wg_ref[:, 0:width], preferred_element_type=F32)
        u = jnp.dot(xn, wu_ref[:, 0:width], preferred_element_type=F32)
        a = (_silu(h) * u).astype(BF16)
        acc_ref[...] += jnp.dot(a, wd_ref[0:width, :], preferred_element_type=F32)

    @pl.when(f < last)
    def _():
        hidden_tile(FF_TILE)

    @pl.when(f == last)
    def _():
        hidden_tile(FF_TAIL)
        res = x_tile() + 0.5 * _rms(acc_ref[...], qg_ref[...])
        if n_out == 1:
            o_refs[0][...] = res
        else:
            @pl.when(i < n_main)
            def _():
                o_refs[0][...] = res

            @pl.when(i >= n_main)
            def _():
                o_refs[1][...] = res


def _ffn(xs, pre_g, wg, wu, wd, post_g, *, split_out, tm=512):
    n_in = len(xs)
    nt = sum(x.shape[0] for x in xs)
    n_main = (nt - tm) // tm
    main = lambda i, f: (jnp.minimum(i, n_main - 1), 0)
    extra = lambda i, f: (0, 0)
    whole = lambda i, f: (i, 0)
    if n_in == 1:
        x_specs = [pl.BlockSpec((tm, D), whole)]
    else:
        x_specs = [pl.BlockSpec((tm, D), main), pl.BlockSpec((tm, D), extra)]
    if split_out:
        out_specs = [pl.BlockSpec((tm, D), main), pl.BlockSpec((tm, D), extra)]
        out_shape = [jax.ShapeDtypeStruct((n_main * tm, D), F32), jax.ShapeDtypeStruct((tm, D), F32)]
    else:
        out_specs = pl.BlockSpec((tm, D), whole)
        out_shape = jax.ShapeDtypeStruct((nt, D), F32)
    return pl.pallas_call(
        functools.partial(_ffn_kernel, n_in=n_in, n_out=2 if split_out else 1, n_main=n_main),
        grid=(nt // tm, pl.cdiv(FF, FF_TILE)),
        in_specs=x_specs + [
            pl.BlockSpec((1, D), lambda i, f: (0, 0)),
            pl.BlockSpec((D, FF_TILE), lambda i, f: (0, f)),
            pl.BlockSpec((D, FF_TILE), lambda i, f: (0, f)),
            pl.BlockSpec((FF_TILE, D), lambda i, f: (f, 0)),
            pl.BlockSpec((1, D), lambda i, f: (0, 0)),
        ],
        out_specs=out_specs,
        out_shape=out_shape,
        scratch_shapes=[pltpu.VMEM((tm, D), BF16), pltpu.VMEM((tm, D), F32)],
        compiler_params=_cparams(("arbitrary", "arbitrary")),
        name="ffn",
    )(*xs, pre_g, wg, wu, wd, post_g)


def _mix_in_kernel(x_ref, g_ref, w_ref, wt_ref, u_ref, z_ref, xbc_ref, dt_ref, *, tn):
    hn = _rms(x_ref[...], g_ref[...]).astype(BF16)

    def cols(start, c):
        return jnp.dot(hn, w_ref[:, start + c * tn:start + (c + 1) * tn], preferred_element_type=F32)

    for c in range(DC // tn):
        u_ref[:, c * tn:(c + 1) * tn] = cols(0, c) * jax.nn.sigmoid(cols(DC, c))
    for c in range(DS // tn):
        z_ref[:, c * tn:(c + 1) * tn] = cols(2 * DC, c)
    for c in range(DX // tn):
        xbc_ref[:, c * tn:(c + 1) * tn] = cols(2 * DC + DS, c)
    dt_ref[...] = jnp.dot(hn, wt_ref[...], preferred_element_type=F32)


def _mix_in(x, g, w, wt, *, tm=256, tn=512):
    nt = x.shape[0]
    row = lambda n: pl.BlockSpec((tm, n), lambda i: (i, 0))
    return pl.pallas_call(
        functools.partial(_mix_in_kernel, tn=tn),
        grid=(nt // tm,),
        in_specs=[row(D), _resident((1, D)), _resident(w.shape), _resident((D, LANE))],
        out_specs=[row(DC), row(DS), row(DX), row(LANE)],
        out_shape=[jax.ShapeDtypeStruct((nt, n), F32) for n in (DC, DS, DX, LANE)],
        compiler_params=_cparams(("parallel",)),
        name="mix_in",
    )(x, g, w, wt)


def _conv_prompt_kernel(u_ref, w_ref, b_ref, lg_ref, lb_ref, a_ref, nb_ref, xe_ref, y_ref, *, tt):
    t = pl.program_id(1)
    off = HIST - (KC - 1)

    @pl.when(t == 0)
    def _():
        xe_ref[:, 0:HIST, :] = jnp.zeros((DC // LANE, HIST, LANE), F32)

    for j in range(DC // LANE):
        xe_ref[j, HIST:HIST + tt, :] = u_ref[:, j * LANE:(j + 1) * LANE]
    for j in range(DC // LANE):
        sl = slice(j * LANE, (j + 1) * LANE)
        acc = jnp.broadcast_to(b_ref[:, sl], (tt, LANE))
        for k in range(KC):
            acc = acc + w_ref[k:k + 1, sl] * xe_ref[j, off + k:off + k + tt, :]
        y_ref[:, sl] = acc
    for j in range(DC // LANE):
        nb_ref[:, j * LANE:(j + 1) * LANE] = xe_ref[j, HIST + tt - (KC - 1):HIST + tt, :]
        xe_ref[j, 0:HIST, :] = xe_ref[j, tt:tt + HIST, :]
    @pl.when(t >= 0)
    def _():
        y = y_ref[...]
        yc = y - jnp.mean(y, axis=-1, keepdims=True)
        a_ref[...] = _silu(yc * lax.rsqrt(jnp.mean(yc * yc, axis=-1, keepdims=True) + EPS) * lg_ref[...]
                           + lb_ref[...])


def _conv_prompt(u, w, b, lg, lb, *, nb, seq, tt=256):
    nt = nb * seq
    nper = seq // tt
    par = lambda r: _resident((r, DC))
    return pl.pallas_call(
        functools.partial(_conv_prompt_kernel, tt=tt),
        grid=(nb, nper),
        in_specs=[pl.BlockSpec((tt, DC), lambda s, t: (s * nper + t, 0)), par(KC), par(1), par(1), par(1)],
        out_specs=[pl.BlockSpec((tt, DC), lambda s, t: (s * nper + t, 0)),
                   pl.BlockSpec((None, KC - 1, DC), lambda s, t: (s, 0, 0))],
        out_shape=[jax.ShapeDtypeStruct((nt, DC), F32), jax.ShapeDtypeStruct((nb, KC - 1, DC), F32)],
        scratch_shapes=[pltpu.VMEM((DC // LANE, HIST + tt, LANE), F32), pltpu.VMEM((tt, DC), F32)],
        compiler_params=_cparams(("parallel", "arbitrary")),
        name="conv_prompt",
    )(u, w, b, lg, lb)


def _conv_sample_kernel(u_ref, hist_ref, w_ref, b_ref, lg_ref, lb_ref, a_ref, nh_ref, us_ref, y_ref, as_ref,
                        *, sb, tv):
    nl = DC // LANE
    for j in range(nl):
        us_ref[j] = u_ref[:, j * LANE:(j + 1) * LANE]
    for j in range(nl):
        sl = slice(j * LANE, (j + 1) * LANE)
        accs = [jnp.broadcast_to(b_ref[:, sl], (sb, LANE)) for _ in range(tv)]
        for m in range(KC - 1 + tv):
            if m < KC - 1:
                xm = hist_ref[m, :, sl]
            else:
                xm = us_ref[j, pl.ds(m - (KC - 1), sb, stride=tv), :]
            for t in range(tv):
                if 0 <= m - t < KC:
                    accs[t] = accs[t] + w_ref[m - t:m - t + 1, sl] * xm
            if m >= tv:
                nh_ref[m - tv, :, sl] = xm
        for t in range(tv):
            y_ref[t, :, sl] = accs[t]
    for t in range(tv):
        y = y_ref[t]
        yc = y - jnp.mean(y, axis=-1, keepdims=True)
        a = _silu(yc * lax.rsqrt(jnp.mean(yc * yc, axis=-1, keepdims=True) + EPS) * lg_ref[...] + lb_ref[...])
        for j in range(nl):
            as_ref[j, pl.ds(t, sb, stride=tv), :] = a[:, j * LANE:(j + 1) * LANE]
    for j in range(nl):
        a_ref[:, j * LANE:(j + 1) * LANE] = as_ref[j]


def _conv_sample(u, hist, w, b, lg, lb, *, row0, tv, sb=32):
    nb = hist.shape[1]
    rows = sb * tv
    par = lambda r: _resident((r, DC))
    blk0 = row0 // rows
    hist_spec = pl.BlockSpec((KC - 1, sb, DC), lambda i: (0, i, 0))
    return pl.pallas_call(
        functools.partial(_conv_sample_kernel, sb=sb, tv=tv),
        grid=(nb // sb,),
        in_specs=[pl.BlockSpec((rows, DC), lambda i: (blk0 + i, 0)), hist_spec, par(KC), par(1), par(1), par(1)],
        out_specs=[pl.BlockSpec((rows, DC), lambda i: (i, 0)), hist_spec],
        out_shape=[jax.ShapeDtypeStruct((nb * tv, DC), F32), jax.ShapeDtypeStruct((KC - 1, nb, DC), F32)],
        scratch_shapes=[pltpu.VMEM((DC // LANE, rows, LANE), F32), pltpu.VMEM((tv, sb, DC), F32),
                        pltpu.VMEM((DC // LANE, rows, LANE), F32)],
        compiler_params=_cparams(("arbitrary",)),
        name="conv_sample",
    )(u, hist, w, b, lg, lb)


def _to_slabs(xe_ref, r0, x):
    for j in range(x.shape[1] // LANE):
        xe_ref[j, r0:r0 + x.shape[0], :] = x[:, j * LANE:(j + 1) * LANE]


def _from_slabs(xe_ref, r0, rows):
    return jnp.concatenate([xe_ref[j, r0:r0 + rows, :] for j in range(xe_ref.shape[0])], axis=1)


def _ssd_chunk(xe_ref, z, dt_raw, st_ref, cw_ref, cb_ref, dtb_ref, alog_ref, dsk_ref, ng_ref, *, L, tv):
    hi = lax.Precision.HIGHEST
    off = HALO - (KS - 1)
    cols = []
    for j in range(DX // LANE):
        sl = slice(j * LANE, (j + 1) * LANE)
        acc = jnp.broadcast_to(cb_ref[:, sl], (L, LANE))
        for k in range(KS):
            acc = acc + cw_ref[k:k + 1, sl] * xe_ref[j, off + k:off + k + L, :]
        cols.append(acc)
    xc = _silu(jnp.concatenate(cols, axis=1))
    xs = xc[:, 0:DS]

    lane = lax.broadcasted_iota(jnp.int32, (L, LANE), 1)
    rowi = lax.broadcasted_iota(jnp.int32, (L, LANE), 0)
    xdt = dt_raw + dtb_ref[...]
    dt = jnp.maximum(xdt, 0.0) + jnp.log1p(jnp.exp(-jnp.abs(xdt)))
    dt = jnp.where((lane < NH) & (rowi < tv), dt, 0.0)
    da = dt * (-jnp.exp(alog_ref[...]))

    r2 = lax.broadcasted_iota(jnp.int32, (L, L), 0)
    c2 = lax.broadcasted_iota(jnp.int32, (L, L), 1)
    causal = r2 >= c2
    a_cum = jnp.dot(causal.astype(F32), da, precision=hi, preferred_element_type=F32)
    a_last = a_cum[L - 1:L, :]

    er = lax.broadcasted_iota(jnp.int32, (LANE, DS), 0)
    ec = lax.broadcasted_iota(jnp.int32, (LANE, DS), 1)
    expand = (ec // HP == er).astype(F32)
    stack = jnp.concatenate([jnp.exp(a_cum), jnp.exp(a_last - a_cum) * dt,
                             jnp.broadcast_to(jnp.exp(a_last), (SUB, LANE))], axis=0)
    stack_x = jnp.dot(stack, expand, precision=hi, preferred_element_type=F32)
    ea_x = stack_x[0:L]
    wend_x = stack_x[L:2 * L]
    cd_x = stack_x[2 * L:2 * L + 1]

    ir = lax.broadcasted_iota(jnp.int32, (LANE, LANE), 0)
    ic = lax.broadcasted_iota(jnp.int32, (LANE, LANE), 1)
    ident = (ir == ic).astype(F32)
    tr = lax.dot_general(ident, jnp.concatenate([dt, a_cum], axis=0), (((1,), (1,)), ((), ())),
                         precision=hi, preferred_element_type=F32)
    dt_t = tr[:, 0:L]
    acum_t = tr[:, L:2 * L]

    lane_x = lax.broadcasted_iota(jnp.int32, (L, LANE), 1)
    hpg = NH // NG
    bms = [xc[:, DS + g * NS:DS + (g + 1) * NS] for g in range(NG)]
    cms = [xc[:, DS + NG * NS + g * NS:DS + NG * NS + (g + 1) * NS] for g in range(NG)]
    gss = [slice(g * (DS // NG), (g + 1) * (DS // NG)) for g in range(NG)]
    cbms = [lax.dot_general(cms[g], bms[g], (((1,), (1,)), ((), ())), preferred_element_type=F32)
            for g in range(NG)]
    st_old = [st_ref[:, gss[g]] for g in range(NG)]
    y_off = [jnp.dot(cms[g], st_old[g], preferred_element_type=F32) * ea_x[:, gss[g]] for g in range(NG)]
    st_new = [jnp.dot(bms[g].T, xs[:, gss[g]] * wend_x[:, gss[g]], preferred_element_type=F32) for g in range(NG)]
    for g in range(NG):
        st_ref[:, gss[g]] = st_old[g] * cd_x[:, gss[g]] + st_new[g]
    segs = [a_cum[:, h:h + 1] - acum_t[h:h + 1, :] for h in range(NH)]
    decs = [jnp.exp(jnp.where(causal, segs[h], -jnp.inf)) for h in range(NH)]
    ws = [cbms[h // hpg] * decs[h] * dt_t[h:h + 1, :] for h in range(NH)]
    y_diag = []
    for pr in range(NH // 2):
        xp = xs[:, 2 * pr * HP:(2 * pr + 2) * HP]
        rhs = jnp.concatenate([jnp.where(lane_x < HP, xp, 0.0), jnp.where(lane_x >= HP, xp, 0.0)], axis=0)
        y_diag.append(jnp.dot(jnp.concatenate([ws[2 * pr], ws[2 * pr + 1]], axis=1), rhs,
                              preferred_element_type=F32))
    y = jnp.concatenate(y_diag, axis=1) + jnp.concatenate(y_off, axis=1) + dsk_ref[...] * xs
    y = y * _silu(z)
    outs = []
    for g in range(NG):
        gs = slice(g * (DS // NG), (g + 1) * (DS // NG))
        outs.append(_rms(y[:, gs], ng_ref[:, gs]))
    return jnp.concatenate(outs, axis=1)


def _state_out(st_ref, h_ref):
    for j in range(DS // LANE):
        blk = st_ref[:, j * LANE:(j + 1) * LANE].T
        for q in range(LANE // HP):
            h_ref[j * (LANE // HP) + q] = blk[q * HP:(q + 1) * HP, :]


def _ssd_prompt_kernel(xbc_ref, z_ref, dt_ref, cw_ref, cb_ref, dtb_ref, alog_ref, dsk_ref, ng_ref,
                       y_ref, nb_ref, h_ref, xe_ref, st_ref, *, L):
    c = pl.program_id(1)

    @pl.when(c == 0)
    def _():
        xe_ref[:, 0:HALO, :] = jnp.zeros((DX // LANE, HALO, LANE), F32)
        st_ref[...] = jnp.zeros_like(st_ref)

    _to_slabs(xe_ref, HALO, xbc_ref[...])
    y_ref[...] = _ssd_chunk(xe_ref, z_ref[...], dt_ref[...], st_ref, cw_ref, cb_ref, dtb_ref, alog_ref, dsk_ref,
                            ng_ref, L=L, tv=L)
    xe_ref[:, 0:HALO, :] = xe_ref[:, L:L + HALO, :]

    @pl.when(c == pl.num_programs(1) - 1)
    def _():
        nb_ref[...] = _from_slabs(xe_ref, HALO - (KS - 1), KS - 1)
        _state_out(st_ref, h_ref)


def _ssd_params_specs():
    return [_resident((KS, DX)), _resident((1, DX)), _resident((1, LANE)), _resident((1, LANE)),
            _resident((1, DS)), _resident((1, DS))]


def _ssd_prompt(xbc, z, dt, params, *, nb, seq):
    nt = nb * seq
    L = CHUNK
    nper = seq // L
    row = lambda n: pl.BlockSpec((L, n), lambda s, c: (s * nper + c, 0))
    return pl.pallas_call(
        functools.partial(_ssd_prompt_kernel, L=L),
        grid=(nb, nper),
        in_specs=[row(DX), row(DS), row(LANE)] + _ssd_params_specs(),
        out_specs=[row(DS),
                   pl.BlockSpec((None, KS - 1, DX), lambda s, c: (s, 0, 0)),
                   pl.BlockSpec((None, NH, HP, NS), lambda s, c: (s, 0, 0, 0))],
        out_shape=[jax.ShapeDtypeStruct((nt, DS), F32), jax.ShapeDtypeStruct((nb, KS - 1, DX), F32),
                   jax.ShapeDtypeStruct((nb, NH, HP, NS), F32)],
        scratch_shapes=[pltpu.VMEM((DX // LANE, HALO + L, LANE), F32), pltpu.VMEM((NS, DS), F32)],
        compiler_params=_cparams(("parallel", "arbitrary")),
        name="ssd_prompt",
    )(xbc, z, dt, *params)


def _ssd_sample_kernel(xbc_ref, z_ref, dt_ref, hist_ref, h0_ref, cw_ref, cb_ref, dtb_ref, alog_ref, dsk_ref,
                       ng_ref, y_ref, nb_ref, h_ref, xe_ref, zb_ref, dtp_ref, st_ref, *, L, tv, nseq):
    for j in range(nseq):
        rows = slice(j * tv, (j + 1) * tv)
        xe_ref[...] = jnp.zeros_like(xe_ref)
        _to_slabs(xe_ref, HALO - (KS - 1), hist_ref[j])
        _to_slabs(xe_ref, HALO, xbc_ref[rows, :])
        zb_ref[...] = jnp.zeros_like(zb_ref)
        zb_ref[0:tv, :] = z_ref[rows, :]
        dtp_ref[...] = jnp.zeros_like(dtp_ref)
        dtp_ref[0:tv, :] = dt_ref[rows, :]
        for q in range(NH * HP // LANE):
            blk = jnp.concatenate([h0_ref[j, q * (LANE // HP) + r] for r in range(LANE // HP)], axis=0)
            st_ref[:, q * LANE:(q + 1) * LANE] = blk.T
        y = _ssd_chunk(xe_ref, zb_ref[...], dtp_ref[...], st_ref, cw_ref, cb_ref, dtb_ref, alog_ref, dsk_ref,
                       ng_ref, L=L, tv=tv)
        y_ref[rows, :] = y[0:tv, :]
        nb_ref[j] = _from_slabs(xe_ref, HALO + tv - (KS - 1), KS - 1)
        _state_out(st_ref, h_ref.at[j])


def _ssd_sample(xbc, z, dt, hist, h0, params, *, row0, tv, nseq=2, L=CHUNK):
    nb = hist.shape[0]
    rows = nseq * tv
    blk0 = row0 // rows
    row = lambda n: pl.BlockSpec((rows, n), lambda i: (blk0 + i, 0))
    return pl.pallas_call(
        functools.partial(_ssd_sample_kernel, L=L, tv=tv, nseq=nseq),
        grid=(nb // nseq,),
        in_specs=[row(DX), row(DS), row(LANE),
                  pl.BlockSpec((nseq, KS - 1, DX), lambda i: (i, 0, 0)),
                  pl.BlockSpec((nseq, NH, HP, NS), lambda i: (i, 0, 0, 0))] + _ssd_params_specs(),
        out_specs=[pl.BlockSpec((rows, DS), lambda i: (i, 0)),
                   pl.BlockSpec((nseq, KS - 1, DX), lambda i: (i, 0, 0)),
                   pl.BlockSpec((nseq, NH, HP, NS), lambda i: (i, 0, 0, 0))],
        out_shape=[jax.ShapeDtypeStruct((nb * tv, DS), F32), jax.ShapeDtypeStruct((nb, KS - 1, DX), F32),
                   jax.ShapeDtypeStruct((nb, NH, HP, NS), F32)],
        scratch_shapes=[pltpu.VMEM((DX // LANE, HALO + L, LANE), F32), pltpu.VMEM((L, DS), F32), pltpu.VMEM((L, LANE), F32),
                        pltpu.VMEM((NS, DS), F32)],
        compiler_params=_cparams(("arbitrary",)),
        name="ssd_sample",
    )(xbc, z, dt, hist, h0, *params)


def _proj_out_kernel(*refs, n, n_main):
    lhs = refs[0:2 * n]
    ws = refs[2 * n:3 * n]
    x_ref, g_ref, o_ref = refs[3 * n:3 * n + 3]
    is_main = pl.program_id(0) < n_main
    m = None
    for k, w_ref in enumerate(ws):
        a = jnp.where(is_main, lhs[2 * k][...], lhs[2 * k + 1][...]).astype(BF16)
        p = jnp.dot(a, w_ref[...], preferred_element_type=F32)
        m = p if m is None else m + p
    o_ref[...] = x_ref[...] + _rms(m, g_ref[...])


def _proj_out(lhs_pairs, ws, x, g, *, tm=512):
    nt = x.shape[0]
    n = len(lhs_pairs)
    n_main = nt // tm - 1
    lhs_specs = []
    for a_main, a_extra in lhs_pairs:
        assert a_main.shape[0] == n_main * tm and a_extra.shape[0] == tm
        lhs_specs.append(pl.BlockSpec((tm, a_main.shape[1]), lambda i: (jnp.minimum(i, n_main - 1), 0)))
        lhs_specs.append(pl.BlockSpec((tm, a_extra.shape[1]), lambda i: (0, 0)))
    return pl.pallas_call(
        functools.partial(_proj_out_kernel, n=n, n_main=n_main),
        grid=(nt // tm,),
        in_specs=lhs_specs + [_resident(w.shape) for w in ws]
                 + [pl.BlockSpec((tm, D), lambda i: (i, 0)), _resident((1, D))],
        out_specs=pl.BlockSpec((tm, D), lambda i: (i, 0)),
        out_shape=jax.ShapeDtypeStruct((nt, D), F32),
        compiler_params=_cparams(("arbitrary",)),
        name="proj_out",
    )(*[a for pair in lhs_pairs for a in pair], *ws, x, g)


def _norm_proj_kernel(*refs, n, heads):
    x_ref, g_ref = refs[0:2]
    ws = refs[2:2 + n]
    outs = refs[2 + n:2 + 2 * n]
    hn = _rms(x_ref[...], g_ref[...]).astype(BF16)
    for w_ref, o_ref in zip(ws, outs):
        for h in range(XH):
            sl = slice(h * XD, (h + 1) * XD)
            r = jnp.dot(hn, w_ref[:, sl], preferred_element_type=F32).astype(o_ref.dtype)
            if heads:
                o_ref[:, h, :] = r
            else:
                o_ref[:, sl] = r


def _norm_proj(x, g, ws, out_dtype, *, heads=False, tm=512):
    nt = x.shape[0]
    n = len(ws)
    if heads:
        tm = NM
        out_specs = [pl.BlockSpec((None, NM, XH, XD), lambda i: (i, 0, 0, 0))] * n
        out_shape = [jax.ShapeDtypeStruct((nt // NM, NM, XH, XD), out_dtype)] * n
    else:
        out_specs = [pl.BlockSpec((tm, D), lambda i: (i, 0))] * n
        out_shape = [jax.ShapeDtypeStruct((nt, D), out_dtype)] * n
    return pl.pallas_call(
        functools.partial(_norm_proj_kernel, n=n, heads=heads),
        grid=(nt // tm,),
        in_specs=[pl.BlockSpec((tm, D), lambda i: (i, 0)), _resident((1, D))] + [_resident((D, D))] * n,
        out_specs=out_specs,
        out_shape=out_shape,
        compiler_params=_cparams(("parallel",)),
        name="norm_proj",
    )(x, g, *ws)


NLT = XD // LANE
LT_STRIDE = NLT * XH
KV_ROWS = NM * LT_STRIDE


def _kv_tiles(x):
    nb = x.shape[0]
    return x.reshape(nb, NM, XH, NLT, LANE).transpose(0, 1, 3, 2, 4).reshape(nb, KV_ROWS, LANE)


def _attn_kernel(q_ref, k_ref, v_ref, o_ref, *, nseq, tv):
    tq = q_ref.shape[0]
    scale = XD ** -0.5
    nt_dims = (((1,), (1,)), ((), ()))
    rowi = lax.broadcasted_iota(jnp.int32, (tq, LANE), 0)
    for h in range(XH):
        acc = [None] * NLT
        for j in range(nseq):
            s = None
            for lt in range(NLT):
                kt = k_ref[j, pl.ds(lt * XH + h, NM, stride=LT_STRIDE), :]
                qs = q_ref[:, h * XD + lt * LANE:h * XD + (lt + 1) * LANE].astype(F32)
                part = lax.dot_general(qs, kt, nt_dims, preferred_element_type=F32)
                s = part if s is None else s + part
            s = s * scale
            e = jnp.exp(s - jnp.max(s, axis=-1, keepdims=True))
            p = e / jnp.sum(e, axis=-1, keepdims=True)
            mine = (rowi >= j * tv) & (rowi < (j + 1) * tv)
            for lt in range(NLT):
                vt = v_ref[j, pl.ds(lt * XH + h, NM, stride=LT_STRIDE), :]
                o = jnp.dot(p, vt, preferred_element_type=F32)
                acc[lt] = o if nseq == 1 else jnp.where(mine, o, 0.0 if acc[lt] is None else acc[lt])
        for lt in range(NLT):
            o_ref[:, h * XD + lt * LANE:h * XD + (lt + 1) * LANE] = acc[lt].astype(o_ref.dtype)


def _group_sum(x, col_lt):
    n = x.shape[1]
    a = x + jnp.where(col_lt % 2 == 0, pltpu.roll(x, n - 1, axis=1), pltpu.roll(x, 1, axis=1))
    return a + jnp.where(col_lt < 2, pltpu.roll(a, n - 2, axis=1), pltpu.roll(a, 2, axis=1))


def _attn_sample_kernel(q_ref, k_ref, v_ref, o_ref, *, nseq, tv):
    assert NLT == 4
    tq = q_ref.shape[0]
    ncol = NM * NLT
    scale = XD ** -0.5
    nt_dims = (((1,), (1,)), ((), ()))
    col_lt = lax.broadcasted_iota(jnp.int32, (tq, ncol), 1) % NLT
    rowi = lax.broadcasted_iota(jnp.int32, (tq, LANE), 0)
    for h in range(XH):
        qp = jnp.concatenate([q_ref[:, h * XD + lt * LANE:h * XD + (lt + 1) * LANE].astype(F32)
                              for lt in range(NLT)], axis=0)
        seqs = range(nseq)
        g = [lax.dot_general(qp, k_ref[j, pl.ds(h, ncol, stride=XH), :], nt_dims, preferred_element_type=F32)
             for j in seqs]
        s4 = [sum(jnp.where(col_lt == lt, g[j][lt * tq:(lt + 1) * tq], 0.0) for lt in range(NLT)) for j in seqs]
        s = [_group_sum(s4[j], col_lt) * scale for j in seqs]
        e = [jnp.exp(s[j] - jnp.max(s[j], axis=-1, keepdims=True)) for j in seqs]
        p = [e[j] / (jnp.sum(e[j], axis=-1, keepdims=True) * (1.0 / NLT)) for j in seqs]
        o = [jnp.dot(jnp.concatenate([jnp.where(col_lt == lt, p[j], 0.0) for lt in range(NLT)], axis=0),
                     v_ref[j, pl.ds(h, ncol, stride=XH), :], preferred_element_type=F32) for j in seqs]
        acc = [None] * NLT
        for j in seqs:
            mine = (rowi >= j * tv) & (rowi < (j + 1) * tv)
            for lt in range(NLT):
                acc[lt] = jnp.where(mine, o[j][lt * tq:(lt + 1) * tq], 0.0 if acc[lt] is None else acc[lt])
        for lt in range(NLT):
            o_ref[:, h * XD + lt * LANE:h * XD + (lt + 1) * LANE] = acc[lt].astype(o_ref.dtype)


def _attn_prompt(q, k, v, *, nb, seq, tq=512):
    nt = nb * seq
    nper = seq // tq
    kv_spec = pl.BlockSpec((1, KV_ROWS, LANE), lambda s, t: (s, 0, 0))
    return pl.pallas_call(
        functools.partial(_attn_kernel, nseq=1, tv=tq),
        grid=(nb, nper),
        in_specs=[pl.BlockSpec((tq, D), lambda s, t: (s * nper + t, 0)), kv_spec, kv_spec],
        out_specs=pl.BlockSpec((tq, D), lambda s, t: (s * nper + t, 0)),
        out_shape=jax.ShapeDtypeStruct((nt, D), BF16),
        compiler_params=_cparams(("parallel", "arbitrary")),
        name="attn_prompt",
    )(q, _kv_tiles(k), _kv_tiles(v))


def _attn_sample(q, k, v, *, row0, tv, nseq=4):
    nb = k.shape[0]
    rows = nseq * tv
    blk0 = row0 // rows
    kv_spec = pl.BlockSpec((nseq, KV_ROWS, LANE), lambda i: (i, 0, 0))
    return pl.pallas_call(
        functools.partial(_attn_sample_kernel, nseq=nseq, tv=tv),
        grid=(nb // nseq,),
        in_specs=[pl.BlockSpec((rows, D), lambda i: (blk0 + i, 0)), kv_spec, kv_spec],
        out_specs=pl.BlockSpec((rows, D), lambda i: (i, 0)),
        out_shape=jax.ShapeDtypeStruct((nb * tv, D), BF16),
        compiler_params=_cparams(("arbitrary",)),
        name="attn_sample",
    )(q, _kv_tiles(k), _kv_tiles(v))


def _row(v):
    return v.reshape(1, -1).astype(F32)


def _pad_lanes(v, n):
    return jnp.pad(v.reshape(1, -1).astype(F32), ((0, 0), (0, n - v.size)))


def _layer(xs, mem, cache_k, cache_v, st_conv, st_sconv, st_ssm, p, *, nbp, seq, nbs, tv, split_out):
    npr = nbp * seq
    bf = lambda w: w.astype(BF16)

    def ffn(xs, pre, wg, wu, wd, post, split_out):
        return _ffn(xs, _row(pre), bf(wg), bf(wu), bf(wd), _row(post), split_out=split_out)

    x = ffn(xs, p['ffn1_pre_g'], p['ffn1_w_gate'], p['ffn1_w_up'], p['ffn1_w_down'], p['ffn1_post_g'], False)

    w_in = bf(p['w_in'])
    wt = jnp.pad(w_in[:, 2 * DC + DS + DX:], ((0, 0), (0, LANE - NH)))
    u, z, xbc, dtr = _mix_in(x, _row(p['mix_pre_g']), w_in, wt)

    cpar = (p['conv_w'].astype(F32), _row(p['conv_b']), _row(p['conv_ln_g']), _row(p['conv_ln_b']))
    a_p, conv_p = _conv_prompt(u, *cpar, nb=nbp, seq=seq)
    a_s, conv_s = _conv_sample(u, jnp.transpose(st_conv, (1, 0, 2)), *cpar, row0=npr, tv=tv)
    conv_s = jnp.transpose(conv_s, (1, 0, 2))

    spar = (p['ssm_conv_w'].astype(F32), _row(p['ssm_conv_b']), _pad_lanes(p['dt_bias'], LANE),
            _pad_lanes(p['a_log'], LANE), jnp.repeat(p['d_skip'].astype(F32), HP).reshape(1, DS),
            _row(p['ssm_norm_g']))
    y_p, sconv_p, ssm_p = _ssd_prompt(xbc, z, dtr, spar, nb=nbp, seq=seq)
    y_s, sconv_s, ssm_s = _ssd_sample(xbc, z, dtr, st_sconv, st_ssm, spar, row0=npr, tv=tv, L=SUB)

    w_out = bf(p['w_out'])
    x = _proj_out([(a_p, a_s), (y_p, y_s)], [w_out[:DC], w_out[DC:]], x, _row(p['mix_post_g']))

    (q,) = _norm_proj(x, _row(p['xattn_pre_g']), [bf(p['w_xq'])], BF16)
    mk, mv = _norm_proj(mem, _row(p['mem_norm_g']), [bf(p['w_xk']), bf(p['w_xv'])], F32, heads=True)
    o_p = _attn_prompt(q, mk, mv, nb=nbp, seq=seq)
    o_s = _attn_sample(q, cache_k, cache_v, row0=npr, tv=tv)
    x = _proj_out([(o_p, o_s)], [bf(p['w_xo'])], x, _row(p['xattn_post_g']))

    x = ffn([x], p['ffn2_pre_g'], p['ffn2_w_gate'], p['ffn2_w_up'], p['ffn2_w_down'], p['ffn2_post_g'], split_out)
    return x, (mk, mv, conv_p, sconv_p, ssm_p, conv_s, sconv_s, ssm_s)


def kernel(x_prompt, x_sample, mem_prompt, cache_mem_k, cache_mem_v, state_conv, state_ssm_conv, state_ssm, ffn1_pre_g, ffn1_w_gate, ffn1_w_up, ffn1_w_down, ffn1_post_g, mix_pre_g, w_in, conv_w, conv_b, conv_ln_g, conv_ln_b, ssm_conv_w, ssm_conv_b, dt_bias, a_log, d_skip, ssm_norm_g, w_out, mix_post_g, xattn_pre_g, mem_norm_g, w_xq, w_xk, w_xv, w_xo, xattn_post_g, ffn2_pre_g, ffn2_w_gate, ffn2_w_up, ffn2_w_down, ffn2_post_g):
    params = dict(ffn1_pre_g=ffn1_pre_g, ffn1_w_gate=ffn1_w_gate, ffn1_w_up=ffn1_w_up, ffn1_w_down=ffn1_w_down,
                  ffn1_post_g=ffn1_post_g, mix_pre_g=mix_pre_g, w_in=w_in, conv_w=conv_w, conv_b=conv_b,
                  conv_ln_g=conv_ln_g, conv_ln_b=conv_ln_b, ssm_conv_w=ssm_conv_w, ssm_conv_b=ssm_conv_b,
                  dt_bias=dt_bias, a_log=a_log, d_skip=d_skip, ssm_norm_g=ssm_norm_g, w_out=w_out,
                  mix_post_g=mix_post_g, xattn_pre_g=xattn_pre_g, mem_norm_g=mem_norm_g, w_xq=w_xq,
                  w_xk=w_xk, w_xv=w_xv, w_xo=w_xo, xattn_post_g=xattn_post_g, ffn2_pre_g=ffn2_pre_g,
                  ffn2_w_gate=ffn2_w_gate, ffn2_w_up=ffn2_w_up, ffn2_w_down=ffn2_w_down, ffn2_post_g=ffn2_post_g)
    depth = ffn1_pre_g.shape[0]
    nbp, seq, _ = x_prompt.shape
    nbs, tv, _ = x_sample.shape
    npr = nbp * seq
    xs = [x_prompt.reshape(npr, D), x_sample.reshape(nbs * tv, D)]
    mem = mem_prompt.reshape(nbp * NM, D)
    per_layer = []
    for layer in range(depth):
        p = {name: w[layer] for name, w in params.items()}
        x, states = _layer(xs, mem, cache_mem_k[layer], cache_mem_v[layer],
                           state_conv[layer], state_ssm_conv[layer], state_ssm[layer], p,
                           nbp=nbp, seq=seq, nbs=nbs, tv=tv, split_out=layer == depth - 1)
        xs = [x]
        per_layer.append(states)
    mk, mv, conv_p, sconv_p, ssm_p, conv_s, sconv_s, ssm_s = [jnp.stack(t) for t in zip(*per_layer)]
    yp, ys = x
    return (yp.reshape(nbp, seq, D), ys.reshape(nbs, tv, D), mk, mv, conv_p, sconv_p, ssm_p, conv_s, sconv_s, ssm_s)
```

```python
import functools

import jax
import jax.numpy as jnp
from jax import lax
from jax.experimental import pallas as pl
from jax.experimental.pallas import tpu as pltpu

F32 = jnp.float32
BF16 = jnp.bfloat16

D = 2048
FF = 5504
DC = 1024
DS = 1024
KC = 31
NH = 16
HP = 64
NG = 2
NS = 128
KS = 4
DX = DS + 2 * NG * NS
CHUNK = 128
NM = 256
XH = 4
XD = D // XH
EPS = 1e-6

LANE = 128
SUB = 8
HIST = 32
HALO = 8

FF_TILE = 1024
FF_TAIL = FF - (FF // FF_TILE) * FF_TILE

VMEM_LIMIT = 56 * 1024 * 1024


def _cparams(sem):
    return pltpu.CompilerParams(dimension_semantics=sem, vmem_limit_bytes=VMEM_LIMIT)


def _rms(x, g):
    return x * lax.rsqrt(jnp.mean(x * x, axis=-1, keepdims=True) + EPS) * g


def _silu(x):
    return x * jax.nn.sigmoid(x)


def _resident(shape):
    return pl.BlockSpec(shape, lambda *_: (0,) * len(shape), pipeline_mode=pl.Buffered(1))


def _ffn_kernel(*refs, n_in, n_out, n_main):
    x_refs = refs[:n_in]
    pg_ref, wg_ref, wu_ref, wd_ref, qg_ref = refs[n_in:n_in + 5]
    o_refs = refs[n_in + 5:n_in + 5 + n_out]
    xn_ref, acc_ref = refs[n_in + 5 + n_out:]
    i = pl.program_id(0)
    f = pl.program_id(1)
    last = pl.num_programs(1) - 1

    def x_tile():
        if n_in == 1:
            return x_refs[0][...]
        return jnp.where(i < n_main, x_refs[0][...], x_refs[1][...])

    @pl.when(f == 0)
    def _():
        xn_ref[...] = _rms(x_tile(), pg_ref[...]).astype(BF16)
        acc_ref[...] = jnp.zeros_like(acc_ref)

    def hidden_tile(width):
        xn = xn_ref[...]
        h = jnp.dot(xn, wg_ref[:, 0:width], preferred_element_type=F32)
        u = jnp.dot(xn, wu_ref[:, 0:width], preferred_element_type=F32)
        a = (_silu(h) * u).astype(BF16)
        acc_ref[...] += jnp.dot(a, wd_ref[0:width, :], preferred_element_type=F32)

    @pl.when(f < last)
    def _():
        hidden_tile(FF_TILE)

    @pl.when(f == last)
    def _():
        hidden_tile(FF_TAIL)
        res = x_tile() + 0.5 * _rms(acc_ref[...], qg_ref[...])
        if n_out == 1:
            o_refs[0][...] = res
        else:
            @pl.when(i < n_main)
            def _():
                o_refs[0][...] = res

            @pl.when(i >= n_main)
            def _():
                o_refs[1][...] = res


def _ffn(xs, pre_g, wg, wu, wd, post_g, *, split_out, tm=512):
    n_in = len(xs)
    nt = sum(x.shape[0] for x in xs)
    n_main = (nt - tm) // tm
    main = lambda i, f: (jnp.minimum(i, n_main - 1), 0)
    extra = lambda i, f: (0, 0)
    whole = lambda i, f: (i, 0)
    if n_in == 1:
        x_specs = [pl.BlockSpec((tm, D), whole)]
    else:
        x_specs = [pl.BlockSpec((tm, D), main), pl.BlockSpec((tm, D), extra)]
    if split_out:
        out_specs = [pl.BlockSpec((tm, D), main), pl.BlockSpec((tm, D), extra)]
        out_shape = [jax.ShapeDtypeStruct((n_main * tm, D), F32), jax.ShapeDtypeStruct((tm, D), F32)]
    else:
        out_specs = pl.BlockSpec((tm, D), whole)
        out_shape = jax.ShapeDtypeStruct((nt, D), F32)
    return pl.pallas_call(
        functools.partial(_ffn_kernel, n_in=n_in, n_out=2 if split_out else 1, n_main=n_main),
        grid=(nt // tm, pl.cdiv(FF, FF_TILE)),
        in_specs=x_specs + [
            pl.BlockSpec((1, D), lambda i, f: (0, 0)),
            pl.BlockSpec((D, FF_TILE), lambda i, f: (0, f)),
            pl.BlockSpec((D, FF_TILE), lambda i, f: (0, f)),
            pl.BlockSpec((FF_TILE, D), lambda i, f: (f, 0)),
            pl.BlockSpec((1, D), lambda i, f: (0, 0)),
        ],
        out_specs=out_specs,
        out_shape=out_shape,
        scratch_shapes=[pltpu.VMEM((tm, D), BF16), pltpu.VMEM((tm, D), F32)],
        compiler_params=_cparams(("arbitrary", "arbitrary")),
        name="ffn",
    )(*xs, pre_g, wg, wu, wd, post_g)


def _mix_in_kernel(x_ref, g_ref, w_ref, wt_ref, u_ref, z_ref, xbc_ref, dt_ref, *, tn):
    hn = _rms(x_ref[...], g_ref[...]).astype(BF16)
    nt_dims = (((1,), (1,)), ((), ()))

    def cols(start, c):
        return lax.dot_general(hn, w_ref[start + c * tn:start + (c + 1) * tn, :], nt_dims,
                               preferred_element_type=F32)

    for c in range(DC // tn):
        u_ref[:, c * tn:(c + 1) * tn] = cols(0, c) * jax.nn.sigmoid(cols(DC, c))
    for c in range(DS // tn):
        z_ref[:, c * tn:(c + 1) * tn] = cols(2 * DC, c)
    for c in range(DX // tn):
        xbc_ref[:, c * tn:(c + 1) * tn] = cols(2 * DC + DS, c)
    dt_ref[...] = lax.dot_general(hn, wt_ref[...], nt_dims, preferred_element_type=F32)


def _mix_in(x, g, w, wt, *, tm=512, tn=512):
    nt = x.shape[0]
    row = lambda n: pl.BlockSpec((tm, n), lambda i: (i, 0))
    return pl.pallas_call(
        functools.partial(_mix_in_kernel, tn=tn),
        grid=(nt // tm,),
        in_specs=[row(D), _resident((1, D)), _resident(w.shape), _resident((LANE, D))],
        out_specs=[row(DC), row(DS), row(DX), row(LANE)],
        out_shape=[jax.ShapeDtypeStruct((nt, n), F32) for n in (DC, DS, DX, LANE)],
        compiler_params=_cparams(("parallel",)),
        name="mix_in",
    )(x, g, w, wt)


def _conv_prompt_kernel(u_ref, w_ref, b_ref, lg_ref, lb_ref, a_ref, nb_ref, xe_ref, y_ref, *, tt):
    t = pl.program_id(1)
    off = HIST - (KC - 1)

    @pl.when(t == 0)
    def _():
        xe_ref[:, 0:HIST, :] = jnp.zeros((DC // LANE, HIST, LANE), F32)

    for j in range(DC // LANE):
        xe_ref[j, HIST:HIST + tt, :] = u_ref[:, j * LANE:(j + 1) * LANE]
    for j in range(DC // LANE):
        sl = slice(j * LANE, (j + 1) * LANE)
        acc = jnp.broadcast_to(b_ref[:, sl], (tt, LANE))
        for k in range(KC):
            acc = acc + w_ref[k:k + 1, sl] * xe_ref[j, off + k:off + k + tt, :]
        y_ref[:, sl] = acc
    for j in range(DC // LANE):
        nb_ref[:, j * LANE:(j + 1) * LANE] = xe_ref[j, HIST + tt - (KC - 1):HIST + tt, :]
        xe_ref[j, 0:HIST, :] = xe_ref[j, tt:tt + HIST, :]
    @pl.when(t >= 0)
    def _():
        y = y_ref[...]
        yc = y - jnp.mean(y, axis=-1, keepdims=True)
        a_ref[...] = _silu(yc * lax.rsqrt(jnp.mean(yc * yc, axis=-1, keepdims=True) + EPS) * lg_ref[...]
                           + lb_ref[...])


def _conv_prompt(u, w, b, lg, lb, *, nb, seq, tt=256):
    nt = nb * seq
    nper = seq // tt
    par = lambda r: _resident((r, DC))
    return pl.pallas_call(
        functools.partial(_conv_prompt_kernel, tt=tt),
        grid=(nb, nper),
        in_specs=[pl.BlockSpec((tt, DC), lambda s, t: (s * nper + t, 0)), par(KC), par(1), par(1), par(1)],
        out_specs=[pl.BlockSpec((tt, DC), lambda s, t: (s * nper + t, 0)),
                   pl.BlockSpec((None, KC - 1, DC), lambda s, t: (s, 0, 0))],
        out_shape=[jax.ShapeDtypeStruct((nt, DC), F32), jax.ShapeDtypeStruct((nb, KC - 1, DC), F32)],
        scratch_shapes=[pltpu.VMEM((DC // LANE, HIST + tt, LANE), F32), pltpu.VMEM((tt, DC), F32)],
        compiler_params=_cparams(("parallel", "arbitrary")),
        name="conv_prompt",
    )(u, w, b, lg, lb)


def _conv_sample_kernel(u_ref, hist_ref, w_ref, b_ref, lg_ref, lb_ref, a_ref, nh_ref, us_ref, y_ref, as_ref,
                        *, sb, tv):
    nl = DC // LANE
    for j in range(nl):
        us_ref[j] = u_ref[:, j * LANE:(j + 1) * LANE]
    for j in range(nl):
        sl = slice(j * LANE, (j + 1) * LANE)
        accs = [jnp.broadcast_to(b_ref[:, sl], (sb, LANE)) for _ in range(tv)]
        for m in range(KC - 1 + tv):
            if m < KC - 1:
                xm = hist_ref[m, :, sl]
            else:
                xm = us_ref[j, pl.ds(m - (KC - 1), sb, stride=tv), :]
            for t in range(tv):
                if 0 <= m - t < KC:
                    accs[t] = accs[t] + w_ref[m - t:m - t + 1, sl] * xm
            if m >= tv:
                nh_ref[m - tv, :, sl] = xm
        for t in range(tv):
            y_ref[t, :, sl] = accs[t]
    for t in range(tv):
        y = y_ref[t]
        yc = y - jnp.mean(y, axis=-1, keepdims=True)
        a = _silu(yc * lax.rsqrt(jnp.mean(yc * yc, axis=-1, keepdims=True) + EPS) * lg_ref[...] + lb_ref[...])
        for j in range(nl):
            as_ref[j, pl.ds(t, sb, stride=tv), :] = a[:, j * LANE:(j + 1) * LANE]
    for j in range(nl):
        a_ref[:, j * LANE:(j + 1) * LANE] = as_ref[j]


def _conv_sample(u, hist, w, b, lg, lb, *, row0, tv, sb=32):
    nb = hist.shape[1]
    rows = sb * tv
    par = lambda r: _resident((r, DC))
    blk0 = row0 // rows
    hist_spec = pl.BlockSpec((KC - 1, sb, DC), lambda i: (0, i, 0))
    return pl.pallas_call(
        functools.partial(_conv_sample_kernel, sb=sb, tv=tv),
        grid=(nb // sb,),
        in_specs=[pl.BlockSpec((rows, DC), lambda i: (blk0 + i, 0)), hist_spec, par(KC), par(1), par(1), par(1)],
        out_specs=[pl.BlockSpec((rows, DC), lambda i: (i, 0)), hist_spec],
        out_shape=[jax.ShapeDtypeStruct((nb * tv, DC), F32), jax.ShapeDtypeStruct((KC - 1, nb, DC), F32)],
        scratch_shapes=[pltpu.VMEM((DC // LANE, rows, LANE), F32), pltpu.VMEM((tv, sb, DC), F32),
                        pltpu.VMEM((DC // LANE, rows, LANE), F32)],
        compiler_params=_cparams(("arbitrary",)),
        name="conv_sample",
    )(u, hist, w, b, lg, lb)


def _to_slabs(xe_ref, r0, x):
    for j in range(x.shape[1] // LANE):
        xe_ref[j, r0:r0 + x.shape[0], :] = x[:, j * LANE:(j + 1) * LANE]


def _from_slabs(xe_ref, r0, rows):
    return jnp.concatenate([xe_ref[j, r0:r0 + rows, :] for j in range(xe_ref.shape[0])], axis=1)


def _ssd_chunk(xe_ref, z, dt_raw, st_ref, cw_ref, cb_ref, dtb_ref, alog_ref, dsk_ref, ng_ref, *, L, tv, lq=None,
               h_io=None):
    hi = lax.Precision.HIGHEST
    lq = L if lq is None else lq
    nsq = L // lq
    off = HALO - (KS - 1)
    cols = []
    for j in range(DX // LANE):
        sl = slice(j * LANE, (j + 1) * LANE)
        acc = jnp.broadcast_to(cb_ref[:, sl], (L, LANE))
        for k in range(KS):
            acc = acc + cw_ref[k:k + 1, sl] * xe_ref[j, off + k:off + k + L, :]
        cols.append(acc)
    xc = _silu(jnp.concatenate(cols, axis=1))
    xs = xc[:, 0:DS]

    lane = lax.broadcasted_iota(jnp.int32, (L, LANE), 1)
    rowi = lax.broadcasted_iota(jnp.int32, (L, LANE), 0)
    xdt = dt_raw + dtb_ref[...]
    dt = jnp.maximum(xdt, 0.0) + jnp.log1p(jnp.exp(-jnp.abs(xdt)))
    dt = jnp.where((lane < NH) & (rowi % lq < tv), dt, 0.0)
    da = dt * (-jnp.exp(alog_ref[...]))

    r2 = lax.broadcasted_iota(jnp.int32, (L, L), 0)
    c2 = lax.broadcasted_iota(jnp.int32, (L, L), 1)
    same = r2 // lq == c2 // lq
    causal = (r2 >= c2) & same
    if nsq == 1:
        a_cum = jnp.dot(causal.astype(F32), da, precision=hi, preferred_element_type=F32)
        a_tot = jnp.broadcast_to(a_cum[L - 1:L, :], (L, LANE))
    else:
        cums = jnp.dot(jnp.concatenate([causal.astype(F32), same.astype(F32)], axis=0), da, precision=hi,
                       preferred_element_type=F32)
        a_cum = cums[0:L]
        a_tot = cums[L:2 * L]

    er = lax.broadcasted_iota(jnp.int32, (LANE, DS), 0)
    ec = lax.broadcasted_iota(jnp.int32, (LANE, DS), 1)
    expand = (ec // HP == er).astype(F32)
    stack = jnp.concatenate([jnp.exp(a_cum), jnp.exp(a_tot - a_cum) * dt, jnp.exp(a_tot[0:SUB])], axis=0)
    stack_x = jnp.dot(stack, expand, precision=hi, preferred_element_type=F32)
    ea_x = stack_x[0:L]
    wend_x = stack_x[L:2 * L]
    cd_x = stack_x[2 * L:2 * L + 1]

    ir = lax.broadcasted_iota(jnp.int32, (LANE, LANE), 0)
    ic = lax.broadcasted_iota(jnp.int32, (LANE, LANE), 1)
    ident = (ir == ic).astype(F32)
    tr = lax.dot_general(ident, jnp.concatenate([dt, a_cum], axis=0), (((1,), (1,)), ((), ())),
                         precision=hi, preferred_element_type=F32)
    dt_t = tr[:, 0:L]
    acum_t = tr[:, L:2 * L]

    lane_x = lax.broadcasted_iota(jnp.int32, (L, LANE), 1)
    hpg = NH // NG
    bms = [xc[:, DS + g * NS:DS + (g + 1) * NS] for g in range(NG)]
    cms = [xc[:, DS + NG * NS + g * NS:DS + NG * NS + (g + 1) * NS] for g in range(NG)]
    gss = [slice(g * (DS // NG), (g + 1) * (DS // NG)) for g in range(NG)]
    cbms = [lax.dot_general(cms[g], bms[g], (((1,), (1,)), ((), ())), preferred_element_type=F32)
            for g in range(NG)]
    if h_io is None:
        st_old = [st_ref[:, gss[g]] for g in range(NG)]
        y_off = [jnp.dot(cms[g], st_old[g], preferred_element_type=F32) * ea_x[:, gss[g]] for g in range(NG)]
        st_new = [jnp.dot(bms[g].T, xs[:, gss[g]] * wend_x[:, gss[g]], preferred_element_type=F32)
                  for g in range(NG)]
        for g in range(NG):
            st_ref[:, gss[g]] = st_old[g] * cd_x[:, gss[g]] + st_new[g]
    else:
        rows_g = hpg * HP
        sq = [slice(q * lq, (q + 1) * lq) for q in range(nsq)]
        hs = [[h_io[q][0][g * hpg:(g + 1) * hpg].reshape(rows_g, NS) for g in range(NG)] for q in range(nsq)]
        y_off = [jnp.concatenate([lax.dot_general(cms[g][sq[q]], hs[q][g], (((1,), (1,)), ((), ())),
                                                  preferred_element_type=F32) for q in range(nsq)], axis=0)
                 * ea_x[:, gss[g]] for g in range(NG)]
        xw = [xs[:, gss[g]] * wend_x[:, gss[g]] for g in range(NG)]
        upd = [[lax.dot_general(xw[g][sq[q]], bms[g][sq[q]], (((0,), (0,)), ((), ())),
                                preferred_element_type=F32) for g in range(NG)] for q in range(nsq)]
        seq_decay = jnp.exp(a_tot)
        for q in range(nsq):
            for h in range(NH):
                g, hl = divmod(h, hpg)
                rows = slice(hl * HP, (hl + 1) * HP)
                h_io[q][1][h] = hs[q][g][rows, :] * seq_decay[q * lq:q * lq + 1, h:h + 1] + upd[q][g][rows, :]
    segs = [a_cum[:, h:h + 1] - acum_t[h:h + 1, :] for h in range(NH)]
    decs = [jnp.exp(jnp.where(causal, segs[h], -jnp.inf)) for h in range(NH)]
    ws = [cbms[h // hpg] * decs[h] * dt_t[h:h + 1, :] for h in range(NH)]
    y_diag = []
    for pr in range(NH // 2):
        xp = xs[:, 2 * pr * HP:(2 * pr + 2) * HP]
        rhs = jnp.concatenate([jnp.where(lane_x < HP, xp, 0.0), jnp.where(lane_x >= HP, xp, 0.0)], axis=0)
        y_diag.append(jnp.dot(jnp.concatenate([ws[2 * pr], ws[2 * pr + 1]], axis=1), rhs,
                              preferred_element_type=F32))
    y = jnp.concatenate(y_diag, axis=1) + jnp.concatenate(y_off, axis=1) + dsk_ref[...] * xs
    y = y * _silu(z)
    outs = []
    for g in range(NG):
        gs = slice(g * (DS // NG), (g + 1) * (DS // NG))
        outs.append(_rms(y[:, gs], ng_ref[:, gs]))
    return jnp.concatenate(outs, axis=1)


def _state_out(st_ref, h_ref):
    for j in range(DS // LANE):
        blk = st_ref[:, j * LANE:(j + 1) * LANE].T
        for q in range(LANE // HP):
            h_ref[j * (LANE // HP) + q] = blk[q * HP:(q + 1) * HP, :]


def _ssd_prompt_kernel(xbc_ref, z_ref, dt_ref, cw_ref, cb_ref, dtb_ref, alog_ref, dsk_ref, ng_ref,
                       y_ref, nb_ref, h_ref, xe_ref, st_ref, *, L):
    c = pl.program_id(1)

    @pl.when(c == 0)
    def _():
        xe_ref[:, 0:HALO, :] = jnp.zeros((DX // LANE, HALO, LANE), F32)
        st_ref[...] = jnp.zeros_like(st_ref)

    _to_slabs(xe_ref, HALO, xbc_ref[...])
    y_ref[...] = _ssd_chunk(xe_ref, z_ref[...], dt_ref[...], st_ref, cw_ref, cb_ref, dtb_ref, alog_ref, dsk_ref,
                            ng_ref, L=L, tv=L)
    xe_ref[:, 0:HALO, :] = xe_ref[:, L:L + HALO, :]

    @pl.when(c == pl.num_programs(1) - 1)
    def _():
        nb_ref[...] = _from_slabs(xe_ref, HALO - (KS - 1), KS - 1)
        _state_out(st_ref, h_ref)


def _ssd_params_specs():
    return [_resident((KS, DX)), _resident((1, DX)), _resident((1, LANE)), _resident((1, LANE)),
            _resident((1, DS)), _resident((1, DS))]


def _ssd_prompt(xbc, z, dt, params, *, nb, seq):
    nt = nb * seq
    L = CHUNK
    nper = seq // L
    row = lambda n: pl.BlockSpec((L, n), lambda s, c: (s * nper + c, 0))
    return pl.pallas_call(
        functools.partial(_ssd_prompt_kernel, L=L),
        grid=(nb, nper),
        in_specs=[row(DX), row(DS), row(LANE)] + _ssd_params_specs(),
        out_specs=[row(DS),
                   pl.BlockSpec((None, KS - 1, DX), lambda s, c: (s, 0, 0)),
                   pl.BlockSpec((None, NH, HP, NS), lambda s, c: (s, 0, 0, 0))],
        out_shape=[jax.ShapeDtypeStruct((nt, DS), F32), jax.ShapeDtypeStruct((nb, KS - 1, DX), F32),
                   jax.ShapeDtypeStruct((nb, NH, HP, NS), F32)],
        scratch_shapes=[pltpu.VMEM((DX // LANE, HALO + L, LANE), F32), pltpu.VMEM((NS, DS), F32)],
        compiler_params=_cparams(("parallel", "arbitrary")),
        name="ssd_prompt",
    )(xbc, z, dt, *params)


def _ssd_sample_kernel(xbc_ref, z_ref, dt_ref, hist_ref, h0_ref, cw_ref, cb_ref, dtb_ref, alog_ref, dsk_ref,
                       ng_ref, y_ref, nb_ref, h_ref, xe_ref, zb_ref, dtp_ref, *, lq, tv, nseq):
    assert lq - tv >= KS - 1 and HALO >= KS - 1
    xe_ref[...] = jnp.zeros_like(xe_ref)
    zb_ref[...] = jnp.zeros_like(zb_ref)
    dtp_ref[...] = jnp.zeros_like(dtp_ref)
    for j in range(nseq):
        rows = slice(j * tv, (j + 1) * tv)
        _to_slabs(xe_ref, HALO + j * lq - (KS - 1), hist_ref[j])
        _to_slabs(xe_ref, HALO + j * lq, xbc_ref[rows, :])
        zb_ref[j * lq:j * lq + tv, :] = z_ref[rows, :]
        dtp_ref[j * lq:j * lq + tv, :] = dt_ref[rows, :]
    y = _ssd_chunk(xe_ref, zb_ref[...], dtp_ref[...], None, cw_ref, cb_ref, dtb_ref, alog_ref, dsk_ref, ng_ref,
                   L=nseq * lq, tv=tv, lq=lq, h_io=[(h0_ref.at[j], h_ref.at[j]) for j in range(nseq)])
    for j in range(nseq):
        y_ref[j * tv:(j + 1) * tv, :] = y[j * lq:j * lq + tv, :]
        nb_ref[j] = _from_slabs(xe_ref, HALO + j * lq + tv - (KS - 1), KS - 1)


def _ssd_sample(xbc, z, dt, hist, h0, params, *, row0, tv, nseq=8, lq=SUB):
    nb = hist.shape[0]
    rows = nseq * tv
    L = nseq * lq
    blk0 = row0 // rows
    row = lambda n: pl.BlockSpec((rows, n), lambda i: (blk0 + i, 0))
    return pl.pallas_call(
        functools.partial(_ssd_sample_kernel, lq=lq, tv=tv, nseq=nseq),
        grid=(nb // nseq,),
        in_specs=[row(DX), row(DS), row(LANE),
                  pl.BlockSpec((nseq, KS - 1, DX), lambda i: (i, 0, 0)),
                  pl.BlockSpec((nseq, NH, HP, NS), lambda i: (i, 0, 0, 0))] + _ssd_params_specs(),
        out_specs=[pl.BlockSpec((rows, DS), lambda i: (i, 0)),
                   pl.BlockSpec((nseq, KS - 1, DX), lambda i: (i, 0, 0)),
                   pl.BlockSpec((nseq, NH, HP, NS), lambda i: (i, 0, 0, 0))],
        out_shape=[jax.ShapeDtypeStruct((nb * tv, DS), F32), jax.ShapeDtypeStruct((nb, KS - 1, DX), F32),
                   jax.ShapeDtypeStruct((nb, NH, HP, NS), F32)],
        scratch_shapes=[pltpu.VMEM((DX // LANE, HALO + L, LANE), F32), pltpu.VMEM((L, DS), F32),
                        pltpu.VMEM((L, LANE), F32)],
        compiler_params=_cparams(("arbitrary",)),
        name="ssd_sample",
    )(xbc, z, dt, hist, h0, *params)


def _proj_out_kernel(*refs, n, n_main):
    lhs = refs[0:2 * n]
    ws = refs[2 * n:3 * n]
    x_ref, g_ref, o_ref = refs[3 * n:3 * n + 3]
    is_main = pl.program_id(0) < n_main
    m = None
    for k, w_ref in enumerate(ws):
        a = jnp.where(is_main, lhs[2 * k][...], lhs[2 * k + 1][...]).astype(BF16)
        p = jnp.dot(a, w_ref[...], preferred_element_type=F32)
        m = p if m is None else m + p
    o_ref[...] = x_ref[...] + _rms(m, g_ref[...])


def _proj_out(lhs_pairs, ws, x, g, *, tm=512):
    nt = x.shape[0]
    n = len(lhs_pairs)
    n_main = nt // tm - 1
    lhs_specs = []
    for a_main, a_extra in lhs_pairs:
        assert a_main.shape[0] == n_main * tm and a_extra.shape[0] == tm
        lhs_specs.append(pl.BlockSpec((tm, a_main.shape[1]), lambda i: (jnp.minimum(i, n_main - 1), 0)))
        lhs_specs.append(pl.BlockSpec((tm, a_extra.shape[1]), lambda i: (0, 0)))
    return pl.pallas_call(
        functools.partial(_proj_out_kernel, n=n, n_main=n_main),
        grid=(nt // tm,),
        in_specs=lhs_specs + [_resident(w.shape) for w in ws]
                 + [pl.BlockSpec((tm, D), lambda i: (i, 0)), _resident((1, D))],
        out_specs=pl.BlockSpec((tm, D), lambda i: (i, 0)),
        out_shape=jax.ShapeDtypeStruct((nt, D), F32),
        compiler_params=_cparams(("arbitrary",)),
        name="proj_out",
    )(*[a for pair in lhs_pairs for a in pair], *ws, x, g)


def _norm_proj_kernel(*refs, n, heads):
    x_ref, g_ref = refs[0:2]
    ws = refs[2:2 + n]
    outs = refs[2 + n:2 + 2 * n]
    hn = _rms(x_ref[...], g_ref[...]).astype(BF16)
    for w_ref, o_ref in zip(ws, outs):
        for h in range(XH):
            sl = slice(h * XD, (h + 1) * XD)
            r = jnp.dot(hn, w_ref[:, sl], preferred_element_type=F32).astype(o_ref.dtype)
            if heads:
                o_ref[:, h, :] = r
            else:
                o_ref[:, sl] = r


def _norm_proj(x, g, ws, out_dtype, *, heads=False, tm=512):
    nt = x.shape[0]
    n = len(ws)
    if heads:
        tm = NM
        out_specs = [pl.BlockSpec((None, NM, XH, XD), lambda i: (i, 0, 0, 0))] * n
        out_shape = [jax.ShapeDtypeStruct((nt // NM, NM, XH, XD), out_dtype)] * n
    else:
        out_specs = [pl.BlockSpec((tm, D), lambda i: (i, 0))] * n
        out_shape = [jax.ShapeDtypeStruct((nt, D), out_dtype)] * n
    return pl.pallas_call(
        functools.partial(_norm_proj_kernel, n=n, heads=heads),
        grid=(nt // tm,),
        in_specs=[pl.BlockSpec((tm, D), lambda i: (i, 0)), _resident((1, D))] + [_resident((D, D))] * n,
        out_specs=out_specs,
        out_shape=out_shape,
        compiler_params=_cparams(("parallel",)),
        name="norm_proj",
    )(x, g, *ws)


NLT = XD // LANE
LT_STRIDE = NLT * XH
KV_ROWS = NM * LT_STRIDE


def _kv_tiles(x):
    nb = x.shape[0]
    return x.reshape(nb, NM, XH, NLT, LANE).transpose(0, 1, 3, 2, 4).reshape(nb, KV_ROWS, LANE)


def _attn_prompt_kernel(q_ref, k_ref, v_ref, o_ref, kh_ref, vh_ref):
    @pl.when(pl.program_id(1) == 0)
    def _():
        for h in range(XH):
            for lt in range(NLT):
                rows = pl.ds(lt * XH + h, NM, stride=LT_STRIDE)
                kh_ref[h, :, lt * LANE:(lt + 1) * LANE] = k_ref[0, rows, :].astype(BF16)
                vh_ref[h, :, lt * LANE:(lt + 1) * LANE] = v_ref[0, rows, :].astype(BF16)

    scale = XD ** -0.5
    nt_dims = (((1,), (1,)), ((), ()))
    heads = range(XH)
    s = [lax.dot_general(q_ref[:, h * XD:(h + 1) * XD], kh_ref[h], nt_dims, preferred_element_type=F32) * scale
         for h in heads]
    e = [jnp.exp(s[h] - jnp.max(s[h], axis=-1, keepdims=True)) for h in heads]
    p = [(e[h] / jnp.sum(e[h], axis=-1, keepdims=True)).astype(BF16) for h in heads]
    for h in heads:
        o_ref[:, h * XD:(h + 1) * XD] = jnp.dot(p[h], vh_ref[h], preferred_element_type=F32).astype(o_ref.dtype)


def _group_sum(x, col_lt):
    n = x.shape[1]
    a = x + jnp.where(col_lt % 2 == 0, pltpu.roll(x, n - 1, axis=1), pltpu.roll(x, 1, axis=1))
    return a + jnp.where(col_lt < 2, pltpu.roll(a, n - 2, axis=1), pltpu.roll(a, 2, axis=1))


def _attn_sample_kernel(q_ref, k_ref, v_ref, o_ref, *, nseq, tv):
    assert NLT == 4
    tq = q_ref.shape[0]
    ncol = NM * NLT
    scale = XD ** -0.5
    nt_dims = (((1,), (1,)), ((), ()))
    col_lt = lax.broadcasted_iota(jnp.int32, (tq, ncol), 1) % NLT
    rowi = lax.broadcasted_iota(jnp.int32, (tq, LANE), 0)
    for h in range(XH):
        qp = jnp.concatenate([q_ref[:, h * XD + lt * LANE:h * XD + (lt + 1) * LANE].astype(F32)
                              for lt in range(NLT)], axis=0)
        seqs = range(nseq)
        g = [lax.dot_general(qp, k_ref[j, pl.ds(h, ncol, stride=XH), :], nt_dims, preferred_element_type=F32)
             for j in seqs]
        s4 = [sum(jnp.where(col_lt == lt, g[j][lt * tq:(lt + 1) * tq], 0.0) for lt in range(NLT)) for j in seqs]
        s = [_group_sum(s4[j], col_lt) * scale for j in seqs]
        e = [jnp.exp(s[j] - jnp.max(s[j], axis=-1, keepdims=True)) for j in seqs]
        p = [e[j] / (jnp.sum(e[j], axis=-1, keepdims=True) * (1.0 / NLT)) for j in seqs]
        o = [jnp.dot(jnp.concatenate([jnp.where(col_lt == lt, p[j], 0.0) for lt in range(NLT)], axis=0),
                     v_ref[j, pl.ds(h, ncol, stride=XH), :], preferred_element_type=F32) for j in seqs]
        acc = [None] * NLT
        for j in seqs:
            mine = (rowi >= j * tv) & (rowi < (j + 1) * tv)
            for lt in range(NLT):
                acc[lt] = jnp.where(mine, o[j][lt * tq:(lt + 1) * tq], 0.0 if acc[lt] is None else acc[lt])
        for lt in range(NLT):
            o_ref[:, h * XD + lt * LANE:h * XD + (lt + 1) * LANE] = acc[lt].astype(o_ref.dtype)


def _attn_prompt(q, k, v, *, nb, seq, tq=512):
    nt = nb * seq
    nper = seq // tq
    kv_spec = pl.BlockSpec((1, KV_ROWS, LANE), lambda s, t: (s, 0, 0))
    return pl.pallas_call(
        _attn_prompt_kernel,
        grid=(nb, nper),
        in_specs=[pl.BlockSpec((tq, D), lambda s, t: (s * nper + t, 0)), kv_spec, kv_spec],
        out_specs=pl.BlockSpec((tq, D), lambda s, t: (s * nper + t, 0)),
        out_shape=jax.ShapeDtypeStruct((nt, D), BF16),
        scratch_shapes=[pltpu.VMEM((XH, NM, XD), BF16), pltpu.VMEM((XH, NM, XD), BF16)],
        compiler_params=_cparams(("arbitrary", "arbitrary")),
        name="attn_prompt",
    )(q, _kv_tiles(k), _kv_tiles(v))


def _attn_sample(q, k, v, *, row0, tv, nseq=4):
    nb = k.shape[0]
    rows = nseq * tv
    blk0 = row0 // rows
    kv_spec = pl.BlockSpec((nseq, KV_ROWS, LANE), lambda i: (i, 0, 0))
    return pl.pallas_call(
        functools.partial(_attn_sample_kernel, nseq=nseq, tv=tv),
        grid=(nb // nseq,),
        in_specs=[pl.BlockSpec((rows, D), lambda i: (blk0 + i, 0)), kv_spec, kv_spec],
        out_specs=pl.BlockSpec((rows, D), lambda i: (i, 0)),
        out_shape=jax.ShapeDtypeStruct((nb * tv, D), BF16),
        compiler_params=_cparams(("arbitrary",)),
        name="attn_sample",
    )(q, _kv_tiles(k), _kv_tiles(v))


def _row(v):
    return v.reshape(1, -1).astype(F32)


def _pad_lanes(v, n):
    return jnp.pad(v.reshape(1, -1).astype(F32), ((0, 0), (0, n - v.size)))


def _layer(xs, mem, cache_k, cache_v, st_conv, st_sconv, st_ssm, p, *, nbp, seq, nbs, tv, split_out):
    npr = nbp * seq
    bf = lambda w: w.astype(BF16)

    def ffn(xs, pre, wg, wu, wd, post, split_out):
        return _ffn(xs, _row(pre), bf(wg), bf(wu), bf(wd), _row(post), split_out=split_out)

    x = ffn(xs, p['ffn1_pre_g'], p['ffn1_w_gate'], p['ffn1_w_up'], p['ffn1_w_down'], p['ffn1_post_g'], False)

    w_in = bf(p['w_in'].T)
    wt = jnp.pad(w_in[2 * DC + DS + DX:], ((0, LANE - NH), (0, 0)))
    u, z, xbc, dtr = _mix_in(x, _row(p['mix_pre_g']), w_in, wt)

    cpar = (p['conv_w'].astype(F32), _row(p['conv_b']), _row(p['conv_ln_g']), _row(p['conv_ln_b']))
    a_p, conv_p = _conv_prompt(u, *cpar, nb=nbp, seq=seq)
    a_s, conv_s = _conv_sample(u, jnp.transpose(st_conv, (1, 0, 2)), *cpar, row0=npr, tv=tv)
    conv_s = jnp.transpose(conv_s, (1, 0, 2))

    spar = (p['ssm_conv_w'].astype(F32), _row(p['ssm_conv_b']), _pad_lanes(p['dt_bias'], LANE),
            _pad_lanes(p['a_log'], LANE), jnp.repeat(p['d_skip'].astype(F32), HP).reshape(1, DS),
            _row(p['ssm_norm_g']))
    y_p, sconv_p, ssm_p = _ssd_prompt(xbc, z, dtr, spar, nb=nbp, seq=seq)
    y_s, sconv_s, ssm_s = _ssd_sample(xbc, z, dtr, st_sconv, st_ssm, spar, row0=npr, tv=tv)

    w_out = bf(p['w_out'])
    x = _proj_out([(a_p, a_s), (y_p, y_s)], [w_out[:DC], w_out[DC:]], x, _row(p['mix_post_g']))

    (q,) = _norm_proj(x, _row(p['xattn_pre_g']), [bf(p['w_xq'])], BF16)
    mk, mv = _norm_proj(mem, _row(p['mem_norm_g']), [bf(p['w_xk']), bf(p['w_xv'])], F32, heads=True)
    o_p = _attn_prompt(q, mk, mv, nb=nbp, seq=seq)
    o_s = _attn_sample(q, cache_k, cache_v, row0=npr, tv=tv)
    x = _proj_out([(o_p, o_s)], [bf(p['w_xo'])], x, _row(p['xattn_post_g']))

    x = ffn([x], p['ffn2_pre_g'], p['ffn2_w_gate'], p['ffn2_w_up'], p['ffn2_w_down'], p['ffn2_post_g'], split_out)
    return x, (mk, mv, conv_p, sconv_p, ssm_p, conv_s, sconv_s, ssm_s)


def kernel(x_prompt, x_sample, mem_prompt, cache_mem_k, cache_mem_v, state_conv, state_ssm_conv, state_ssm, ffn1_pre_g, ffn1_w_gate, ffn1_w_up, ffn1_w_down, ffn1_post_g, mix_pre_g, w_in, conv_w, conv_b, conv_ln_g, conv_ln_b, ssm_conv_w, ssm_conv_b, dt_bias, a_log, d_skip, ssm_norm_g, w_out, mix_post_g, xattn_pre_g, mem_norm_g, w_xq, w_xk, w_xv, w_xo, xattn_post_g, ffn2_pre_g, ffn2_w_gate, ffn2_w_up, ffn2_w_down, ffn2_post_g):
    params = dict(ffn1_pre_g=ffn1_pre_g, ffn1_w_gate=ffn1_w_gate, ffn1_w_up=ffn1_w_up, ffn1_w_down=ffn1_w_down,
                  ffn1_post_g=ffn1_post_g, mix_pre_g=mix_pre_g, w_in=w_in, conv_w=conv_w, conv_b=conv_b,
                  conv_ln_g=conv_ln_g, conv_ln_b=conv_ln_b, ssm_conv_w=ssm_conv_w, ssm_conv_b=ssm_conv_b,
                  dt_bias=dt_bias, a_log=a_log, d_skip=d_skip, ssm_norm_g=ssm_norm_g, w_out=w_out,
                  mix_post_g=mix_post_g, xattn_pre_g=xattn_pre_g, mem_norm_g=mem_norm_g, w_xq=w_xq,
                  w_xk=w_xk, w_xv=w_xv, w_xo=w_xo, xattn_post_g=xattn_post_g, ffn2_pre_g=ffn2_pre_g,
                  ffn2_w_gate=ffn2_w_gate, ffn2_w_up=ffn2_w_up, ffn2_w_down=ffn2_w_down, ffn2_post_g=ffn2_post_g)
    depth = ffn1_pre_g.shape[0]
    nbp, seq, _ = x_prompt.shape
    nbs, tv, _ = x_sample.shape
    npr = nbp * seq
    xs = [x_prompt.reshape(npr, D), x_sample.reshape(nbs * tv, D)]
    mem = mem_prompt.reshape(nbp * NM, D)
    per_layer = []
    for layer in range(depth):
        p = {name: w[layer] for name, w in params.items()}
        x, states = _layer(xs, mem, cache_mem_k[layer], cache_mem_v[layer],
                           state_conv[layer], state_ssm_conv[layer], state_ssm[layer], p,
                           nbp=nbp, seq=seq, nbs=nbs, tv=tv, split_out=layer == depth - 1)
        xs = [x]
        per_layer.append(states)
    mk, mv, conv_p, sconv_p, ssm_p, conv_s, sconv_s, ssm_s = [jnp.stack(t) for t in zip(*per_layer)]
    yp, ys = x
    return (yp.reshape(nbp, seq, D), ys.reshape(nbs, tv, D), mk, mv, conv_p, sconv_p, ssm_p, conv_s, sconv_s, ssm_s)
```

```python
import functools

import jax
import jax.numpy as jnp
from jax import lax
from jax.experimental import pallas as pl
from jax.experimental.pallas import tpu as pltpu

F32 = jnp.float32
BF16 = jnp.bfloat16

D = 2048
FF = 5504
DC = 1024
DS = 1024
KC = 31
NH = 16
HP = 64
NG = 2
NS = 128
KS = 4
DX = DS + 2 * NG * NS
CHUNK = 128
NM = 256
XH = 4
XD = D // XH
EPS = 1e-6

LANE = 128
SUB = 8
HIST = 32
HALO = 8

FF_TILE = 1024
FF_TAIL = FF - (FF // FF_TILE) * FF_TILE

VMEM_LIMIT = 56 * 1024 * 1024


def _cparams(sem):
    return pltpu.CompilerParams(dimension_semantics=sem, vmem_limit_bytes=VMEM_LIMIT)


def _rms(x, g):
    return x * lax.rsqrt(jnp.mean(x * x, axis=-1, keepdims=True) + EPS) * g


def _silu(x):
    return x * jax.nn.sigmoid(x)


def _resident(shape):
    return pl.BlockSpec(shape, lambda *_: (0,) * len(shape), pipeline_mode=pl.Buffered(1))


def _ffn_kernel(*refs, n_in, n_out, n_main):
    x_refs = refs[:n_in]
    pg_ref, wg_ref, wu_ref, wd_ref, qg_ref = refs[n_in:n_in + 5]
    o_refs = refs[n_in + 5:n_in + 5 + n_out]
    xn_ref, acc_ref = refs[n_in + 5 + n_out:]
    i = pl.program_id(0)
    f = pl.program_id(1)
    last = pl.num_programs(1) - 1

    def x_tile():
        if n_in == 1:
            return x_refs[0][...]
        return jnp.where(i < n_main, x_refs[0][...], x_refs[1][...])

    @pl.when(f == 0)
    def _():
        xn_ref[...] = _rms(x_tile(), pg_ref[...]).astype(BF16)
        acc_ref[...] = jnp.zeros_like(acc_ref)

    def hidden_tile(width):
        xn = xn_ref[...]
        h = jnp.dot(xn, wg_ref[:, 0:width], preferred_element_type=F32)
        u = jnp.dot(xn, wu_ref[:, 0:width], preferred_element_type=F32)
        a = (_silu(h) * u).astype(BF16)
        acc_ref[...] += jnp.dot(a, wd_ref[0:width, :], preferred_element_type=F32)

    @pl.when(f < last)
    def _():
        hidden_tile(FF_TILE)

    @pl.when(f == last)
    def _():
        hidden_tile(FF_TAIL)
        res = x_tile() + 0.5 * _rms(acc_ref[...], qg_ref[...])
        if n_out == 1:
            o_refs[0][...] = res
        else:
            @pl.when(i < n_main)
            def _():
                o_refs[0][...] = res

            @pl.when(i >= n_main)
            def _():
                o_refs[1][...] = res


def _ffn(xs, pre_g, wg, wu, wd, post_g, *, split_out, tm=512):
    n_in = len(xs)
    nt = sum(x.shape[0] for x in xs)
    n_main = (nt - tm) // tm
    main = lambda i, f: (jnp.minimum(i, n_main - 1), 0)
    extra = lambda i, f: (0, 0)
    whole = lambda i, f: (i, 0)
    if n_in == 1:
        x_specs = [pl.BlockSpec((tm, D), whole)]
    else:
        x_specs = [pl.BlockSpec((tm, D), main), pl.BlockSpec((tm, D), extra)]
    if split_out:
        out_specs = [pl.BlockSpec((tm, D), main), pl.BlockSpec((tm, D), extra)]
        out_shape = [jax.ShapeDtypeStruct((n_main * tm, D), F32), jax.ShapeDtypeStruct((tm, D), F32)]
    else:
        out_specs = pl.BlockSpec((tm, D), whole)
        out_shape = jax.ShapeDtypeStruct((nt, D), F32)
    return pl.pallas_call(
        functools.partial(_ffn_kernel, n_in=n_in, n_out=2 if split_out else 1, n_main=n_main),
        grid=(nt // tm, pl.cdiv(FF, FF_TILE)),
        in_specs=x_specs + [
            pl.BlockSpec((1, D), lambda i, f: (0, 0)),
            pl.BlockSpec((D, FF_TILE), lambda i, f: (0, f)),
            pl.BlockSpec((D, FF_TILE), lambda i, f: (0, f)),
            pl.BlockSpec((FF_TILE, D), lambda i, f: (f, 0)),
            pl.BlockSpec((1, D), lambda i, f: (0, 0)),
        ],
        out_specs=out_specs,
        out_shape=out_shape,
        scratch_shapes=[pltpu.VMEM((tm, D), BF16), pltpu.VMEM((tm, D), F32)],
        compiler_params=_cparams(("arbitrary", "arbitrary")),
        name="ffn",
    )(*xs, pre_g, wg, wu, wd, post_g)


def _mix_in_kernel(x_ref, g_ref, w_ref, wt_ref, u_ref, z_ref, xbc_ref, dt_ref, *, tn):
    hn = _rms(x_ref[...], g_ref[...]).astype(BF16)
    nt_dims = (((1,), (1,)), ((), ()))

    def cols(start, c):
        return lax.dot_general(hn, w_ref[start + c * tn:start + (c + 1) * tn, :], nt_dims,
                               preferred_element_type=F32)

    for c in range(DC // tn):
        u_ref[:, c * tn:(c + 1) * tn] = cols(0, c) * jax.nn.sigmoid(cols(DC, c))
    for c in range(DS // tn):
        z_ref[:, c * tn:(c + 1) * tn] = cols(2 * DC, c)
    for c in range(DX // tn):
        xbc_ref[:, c * tn:(c + 1) * tn] = cols(2 * DC + DS, c)
    dt_ref[...] = lax.dot_general(hn, wt_ref[...], nt_dims, preferred_element_type=F32)


def _mix_in(x, g, w, wt, *, tm=512, tn=512):
    nt = x.shape[0]
    row = lambda n: pl.BlockSpec((tm, n), lambda i: (i, 0))
    return pl.pallas_call(
        functools.partial(_mix_in_kernel, tn=tn),
        grid=(nt // tm,),
        in_specs=[row(D), _resident((1, D)), _resident(w.shape), _resident((LANE, D))],
        out_specs=[row(DC), row(DS), row(DX), row(LANE)],
        out_shape=[jax.ShapeDtypeStruct((nt, n), F32) for n in (DC, DS, DX, LANE)],
        compiler_params=_cparams(("parallel",)),
        name="mix_in",
    )(x, g, w, wt)


def _conv_prompt_kernel(u_ref, w_ref, b_ref, lg_ref, lb_ref, a_ref, nb_ref, xe_ref, y_ref, *, tt):
    t = pl.program_id(1)
    off = HIST - (KC - 1)

    @pl.when(t == 0)
    def _():
        xe_ref[:, 0:HIST, :] = jnp.zeros((DC // LANE, HIST, LANE), F32)

    for j in range(DC // LANE):
        xe_ref[j, HIST:HIST + tt, :] = u_ref[:, j * LANE:(j + 1) * LANE]
    for j in range(DC // LANE):
        sl = slice(j * LANE, (j + 1) * LANE)
        acc = jnp.broadcast_to(b_ref[:, sl], (tt, LANE))
        for k in range(KC):
            acc = acc + w_ref[k:k + 1, sl] * xe_ref[j, off + k:off + k + tt, :]
        y_ref[:, sl] = acc
    for j in range(DC // LANE):
        nb_ref[:, j * LANE:(j + 1) * LANE] = xe_ref[j, HIST + tt - (KC - 1):HIST + tt, :]
        xe_ref[j, 0:HIST, :] = xe_ref[j, tt:tt + HIST, :]
    @pl.when(t >= 0)
    def _():
        y = y_ref[...]
        yc = y - jnp.mean(y, axis=-1, keepdims=True)
        a_ref[...] = _silu(yc * lax.rsqrt(jnp.mean(yc * yc, axis=-1, keepdims=True) + EPS) * lg_ref[...]
                           + lb_ref[...])


def _conv_prompt(u, w, b, lg, lb, *, nb, seq, tt=256):
    nt = nb * seq
    nper = seq // tt
    par = lambda r: _resident((r, DC))
    return pl.pallas_call(
        functools.partial(_conv_prompt_kernel, tt=tt),
        grid=(nb, nper),
        in_specs=[pl.BlockSpec((tt, DC), lambda s, t: (s * nper + t, 0)), par(KC), par(1), par(1), par(1)],
        out_specs=[pl.BlockSpec((tt, DC), lambda s, t: (s * nper + t, 0)),
                   pl.BlockSpec((None, KC - 1, DC), lambda s, t: (s, 0, 0))],
        out_shape=[jax.ShapeDtypeStruct((nt, DC), F32), jax.ShapeDtypeStruct((nb, KC - 1, DC), F32)],
        scratch_shapes=[pltpu.VMEM((DC // LANE, HIST + tt, LANE), F32), pltpu.VMEM((tt, DC), F32)],
        compiler_params=_cparams(("parallel", "arbitrary")),
        name="conv_prompt",
    )(u, w, b, lg, lb)


def _conv_sample_kernel(u_ref, hist_ref, w_ref, b_ref, lg_ref, lb_ref, a_ref, nh_ref, us_ref, y_ref, as_ref,
                        *, sb, tv):
    nl = DC // LANE
    for j in range(nl):
        us_ref[j] = u_ref[:, j * LANE:(j + 1) * LANE]
    for j in range(nl):
        sl = slice(j * LANE, (j + 1) * LANE)
        accs = [jnp.broadcast_to(b_ref[:, sl], (sb, LANE)) for _ in range(tv)]
        for m in range(KC - 1 + tv):
            if m < KC - 1:
                xm = hist_ref[m, :, sl]
            else:
                xm = us_ref[j, pl.ds(m - (KC - 1), sb, stride=tv), :]
            for t in range(tv):
                if 0 <= m - t < KC:
                    accs[t] = accs[t] + w_ref[m - t:m - t + 1, sl] * xm
            if m >= tv:
                nh_ref[m - tv, :, sl] = xm
        for t in range(tv):
            y_ref[t, :, sl] = accs[t]
    for t in range(tv):
        y = y_ref[t]
        yc = y - jnp.mean(y, axis=-1, keepdims=True)
        a = _silu(yc * lax.rsqrt(jnp.mean(yc * yc, axis=-1, keepdims=True) + EPS) * lg_ref[...] + lb_ref[...])
        for j in range(nl):
            as_ref[j, pl.ds(t, sb, stride=tv), :] = a[:, j * LANE:(j + 1) * LANE]
    for j in range(nl):
        a_ref[:, j * LANE:(j + 1) * LANE] = as_ref[j]


def _conv_sample(u, hist, w, b, lg, lb, *, row0, tv, sb=32):
    nb = hist.shape[1]
    rows = sb * tv
    par = lambda r: _resident((r, DC))
    blk0 = row0 // rows
    hist_spec = pl.BlockSpec((KC - 1, sb, DC), lambda i: (0, i, 0))
    return pl.pallas_call(
        functools.partial(_conv_sample_kernel, sb=sb, tv=tv),
        grid=(nb // sb,),
        in_specs=[pl.BlockSpec((rows, DC), lambda i: (blk0 + i, 0)), hist_spec, par(KC), par(1), par(1), par(1)],
        out_specs=[pl.BlockSpec((rows, DC), lambda i: (i, 0)), hist_spec],
        out_shape=[jax.ShapeDtypeStruct((nb * tv, DC), F32), jax.ShapeDtypeStruct((KC - 1, nb, DC), F32)],
        scratch_shapes=[pltpu.VMEM((DC // LANE, rows, LANE), F32), pltpu.VMEM((tv, sb, DC), F32),
                        pltpu.VMEM((DC // LANE, rows, LANE), F32)],
        compiler_params=_cparams(("arbitrary",)),
        name="conv_sample",
    )(u, hist, w, b, lg, lb)


def _dot01(a, b, dims, *, data):
    x = b if data else a
    one = (a if data else b).astype(BF16)
    t0 = x.astype(BF16)
    r1 = x - t0.astype(F32)
    t1 = r1.astype(BF16)
    t2 = (r1 - t1.astype(F32)).astype(BF16)
    acc = None
    for t in (t0, t1, t2):
        lhs, rhs = (one, t) if data else (t, one)
        p = lax.dot_general(lhs, rhs, dims, preferred_element_type=F32)
        acc = p if acc is None else acc + p
    return acc


def _to_slabs(xe_ref, r0, x):
    for j in range(x.shape[1] // LANE):
        xe_ref[j, r0:r0 + x.shape[0], :] = x[:, j * LANE:(j + 1) * LANE]


def _from_slabs(xe_ref, r0, rows):
    return jnp.concatenate([xe_ref[j, r0:r0 + rows, :] for j in range(xe_ref.shape[0])], axis=1)


def _ssd_chunk(xe_ref, z, dt_raw, st_ref, cw_ref, cb_ref, dtb_ref, alog_ref, dsk_ref, ng_ref, *, L, tv, lq=None,
               h_io=None):
    mm_dims = (((1,), (0,)), ((), ()))
    lq = L if lq is None else lq
    nsq = L // lq
    off = HALO - (KS - 1)
    cols = []
    for j in range(DX // LANE):
        sl = slice(j * LANE, (j + 1) * LANE)
        acc = jnp.broadcast_to(cb_ref[:, sl], (L, LANE))
        for k in range(KS):
            acc = acc + cw_ref[k:k + 1, sl] * xe_ref[j, off + k:off + k + L, :]
        cols.append(acc)
    xc = _silu(jnp.concatenate(cols, axis=1))
    xs = xc[:, 0:DS]

    lane = lax.broadcasted_iota(jnp.int32, (L, LANE), 1)
    rowi = lax.broadcasted_iota(jnp.int32, (L, LANE), 0)
    xdt = dt_raw + dtb_ref[...]
    dt = jnp.maximum(xdt, 0.0) + jnp.log1p(jnp.exp(-jnp.abs(xdt)))
    dt = jnp.where((lane < NH) & (rowi % lq < tv), dt, 0.0)
    da = dt * (-jnp.exp(alog_ref[...]))

    r2 = lax.broadcasted_iota(jnp.int32, (L, L), 0)
    c2 = lax.broadcasted_iota(jnp.int32, (L, L), 1)
    same = r2 // lq == c2 // lq
    causal = (r2 >= c2) & same
    if nsq == 1:
        a_cum = _dot01(causal.astype(F32), da, mm_dims, data=1)
        a_tot = jnp.broadcast_to(a_cum[L - 1:L, :], (L, LANE))
    else:
        cums = _dot01(jnp.concatenate([causal.astype(F32), same.astype(F32)], axis=0), da, mm_dims, data=1)
        a_cum = cums[0:L]
        a_tot = cums[L:2 * L]

    er = lax.broadcasted_iota(jnp.int32, (LANE, DS), 0)
    ec = lax.broadcasted_iota(jnp.int32, (LANE, DS), 1)
    expand = (ec // HP == er).astype(F32)
    stack = jnp.concatenate([jnp.exp(a_cum), jnp.exp(a_tot - a_cum) * dt, jnp.exp(a_tot[0:SUB])], axis=0)
    stack_x = _dot01(stack, expand, mm_dims, data=0)
    ea_x = stack_x[0:L]
    wend_x = stack_x[L:2 * L]
    cd_x = stack_x[2 * L:2 * L + 1]

    ir = lax.broadcasted_iota(jnp.int32, (LANE, LANE), 0)
    ic = lax.broadcasted_iota(jnp.int32, (LANE, LANE), 1)
    ident = (ir == ic).astype(F32)
    tr = _dot01(ident, jnp.concatenate([dt, a_cum], axis=0), (((1,), (1,)), ((), ())), data=1)
    dt_t = tr[:, 0:L]
    acum_t = tr[:, L:2 * L]

    lane_x = lax.broadcasted_iota(jnp.int32, (L, LANE), 1)
    hpg = NH // NG
    bms = [xc[:, DS + g * NS:DS + (g + 1) * NS] for g in range(NG)]
    cms = [xc[:, DS + NG * NS + g * NS:DS + NG * NS + (g + 1) * NS] for g in range(NG)]
    gss = [slice(g * (DS // NG), (g + 1) * (DS // NG)) for g in range(NG)]
    cbms = [lax.dot_general(cms[g], bms[g], (((1,), (1,)), ((), ())), preferred_element_type=F32)
            for g in range(NG)]
    if h_io is None:
        st_old = [st_ref[:, gss[g]] for g in range(NG)]
        y_off = [jnp.dot(cms[g], st_old[g], preferred_element_type=F32) * ea_x[:, gss[g]] for g in range(NG)]
        st_new = [jnp.dot(bms[g].T, xs[:, gss[g]] * wend_x[:, gss[g]], preferred_element_type=F32)
                  for g in range(NG)]
        for g in range(NG):
            st_ref[:, gss[g]] = st_old[g] * cd_x[:, gss[g]] + st_new[g]
    else:
        rows_g = hpg * HP
        sq = [slice(q * lq, (q + 1) * lq) for q in range(nsq)]
        hs = [[h_io[q][0][g * hpg:(g + 1) * hpg].reshape(rows_g, NS) for g in range(NG)] for q in range(nsq)]
        y_off = [jnp.concatenate([lax.dot_general(cms[g][sq[q]], hs[q][g], (((1,), (1,)), ((), ())),
                                                  preferred_element_type=F32) for q in range(nsq)], axis=0)
                 * ea_x[:, gss[g]] for g in range(NG)]
        xw = [xs[:, gss[g]] * wend_x[:, gss[g]] for g in range(NG)]
        upd = [[lax.dot_general(xw[g][sq[q]], bms[g][sq[q]], (((0,), (0,)), ((), ())),
                                preferred_element_type=F32) for g in range(NG)] for q in range(nsq)]
        seq_decay = jnp.exp(a_tot)
        for q in range(nsq):
            for h in range(NH):
                g, hl = divmod(h, hpg)
                rows = slice(hl * HP, (hl + 1) * HP)
                h_io[q][1][h] = hs[q][g][rows, :] * seq_decay[q * lq:q * lq + 1, h:h + 1] + upd[q][g][rows, :]
    segs = [a_cum[:, h:h + 1] - acum_t[h:h + 1, :] for h in range(NH)]
    decs = [jnp.exp(jnp.where(causal, segs[h], -jnp.inf)) for h in range(NH)]
    ws = [cbms[h // hpg] * decs[h] * dt_t[h:h + 1, :] for h in range(NH)]
    y_diag = []
    for pr in range(NH // 2):
        xp = xs[:, 2 * pr * HP:(2 * pr + 2) * HP]
        rhs = jnp.concatenate([jnp.where(lane_x < HP, xp, 0.0), jnp.where(lane_x >= HP, xp, 0.0)], axis=0)
        y_diag.append(jnp.dot(jnp.concatenate([ws[2 * pr], ws[2 * pr + 1]], axis=1), rhs,
                              preferred_element_type=F32))
    y = jnp.concatenate(y_diag, axis=1) + jnp.concatenate(y_off, axis=1) + dsk_ref[...] * xs
    y = y * _silu(z)
    outs = []
    for g in range(NG):
        gs = slice(g * (DS // NG), (g + 1) * (DS // NG))
        outs.append(_rms(y[:, gs], ng_ref[:, gs]))
    return jnp.concatenate(outs, axis=1)


def _state_out(st_ref, h_ref):
    for j in range(DS // LANE):
        blk = st_ref[:, j * LANE:(j + 1) * LANE].T
        for q in range(LANE // HP):
            h_ref[j * (LANE // HP) + q] = blk[q * HP:(q + 1) * HP, :]


def _ssd_prompt_kernel(xbc_ref, z_ref, dt_ref, cw_ref, cb_ref, dtb_ref, alog_ref, dsk_ref, ng_ref,
                       y_ref, nb_ref, h_ref, xe_ref, st_ref, *, L):
    c = pl.program_id(1)

    @pl.when(c == 0)
    def _():
        xe_ref[:, 0:HALO, :] = jnp.zeros((DX // LANE, HALO, LANE), F32)
        st_ref[...] = jnp.zeros_like(st_ref)

    _to_slabs(xe_ref, HALO, xbc_ref[...])
    y_ref[...] = _ssd_chunk(xe_ref, z_ref[...], dt_ref[...], st_ref, cw_ref, cb_ref, dtb_ref, alog_ref, dsk_ref,
                            ng_ref, L=L, tv=L)
    xe_ref[:, 0:HALO, :] = xe_ref[:, L:L + HALO, :]

    @pl.when(c == pl.num_programs(1) - 1)
    def _():
        nb_ref[...] = _from_slabs(xe_ref, HALO - (KS - 1), KS - 1)
        _state_out(st_ref, h_ref)


def _ssd_params_specs():
    return [_resident((KS, DX)), _resident((1, DX)), _resident((1, LANE)), _resident((1, LANE)),
            _resident((1, DS)), _resident((1, DS))]


def _ssd_prompt(xbc, z, dt, params, *, nb, seq):
    nt = nb * seq
    L = CHUNK
    nper = seq // L
    row = lambda n: pl.BlockSpec((L, n), lambda s, c: (s * nper + c, 0))
    return pl.pallas_call(
        functools.partial(_ssd_prompt_kernel, L=L),
        grid=(nb, nper),
        in_specs=[row(DX), row(DS), row(LANE)] + _ssd_params_specs(),
        out_specs=[row(DS),
                   pl.BlockSpec((None, KS - 1, DX), lambda s, c: (s, 0, 0)),
                   pl.BlockSpec((None, NH, HP, NS), lambda s, c: (s, 0, 0, 0))],
        out_shape=[jax.ShapeDtypeStruct((nt, DS), F32), jax.ShapeDtypeStruct((nb, KS - 1, DX), F32),
                   jax.ShapeDtypeStruct((nb, NH, HP, NS), F32)],
        scratch_shapes=[pltpu.VMEM((DX // LANE, HALO + L, LANE), F32), pltpu.VMEM((NS, DS), F32)],
        compiler_params=_cparams(("parallel", "arbitrary")),
        name="ssd_prompt",
    )(xbc, z, dt, *params)


def _ssd_sample_kernel(xbc_ref, z_ref, dt_ref, hist_ref, h0_ref, cw_ref, cb_ref, dtb_ref, alog_ref, dsk_ref,
                       ng_ref, y_ref, nb_ref, h_ref, xe_ref, zb_ref, dtp_ref, *, lq, tv, nseq):
    assert lq - tv >= KS - 1 and HALO >= KS - 1
    xe_ref[...] = jnp.zeros_like(xe_ref)
    zb_ref[...] = jnp.zeros_like(zb_ref)
    dtp_ref[...] = jnp.zeros_like(dtp_ref)
    for j in range(nseq):
        rows = slice(j * tv, (j + 1) * tv)
        _to_slabs(xe_ref, HALO + j * lq - (KS - 1), hist_ref[j])
        _to_slabs(xe_ref, HALO + j * lq, xbc_ref[rows, :])
        zb_ref[j * lq:j * lq + tv, :] = z_ref[rows, :]
        dtp_ref[j * lq:j * lq + tv, :] = dt_ref[rows, :]
    y = _ssd_chunk(xe_ref, zb_ref[...], dtp_ref[...], None, cw_ref, cb_ref, dtb_ref, alog_ref, dsk_ref, ng_ref,
                   L=nseq * lq, tv=tv, lq=lq, h_io=[(h0_ref.at[j], h_ref.at[j]) for j in range(nseq)])
    for j in range(nseq):
        y_ref[j * tv:(j + 1) * tv, :] = y[j * lq:j * lq + tv, :]
        nb_ref[j] = _from_slabs(xe_ref, HALO + j * lq + tv - (KS - 1), KS - 1)


def _ssd_sample(xbc, z, dt, hist, h0, params, *, row0, tv, nseq=8, lq=SUB):
    nb = hist.shape[0]
    rows = nseq * tv
    L = nseq * lq
    blk0 = row0 // rows
    row = lambda n: pl.BlockSpec((rows, n), lambda i: (blk0 + i, 0))
    return pl.pallas_call(
        functools.partial(_ssd_sample_kernel, lq=lq, tv=tv, nseq=nseq),
        grid=(nb // nseq,),
        in_specs=[row(DX), row(DS), row(LANE),
                  pl.BlockSpec((nseq, KS - 1, DX), lambda i: (i, 0, 0)),
                  pl.BlockSpec((nseq, NH, HP, NS), lambda i: (i, 0, 0, 0))] + _ssd_params_specs(),
        out_specs=[pl.BlockSpec((rows, DS), lambda i: (i, 0)),
                   pl.BlockSpec((nseq, KS - 1, DX), lambda i: (i, 0, 0)),
                   pl.BlockSpec((nseq, NH, HP, NS), lambda i: (i, 0, 0, 0))],
        out_shape=[jax.ShapeDtypeStruct((nb * tv, DS), F32), jax.ShapeDtypeStruct((nb, KS - 1, DX), F32),
                   jax.ShapeDtypeStruct((nb, NH, HP, NS), F32)],
        scratch_shapes=[pltpu.VMEM((DX // LANE, HALO + L, LANE), F32), pltpu.VMEM((L, DS), F32),
                        pltpu.VMEM((L, LANE), F32)],
        compiler_params=_cparams(("arbitrary",)),
        name="ssd_sample",
    )(xbc, z, dt, hist, h0, *params)


def _proj_out_kernel(*refs, n, n_main):
    lhs = refs[0:2 * n]
    ws = refs[2 * n:3 * n]
    x_ref, g_ref, o_ref = refs[3 * n:3 * n + 3]
    is_main = pl.program_id(0) < n_main
    m = None
    for k, w_ref in enumerate(ws):
        a = jnp.where(is_main, lhs[2 * k][...], lhs[2 * k + 1][...]).astype(BF16)
        p = jnp.dot(a, w_ref[...].astype(BF16), preferred_element_type=F32)
        m = p if m is None else m + p
    o_ref[...] = x_ref[...] + _rms(m, g_ref[...])


def _proj_out(lhs_pairs, ws, x, g, *, tm=512):
    nt = x.shape[0]
    n = len(lhs_pairs)
    n_main = nt // tm - 1
    lhs_specs = []
    for a_main, a_extra in lhs_pairs:
        assert a_main.shape[0] == n_main * tm and a_extra.shape[0] == tm
        lhs_specs.append(pl.BlockSpec((tm, a_main.shape[1]), lambda i: (jnp.minimum(i, n_main - 1), 0)))
        lhs_specs.append(pl.BlockSpec((tm, a_extra.shape[1]), lambda i: (0, 0)))
    return pl.pallas_call(
        functools.partial(_proj_out_kernel, n=n, n_main=n_main),
        grid=(nt // tm,),
        in_specs=lhs_specs + [_resident(w.shape) for w in ws]
                 + [pl.BlockSpec((tm, D), lambda i: (i, 0)), _resident((1, D))],
        out_specs=pl.BlockSpec((tm, D), lambda i: (i, 0)),
        out_shape=jax.ShapeDtypeStruct((nt, D), F32),
        compiler_params=_cparams(("arbitrary",)),
        name="proj_out",
    )(*[a for pair in lhs_pairs for a in pair], *ws, x, g)


def _norm_proj_kernel(*refs, n, heads):
    x_ref, g_ref = refs[0:2]
    ws = refs[2:2 + n]
    outs = refs[2 + n:2 + 2 * n]
    hn = _rms(x_ref[...], g_ref[...]).astype(BF16)
    for w_ref, o_ref in zip(ws, outs):
        for h in range(XH):
            sl = slice(h * XD, (h + 1) * XD)
            r = jnp.dot(hn, w_ref[:, sl].astype(BF16), preferred_element_type=F32).astype(o_ref.dtype)
            if heads:
                o_ref[:, h, :] = r
            else:
                o_ref[:, sl] = r


def _norm_proj(x, g, ws, out_dtype, *, heads=False, tm=512):
    nt = x.shape[0]
    n = len(ws)
    if heads:
        tm = NM
        out_specs = [pl.BlockSpec((None, NM, XH, XD), lambda i: (i, 0, 0, 0))] * n
        out_shape = [jax.ShapeDtypeStruct((nt // NM, NM, XH, XD), out_dtype)] * n
    else:
        out_specs = [pl.BlockSpec((tm, D), lambda i: (i, 0))] * n
        out_shape = [jax.ShapeDtypeStruct((nt, D), out_dtype)] * n
    return pl.pallas_call(
        functools.partial(_norm_proj_kernel, n=n, heads=heads),
        grid=(nt // tm,),
        in_specs=[pl.BlockSpec((tm, D), lambda i: (i, 0)), _resident((1, D))] + [_resident((D, D))] * n,
        out_specs=out_specs,
        out_shape=out_shape,
        compiler_params=_cparams(("parallel",)),
        name="norm_proj",
    )(x, g, *ws)


NLT = XD // LANE
LT_STRIDE = NLT * XH
KV_ROWS = NM * LT_STRIDE


def _kv_tiles(x):
    nb = x.shape[0]
    return x.reshape(nb, NM, XH, NLT, LANE).transpose(0, 1, 3, 2, 4).reshape(nb, KV_ROWS, LANE)


def _attn_prompt_kernel(q_ref, k_ref, v_ref, o_ref, kh_ref, vh_ref):
    @pl.when(pl.program_id(1) == 0)
    def _():
        for h in range(XH):
            for lt in range(NLT):
                rows = pl.ds(lt * XH + h, NM, stride=LT_STRIDE)
                kh_ref[h, :, lt * LANE:(lt + 1) * LANE] = k_ref[0, rows, :].astype(BF16)
                vh_ref[h, :, lt * LANE:(lt + 1) * LANE] = v_ref[0, rows, :].astype(BF16)

    scale = XD ** -0.5
    nt_dims = (((1,), (1,)), ((), ()))
    heads = range(XH)
    s = [lax.dot_general(q_ref[:, h * XD:(h + 1) * XD], kh_ref[h], nt_dims, preferred_element_type=F32) * scale
         for h in heads]
    e = [jnp.exp(s[h] - jnp.max(s[h], axis=-1, keepdims=True)) for h in heads]
    p = [(e[h] / jnp.sum(e[h], axis=-1, keepdims=True)).astype(BF16) for h in heads]
    for h in heads:
        o_ref[:, h * XD:(h + 1) * XD] = jnp.dot(p[h], vh_ref[h], preferred_element_type=F32).astype(o_ref.dtype)


def _group_sum(x, col_lt):
    n = x.shape[1]
    a = x + jnp.where(col_lt % 2 == 0, pltpu.roll(x, n - 1, axis=1), pltpu.roll(x, 1, axis=1))
    return a + jnp.where(col_lt < 2, pltpu.roll(a, n - 2, axis=1), pltpu.roll(a, 2, axis=1))


def _attn_sample_kernel(q_ref, k_ref, v_ref, o_ref, *, nseq, tv):
    assert NLT == 4
    tq = q_ref.shape[0]
    ncol = NM * NLT
    scale = XD ** -0.5
    nt_dims = (((1,), (1,)), ((), ()))
    col_lt = lax.broadcasted_iota(jnp.int32, (tq, ncol), 1) % NLT
    rowi = lax.broadcasted_iota(jnp.int32, (tq, LANE), 0)
    for h in range(XH):
        qp = jnp.concatenate([q_ref[:, h * XD + lt * LANE:h * XD + (lt + 1) * LANE].astype(F32)
                              for lt in range(NLT)], axis=0)
        seqs = range(nseq)
        g = [lax.dot_general(qp, k_ref[j, pl.ds(h, ncol, stride=XH), :], nt_dims, preferred_element_type=F32)
             for j in seqs]
        s4 = [sum(jnp.where(col_lt == lt, g[j][lt * tq:(lt + 1) * tq], 0.0) for lt in range(NLT)) for j in seqs]
        s = [_group_sum(s4[j], col_lt) * scale for j in seqs]
        e = [jnp.exp(s[j] - jnp.max(s[j], axis=-1, keepdims=True)) for j in seqs]
        p = [e[j] / (jnp.sum(e[j], axis=-1, keepdims=True) * (1.0 / NLT)) for j in seqs]
        o = [jnp.dot(jnp.concatenate([jnp.where(col_lt == lt, p[j], 0.0) for lt in range(NLT)], axis=0),
                     v_ref[j, pl.ds(h, ncol, stride=XH), :], preferred_element_type=F32) for j in seqs]
        acc = [None] * NLT
        for j in seqs:
            mine = (rowi >= j * tv) & (rowi < (j + 1) * tv)
            for lt in range(NLT):
                acc[lt] = jnp.where(mine, o[j][lt * tq:(lt + 1) * tq], 0.0 if acc[lt] is None else acc[lt])
        for lt in range(NLT):
            o_ref[:, h * XD + lt * LANE:h * XD + (lt + 1) * LANE] = acc[lt].astype(o_ref.dtype)


def _attn_prompt(q, k, v, *, nb, seq, tq=512):
    nt = nb * seq
    nper = seq // tq
    kv_spec = pl.BlockSpec((1, KV_ROWS, LANE), lambda s, t: (s, 0, 0))
    return pl.pallas_call(
        _attn_prompt_kernel,
        grid=(nb, nper),
        in_specs=[pl.BlockSpec((tq, D), lambda s, t: (s * nper + t, 0)), kv_spec, kv_spec],
        out_specs=pl.BlockSpec((tq, D), lambda s, t: (s * nper + t, 0)),
        out_shape=jax.ShapeDtypeStruct((nt, D), BF16),
        scratch_shapes=[pltpu.VMEM((XH, NM, XD), BF16), pltpu.VMEM((XH, NM, XD), BF16)],
        compiler_params=_cparams(("arbitrary", "arbitrary")),
        name="attn_prompt",
    )(q, _kv_tiles(k), _kv_tiles(v))


def _attn_sample(q, k, v, *, row0, tv, nseq=4):
    nb = k.shape[0]
    rows = nseq * tv
    blk0 = row0 // rows
    kv_spec = pl.BlockSpec((nseq, KV_ROWS, LANE), lambda i: (i, 0, 0))
    return pl.pallas_call(
        functools.partial(_attn_sample_kernel, nseq=nseq, tv=tv),
        grid=(nb // nseq,),
        in_specs=[pl.BlockSpec((rows, D), lambda i: (blk0 + i, 0)), kv_spec, kv_spec],
        out_specs=pl.BlockSpec((rows, D), lambda i: (i, 0)),
        out_shape=jax.ShapeDtypeStruct((nb * tv, D), BF16),
        compiler_params=_cparams(("arbitrary",)),
        name="attn_sample",
    )(q, _kv_tiles(k), _kv_tiles(v))


def _row(v):
    return v.reshape(1, -1).astype(F32)


def _pad_lanes(v, n):
    return jnp.pad(v.reshape(1, -1).astype(F32), ((0, 0), (0, n - v.size)))


def _layer(xs, mem, cache_k, cache_v, st_conv, st_sconv, st_ssm, p, *, nbp, seq, nbs, tv, split_out):
    npr = nbp * seq
    bf = lambda w: w.astype(BF16)

    def ffn(xs, pre, wg, wu, wd, post, split_out):
        return _ffn(xs, _row(pre), bf(wg), bf(wu), bf(wd), _row(post), split_out=split_out)

    x = ffn(xs, p['ffn1_pre_g'], p['ffn1_w_gate'], p['ffn1_w_up'], p['ffn1_w_down'], p['ffn1_post_g'], False)

    w_in = bf(p['w_in'].T)
    wt = jnp.pad(w_in[2 * DC + DS + DX:], ((0, LANE - NH), (0, 0)))
    u, z, xbc, dtr = _mix_in(x, _row(p['mix_pre_g']), w_in, wt)

    cpar = (p['conv_w'].astype(F32), _row(p['conv_b']), _row(p['conv_ln_g']), _row(p['conv_ln_b']))
    a_p, conv_p = _conv_prompt(u, *cpar, nb=nbp, seq=seq)
    a_s, conv_s = _conv_sample(u, jnp.transpose(st_conv, (1, 0, 2)), *cpar, row0=npr, tv=tv)
    conv_s = jnp.transpose(conv_s, (1, 0, 2))

    spar = (p['ssm_conv_w'].astype(F32), _row(p['ssm_conv_b']), _pad_lanes(p['dt_bias'], LANE),
            _pad_lanes(p['a_log'], LANE), jnp.repeat(p['d_skip'].astype(F32), HP).reshape(1, DS),
            _row(p['ssm_norm_g']))
    y_p, sconv_p, ssm_p = _ssd_prompt(xbc, z, dtr, spar, nb=nbp, seq=seq)
    y_s, sconv_s, ssm_s = _ssd_sample(xbc, z, dtr, st_sconv, st_ssm, spar, row0=npr, tv=tv)

    w_out = bf(p['w_out'])
    x = _proj_out([(a_p, a_s), (y_p, y_s)], [w_out[:DC], w_out[DC:]], x, _row(p['mix_post_g']))

    (q,) = _norm_proj(x, _row(p['xattn_pre_g']), [p['w_xq']], BF16)
    mk, mv = _norm_proj(mem, _row(p['mem_norm_g']), [p['w_xk'], p['w_xv']], F32, heads=True)
    o_p = _attn_prompt(q, mk, mv, nb=nbp, seq=seq)
    o_s = _attn_sample(q, cache_k, cache_v, row0=npr, tv=tv)
    x = _proj_out([(o_p, o_s)], [p['w_xo']], x, _row(p['xattn_post_g']))

    x = ffn([x], p['ffn2_pre_g'], p['ffn2_w_gate'], p['ffn2_w_up'], p['ffn2_w_down'], p['ffn2_post_g'], split_out)
    return x, (mk, mv, conv_p, sconv_p, ssm_p, conv_s, sconv_s, ssm_s)


def kernel(x_prompt, x_sample, mem_prompt, cache_mem_k, cache_mem_v, state_conv, state_ssm_conv, state_ssm, ffn1_pre_g, ffn1_w_gate, ffn1_w_up, ffn1_w_down, ffn1_post_g, mix_pre_g, w_in, conv_w, conv_b, conv_ln_g, conv_ln_b, ssm_conv_w, ssm_conv_b, dt_bias, a_log, d_skip, ssm_norm_g, w_out, mix_post_g, xattn_pre_g, mem_norm_g, w_xq, w_xk, w_xv, w_xo, xattn_post_g, ffn2_pre_g, ffn2_w_gate, ffn2_w_up, ffn2_w_down, ffn2_post_g):
    params = dict(ffn1_pre_g=ffn1_pre_g, ffn1_w_gate=ffn1_w_gate, ffn1_w_up=ffn1_w_up, ffn1_w_down=ffn1_w_down,
                  ffn1_post_g=ffn1_post_g, mix_pre_g=mix_pre_g, w_in=w_in, conv_w=conv_w, conv_b=conv_b,
                  conv_ln_g=conv_ln_g, conv_ln_b=conv_ln_b, ssm_conv_w=ssm_conv_w, ssm_conv_b=ssm_conv_b,
                  dt_bias=dt_bias, a_log=a_log, d_skip=d_skip, ssm_norm_g=ssm_norm_g, w_out=w_out,
                  mix_post_g=mix_post_g, xattn_pre_g=xattn_pre_g, mem_norm_g=mem_norm_g, w_xq=w_xq,
                  w_xk=w_xk, w_xv=w_xv, w_xo=w_xo, xattn_post_g=xattn_post_g, ffn2_pre_g=ffn2_pre_g,
                  ffn2_w_gate=ffn2_w_gate, ffn2_w_up=ffn2_w_up, ffn2_w_down=ffn2_w_down, ffn2_post_g=ffn2_post_g)
    depth = ffn1_pre_g.shape[0]
    nbp, seq, _ = x_prompt.shape
    nbs, tv, _ = x_sample.shape
    npr = nbp * seq
    xs = [x_prompt.reshape(npr, D), x_sample.reshape(nbs * tv, D)]
    mem = mem_prompt.reshape(nbp * NM, D)
    per_layer = []
    for layer in range(depth):
        p = {name: w[layer] for name, w in params.items()}
        x, states = _layer(xs, mem, cache_mem_k[layer], cache_mem_v[layer],
                           state_conv[layer], state_ssm_conv[layer], state_ssm[layer], p,
                           nbp=nbp, seq=seq, nbs=nbs, tv=tv, split_out=layer == depth - 1)
        xs = [x]
        per_layer.append(states)
    mk, mv, conv_p, sconv_p, ssm_p, conv_s, sconv_s, ssm_s = [jnp.stack(t) for t in zip(*per_layer)]
    yp, ys = x
    return (yp.reshape(nbp, seq, D), ys.reshape(nbs, tv, D), mk, mv, conv_p, sconv_p, ssm_p, conv_s, sconv_s, ssm_s)
```

```python
import functools

import jax
import jax.numpy as jnp
from jax import lax
from jax.experimental import pallas as pl
from jax.experimental.pallas import tpu as pltpu

F32 = jnp.float32
BF16 = jnp.bfloat16

D = 2048
FF = 5504
DC = 1024
DS = 1024
KC = 31
NH = 16
HP = 64
NG = 2
NS = 128
KS = 4
DX = DS + 2 * NG * NS
CHUNK = 128
NM = 256
XH = 4
XD = D // XH
EPS = 1e-6

LANE = 128
SUB = 8
HIST = 32
HALO = 8

FF_TILE = 1024

VMEM_LIMIT = 56 * 1024 * 1024


def _cparams(sem):
    return pltpu.CompilerParams(dimension_semantics=sem, vmem_limit_bytes=VMEM_LIMIT)


def _rms(x, g):
    return x * lax.rsqrt(jnp.mean(x * x, axis=-1, keepdims=True) + EPS) * g


def _silu(x):
    return x * jax.nn.sigmoid(x)


def _resident(shape):
    return pl.BlockSpec(shape, lambda *_: (0,) * len(shape), pipeline_mode=pl.Buffered(1))


def _ffn_kernel(*refs, n_in, n_out, n_main, tf, n_side):
    x_refs = refs[:n_in]
    pg_ref, wg_ref, wu_ref, wd_ref, qg_ref = refs[n_in:n_in + 5]
    side_in = refs[n_in + 5:n_in + 5 + n_side]
    k = n_in + 5 + n_side
    o_refs = refs[k:k + n_out]
    side_out = refs[k + n_out:k + n_out + n_side]
    xn_ref, acc_ref = refs[k + n_out + n_side:]
    i = pl.program_id(0)
    f = pl.program_id(1)
    last = pl.num_programs(1) - 1

    def x_tile():
        if n_in == 1:
            return x_refs[0][...]
        return jnp.where(i < n_main, x_refs[0][...], x_refs[1][...])

    @pl.when(f == 0)
    def _():
        xn_ref[...] = _rms(x_tile(), pg_ref[...]).astype(BF16)
        acc_ref[...] = jnp.zeros_like(acc_ref)

    def hidden_tile(width):
        xn = xn_ref[...]
        h = jnp.dot(xn, wg_ref[:, 0:width], preferred_element_type=F32)
        u = jnp.dot(xn, wu_ref[:, 0:width], preferred_element_type=F32)
        a = (_silu(h) * u).astype(BF16)
        acc_ref[...] += jnp.dot(a, wd_ref[0:width, :], preferred_element_type=F32)

    @pl.when(f < last)
    def _():
        hidden_tile(tf)

    @pl.when(f == last)
    def _():
        hidden_tile(FF - (FF // tf) * tf)
        res = x_tile() + 0.5 * _rms(acc_ref[...], qg_ref[...])
        if n_out == 1:
            o_refs[0][...] = res
        else:
            @pl.when(i < n_main)
            def _():
                o_refs[0][...] = res

            @pl.when(i >= n_main)
            def _():
                o_refs[1][...] = res

    step = i * pl.num_programs(1) + f
    for m in range(n_side):
        @pl.when((step >= m * SIDE_STRIPS) & (step < (m + 1) * SIDE_STRIPS))
        def _(m=m):
            side_out[m][...] = side_in[m][...].astype(BF16)


SIDE_STRIPS = FF // LANE


def _side_specs(ws, nf):
    specs = []
    for m, w in enumerate(ws):
        strip = lambda i, f, m=m: jnp.clip(i * nf + f - m * SIDE_STRIPS, 0, SIDE_STRIPS - 1)
        if w.shape[0] == FF:
            specs.append(pl.BlockSpec((LANE, w.shape[1]), lambda i, f, strip=strip: (strip(i, f), 0)))
        else:
            specs.append(pl.BlockSpec((w.shape[0], LANE), lambda i, f, strip=strip: (0, strip(i, f))))
    return specs


def _ffn(xs, pre_g, wg, wu, wd, post_g, *, split_out, tm=512, tf=FF_TILE, side=()):
    n_in = len(xs)
    nt = sum(x.shape[0] for x in xs)
    n_main = (nt - tm) // tm
    nf = pl.cdiv(FF, tf)
    assert len(side) * SIDE_STRIPS <= (nt // tm) * nf
    main = lambda i, f: (jnp.minimum(i, n_main - 1), 0)
    extra = lambda i, f: (0, 0)
    whole = lambda i, f: (i, 0)
    if n_in == 1:
        x_specs = [pl.BlockSpec((tm, D), whole)]
    else:
        x_specs = [pl.BlockSpec((tm, D), main), pl.BlockSpec((tm, D), extra)]
    if split_out:
        out_specs = [pl.BlockSpec((tm, D), main), pl.BlockSpec((tm, D), extra)]
        out_shape = [jax.ShapeDtypeStruct((n_main * tm, D), F32), jax.ShapeDtypeStruct((tm, D), F32)]
    else:
        out_specs = [pl.BlockSpec((tm, D), whole)]
        out_shape = [jax.ShapeDtypeStruct((nt, D), F32)]
    res = pl.pallas_call(
        functools.partial(_ffn_kernel, n_in=n_in, n_out=len(out_specs), n_main=n_main, tf=tf, n_side=len(side)),
        grid=(nt // tm, nf),
        in_specs=x_specs + [
            pl.BlockSpec((1, D), lambda i, f: (0, 0)),
            pl.BlockSpec((D, tf), lambda i, f: (0, f)),
            pl.BlockSpec((D, tf), lambda i, f: (0, f)),
            pl.BlockSpec((tf, D), lambda i, f: (f, 0)),
            pl.BlockSpec((1, D), lambda i, f: (0, 0)),
        ] + _side_specs(side, nf),
        out_specs=out_specs + _side_specs(side, nf),
        out_shape=out_shape + [jax.ShapeDtypeStruct(w.shape, BF16) for w in side],
        scratch_shapes=[pltpu.VMEM((tm, D), BF16), pltpu.VMEM((tm, D), F32)],
        compiler_params=_cparams(("arbitrary", "arbitrary")),
        name="ffn",
    )(*xs, pre_g, wg, wu, wd, post_g, *side)
    main_res = res[0] if len(out_specs) == 1 else tuple(res[:2])
    return (main_res, *res[len(out_specs):]) if side else main_res


def _mix_in_kernel(x_ref, g_ref, w_ref, wt_ref, u_ref, z_ref, xbc_ref, dt_ref, *, tn):
    hn = _rms(x_ref[...], g_ref[...]).astype(BF16)
    nt_dims = (((1,), (1,)), ((), ()))

    def cols(start, c):
        return lax.dot_general(hn, w_ref[start + c * tn:start + (c + 1) * tn, :], nt_dims,
                               preferred_element_type=F32)

    for c in range(DC // tn):
        u_ref[:, c * tn:(c + 1) * tn] = cols(0, c) * jax.nn.sigmoid(cols(DC, c))
    for c in range(DS // tn):
        z_ref[:, c * tn:(c + 1) * tn] = cols(2 * DC, c)
    for c in range(DX // tn):
        xbc_ref[:, c * tn:(c + 1) * tn] = cols(2 * DC + DS, c)
    dt_ref[...] = lax.dot_general(hn, wt_ref[...], nt_dims, preferred_element_type=F32)


def _mix_in(x, g, w, wt, *, tm=512, tn=512):
    nt = x.shape[0]
    row = lambda n: pl.BlockSpec((tm, n), lambda i: (i, 0))
    return pl.pallas_call(
        functools.partial(_mix_in_kernel, tn=tn),
        grid=(nt // tm,),
        in_specs=[row(D), _resident((1, D)), _resident(w.shape), _resident((LANE, D))],
        out_specs=[row(DC), row(DS), row(DX), row(LANE)],
        out_shape=[jax.ShapeDtypeStruct((nt, n), F32) for n in (DC, DS, DX, LANE)],
        compiler_params=_cparams(("parallel",)),
        name="mix_in",
    )(x, g, w, wt)


def _conv_prompt_kernel(u_ref, w_ref, b_ref, lg_ref, lb_ref, a_ref, nb_ref, xe_ref, y_ref, *, tt):
    t = pl.program_id(1)
    off = HIST - (KC - 1)

    @pl.when(t == 0)
    def _():
        xe_ref[:, 0:HIST, :] = jnp.zeros((DC // LANE, HIST, LANE), F32)

    for j in range(DC // LANE):
        xe_ref[j, HIST:HIST + tt, :] = u_ref[:, j * LANE:(j + 1) * LANE]
    for j in range(DC // LANE):
        sl = slice(j * LANE, (j + 1) * LANE)
        acc = jnp.broadcast_to(b_ref[:, sl], (tt, LANE))
        for k in range(KC):
            acc = acc + w_ref[k:k + 1, sl] * xe_ref[j, off + k:off + k + tt, :]
        y_ref[:, sl] = acc
    for j in range(DC // LANE):
        nb_ref[:, j * LANE:(j + 1) * LANE] = xe_ref[j, HIST + tt - (KC - 1):HIST + tt, :]
        xe_ref[j, 0:HIST, :] = xe_ref[j, tt:tt + HIST, :]
    @pl.when(t >= 0)
    def _():
        y = y_ref[...]
        yc = y - jnp.mean(y, axis=-1, keepdims=True)
        a_ref[...] = _silu(yc * lax.rsqrt(jnp.mean(yc * yc, axis=-1, keepdims=True) + EPS) * lg_ref[...]
                           + lb_ref[...])


def _conv_prompt(u, w, b, lg, lb, *, nb, seq, tt=256):
    nt = nb * seq
    nper = seq // tt
    par = lambda r: _resident((r, DC))
    return pl.pallas_call(
        functools.partial(_conv_prompt_kernel, tt=tt),
        grid=(nb, nper),
        in_specs=[pl.BlockSpec((tt, DC), lambda s, t: (s * nper + t, 0)), par(KC), par(1), par(1), par(1)],
        out_specs=[pl.BlockSpec((tt, DC), lambda s, t: (s * nper + t, 0)),
                   pl.BlockSpec((None, KC - 1, DC), lambda s, t: (s, 0, 0))],
        out_shape=[jax.ShapeDtypeStruct((nt, DC), F32), jax.ShapeDtypeStruct((nb, KC - 1, DC), F32)],
        scratch_shapes=[pltpu.VMEM((DC // LANE, HIST + tt, LANE), F32), pltpu.VMEM((tt, DC), F32)],
        compiler_params=_cparams(("parallel", "arbitrary")),
        name="conv_prompt",
    )(u, w, b, lg, lb)


def _conv_sample_kernel(u_ref, hist_ref, w_ref, b_ref, lg_ref, lb_ref, a_ref, nh_ref, us_ref, y_ref, as_ref,
                        *, sb, tv):
    nl = DC // LANE
    for j in range(nl):
        us_ref[j] = u_ref[:, j * LANE:(j + 1) * LANE]
    for j in range(nl):
        sl = slice(j * LANE, (j + 1) * LANE)
        accs = [jnp.broadcast_to(b_ref[:, sl], (sb, LANE)) for _ in range(tv)]
        for m in range(KC - 1 + tv):
            if m < KC - 1:
                xm = hist_ref[m, :, sl]
            else:
                xm = us_ref[j, pl.ds(m - (KC - 1), sb, stride=tv), :]
            for t in range(tv):
                if 0 <= m - t < KC:
                    accs[t] = accs[t] + w_ref[m - t:m - t + 1, sl] * xm
            if m >= tv:
                nh_ref[m - tv, :, sl] = xm
        for t in range(tv):
            y_ref[t, :, sl] = accs[t]
    for t in range(tv):
        y = y_ref[t]
        yc = y - jnp.mean(y, axis=-1, keepdims=True)
        a = _silu(yc * lax.rsqrt(jnp.mean(yc * yc, axis=-1, keepdims=True) + EPS) * lg_ref[...] + lb_ref[...])
        for j in range(nl):
            as_ref[j, pl.ds(t, sb, stride=tv), :] = a[:, j * LANE:(j + 1) * LANE]
    for j in range(nl):
        a_ref[:, j * LANE:(j + 1) * LANE] = as_ref[j]


def _conv_sample(u, hist, w, b, lg, lb, *, row0, tv, sb=32):
    nb = hist.shape[1]
    rows = sb * tv
    par = lambda r: _resident((r, DC))
    blk0 = row0 // rows
    hist_spec = pl.BlockSpec((KC - 1, sb, DC), lambda i: (0, i, 0))
    return pl.pallas_call(
        functools.partial(_conv_sample_kernel, sb=sb, tv=tv),
        grid=(nb // sb,),
        in_specs=[pl.BlockSpec((rows, DC), lambda i: (blk0 + i, 0)), hist_spec, par(KC), par(1), par(1), par(1)],
        out_specs=[pl.BlockSpec((rows, DC), lambda i: (i, 0)), hist_spec],
        out_shape=[jax.ShapeDtypeStruct((nb * tv, DC), F32), jax.ShapeDtypeStruct((KC - 1, nb, DC), F32)],
        scratch_shapes=[pltpu.VMEM((DC // LANE, rows, LANE), F32), pltpu.VMEM((tv, sb, DC), F32),
                        pltpu.VMEM((DC // LANE, rows, LANE), F32)],
        compiler_params=_cparams(("arbitrary",)),
        name="conv_sample",
    )(u, hist, w, b, lg, lb)


def _dot01(a, b, dims, *, data):
    x = b if data else a
    one = (a if data else b).astype(BF16)
    t0 = x.astype(BF16)
    r1 = x - t0.astype(F32)
    t1 = r1.astype(BF16)
    t2 = (r1 - t1.astype(F32)).astype(BF16)
    acc = None
    for t in (t0, t1, t2):
        lhs, rhs = (one, t) if data else (t, one)
        p = lax.dot_general(lhs, rhs, dims, preferred_element_type=F32)
        acc = p if acc is None else acc + p
    return acc


def _to_slabs(xe_ref, r0, x):
    for j in range(x.shape[1] // LANE):
        xe_ref[j, r0:r0 + x.shape[0], :] = x[:, j * LANE:(j + 1) * LANE]


def _from_slabs(xe_ref, r0, rows):
    return jnp.concatenate([xe_ref[j, r0:r0 + rows, :] for j in range(xe_ref.shape[0])], axis=1)


def _ssd_chunk(xe_ref, z, dt_raw, st_ref, cw_ref, cb_ref, dtb_ref, alog_ref, dsk_ref, ng_ref, *, L, tv, lq=None,
               h_io=None):
    mm_dims = (((1,), (0,)), ((), ()))
    lq = L if lq is None else lq
    nsq = L // lq
    off = HALO - (KS - 1)
    cols = []
    for j in range(DX // LANE):
        sl = slice(j * LANE, (j + 1) * LANE)
        acc = jnp.broadcast_to(cb_ref[:, sl], (L, LANE))
        for k in range(KS):
            acc = acc + cw_ref[k:k + 1, sl] * xe_ref[j, off + k:off + k + L, :]
        cols.append(acc)
    xc = _silu(jnp.concatenate(cols, axis=1))
    xs = xc[:, 0:DS]

    lane = lax.broadcasted_iota(jnp.int32, (L, LANE), 1)
    rowi = lax.broadcasted_iota(jnp.int32, (L, LANE), 0)
    xdt = dt_raw + dtb_ref[...]
    dt = jnp.maximum(xdt, 0.0) + jnp.log1p(jnp.exp(-jnp.abs(xdt)))
    dt = jnp.where((lane < NH) & (rowi % lq < tv), dt, 0.0)
    da = dt * (-jnp.exp(alog_ref[...]))

    r2 = lax.broadcasted_iota(jnp.int32, (L, L), 0)
    c2 = lax.broadcasted_iota(jnp.int32, (L, L), 1)
    same = r2 // lq == c2 // lq
    causal = (r2 >= c2) & same
    if nsq == 1:
        a_cum = _dot01(causal.astype(F32), da, mm_dims, data=1)
        a_tot = jnp.broadcast_to(a_cum[L - 1:L, :], (L, LANE))
    else:
        cums = _dot01(jnp.concatenate([causal.astype(F32), same.astype(F32)], axis=0), da, mm_dims, data=1)
        a_cum = cums[0:L]
        a_tot = cums[L:2 * L]

    er = lax.broadcasted_iota(jnp.int32, (LANE, DS), 0)
    ec = lax.broadcasted_iota(jnp.int32, (LANE, DS), 1)
    expand = (ec // HP == er).astype(F32)
    stack = jnp.concatenate([jnp.exp(a_cum), jnp.exp(a_tot - a_cum) * dt, jnp.exp(a_tot[0:SUB])], axis=0)
    stack_x = _dot01(stack, expand, mm_dims, data=0)
    ea_x = stack_x[0:L]
    wend_x = stack_x[L:2 * L]
    cd_x = stack_x[2 * L:2 * L + 1]

    ir = lax.broadcasted_iota(jnp.int32, (LANE, LANE), 0)
    ic = lax.broadcasted_iota(jnp.int32, (LANE, LANE), 1)
    ident = (ir == ic).astype(F32)
    tr = _dot01(ident, jnp.concatenate([dt, a_cum], axis=0), (((1,), (1,)), ((), ())), data=1)
    dt_t = tr[:, 0:L]
    acum_t = tr[:, L:2 * L]

    lane_x = lax.broadcasted_iota(jnp.int32, (L, LANE), 1)
    hpg = NH // NG
    bms = [xc[:, DS + g * NS:DS + (g + 1) * NS] for g in range(NG)]
    cms = [xc[:, DS + NG * NS + g * NS:DS + NG * NS + (g + 1) * NS] for g in range(NG)]
    gss = [slice(g * (DS // NG), (g + 1) * (DS // NG)) for g in range(NG)]
    cbms = [lax.dot_general(cms[g], bms[g], (((1,), (1,)), ((), ())), preferred_element_type=F32)
            for g in range(NG)]
    if h_io is None:
        st_old = [st_ref[:, gss[g]] for g in range(NG)]
        y_off = [jnp.dot(cms[g], st_old[g], preferred_element_type=F32) * ea_x[:, gss[g]] for g in range(NG)]
        st_new = [jnp.dot(bms[g].T, xs[:, gss[g]] * wend_x[:, gss[g]], preferred_element_type=F32)
                  for g in range(NG)]
        for g in range(NG):
            st_ref[:, gss[g]] = st_old[g] * cd_x[:, gss[g]] + st_new[g]
    else:
        rows_g = hpg * HP
        sq = [slice(q * lq, (q + 1) * lq) for q in range(nsq)]
        hs = [[h_io[q][0][g * hpg:(g + 1) * hpg].reshape(rows_g, NS) for g in range(NG)] for q in range(nsq)]
        y_off = [jnp.concatenate([lax.dot_general(cms[g][sq[q]], hs[q][g], (((1,), (1,)), ((), ())),
                                                  preferred_element_type=F32) for q in range(nsq)], axis=0)
                 * ea_x[:, gss[g]] for g in range(NG)]
        xw = [xs[:, gss[g]] * wend_x[:, gss[g]] for g in range(NG)]
        upd = [[lax.dot_general(xw[g][sq[q]], bms[g][sq[q]], (((0,), (0,)), ((), ())),
                                preferred_element_type=F32) for g in range(NG)] for q in range(nsq)]
        seq_decay = jnp.exp(a_tot)
        for q in range(nsq):
            for h in range(NH):
                g, hl = divmod(h, hpg)
                rows = slice(hl * HP, (hl + 1) * HP)
                h_io[q][1][h] = hs[q][g][rows, :] * seq_decay[q * lq:q * lq + 1, h:h + 1] + upd[q][g][rows, :]
    segs = [a_cum[:, h:h + 1] - acum_t[h:h + 1, :] for h in range(NH)]
    decs = [jnp.exp(jnp.where(causal, segs[h], -jnp.inf)) for h in range(NH)]
    ws = [cbms[h // hpg] * decs[h] * dt_t[h:h + 1, :] for h in range(NH)]
    y_diag = []
    for pr in range(NH // 2):
        xp = xs[:, 2 * pr * HP:(2 * pr + 2) * HP]
        rhs = jnp.concatenate([jnp.where(lane_x < HP, xp, 0.0), jnp.where(lane_x >= HP, xp, 0.0)], axis=0)
        y_diag.append(jnp.dot(jnp.concatenate([ws[2 * pr], ws[2 * pr + 1]], axis=1), rhs,
                              preferred_element_type=F32))
    y = jnp.concatenate(y_diag, axis=1) + jnp.concatenate(y_off, axis=1) + dsk_ref[...] * xs
    y = y * _silu(z)
    outs = []
    for g in range(NG):
        gs = slice(g * (DS // NG), (g + 1) * (DS // NG))
        outs.append(_rms(y[:, gs], ng_ref[:, gs]))
    return jnp.concatenate(outs, axis=1)


def _state_out(st_ref, h_ref):
    for j in range(DS // LANE):
        blk = st_ref[:, j * LANE:(j + 1) * LANE].T
        for q in range(LANE // HP):
            h_ref[j * (LANE // HP) + q] = blk[q * HP:(q + 1) * HP, :]


def _ssd_prompt_kernel(xbc_ref, z_ref, dt_ref, cw_ref, cb_ref, dtb_ref, alog_ref, dsk_ref, ng_ref,
                       y_ref, nb_ref, h_ref, xe_ref, st_ref, *, L):
    c = pl.program_id(1)

    @pl.when(c == 0)
    def _():
        xe_ref[:, 0:HALO, :] = jnp.zeros((DX // LANE, HALO, LANE), F32)
        st_ref[...] = jnp.zeros_like(st_ref)

    _to_slabs(xe_ref, HALO, xbc_ref[...])
    y_ref[...] = _ssd_chunk(xe_ref, z_ref[...], dt_ref[...], st_ref, cw_ref, cb_ref, dtb_ref, alog_ref, dsk_ref,
                            ng_ref, L=L, tv=L)
    xe_ref[:, 0:HALO, :] = xe_ref[:, L:L + HALO, :]

    @pl.when(c == pl.num_programs(1) - 1)
    def _():
        nb_ref[...] = _from_slabs(xe_ref, HALO - (KS - 1), KS - 1)
        _state_out(st_ref, h_ref)


def _ssd_params_specs():
    return [_resident((KS, DX)), _resident((1, DX)), _resident((1, LANE)), _resident((1, LANE)),
            _resident((1, DS)), _resident((1, DS))]


def _ssd_prompt(xbc, z, dt, params, *, nb, seq):
    nt = nb * seq
    L = CHUNK
    nper = seq // L
    row = lambda n: pl.BlockSpec((L, n), lambda s, c: (s * nper + c, 0))
    return pl.pallas_call(
        functools.partial(_ssd_prompt_kernel, L=L),
        grid=(nb, nper),
        in_specs=[row(DX), row(DS), row(LANE)] + _ssd_params_specs(),
        out_specs=[row(DS),
                   pl.BlockSpec((None, KS - 1, DX), lambda s, c: (s, 0, 0)),
                   pl.BlockSpec((None, NH, HP, NS), lambda s, c: (s, 0, 0, 0))],
        out_shape=[jax.ShapeDtypeStruct((nt, DS), F32), jax.ShapeDtypeStruct((nb, KS - 1, DX), F32),
                   jax.ShapeDtypeStruct((nb, NH, HP, NS), F32)],
        scratch_shapes=[pltpu.VMEM((DX // LANE, HALO + L, LANE), F32), pltpu.VMEM((NS, DS), F32)],
        compiler_params=_cparams(("parallel", "arbitrary")),
        name="ssd_prompt",
    )(xbc, z, dt, *params)


def _ssd_sample_kernel(xbc_ref, z_ref, dt_ref, hist_ref, h0_ref, cw_ref, cb_ref, dtb_ref, alog_ref, dsk_ref,
                       ng_ref, y_ref, nb_ref, h_ref, xe_ref, zb_ref, dtp_ref, *, lq, tv, nseq):
    assert lq - tv >= KS - 1 and HALO >= KS - 1
    xe_ref[...] = jnp.zeros_like(xe_ref)
    zb_ref[...] = jnp.zeros_like(zb_ref)
    dtp_ref[...] = jnp.zeros_like(dtp_ref)
    for j in range(nseq):
        rows = slice(j * tv, (j + 1) * tv)
        _to_slabs(xe_ref, HALO + j * lq - (KS - 1), hist_ref[j])
        _to_slabs(xe_ref, HALO + j * lq, xbc_ref[rows, :])
        zb_ref[j * lq:j * lq + tv, :] = z_ref[rows, :]
        dtp_ref[j * lq:j * lq + tv, :] = dt_ref[rows, :]
    y = _ssd_chunk(xe_ref, zb_ref[...], dtp_ref[...], None, cw_ref, cb_ref, dtb_ref, alog_ref, dsk_ref, ng_ref,
                   L=nseq * lq, tv=tv, lq=lq, h_io=[(h0_ref.at[j], h_ref.at[j]) for j in range(nseq)])
    for j in range(nseq):
        y_ref[j * tv:(j + 1) * tv, :] = y[j * lq:j * lq + tv, :]
        nb_ref[j] = _from_slabs(xe_ref, HALO + j * lq + tv - (KS - 1), KS - 1)


def _ssd_sample(xbc, z, dt, hist, h0, params, *, row0, tv, nseq=8, lq=SUB):
    nb = hist.shape[0]
    rows = nseq * tv
    L = nseq * lq
    blk0 = row0 // rows
    row = lambda n: pl.BlockSpec((rows, n), lambda i: (blk0 + i, 0))
    return pl.pallas_call(
        functools.partial(_ssd_sample_kernel, lq=lq, tv=tv, nseq=nseq),
        grid=(nb // nseq,),
        in_specs=[row(DX), row(DS), row(LANE),
                  pl.BlockSpec((nseq, KS - 1, DX), lambda i: (i, 0, 0)),
                  pl.BlockSpec((nseq, NH, HP, NS), lambda i: (i, 0, 0, 0))] + _ssd_params_specs(),
        out_specs=[pl.BlockSpec((rows, DS), lambda i: (i, 0)),
                   pl.BlockSpec((nseq, KS - 1, DX), lambda i: (i, 0, 0)),
                   pl.BlockSpec((nseq, NH, HP, NS), lambda i: (i, 0, 0, 0))],
        out_shape=[jax.ShapeDtypeStruct((nb * tv, DS), F32), jax.ShapeDtypeStruct((nb, KS - 1, DX), F32),
                   jax.ShapeDtypeStruct((nb, NH, HP, NS), F32)],
        scratch_shapes=[pltpu.VMEM((DX // LANE, HALO + L, LANE), F32), pltpu.VMEM((L, DS), F32),
                        pltpu.VMEM((L, LANE), F32)],
        compiler_params=_cparams(("arbitrary",)),
        name="ssd_sample",
    )(xbc, z, dt, hist, h0, *params)


def _proj_out_kernel(*refs, n, n_main):
    lhs = refs[0:2 * n]
    ws = refs[2 * n:3 * n]
    x_ref, g_ref, o_ref = refs[3 * n:3 * n + 3]
    is_main = pl.program_id(0) < n_main
    m = None
    for k, w_ref in enumerate(ws):
        a = jnp.where(is_main, lhs[2 * k][...], lhs[2 * k + 1][...]).astype(BF16)
        p = jnp.dot(a, w_ref[...].astype(BF16), preferred_element_type=F32)
        m = p if m is None else m + p
    o_ref[...] = x_ref[...] + _rms(m, g_ref[...])


def _proj_out(lhs_pairs, ws, x, g, *, tm=512):
    nt = x.shape[0]
    n = len(lhs_pairs)
    n_main = nt // tm - 1
    lhs_specs = []
    for a_main, a_extra in lhs_pairs:
        assert a_main.shape[0] == n_main * tm and a_extra.shape[0] == tm
        lhs_specs.append(pl.BlockSpec((tm, a_main.shape[1]), lambda i: (jnp.minimum(i, n_main - 1), 0)))
        lhs_specs.append(pl.BlockSpec((tm, a_extra.shape[1]), lambda i: (0, 0)))
    return pl.pallas_call(
        functools.partial(_proj_out_kernel, n=n, n_main=n_main),
        grid=(nt // tm,),
        in_specs=lhs_specs + [_resident(w.shape) for w in ws]
                 + [pl.BlockSpec((tm, D), lambda i: (i, 0)), _resident((1, D))],
        out_specs=pl.BlockSpec((tm, D), lambda i: (i, 0)),
        out_shape=jax.ShapeDtypeStruct((nt, D), F32),
        compiler_params=_cparams(("arbitrary",)),
        name="proj_out",
    )(*[a for pair in lhs_pairs for a in pair], *ws, x, g)


def _norm_proj_kernel(*refs, n, heads):
    x_ref, g_ref = refs[0:2]
    ws = refs[2:2 + n]
    outs = refs[2 + n:2 + 2 * n]
    hn = _rms(x_ref[...], g_ref[...]).astype(BF16)
    for w_ref, o_ref in zip(ws, outs):
        for h in range(XH):
            sl = slice(h * XD, (h + 1) * XD)
            r = jnp.dot(hn, w_ref[:, sl].astype(BF16), preferred_element_type=F32).astype(o_ref.dtype)
            if heads:
                o_ref[:, h, :] = r
            else:
                o_ref[:, sl] = r


def _norm_proj(x, g, ws, out_dtype, *, heads=False, tm=512):
    nt = x.shape[0]
    n = len(ws)
    if heads:
        tm = NM
        out_specs = [pl.BlockSpec((None, NM, XH, XD), lambda i: (i, 0, 0, 0))] * n
        out_shape = [jax.ShapeDtypeStruct((nt // NM, NM, XH, XD), out_dtype)] * n
    else:
        out_specs = [pl.BlockSpec((tm, D), lambda i: (i, 0))] * n
        out_shape = [jax.ShapeDtypeStruct((nt, D), out_dtype)] * n
    return pl.pallas_call(
        functools.partial(_norm_proj_kernel, n=n, heads=heads),
        grid=(nt // tm,),
        in_specs=[pl.BlockSpec((tm, D), lambda i: (i, 0)), _resident((1, D))] + [_resident((D, D))] * n,
        out_specs=out_specs,
        out_shape=out_shape,
        compiler_params=_cparams(("parallel",)),
        name="norm_proj",
    )(x, g, *ws)


NLT = XD // LANE
LT_STRIDE = NLT * XH
KV_ROWS = NM * LT_STRIDE


def _kv_tiles(x):
    nb = x.shape[0]
    return x.reshape(nb, NM, XH, NLT, LANE).transpose(0, 1, 3, 2, 4).reshape(nb, KV_ROWS, LANE)


def _attn_prompt_kernel(q_ref, k_ref, v_ref, o_ref, kh_ref, vh_ref):
    @pl.when(pl.program_id(1) == 0)
    def _():
        for h in range(XH):
            for lt in range(NLT):
                rows = pl.ds(lt * XH + h, NM, stride=LT_STRIDE)
                kh_ref[h, :, lt * LANE:(lt + 1) * LANE] = k_ref[0, rows, :].astype(BF16)
                vh_ref[h, :, lt * LANE:(lt + 1) * LANE] = v_ref[0, rows, :].astype(BF16)

    scale = XD ** -0.5
    nt_dims = (((1,), (1,)), ((), ()))
    heads = range(XH)
    s = [lax.dot_general(q_ref[:, h * XD:(h + 1) * XD], kh_ref[h], nt_dims, preferred_element_type=F32) * scale
         for h in heads]
    e = [jnp.exp(s[h] - jnp.max(s[h], axis=-1, keepdims=True)) for h in heads]
    p = [(e[h] / jnp.sum(e[h], axis=-1, keepdims=True)).astype(BF16) for h in heads]
    for h in heads:
        o_ref[:, h * XD:(h + 1) * XD] = jnp.dot(p[h], vh_ref[h], preferred_element_type=F32).astype(o_ref.dtype)


def _group_sum(x, col_lt):
    n = x.shape[1]
    a = x + jnp.where(col_lt % 2 == 0, pltpu.roll(x, n - 1, axis=1), pltpu.roll(x, 1, axis=1))
    return a + jnp.where(col_lt < 2, pltpu.roll(a, n - 2, axis=1), pltpu.roll(a, 2, axis=1))


def _attn_sample_kernel(q_ref, k_ref, v_ref, o_ref, *, nseq, tv):
    assert NLT == 4
    tq = q_ref.shape[0]
    ncol = NM * NLT
    scale = XD ** -0.5
    nt_dims = (((1,), (1,)), ((), ()))
    col_lt = lax.broadcasted_iota(jnp.int32, (tq, ncol), 1) % NLT
    rowi = lax.broadcasted_iota(jnp.int32, (tq, LANE), 0)
    for h in range(XH):
        qp = jnp.concatenate([q_ref[:, h * XD + lt * LANE:h * XD + (lt + 1) * LANE].astype(F32)
                              for lt in range(NLT)], axis=0)
        seqs = range(nseq)
        g = [lax.dot_general(qp, k_ref[j, pl.ds(h, ncol, stride=XH), :], nt_dims, preferred_element_type=F32)
             for j in seqs]
        s4 = [sum(jnp.where(col_lt == lt, g[j][lt * tq:(lt + 1) * tq], 0.0) for lt in range(NLT)) for j in seqs]
        s = [_group_sum(s4[j], col_lt) * scale for j in seqs]
        e = [jnp.exp(s[j] - jnp.max(s[j], axis=-1, keepdims=True)) for j in seqs]
        p = [e[j] / (jnp.sum(e[j], axis=-1, keepdims=True) * (1.0 / NLT)) for j in seqs]
        o = [jnp.dot(jnp.concatenate([jnp.where(col_lt == lt, p[j], 0.0) for lt in range(NLT)], axis=0),
                     v_ref[j, pl.ds(h, ncol, stride=XH), :], preferred_element_type=F32) for j in seqs]
        acc = [None] * NLT
        for j in seqs:
            mine = (rowi >= j * tv) & (rowi < (j + 1) * tv)
            for lt in range(NLT):
                acc[lt] = jnp.where(mine, o[j][lt * tq:(lt + 1) * tq], 0.0 if acc[lt] is None else acc[lt])
        for lt in range(NLT):
            o_ref[:, h * XD + lt * LANE:h * XD + (lt + 1) * LANE] = acc[lt].astype(o_ref.dtype)


def _attn_prompt(q, k, v, *, nb, seq, tq=512):
    nt = nb * seq
    nper = seq // tq
    kv_spec = pl.BlockSpec((1, KV_ROWS, LANE), lambda s, t: (s, 0, 0))
    return pl.pallas_call(
        _attn_prompt_kernel,
        grid=(nb, nper),
        in_specs=[pl.BlockSpec((tq, D), lambda s, t: (s * nper + t, 0)), kv_spec, kv_spec],
        out_specs=pl.BlockSpec((tq, D), lambda s, t: (s * nper + t, 0)),
        out_shape=jax.ShapeDtypeStruct((nt, D), BF16),
        scratch_shapes=[pltpu.VMEM((XH, NM, XD), BF16), pltpu.VMEM((XH, NM, XD), BF16)],
        compiler_params=_cparams(("arbitrary", "arbitrary")),
        name="attn_prompt",
    )(q, _kv_tiles(k), _kv_tiles(v))


def _attn_sample(q, k, v, *, row0, tv, nseq=4):
    nb = k.shape[0]
    rows = nseq * tv
    blk0 = row0 // rows
    kv_spec = pl.BlockSpec((nseq, KV_ROWS, LANE), lambda i: (i, 0, 0))
    return pl.pallas_call(
        functools.partial(_attn_sample_kernel, nseq=nseq, tv=tv),
        grid=(nb // nseq,),
        in_specs=[pl.BlockSpec((rows, D), lambda i: (blk0 + i, 0)), kv_spec, kv_spec],
        out_specs=pl.BlockSpec((rows, D), lambda i: (i, 0)),
        out_shape=jax.ShapeDtypeStruct((nb * tv, D), BF16),
        compiler_params=_cparams(("arbitrary",)),
        name="attn_sample",
    )(q, _kv_tiles(k), _kv_tiles(v))


def _row(v):
    return v.reshape(1, -1).astype(F32)


def _pad_lanes(v, n):
    return jnp.pad(v.reshape(1, -1).astype(F32), ((0, 0), (0, n - v.size)))


def _layer(xs, mem, cache_k, cache_v, st_conv, st_sconv, st_ssm, p, *, nbp, seq, nbs, tv, split_out):
    npr = nbp * seq
    bf = lambda w: w.astype(BF16)

    x, wg2, wu2, wd2 = _ffn(xs, _row(p['ffn1_pre_g']), bf(p['ffn1_w_gate']), bf(p['ffn1_w_up']),
                            bf(p['ffn1_w_down']), _row(p['ffn1_post_g']), split_out=False, tf=FF_TILE // 2,
                            side=(p['ffn2_w_gate'], p['ffn2_w_up'], p['ffn2_w_down']))

    w_in = bf(p['w_in'].T)
    wt = jnp.pad(w_in[2 * DC + DS + DX:], ((0, LANE - NH), (0, 0)))
    u, z, xbc, dtr = _mix_in(x, _row(p['mix_pre_g']), w_in, wt)

    cpar = (p['conv_w'].astype(F32), _row(p['conv_b']), _row(p['conv_ln_g']), _row(p['conv_ln_b']))
    a_p, conv_p = _conv_prompt(u, *cpar, nb=nbp, seq=seq)
    a_s, conv_s = _conv_sample(u, jnp.transpose(st_conv, (1, 0, 2)), *cpar, row0=npr, tv=tv)
    conv_s = jnp.transpose(conv_s, (1, 0, 2))

    spar = (p['ssm_conv_w'].astype(F32), _row(p['ssm_conv_b']), _pad_lanes(p['dt_bias'], LANE),
            _pad_lanes(p['a_log'], LANE), jnp.repeat(p['d_skip'].astype(F32), HP).reshape(1, DS),
            _row(p['ssm_norm_g']))
    y_p, sconv_p, ssm_p = _ssd_prompt(xbc, z, dtr, spar, nb=nbp, seq=seq)
    y_s, sconv_s, ssm_s = _ssd_sample(xbc, z, dtr, st_sconv, st_ssm, spar, row0=npr, tv=tv)

    w_out = bf(p['w_out'])
    x = _proj_out([(a_p, a_s), (y_p, y_s)], [w_out[:DC], w_out[DC:]], x, _row(p['mix_post_g']))

    (q,) = _norm_proj(x, _row(p['xattn_pre_g']), [p['w_xq']], BF16)
    mk, mv = _norm_proj(mem, _row(p['mem_norm_g']), [p['w_xk'], p['w_xv']], F32, heads=True)
    o_p = _attn_prompt(q, mk, mv, nb=nbp, seq=seq)
    o_s = _attn_sample(q, cache_k, cache_v, row0=npr, tv=tv)
    x = _proj_out([(o_p, o_s)], [p['w_xo']], x, _row(p['xattn_post_g']))

    x = _ffn([x], _row(p['ffn2_pre_g']), wg2, wu2, wd2, _row(p['ffn2_post_g']), split_out=split_out)
    return x, (mk, mv, conv_p, sconv_p, ssm_p, conv_s, sconv_s, ssm_s)


def kernel(x_prompt, x_sample, mem_prompt, cache_mem_k, cache_mem_v, state_conv, state_ssm_conv, state_ssm, ffn1_pre_g, ffn1_w_gate, ffn1_w_up, ffn1_w_down, ffn1_post_g, mix_pre_g, w_in, conv_w, conv_b, conv_ln_g, conv_ln_b, ssm_conv_w, ssm_conv_b, dt_bias, a_log, d_skip, ssm_norm_g, w_out, mix_post_g, xattn_pre_g, mem_norm_g, w_xq, w_xk, w_xv, w_xo, xattn_post_g, ffn2_pre_g, ffn2_w_gate, ffn2_w_up, ffn2_w_down, ffn2_post_g):
    params = dict(ffn1_pre_g=ffn1_pre_g, ffn1_w_gate=ffn1_w_gate, ffn1_w_up=ffn1_w_up, ffn1_w_down=ffn1_w_down,
                  ffn1_post_g=ffn1_post_g, mix_pre_g=mix_pre_g, w_in=w_in, conv_w=conv_w, conv_b=conv_b,
                  conv_ln_g=conv_ln_g, conv_ln_b=conv_ln_b, ssm_conv_w=ssm_conv_w, ssm_conv_b=ssm_conv_b,
                  dt_bias=dt_bias, a_log=a_log, d_skip=d_skip, ssm_norm_g=ssm_norm_g, w_out=w_out,
                  mix_post_g=mix_post_g, xattn_pre_g=xattn_pre_g, mem_norm_g=mem_norm_g, w_xq=w_xq,
                  w_xk=w_xk, w_xv=w_xv, w_xo=w_xo, xattn_post_g=xattn_post_g, ffn2_pre_g=ffn2_pre_g,
                  ffn2_w_gate=ffn2_w_gate, ffn2_w_up=ffn2_w_up, ffn2_w_down=ffn2_w_down, ffn2_post_g=ffn2_post_g)
    depth = ffn1_pre_g.shape[0]
    nbp, seq, _ = x_prompt.shape
    nbs, tv, _ = x_sample.shape
    npr = nbp * seq
    xs = [x_prompt.reshape(npr, D), x_sample.reshape(nbs * tv, D)]
    mem = mem_prompt.reshape(nbp * NM, D)
    per_layer = []
    for layer in range(depth):
        p = {name: w[layer] for name, w in params.items()}
        x, states = _layer(xs, mem, cache_mem_k[layer], cache_mem_v[layer],
                           state_conv[layer], state_ssm_conv[layer], state_ssm[layer], p,
                           nbp=nbp, seq=seq, nbs=nbs, tv=tv, split_out=layer == depth - 1)
        xs = [x]
        per_layer.append(states)
    mk, mv, conv_p, sconv_p, ssm_p, conv_s, sconv_s, ssm_s = [jnp.stack(t) for t in zip(*per_layer)]
    yp, ys = x
    return (yp.reshape(nbp, seq, D), ys.reshape(nbs, tv, D), mk, mv, conv_p, sconv_p, ssm_p, conv_s, sconv_s, ssm_s)
```

```python
import functools

import jax
import jax.numpy as jnp
from jax import lax
from jax.experimental import pallas as pl
from jax.experimental.pallas import tpu as pltpu

F32 = jnp.float32
BF16 = jnp.bfloat16

D = 2048
FF = 5504
DC = 1024
DS = 1024
KC = 31
NH = 16
HP = 64
NG = 2
NS = 128
KS = 4
DX = DS + 2 * NG * NS
CHUNK = 128
NM = 256
XH = 4
XD = D // XH
EPS = 1e-6

LANE = 128
SUB = 8
HIST = 32
HALO = 8

FF_TILE = 1024

VMEM_LIMIT = 60 * 1024 * 1024


def _cparams(sem):
    return pltpu.CompilerParams(dimension_semantics=sem, vmem_limit_bytes=VMEM_LIMIT)


def _rms(x, g):
    return x * lax.rsqrt(jnp.mean(x * x, axis=-1, keepdims=True) + EPS) * g


def _silu(x):
    return x * jax.nn.sigmoid(x)


def _resident(shape):
    return pl.BlockSpec(shape, lambda *_: (0,) * len(shape), pipeline_mode=pl.Buffered(1))


def _ffn_kernel(*refs, n_in, n_out, n_main, tf, side_plan):
    n_side = len(side_plan)
    x_refs = refs[:n_in]
    pg_ref, wg_ref, wu_ref, wd_ref, qg_ref = refs[n_in:n_in + 5]
    side_in = refs[n_in + 5:n_in + 5 + n_side]
    k = n_in + 5 + n_side
    o_refs = refs[k:k + n_out]
    side_out = refs[k + n_out:k + n_out + n_side]
    xn_ref, acc_ref = refs[k + n_out + n_side:]
    i = pl.program_id(0)
    f = pl.program_id(1)
    last = pl.num_programs(1) - 1

    def x_tile():
        if n_in == 1:
            return x_refs[0][...]
        return jnp.where(i < n_main, x_refs[0][...], x_refs[1][...])

    @pl.when(f == 0)
    def _():
        xn_ref[...] = _rms(x_tile(), pg_ref[...]).astype(BF16)
        acc_ref[...] = jnp.zeros_like(acc_ref)

    def hidden_tile(width):
        xn = xn_ref[...]
        h = jnp.dot(xn, wg_ref[:, 0:width], preferred_element_type=F32)
        u = jnp.dot(xn, wu_ref[:, 0:width], preferred_element_type=F32)
        a = (_silu(h) * u).astype(BF16)
        acc_ref[...] += jnp.dot(a, wd_ref[0:width, :], preferred_element_type=F32)

    @pl.when(f < last)
    def _():
        hidden_tile(tf)

    @pl.when(f == last)
    def _():
        hidden_tile(FF - (FF // tf) * tf)
        res = x_tile() + 0.5 * _rms(acc_ref[...], qg_ref[...])
        if n_out == 1:
            o_refs[0][...] = res
        else:
            @pl.when(i < n_main)
            def _():
                o_refs[0][...] = res

            @pl.when(i >= n_main)
            def _():
                o_refs[1][...] = res

    step = i * pl.num_programs(1) + f
    for m, (start, n) in enumerate(side_plan):
        @pl.when((step >= start) & (step < start + n))
        def _(m=m):
            side_out[m][...] = side_in[m][...].astype(BF16)


def _side_plan(side):
    plan, start = [], 0
    for w, axis in side:
        n = pl.cdiv(w.shape[axis], LANE)
        plan.append((start, n))
        start += n
    return tuple(plan), start


def _side_specs(side, nf):
    specs = []
    for (w, axis), (start, n) in zip(side, _side_plan(side)[0]):
        strip = lambda i, f, start=start, n=n: jnp.clip(i * nf + f - start, 0, n - 1)
        if axis == 0:
            specs.append(pl.BlockSpec((LANE, w.shape[1]), lambda i, f, strip=strip: (strip(i, f), 0)))
        else:
            specs.append(pl.BlockSpec((w.shape[0], LANE), lambda i, f, strip=strip: (0, strip(i, f))))
    return specs


def _ffn(xs, pre_g, wg, wu, wd, post_g, *, split_out, tm=512, tf=FF_TILE, side=()):
    n_in = len(xs)
    nt = sum(x.shape[0] for x in xs)
    n_main = (nt - tm) // tm
    nf = pl.cdiv(FF, tf)
    side_plan, side_steps = _side_plan(side)
    assert side_steps <= (nt // tm) * nf
    main = lambda i, f: (jnp.minimum(i, n_main - 1), 0)
    extra = lambda i, f: (0, 0)
    whole = lambda i, f: (i, 0)
    if n_in == 1:
        x_specs = [pl.BlockSpec((tm, D), whole)]
    else:
        x_specs = [pl.BlockSpec((tm, D), main), pl.BlockSpec((tm, D), extra)]
    if split_out:
        out_specs = [pl.BlockSpec((tm, D), main), pl.BlockSpec((tm, D), extra)]
        out_shape = [jax.ShapeDtypeStruct((n_main * tm, D), F32), jax.ShapeDtypeStruct((tm, D), F32)]
    else:
        out_specs = [pl.BlockSpec((tm, D), whole)]
        out_shape = [jax.ShapeDtypeStruct((nt, D), F32)]
    res = pl.pallas_call(
        functools.partial(_ffn_kernel, n_in=n_in, n_out=len(out_specs), n_main=n_main, tf=tf, side_plan=side_plan),
        grid=(nt // tm, nf),
        in_specs=x_specs + [
            pl.BlockSpec((1, D), lambda i, f: (0, 0)),
            pl.BlockSpec((D, tf), lambda i, f: (0, f)),
            pl.BlockSpec((D, tf), lambda i, f: (0, f)),
            pl.BlockSpec((tf, D), lambda i, f: (f, 0)),
            pl.BlockSpec((1, D), lambda i, f: (0, 0)),
        ] + _side_specs(side, nf),
        out_specs=out_specs + _side_specs(side, nf),
        out_shape=out_shape + [jax.ShapeDtypeStruct(w.shape, BF16) for w, _ in side],
        scratch_shapes=[pltpu.VMEM((tm, D), BF16), pltpu.VMEM((tm, D), F32)],
        compiler_params=_cparams(("arbitrary", "arbitrary")),
        name="ffn",
    )(*xs, pre_g, wg, wu, wd, post_g, *[w for w, _ in side])
    main_res = res[0] if len(out_specs) == 1 else tuple(res[:2])
    return (main_res, *res[len(out_specs):]) if side else main_res


def _mix_in_kernel(x_ref, g_ref, w_ref, wt_ref, u_ref, z_ref, xbc_ref, dt_ref, *, tn):
    hn = _rms(x_ref[...], g_ref[...]).astype(BF16)
    nt_dims = (((1,), (1,)), ((), ()))

    def cols(start, c):
        return lax.dot_general(hn, w_ref[start + c * tn:start + (c + 1) * tn, :], nt_dims,
                               preferred_element_type=F32)

    for c in range(DC // tn):
        u_ref[:, c * tn:(c + 1) * tn] = cols(0, c) * jax.nn.sigmoid(cols(DC, c))
    for c in range(DS // tn):
        z_ref[:, c * tn:(c + 1) * tn] = cols(2 * DC, c)
    for c in range(DX // tn):
        xbc_ref[:, c * tn:(c + 1) * tn] = cols(2 * DC + DS, c)
    dt_ref[...] = lax.dot_general(hn, wt_ref[...], nt_dims, preferred_element_type=F32)


def _mix_in(x, g, w, wt, *, tm=512, tn=512):
    nt = x.shape[0]
    row = lambda n: pl.BlockSpec((tm, n), lambda i: (i, 0))
    return pl.pallas_call(
        functools.partial(_mix_in_kernel, tn=tn),
        grid=(nt // tm,),
        in_specs=[row(D), _resident((1, D)), _resident(w.shape), _resident((LANE, D))],
        out_specs=[row(DC), row(DS), row(DX), row(LANE)],
        out_shape=[jax.ShapeDtypeStruct((nt, n), F32) for n in (DC, DS, DX, LANE)],
        compiler_params=_cparams(("parallel",)),
        name="mix_in",
    )(x, g, w, wt)


def _conv_prompt_kernel(u_ref, w_ref, b_ref, lg_ref, lb_ref, a_ref, nb_ref, xe_ref, y_ref, *, tt):
    t = pl.program_id(1)
    off = HIST - (KC - 1)

    @pl.when(t == 0)
    def _():
        xe_ref[:, 0:HIST, :] = jnp.zeros((DC // LANE, HIST, LANE), F32)

    for j in range(DC // LANE):
        xe_ref[j, HIST:HIST + tt, :] = u_ref[:, j * LANE:(j + 1) * LANE]
    for j in range(DC // LANE):
        sl = slice(j * LANE, (j + 1) * LANE)
        acc = jnp.broadcast_to(b_ref[:, sl], (tt, LANE))
        for k in range(KC):
            acc = acc + w_ref[k:k + 1, sl] * xe_ref[j, off + k:off + k + tt, :]
        y_ref[:, sl] = acc
    for j in range(DC // LANE):
        nb_ref[:, j * LANE:(j + 1) * LANE] = xe_ref[j, HIST + tt - (KC - 1):HIST + tt, :]
        xe_ref[j, 0:HIST, :] = xe_ref[j, tt:tt + HIST, :]
    @pl.when(t >= 0)
    def _():
        y = y_ref[...]
        yc = y - jnp.mean(y, axis=-1, keepdims=True)
        a_ref[...] = _silu(yc * lax.rsqrt(jnp.mean(yc * yc, axis=-1, keepdims=True) + EPS) * lg_ref[...]
                           + lb_ref[...])


def _conv_prompt(u, w, b, lg, lb, *, nb, seq, tt=256):
    nt = nb * seq
    nper = seq // tt
    par = lambda r: _resident((r, DC))
    return pl.pallas_call(
        functools.partial(_conv_prompt_kernel, tt=tt),
        grid=(nb, nper),
        in_specs=[pl.BlockSpec((tt, DC), lambda s, t: (s * nper + t, 0)), par(KC), par(1), par(1), par(1)],
        out_specs=[pl.BlockSpec((tt, DC), lambda s, t: (s * nper + t, 0)),
                   pl.BlockSpec((None, KC - 1, DC), lambda s, t: (s, 0, 0))],
        out_shape=[jax.ShapeDtypeStruct((nt, DC), F32), jax.ShapeDtypeStruct((nb, KC - 1, DC), F32)],
        scratch_shapes=[pltpu.VMEM((DC // LANE, HIST + tt, LANE), F32), pltpu.VMEM((tt, DC), F32)],
        compiler_params=_cparams(("parallel", "arbitrary")),
        name="conv_prompt",
    )(u, w, b, lg, lb)


def _conv_sample_kernel(u_ref, hist_ref, w_ref, b_ref, lg_ref, lb_ref, a_ref, nh_ref, us_ref, y_ref, as_ref,
                        *, sb, tv):
    nl = DC // LANE
    for j in range(nl):
        us_ref[j] = u_ref[:, j * LANE:(j + 1) * LANE]
    for j in range(nl):
        sl = slice(j * LANE, (j + 1) * LANE)
        accs = [jnp.broadcast_to(b_ref[:, sl], (sb, LANE)) for _ in range(tv)]
        for m in range(KC - 1 + tv):
            if m < KC - 1:
                xm = hist_ref[m, :, sl]
            else:
                xm = us_ref[j, pl.ds(m - (KC - 1), sb, stride=tv), :]
            for t in range(tv):
                if 0 <= m - t < KC:
                    accs[t] = accs[t] + w_ref[m - t:m - t + 1, sl] * xm
            if m >= tv:
                nh_ref[m - tv, :, sl] = xm
        for t in range(tv):
            y_ref[t, :, sl] = accs[t]
    for t in range(tv):
        y = y_ref[t]
        yc = y - jnp.mean(y, axis=-1, keepdims=True)
        a = _silu(yc * lax.rsqrt(jnp.mean(yc * yc, axis=-1, keepdims=True) + EPS) * lg_ref[...] + lb_ref[...])
        for j in range(nl):
            as_ref[j, pl.ds(t, sb, stride=tv), :] = a[:, j * LANE:(j + 1) * LANE]
    for j in range(nl):
        a_ref[:, j * LANE:(j + 1) * LANE] = as_ref[j]


def _conv_sample(u, hist, w, b, lg, lb, *, row0, tv, sb=32):
    nb = hist.shape[1]
    rows = sb * tv
    par = lambda r: _resident((r, DC))
    blk0 = row0 // rows
    hist_spec = pl.BlockSpec((KC - 1, sb, DC), lambda i: (0, i, 0))
    return pl.pallas_call(
        functools.partial(_conv_sample_kernel, sb=sb, tv=tv),
        grid=(nb // sb,),
        in_specs=[pl.BlockSpec((rows, DC), lambda i: (blk0 + i, 0)), hist_spec, par(KC), par(1), par(1), par(1)],
        out_specs=[pl.BlockSpec((rows, DC), lambda i: (i, 0)), hist_spec],
        out_shape=[jax.ShapeDtypeStruct((nb * tv, DC), F32), jax.ShapeDtypeStruct((KC - 1, nb, DC), F32)],
        scratch_shapes=[pltpu.VMEM((DC // LANE, rows, LANE), F32), pltpu.VMEM((tv, sb, DC), F32),
                        pltpu.VMEM((DC // LANE, rows, LANE), F32)],
        compiler_params=_cparams(("arbitrary",)),
        name="conv_sample",
    )(u, hist, w, b, lg, lb)


def _dot01(a, b, dims, *, data):
    x = b if data else a
    one = (a if data else b).astype(BF16)
    t0 = x.astype(BF16)
    r1 = x - t0.astype(F32)
    t1 = r1.astype(BF16)
    t2 = (r1 - t1.astype(F32)).astype(BF16)
    acc = None
    for t in (t0, t1, t2):
        lhs, rhs = (one, t) if data else (t, one)
        p = lax.dot_general(lhs, rhs, dims, preferred_element_type=F32)
        acc = p if acc is None else acc + p
    return acc


def _to_slabs(xe_ref, r0, x):
    for j in range(x.shape[1] // LANE):
        xe_ref[j, r0:r0 + x.shape[0], :] = x[:, j * LANE:(j + 1) * LANE]


def _from_slabs(xe_ref, r0, rows):
    return jnp.concatenate([xe_ref[j, r0:r0 + rows, :] for j in range(xe_ref.shape[0])], axis=1)


def _ssd_chunk(xe_ref, z, dt_raw, st_ref, cw_ref, cb_ref, dtb_ref, alog_ref, dsk_ref, ng_ref, *, L, tv, lq=None,
               h_io=None):
    mm_dims = (((1,), (0,)), ((), ()))
    lq = L if lq is None else lq
    nsq = L // lq
    off = HALO - (KS - 1)
    cols = []
    for j in range(DX // LANE):
        sl = slice(j * LANE, (j + 1) * LANE)
        acc = jnp.broadcast_to(cb_ref[:, sl], (L, LANE))
        for k in range(KS):
            acc = acc + cw_ref[k:k + 1, sl] * xe_ref[j, off + k:off + k + L, :]
        cols.append(acc)
    xc = _silu(jnp.concatenate(cols, axis=1))
    xs = xc[:, 0:DS]

    lane = lax.broadcasted_iota(jnp.int32, (L, LANE), 1)
    rowi = lax.broadcasted_iota(jnp.int32, (L, LANE), 0)
    xdt = dt_raw + dtb_ref[...]
    dt = jnp.maximum(xdt, 0.0) + jnp.log1p(jnp.exp(-jnp.abs(xdt)))
    dt = jnp.where((lane < NH) & (rowi % lq < tv), dt, 0.0)
    da = dt * (-jnp.exp(alog_ref[...]))

    r2 = lax.broadcasted_iota(jnp.int32, (L, L), 0)
    c2 = lax.broadcasted_iota(jnp.int32, (L, L), 1)
    same = r2 // lq == c2 // lq
    causal = (r2 >= c2) & same
    if nsq == 1:
        a_cum = _dot01(causal.astype(F32), da, mm_dims, data=1)
        a_tot = jnp.broadcast_to(a_cum[L - 1:L, :], (L, LANE))
    else:
        cums = _dot01(jnp.concatenate([causal.astype(F32), same.astype(F32)], axis=0), da, mm_dims, data=1)
        a_cum = cums[0:L]
        a_tot = cums[L:2 * L]

    er = lax.broadcasted_iota(jnp.int32, (LANE, DS), 0)
    ec = lax.broadcasted_iota(jnp.int32, (LANE, DS), 1)
    expand = (ec // HP == er).astype(F32)
    stack = jnp.concatenate([jnp.exp(a_cum), jnp.exp(a_tot - a_cum) * dt, jnp.exp(a_tot[0:SUB])], axis=0)
    stack_x = _dot01(stack, expand, mm_dims, data=0)
    ea_x = stack_x[0:L]
    wend_x = stack_x[L:2 * L]
    cd_x = stack_x[2 * L:2 * L + 1]

    ir = lax.broadcasted_iota(jnp.int32, (LANE, LANE), 0)
    ic = lax.broadcasted_iota(jnp.int32, (LANE, LANE), 1)
    ident = (ir == ic).astype(F32)
    tr = _dot01(ident, jnp.concatenate([dt, a_cum], axis=0), (((1,), (1,)), ((), ())), data=1)
    dt_t = tr[:, 0:L]
    acum_t = tr[:, L:2 * L]

    lane_x = lax.broadcasted_iota(jnp.int32, (L, LANE), 1)
    hpg = NH // NG
    bms = [xc[:, DS + g * NS:DS + (g + 1) * NS] for g in range(NG)]
    cms = [xc[:, DS + NG * NS + g * NS:DS + NG * NS + (g + 1) * NS] for g in range(NG)]
    gss = [slice(g * (DS // NG), (g + 1) * (DS // NG)) for g in range(NG)]
    cbms = [lax.dot_general(cms[g], bms[g], (((1,), (1,)), ((), ())), preferred_element_type=F32)
            for g in range(NG)]
    if h_io is None:
        st_old = [st_ref[:, gss[g]] for g in range(NG)]
        y_off = [jnp.dot(cms[g], st_old[g], preferred_element_type=F32) * ea_x[:, gss[g]] for g in range(NG)]
        st_new = [jnp.dot(bms[g].T, xs[:, gss[g]] * wend_x[:, gss[g]], preferred_element_type=F32)
                  for g in range(NG)]
        for g in range(NG):
            st_ref[:, gss[g]] = st_old[g] * cd_x[:, gss[g]] + st_new[g]
    else:
        rows_g = hpg * HP
        sq = [slice(q * lq, (q + 1) * lq) for q in range(nsq)]
        hs = [[h_io[q][0][g * hpg:(g + 1) * hpg].reshape(rows_g, NS) for g in range(NG)] for q in range(nsq)]
        y_off = [jnp.concatenate([lax.dot_general(cms[g][sq[q]], hs[q][g], (((1,), (1,)), ((), ())),
                                                  preferred_element_type=F32) for q in range(nsq)], axis=0)
                 * ea_x[:, gss[g]] for g in range(NG)]
        xw = [xs[:, gss[g]] * wend_x[:, gss[g]] for g in range(NG)]
        upd = [[lax.dot_general(xw[g][sq[q]], bms[g][sq[q]], (((0,), (0,)), ((), ())),
                                preferred_element_type=F32) for g in range(NG)] for q in range(nsq)]
        seq_decay = jnp.exp(a_tot)
        for q in range(nsq):
            for h in range(NH):
                g, hl = divmod(h, hpg)
                rows = slice(hl * HP, (hl + 1) * HP)
                h_io[q][1][h] = hs[q][g][rows, :] * seq_decay[q * lq:q * lq + 1, h:h + 1] + upd[q][g][rows, :]
    segs = [a_cum[:, h:h + 1] - acum_t[h:h + 1, :] for h in range(NH)]
    decs = [jnp.exp(jnp.where(causal, segs[h], -jnp.inf)) for h in range(NH)]
    ws = [cbms[h // hpg] * decs[h] * dt_t[h:h + 1, :] for h in range(NH)]
    y_diag = []
    for pr in range(NH // 2):
        xp = xs[:, 2 * pr * HP:(2 * pr + 2) * HP]
        rhs = jnp.concatenate([jnp.where(lane_x < HP, xp, 0.0), jnp.where(lane_x >= HP, xp, 0.0)], axis=0)
        y_diag.append(jnp.dot(jnp.concatenate([ws[2 * pr], ws[2 * pr + 1]], axis=1), rhs,
                              preferred_element_type=F32))
    y = jnp.concatenate(y_diag, axis=1) + jnp.concatenate(y_off, axis=1) + dsk_ref[...] * xs
    y = y * _silu(z)
    outs = []
    for g in range(NG):
        gs = slice(g * (DS // NG), (g + 1) * (DS // NG))
        outs.append(_rms(y[:, gs], ng_ref[:, gs]))
    return jnp.concatenate(outs, axis=1)


def _state_out(st_ref, h_ref):
    for j in range(DS // LANE):
        blk = st_ref[:, j * LANE:(j + 1) * LANE].T
        for q in range(LANE // HP):
            h_ref[j * (LANE // HP) + q] = blk[q * HP:(q + 1) * HP, :]


def _ssd_prompt_kernel(xbc_ref, z_ref, dt_ref, cw_ref, cb_ref, dtb_ref, alog_ref, dsk_ref, ng_ref,
                       y_ref, nb_ref, h_ref, xe_ref, st_ref, *, L):
    c = pl.program_id(1)

    @pl.when(c == 0)
    def _():
        xe_ref[:, 0:HALO, :] = jnp.zeros((DX // LANE, HALO, LANE), F32)
        st_ref[...] = jnp.zeros_like(st_ref)

    _to_slabs(xe_ref, HALO, xbc_ref[...])
    y_ref[...] = _ssd_chunk(xe_ref, z_ref[...], dt_ref[...], st_ref, cw_ref, cb_ref, dtb_ref, alog_ref, dsk_ref,
                            ng_ref, L=L, tv=L)
    xe_ref[:, 0:HALO, :] = xe_ref[:, L:L + HALO, :]

    @pl.when(c == pl.num_programs(1) - 1)
    def _():
        nb_ref[...] = _from_slabs(xe_ref, HALO - (KS - 1), KS - 1)
        _state_out(st_ref, h_ref)


def _ssd_params_specs():
    return [_resident((KS, DX)), _resident((1, DX)), _resident((1, LANE)), _resident((1, LANE)),
            _resident((1, DS)), _resident((1, DS))]


def _ssd_prompt(xbc, z, dt, params, *, nb, seq):
    nt = nb * seq
    L = CHUNK
    nper = seq // L
    row = lambda n: pl.BlockSpec((L, n), lambda s, c: (s * nper + c, 0))
    return pl.pallas_call(
        functools.partial(_ssd_prompt_kernel, L=L),
        grid=(nb, nper),
        in_specs=[row(DX), row(DS), row(LANE)] + _ssd_params_specs(),
        out_specs=[row(DS),
                   pl.BlockSpec((None, KS - 1, DX), lambda s, c: (s, 0, 0)),
                   pl.BlockSpec((None, NH, HP, NS), lambda s, c: (s, 0, 0, 0))],
        out_shape=[jax.ShapeDtypeStruct((nt, DS), F32), jax.ShapeDtypeStruct((nb, KS - 1, DX), F32),
                   jax.ShapeDtypeStruct((nb, NH, HP, NS), F32)],
        scratch_shapes=[pltpu.VMEM((DX // LANE, HALO + L, LANE), F32), pltpu.VMEM((NS, DS), F32)],
        compiler_params=_cparams(("parallel", "arbitrary")),
        name="ssd_prompt",
    )(xbc, z, dt, *params)


def _ssd_sample_kernel(xbc_ref, z_ref, dt_ref, hist_ref, h0_ref, cw_ref, cb_ref, dtb_ref, alog_ref, dsk_ref,
                       ng_ref, y_ref, nb_ref, h_ref, xe_ref, zb_ref, dtp_ref, *, lq, tv, nseq):
    assert lq - tv >= KS - 1 and HALO >= KS - 1
    xe_ref[...] = jnp.zeros_like(xe_ref)
    zb_ref[...] = jnp.zeros_like(zb_ref)
    dtp_ref[...] = jnp.zeros_like(dtp_ref)
    for j in range(nseq):
        rows = slice(j * tv, (j + 1) * tv)
        _to_slabs(xe_ref, HALO + j * lq - (KS - 1), hist_ref[j])
        _to_slabs(xe_ref, HALO + j * lq, xbc_ref[rows, :])
        zb_ref[j * lq:j * lq + tv, :] = z_ref[rows, :]
        dtp_ref[j * lq:j * lq + tv, :] = dt_ref[rows, :]
    y = _ssd_chunk(xe_ref, zb_ref[...], dtp_ref[...], None, cw_ref, cb_ref, dtb_ref, alog_ref, dsk_ref, ng_ref,
                   L=nseq * lq, tv=tv, lq=lq, h_io=[(h0_ref.at[j], h_ref.at[j]) for j in range(nseq)])
    for j in range(nseq):
        y_ref[j * tv:(j + 1) * tv, :] = y[j * lq:j * lq + tv, :]
        nb_ref[j] = _from_slabs(xe_ref, HALO + j * lq + tv - (KS - 1), KS - 1)


def _ssd_sample(xbc, z, dt, hist, h0, params, *, row0, tv, nseq=8, lq=SUB):
    nb = hist.shape[0]
    rows = nseq * tv
    L = nseq * lq
    blk0 = row0 // rows
    row = lambda n: pl.BlockSpec((rows, n), lambda i: (blk0 + i, 0))
    return pl.pallas_call(
        functools.partial(_ssd_sample_kernel, lq=lq, tv=tv, nseq=nseq),
        grid=(nb // nseq,),
        in_specs=[row(DX), row(DS), row(LANE),
                  pl.BlockSpec((nseq, KS - 1, DX), lambda i: (i, 0, 0)),
                  pl.BlockSpec((nseq, NH, HP, NS), lambda i: (i, 0, 0, 0))] + _ssd_params_specs(),
        out_specs=[pl.BlockSpec((rows, DS), lambda i: (i, 0)),
                   pl.BlockSpec((nseq, KS - 1, DX), lambda i: (i, 0, 0)),
                   pl.BlockSpec((nseq, NH, HP, NS), lambda i: (i, 0, 0, 0))],
        out_shape=[jax.ShapeDtypeStruct((nb * tv, DS), F32), jax.ShapeDtypeStruct((nb, KS - 1, DX), F32),
                   jax.ShapeDtypeStruct((nb, NH, HP, NS), F32)],
        scratch_shapes=[pltpu.VMEM((DX // LANE, HALO + L, LANE), F32), pltpu.VMEM((L, DS), F32),
                        pltpu.VMEM((L, LANE), F32)],
        compiler_params=_cparams(("arbitrary",)),
        name="ssd_sample",
    )(xbc, z, dt, hist, h0, *params)


def _proj_out_kernel(*refs, n, n_main):
    lhs = refs[0:2 * n]
    ws = refs[2 * n:3 * n]
    x_ref, g_ref, o_ref = refs[3 * n:3 * n + 3]
    is_main = pl.program_id(0) < n_main
    m = None
    for k, w_ref in enumerate(ws):
        a = jnp.where(is_main, lhs[2 * k][...], lhs[2 * k + 1][...]).astype(BF16)
        p = jnp.dot(a, w_ref[...].astype(BF16), preferred_element_type=F32)
        m = p if m is None else m + p
    o_ref[...] = x_ref[...] + _rms(m, g_ref[...])


def _proj_out(lhs_pairs, ws, x, g, *, tm=512):
    nt = x.shape[0]
    n = len(lhs_pairs)
    n_main = nt // tm - 1
    lhs_specs = []
    for a_main, a_extra in lhs_pairs:
        assert a_main.shape[0] == n_main * tm and a_extra.shape[0] == tm
        lhs_specs.append(pl.BlockSpec((tm, a_main.shape[1]), lambda i: (jnp.minimum(i, n_main - 1), 0)))
        lhs_specs.append(pl.BlockSpec((tm, a_extra.shape[1]), lambda i: (0, 0)))
    return pl.pallas_call(
        functools.partial(_proj_out_kernel, n=n, n_main=n_main),
        grid=(nt // tm,),
        in_specs=lhs_specs + [_resident(w.shape) for w in ws]
                 + [pl.BlockSpec((tm, D), lambda i: (i, 0)), _resident((1, D))],
        out_specs=pl.BlockSpec((tm, D), lambda i: (i, 0)),
        out_shape=jax.ShapeDtypeStruct((nt, D), F32),
        compiler_params=_cparams(("arbitrary",)),
        name="proj_out",
    )(*[a for pair in lhs_pairs for a in pair], *ws, x, g)


def _norm_proj_kernel(*refs, n, heads):
    x_ref, g_ref = refs[0:2]
    ws = refs[2:2 + n]
    outs = refs[2 + n:2 + 2 * n]
    hn = _rms(x_ref[...], g_ref[...]).astype(BF16)
    for w_ref, o_ref in zip(ws, outs):
        for h in range(XH):
            sl = slice(h * XD, (h + 1) * XD)
            r = jnp.dot(hn, w_ref[:, sl].astype(BF16), preferred_element_type=F32).astype(o_ref.dtype)
            if heads:
                o_ref[:, h, :] = r
            else:
                o_ref[:, sl] = r


def _norm_proj(x, g, ws, out_dtype, *, heads=False, tm=512):
    nt = x.shape[0]
    n = len(ws)
    if heads:
        tm = NM
        out_specs = [pl.BlockSpec((None, NM, XH, XD), lambda i: (i, 0, 0, 0))] * n
        out_shape = [jax.ShapeDtypeStruct((nt // NM, NM, XH, XD), out_dtype)] * n
    else:
        out_specs = [pl.BlockSpec((tm, D), lambda i: (i, 0))] * n
        out_shape = [jax.ShapeDtypeStruct((nt, D), out_dtype)] * n
    return pl.pallas_call(
        functools.partial(_norm_proj_kernel, n=n, heads=heads),
        grid=(nt // tm,),
        in_specs=[pl.BlockSpec((tm, D), lambda i: (i, 0)), _resident((1, D))] + [_resident((D, D))] * n,
        out_specs=out_specs,
        out_shape=out_shape,
        compiler_params=_cparams(("parallel",)),
        name="norm_proj",
    )(x, g, *ws)


NLT = XD // LANE
LT_STRIDE = NLT * XH
KV_ROWS = NM * LT_STRIDE


def _kv_tiles(x):
    nb = x.shape[0]
    return x.reshape(nb, NM, XH, NLT, LANE).transpose(0, 1, 3, 2, 4).reshape(nb, KV_ROWS, LANE)


def _attn_prompt_kernel(q_ref, k_ref, v_ref, o_ref, kh_ref, vh_ref):
    @pl.when(pl.program_id(1) == 0)
    def _():
        for h in range(XH):
            for lt in range(NLT):
                rows = pl.ds(lt * XH + h, NM, stride=LT_STRIDE)
                kh_ref[h, :, lt * LANE:(lt + 1) * LANE] = k_ref[0, rows, :].astype(BF16)
                vh_ref[h, :, lt * LANE:(lt + 1) * LANE] = v_ref[0, rows, :].astype(BF16)

    scale = XD ** -0.5
    nt_dims = (((1,), (1,)), ((), ()))
    heads = range(XH)
    s = [lax.dot_general(q_ref[:, h * XD:(h + 1) * XD], kh_ref[h], nt_dims, preferred_element_type=F32) * scale
         for h in heads]
    e = [jnp.exp(s[h] - jnp.max(s[h], axis=-1, keepdims=True)) for h in heads]
    p = [(e[h] / jnp.sum(e[h], axis=-1, keepdims=True)).astype(BF16) for h in heads]
    for h in heads:
        o_ref[:, h * XD:(h + 1) * XD] = jnp.dot(p[h], vh_ref[h], preferred_element_type=F32).astype(o_ref.dtype)


def _group_sum(x, col_lt):
    n = x.shape[1]
    a = x + jnp.where(col_lt % 2 == 0, pltpu.roll(x, n - 1, axis=1), pltpu.roll(x, 1, axis=1))
    return a + jnp.where(col_lt < 2, pltpu.roll(a, n - 2, axis=1), pltpu.roll(a, 2, axis=1))


def _attn_sample_kernel(q_ref, k_ref, v_ref, o_ref, *, nseq, tv):
    assert NLT == 4
    tq = q_ref.shape[0]
    ncol = NM * NLT
    scale = XD ** -0.5
    nt_dims = (((1,), (1,)), ((), ()))
    col_lt = lax.broadcasted_iota(jnp.int32, (tq, ncol), 1) % NLT
    rowi = lax.broadcasted_iota(jnp.int32, (tq, LANE), 0)
    for h in range(XH):
        qp = jnp.concatenate([q_ref[:, h * XD + lt * LANE:h * XD + (lt + 1) * LANE].astype(F32)
                              for lt in range(NLT)], axis=0)
        seqs = range(nseq)
        g = [lax.dot_general(qp, k_ref[j, pl.ds(h, ncol, stride=XH), :], nt_dims, preferred_element_type=F32)
             for j in seqs]
        s4 = [sum(jnp.where(col_lt == lt, g[j][lt * tq:(lt + 1) * tq], 0.0) for lt in range(NLT)) for j in seqs]
        s = [_group_sum(s4[j], col_lt) * scale for j in seqs]
        e = [jnp.exp(s[j] - jnp.max(s[j], axis=-1, keepdims=True)) for j in seqs]
        p = [e[j] / (jnp.sum(e[j], axis=-1, keepdims=True) * (1.0 / NLT)) for j in seqs]
        o = [jnp.dot(jnp.concatenate([jnp.where(col_lt == lt, p[j], 0.0) for lt in range(NLT)], axis=0),
                     v_ref[j, pl.ds(h, ncol, stride=XH), :], preferred_element_type=F32) for j in seqs]
        acc = [None] * NLT
        for j in seqs:
            mine = (rowi >= j * tv) & (rowi < (j + 1) * tv)
            for lt in range(NLT):
                acc[lt] = jnp.where(mine, o[j][lt * tq:(lt + 1) * tq], 0.0 if acc[lt] is None else acc[lt])
        for lt in range(NLT):
            o_ref[:, h * XD + lt * LANE:h * XD + (lt + 1) * LANE] = acc[lt].astype(o_ref.dtype)


def _attn_prompt(q, k, v, *, nb, seq, tq=512):
    nt = nb * seq
    nper = seq // tq
    kv_spec = pl.BlockSpec((1, KV_ROWS, LANE), lambda s, t: (s, 0, 0))
    return pl.pallas_call(
        _attn_prompt_kernel,
        grid=(nb, nper),
        in_specs=[pl.BlockSpec((tq, D), lambda s, t: (s * nper + t, 0)), kv_spec, kv_spec],
        out_specs=pl.BlockSpec((tq, D), lambda s, t: (s * nper + t, 0)),
        out_shape=jax.ShapeDtypeStruct((nt, D), BF16),
        scratch_shapes=[pltpu.VMEM((XH, NM, XD), BF16), pltpu.VMEM((XH, NM, XD), BF16)],
        compiler_params=_cparams(("arbitrary", "arbitrary")),
        name="attn_prompt",
    )(q, _kv_tiles(k), _kv_tiles(v))


def _attn_sample(q, k, v, *, row0, tv, nseq=4):
    nb = k.shape[0]
    rows = nseq * tv
    blk0 = row0 // rows
    kv_spec = pl.BlockSpec((nseq, KV_ROWS, LANE), lambda i: (i, 0, 0))
    return pl.pallas_call(
        functools.partial(_attn_sample_kernel, nseq=nseq, tv=tv),
        grid=(nb // nseq,),
        in_specs=[pl.BlockSpec((rows, D), lambda i: (blk0 + i, 0)), kv_spec, kv_spec],
        out_specs=pl.BlockSpec((rows, D), lambda i: (i, 0)),
        out_shape=jax.ShapeDtypeStruct((nb * tv, D), BF16),
        compiler_params=_cparams(("arbitrary",)),
        name="attn_sample",
    )(q, _kv_tiles(k), _kv_tiles(v))


def _row(v):
    return v.reshape(1, -1).astype(F32)


def _pad_lanes(v, n):
    return jnp.pad(v.reshape(1, -1).astype(F32), ((0, 0), (0, n - v.size)))


def _layer(xs, mem, cache_k, cache_v, st_conv, st_sconv, st_ssm, p, *, nbp, seq, nbs, tv, split_out):
    npr = nbp * seq
    bf = lambda w: w.astype(BF16)

    side = [(p['ffn2_w_gate'], 1), (p['ffn2_w_up'], 1), (p['ffn2_w_down'], 0),
            (p['w_in'].T, 0),
            (p['w_out'], 0)]
    x, wg2, wu2, wd2, w_in, w_out = _ffn(xs, _row(p['ffn1_pre_g']), bf(p['ffn1_w_gate']), bf(p['ffn1_w_up']),
                                         bf(p['ffn1_w_down']), _row(p['ffn1_post_g']), split_out=False,
                                         tf=FF_TILE // 2, side=side)

    wt = jnp.pad(w_in[2 * DC + DS + DX:], ((0, LANE - NH), (0, 0)))
    u, z, xbc, dtr = _mix_in(x, _row(p['mix_pre_g']), w_in, wt)

    cpar = (p['conv_w'].astype(F32), _row(p['conv_b']), _row(p['conv_ln_g']), _row(p['conv_ln_b']))
    a_p, conv_p = _conv_prompt(u, *cpar, nb=nbp, seq=seq)
    a_s, conv_s = _conv_sample(u, jnp.transpose(st_conv, (1, 0, 2)), *cpar, row0=npr, tv=tv)
    conv_s = jnp.transpose(conv_s, (1, 0, 2))

    spar = (p['ssm_conv_w'].astype(F32), _row(p['ssm_conv_b']), _pad_lanes(p['dt_bias'], LANE),
            _pad_lanes(p['a_log'], LANE), jnp.repeat(p['d_skip'].astype(F32), HP).reshape(1, DS),
            _row(p['ssm_norm_g']))
    y_p, sconv_p, ssm_p = _ssd_prompt(xbc, z, dtr, spar, nb=nbp, seq=seq)
    y_s, sconv_s, ssm_s = _ssd_sample(xbc, z, dtr, st_sconv, st_ssm, spar, row0=npr, tv=tv)

    x = _proj_out([(a_p, a_s), (y_p, y_s)], [w_out[:DC], w_out[DC:]], x, _row(p['mix_post_g']))

    (q,) = _norm_proj(x, _row(p['xattn_pre_g']), [p['w_xq']], BF16)
    mk, mv = _norm_proj(mem, _row(p['mem_norm_g']), [p['w_xk'], p['w_xv']], F32, heads=True)
    o_p = _attn_prompt(q, mk, mv, nb=nbp, seq=seq)
    o_s = _attn_sample(q, cache_k, cache_v, row0=npr, tv=tv)
    x = _proj_out([(o_p, o_s)], [p['w_xo']], x, _row(p['xattn_post_g']))

    x = _ffn([x], _row(p['ffn2_pre_g']), wg2, wu2, wd2, _row(p['ffn2_post_g']), split_out=split_out)
    return x, (mk, mv, conv_p, sconv_p, ssm_p, conv_s, sconv_s, ssm_s)


def kernel(x_prompt, x_sample, mem_prompt, cache_mem_k, cache_mem_v, state_conv, state_ssm_conv, state_ssm, ffn1_pre_g, ffn1_w_gate, ffn1_w_up, ffn1_w_down, ffn1_post_g, mix_pre_g, w_in, conv_w, conv_b, conv_ln_g, conv_ln_b, ssm_conv_w, ssm_conv_b, dt_bias, a_log, d_skip, ssm_norm_g, w_out, mix_post_g, xattn_pre_g, mem_norm_g, w_xq, w_xk, w_xv, w_xo, xattn_post_g, ffn2_pre_g, ffn2_w_gate, ffn2_w_up, ffn2_w_down, ffn2_post_g):
    params = dict(ffn1_pre_g=ffn1_pre_g, ffn1_w_gate=ffn1_w_gate, ffn1_w_up=ffn1_w_up, ffn1_w_down=ffn1_w_down,
                  ffn1_post_g=ffn1_post_g, mix_pre_g=mix_pre_g, w_in=w_in, conv_w=conv_w, conv_b=conv_b,
                  conv_ln_g=conv_ln_g, conv_ln_b=conv_ln_b, ssm_conv_w=ssm_conv_w, ssm_conv_b=ssm_conv_b,
                  dt_bias=dt_bias, a_log=a_log, d_skip=d_skip, ssm_norm_g=ssm_norm_g, w_out=w_out,
                  mix_post_g=mix_post_g, xattn_pre_g=xattn_pre_g, mem_norm_g=mem_norm_g, w_xq=w_xq,
                  w_xk=w_xk, w_xv=w_xv, w_xo=w_xo, xattn_post_g=xattn_post_g, ffn2_pre_g=ffn2_pre_g,
                  ffn2_w_gate=ffn2_w_gate, ffn2_w_up=ffn2_w_up, ffn2_w_down=ffn2_w_down, ffn2_post_g=ffn2_post_g)
    depth = ffn1_pre_g.shape[0]
    nbp, seq, _ = x_prompt.shape
    nbs, tv, _ = x_sample.shape
    npr = nbp * seq
    xs = [x_prompt.reshape(npr, D), x_sample.reshape(nbs * tv, D)]
    mem = mem_prompt.reshape(nbp * NM, D)
    per_layer = []
    for layer in range(depth):
        p = {name: w[layer] for name, w in params.items()}
        x, states = _layer(xs, mem, cache_mem_k[layer], cache_mem_v[layer],
                           state_conv[layer], state_ssm_conv[layer], state_ssm[layer], p,
                           nbp=nbp, seq=seq, nbs=nbs, tv=tv, split_out=layer == depth - 1)
        xs = [x]
        per_layer.append(states)
    mk, mv, conv_p, sconv_p, ssm_p, conv_s, sconv_s, ssm_s = [jnp.stack(t) for t in zip(*per_layer)]
    yp, ys = x
    return (yp.reshape(nbp, seq, D), ys.reshape(nbs, tv, D), mk, mv, conv_p, sconv_p, ssm_p, conv_s, sconv_s, ssm_s)
```

```python
import functools

import jax
import jax.numpy as jnp
from jax import lax
from jax.experimental import pallas as pl
from jax.experimental.pallas import tpu as pltpu

F32 = jnp.float32
BF16 = jnp.bfloat16

D = 2048
FF = 5504
DC = 1024
DS = 1024
KC = 31
NH = 16
HP = 64
NG = 2
NS = 128
KS = 4
DX = DS + 2 * NG * NS
CHUNK = 128
NM = 256
XH = 4
XD = D // XH
EPS = 1e-6

LANE = 128
SUB = 8
HIST = 32
HALO = 8

FF_TILE = 1024
SIDE_WIDE = 3 * LANE

VMEM_LIMIT = 60 * 1024 * 1024


def _cparams(sem):
    return pltpu.CompilerParams(dimension_semantics=sem, vmem_limit_bytes=VMEM_LIMIT)


def _rms(x, g):
    return x * lax.rsqrt(jnp.mean(x * x, axis=-1, keepdims=True) + EPS) * g


def _silu(x):
    return x * jax.nn.sigmoid(x)


def _resident(shape):
    return pl.BlockSpec(shape, lambda *_: (0,) * len(shape), pipeline_mode=pl.Buffered(1))


def _ffn_kernel(*refs, n_in, n_out, n_main, tf, side_plan):
    n_side = len(side_plan)
    x_refs = refs[:n_in]
    pg_ref, wg_ref, wu_ref, wd_ref, qg_ref = refs[n_in:n_in + 5]
    side_in = refs[n_in + 5:n_in + 5 + n_side]
    k = n_in + 5 + n_side
    o_refs = refs[k:k + n_out]
    side_out = refs[k + n_out:k + n_out + n_side]
    xn_ref, acc_ref = refs[k + n_out + n_side:]
    i = pl.program_id(0)
    f = pl.program_id(1)
    last = pl.num_programs(1) - 1

    def x_tile():
        if n_in == 1:
            return x_refs[0][...]
        return jnp.where(i < n_main, x_refs[0][...], x_refs[1][...])

    @pl.when(f == 0)
    def _():
        xn_ref[...] = _rms(x_tile(), pg_ref[...]).astype(BF16)
        acc_ref[...] = jnp.zeros_like(acc_ref)

    def hidden_tile(width):
        xn = xn_ref[...]
        h = jnp.dot(xn, wg_ref[:, 0:width], preferred_element_type=F32)
        u = jnp.dot(xn, wu_ref[:, 0:width], preferred_element_type=F32)
        a = (_silu(h) * u).astype(BF16)
        acc_ref[...] += jnp.dot(a, wd_ref[0:width, :], preferred_element_type=F32)

    @pl.when(f < last)
    def _():
        hidden_tile(tf)

    @pl.when(f == last)
    def _():
        hidden_tile(FF - (FF // tf) * tf)
        res = x_tile() + 0.5 * _rms(acc_ref[...], qg_ref[...])
        if n_out == 1:
            o_refs[0][...] = res
        else:
            @pl.when(i < n_main)
            def _():
                o_refs[0][...] = res

            @pl.when(i >= n_main)
            def _():
                o_refs[1][...] = res

    _side_cast(i * pl.num_programs(1) + f, side_plan, side_in, side_out)


def _side_plan(side):
    plan, start = [], 0
    for w, axis, width in side:
        n = pl.cdiv(w.shape[axis], width)
        plan.append((start, n))
        start += n
    return tuple(plan), start


def _side_specs(side, step_of):
    specs = []
    for (w, axis, width), (start, n) in zip(side, _side_plan(side)[0]):
        strip = lambda *idx, start=start, n=n: jnp.clip(step_of(*idx) - start, 0, n - 1)
        if axis == 0:
            specs.append(pl.BlockSpec((width, w.shape[1]), lambda *idx, strip=strip: (strip(*idx), 0)))
        else:
            specs.append(pl.BlockSpec((w.shape[0], width), lambda *idx, strip=strip: (0, strip(*idx))))
    return specs


def _side_cast(step, side_plan, side_in, side_out):
    for m, (start, n) in enumerate(side_plan):
        @pl.when((step >= start) & (step < start + n))
        def _(m=m):
            side_out[m][...] = side_in[m][...].astype(BF16)


def _side_shapes(side):
    return [jax.ShapeDtypeStruct(w.shape, BF16) for w, _, _ in side]


def _ffn(xs, pre_g, wg, wu, wd, post_g, *, split_out, tm=512, tf=FF_TILE, side=()):
    n_in = len(xs)
    nt = sum(x.shape[0] for x in xs)
    n_main = (nt - tm) // tm
    nf = pl.cdiv(FF, tf)
    side_plan, side_steps = _side_plan(side)
    assert side_steps <= (nt // tm) * nf
    main = lambda i, f: (jnp.minimum(i, n_main - 1), 0)
    extra = lambda i, f: (0, 0)
    whole = lambda i, f: (i, 0)
    if n_in == 1:
        x_specs = [pl.BlockSpec((tm, D), whole)]
    else:
        x_specs = [pl.BlockSpec((tm, D), main), pl.BlockSpec((tm, D), extra)]
    if split_out:
        out_specs = [pl.BlockSpec((tm, D), main), pl.BlockSpec((tm, D), extra)]
        out_shape = [jax.ShapeDtypeStruct((n_main * tm, D), F32), jax.ShapeDtypeStruct((tm, D), F32)]
    else:
        out_specs = [pl.BlockSpec((tm, D), whole)]
        out_shape = [jax.ShapeDtypeStruct((nt, D), F32)]
    res = pl.pallas_call(
        functools.partial(_ffn_kernel, n_in=n_in, n_out=len(out_specs), n_main=n_main, tf=tf, side_plan=side_plan),
        grid=(nt // tm, nf),
        in_specs=x_specs + [
            pl.BlockSpec((1, D), lambda i, f: (0, 0)),
            pl.BlockSpec((D, tf), lambda i, f: (0, f)),
            pl.BlockSpec((D, tf), lambda i, f: (0, f)),
            pl.BlockSpec((tf, D), lambda i, f: (f, 0)),
            pl.BlockSpec((1, D), lambda i, f: (0, 0)),
        ] + _side_specs(side, lambda i, f: i * nf + f),
        out_specs=out_specs + _side_specs(side, lambda i, f: i * nf + f),
        out_shape=out_shape + _side_shapes(side),
        scratch_shapes=[pltpu.VMEM((tm, D), BF16), pltpu.VMEM((tm, D), F32)],
        compiler_params=_cparams(("arbitrary", "arbitrary")),
        name="ffn",
    )(*xs, pre_g, wg, wu, wd, post_g, *[w for w, _, _ in side])
    main_res = res[0] if len(out_specs) == 1 else tuple(res[:2])
    return (main_res, *res[len(out_specs):]) if side else main_res


def _mix_in_kernel(x_ref, g_ref, w_ref, wt_ref, u_ref, z_ref, xbc_ref, dt_ref, *, tn):
    hn = _rms(x_ref[...], g_ref[...]).astype(BF16)
    nt_dims = (((1,), (1,)), ((), ()))

    def cols(start, c):
        return lax.dot_general(hn, w_ref[start + c * tn:start + (c + 1) * tn, :], nt_dims,
                               preferred_element_type=F32)

    for c in range(DC // tn):
        u_ref[:, c * tn:(c + 1) * tn] = cols(0, c) * jax.nn.sigmoid(cols(DC, c))
    for c in range(DS // tn):
        z_ref[:, c * tn:(c + 1) * tn] = cols(2 * DC, c)
    for c in range(DX // tn):
        xbc_ref[:, c * tn:(c + 1) * tn] = cols(2 * DC + DS, c)
    dt_ref[...] = lax.dot_general(hn, wt_ref[...], nt_dims, preferred_element_type=F32)


def _mix_in(x, g, w, wt, *, tm=512, tn=512):
    nt = x.shape[0]
    row = lambda n: pl.BlockSpec((tm, n), lambda i: (i, 0))
    return pl.pallas_call(
        functools.partial(_mix_in_kernel, tn=tn),
        grid=(nt // tm,),
        in_specs=[row(D), _resident((1, D)), _resident(w.shape), _resident((LANE, D))],
        out_specs=[row(DC), row(DS), row(DX), row(LANE)],
        out_shape=[jax.ShapeDtypeStruct((nt, n), F32) for n in (DC, DS, DX, LANE)],
        compiler_params=_cparams(("parallel",)),
        name="mix_in",
    )(x, g, w, wt)


def _conv_prompt_kernel(u_ref, w_ref, b_ref, lg_ref, lb_ref, a_ref, nb_ref, xe_ref, y_ref, *, tt):
    t = pl.program_id(1)
    off = HIST - (KC - 1)

    @pl.when(t == 0)
    def _():
        xe_ref[:, 0:HIST, :] = jnp.zeros((DC // LANE, HIST, LANE), F32)

    for j in range(DC // LANE):
        xe_ref[j, HIST:HIST + tt, :] = u_ref[:, j * LANE:(j + 1) * LANE]
    for j in range(DC // LANE):
        sl = slice(j * LANE, (j + 1) * LANE)
        acc = jnp.broadcast_to(b_ref[:, sl], (tt, LANE))
        for k in range(KC):
            acc = acc + w_ref[k:k + 1, sl] * xe_ref[j, off + k:off + k + tt, :]
        y_ref[:, sl] = acc
    for j in range(DC // LANE):
        nb_ref[:, j * LANE:(j + 1) * LANE] = xe_ref[j, HIST + tt - (KC - 1):HIST + tt, :]
        xe_ref[j, 0:HIST, :] = xe_ref[j, tt:tt + HIST, :]
    @pl.when(t >= 0)
    def _():
        y = y_ref[...]
        yc = y - jnp.mean(y, axis=-1, keepdims=True)
        a_ref[...] = _silu(yc * lax.rsqrt(jnp.mean(yc * yc, axis=-1, keepdims=True) + EPS) * lg_ref[...]
                           + lb_ref[...])


def _conv_prompt(u, w, b, lg, lb, *, nb, seq, tt=256):
    nt = nb * seq
    nper = seq // tt
    par = lambda r: _resident((r, DC))
    return pl.pallas_call(
        functools.partial(_conv_prompt_kernel, tt=tt),
        grid=(nb, nper),
        in_specs=[pl.BlockSpec((tt, DC), lambda s, t: (s * nper + t, 0)), par(KC), par(1), par(1), par(1)],
        out_specs=[pl.BlockSpec((tt, DC), lambda s, t: (s * nper + t, 0)),
                   pl.BlockSpec((None, KC - 1, DC), lambda s, t: (s, 0, 0))],
        out_shape=[jax.ShapeDtypeStruct((nt, DC), F32), jax.ShapeDtypeStruct((nb, KC - 1, DC), F32)],
        scratch_shapes=[pltpu.VMEM((DC // LANE, HIST + tt, LANE), F32), pltpu.VMEM((tt, DC), F32)],
        compiler_params=_cparams(("parallel", "arbitrary")),
        name="conv_prompt",
    )(u, w, b, lg, lb)


def _conv_sample_kernel(u_ref, hist_ref, w_ref, b_ref, lg_ref, lb_ref, a_ref, nh_ref, us_ref, y_ref, as_ref,
                        *, sb, tv):
    nl = DC // LANE
    for j in range(nl):
        us_ref[j] = u_ref[:, j * LANE:(j + 1) * LANE]
    for j in range(nl):
        sl = slice(j * LANE, (j + 1) * LANE)
        accs = [jnp.broadcast_to(b_ref[:, sl], (sb, LANE)) for _ in range(tv)]
        for m in range(KC - 1 + tv):
            if m < KC - 1:
                xm = hist_ref[m, :, sl]
            else:
                xm = us_ref[j, pl.ds(m - (KC - 1), sb, stride=tv), :]
            for t in range(tv):
                if 0 <= m - t < KC:
                    accs[t] = accs[t] + w_ref[m - t:m - t + 1, sl] * xm
            if m >= tv:
                nh_ref[m - tv, :, sl] = xm
        for t in range(tv):
            y_ref[t, :, sl] = accs[t]
    for t in range(tv):
        y = y_ref[t]
        yc = y - jnp.mean(y, axis=-1, keepdims=True)
        a = _silu(yc * lax.rsqrt(jnp.mean(yc * yc, axis=-1, keepdims=True) + EPS) * lg_ref[...] + lb_ref[...])
        for j in range(nl):
            as_ref[j, pl.ds(t, sb, stride=tv), :] = a[:, j * LANE:(j + 1) * LANE]
    for j in range(nl):
        a_ref[:, j * LANE:(j + 1) * LANE] = as_ref[j]


def _conv_sample(u, hist, w, b, lg, lb, *, row0, tv, sb=32):
    nb = hist.shape[1]
    rows = sb * tv
    par = lambda r: _resident((r, DC))
    blk0 = row0 // rows
    hist_spec = pl.BlockSpec((KC - 1, sb, DC), lambda i: (0, i, 0))
    return pl.pallas_call(
        functools.partial(_conv_sample_kernel, sb=sb, tv=tv),
        grid=(nb // sb,),
        in_specs=[pl.BlockSpec((rows, DC), lambda i: (blk0 + i, 0)), hist_spec, par(KC), par(1), par(1), par(1)],
        out_specs=[pl.BlockSpec((rows, DC), lambda i: (i, 0)), hist_spec],
        out_shape=[jax.ShapeDtypeStruct((nb * tv, DC), F32), jax.ShapeDtypeStruct((KC - 1, nb, DC), F32)],
        scratch_shapes=[pltpu.VMEM((DC // LANE, rows, LANE), F32), pltpu.VMEM((tv, sb, DC), F32),
                        pltpu.VMEM((DC // LANE, rows, LANE), F32)],
        compiler_params=_cparams(("arbitrary",)),
        name="conv_sample",
    )(u, hist, w, b, lg, lb)


def _dot01(a, b, dims, *, data):
    x = b if data else a
    one = (a if data else b).astype(BF16)
    t0 = x.astype(BF16)
    r1 = x - t0.astype(F32)
    t1 = r1.astype(BF16)
    t2 = (r1 - t1.astype(F32)).astype(BF16)
    acc = None
    for t in (t0, t1, t2):
        lhs, rhs = (one, t) if data else (t, one)
        p = lax.dot_general(lhs, rhs, dims, preferred_element_type=F32)
        acc = p if acc is None else acc + p
    return acc


def _to_slabs(xe_ref, r0, x):
    for j in range(x.shape[1] // LANE):
        xe_ref[j, r0:r0 + x.shape[0], :] = x[:, j * LANE:(j + 1) * LANE]


def _from_slabs(xe_ref, r0, rows):
    return jnp.concatenate([xe_ref[j, r0:r0 + rows, :] for j in range(xe_ref.shape[0])], axis=1)


def _ssd_chunk(xe_ref, z, dt_raw, st_ref, cw_ref, cb_ref, dtb_ref, alog_ref, dsk_ref, ng_ref, *, L, tv, lq=None,
               h_io=None):
    mm_dims = (((1,), (0,)), ((), ()))
    lq = L if lq is None else lq
    nsq = L // lq
    off = HALO - (KS - 1)
    cols = []
    for j in range(DX // LANE):
        sl = slice(j * LANE, (j + 1) * LANE)
        acc = jnp.broadcast_to(cb_ref[:, sl], (L, LANE))
        for k in range(KS):
            acc = acc + cw_ref[k:k + 1, sl] * xe_ref[j, off + k:off + k + L, :]
        cols.append(acc)
    xc = _silu(jnp.concatenate(cols, axis=1))
    xs = xc[:, 0:DS]

    lane = lax.broadcasted_iota(jnp.int32, (L, LANE), 1)
    rowi = lax.broadcasted_iota(jnp.int32, (L, LANE), 0)
    xdt = dt_raw + dtb_ref[...]
    dt = jnp.maximum(xdt, 0.0) + jnp.log1p(jnp.exp(-jnp.abs(xdt)))
    dt = jnp.where((lane < NH) & (rowi % lq < tv), dt, 0.0)
    da = dt * (-jnp.exp(alog_ref[...]))

    r2 = lax.broadcasted_iota(jnp.int32, (L, L), 0)
    c2 = lax.broadcasted_iota(jnp.int32, (L, L), 1)
    same = r2 // lq == c2 // lq
    causal = (r2 >= c2) & same
    if nsq == 1:
        a_cum = _dot01(causal.astype(F32), da, mm_dims, data=1)
        a_tot = jnp.broadcast_to(a_cum[L - 1:L, :], (L, LANE))
    else:
        cums = _dot01(jnp.concatenate([causal.astype(F32), same.astype(F32)], axis=0), da, mm_dims, data=1)
        a_cum = cums[0:L]
        a_tot = cums[L:2 * L]

    er = lax.broadcasted_iota(jnp.int32, (LANE, DS), 0)
    ec = lax.broadcasted_iota(jnp.int32, (LANE, DS), 1)
    expand = (ec // HP == er).astype(F32)
    stack = jnp.concatenate([jnp.exp(a_cum), jnp.exp(a_tot - a_cum) * dt, jnp.exp(a_tot[0:SUB])], axis=0)
    stack_x = _dot01(stack, expand, mm_dims, data=0)
    ea_x = stack_x[0:L]
    wend_x = stack_x[L:2 * L]
    cd_x = stack_x[2 * L:2 * L + 1]

    ir = lax.broadcasted_iota(jnp.int32, (LANE, LANE), 0)
    ic = lax.broadcasted_iota(jnp.int32, (LANE, LANE), 1)
    ident = (ir == ic).astype(F32)
    tr = _dot01(ident, jnp.concatenate([dt, a_cum], axis=0), (((1,), (1,)), ((), ())), data=1)
    dt_t = tr[:, 0:L]
    acum_t = tr[:, L:2 * L]

    lane_x = lax.broadcasted_iota(jnp.int32, (L, LANE), 1)
    hpg = NH // NG
    bms = [xc[:, DS + g * NS:DS + (g + 1) * NS] for g in range(NG)]
    cms = [xc[:, DS + NG * NS + g * NS:DS + NG * NS + (g + 1) * NS] for g in range(NG)]
    gss = [slice(g * (DS // NG), (g + 1) * (DS // NG)) for g in range(NG)]
    cbms = [lax.dot_general(cms[g], bms[g], (((1,), (1,)), ((), ())), preferred_element_type=F32)
            for g in range(NG)]
    if h_io is None:
        st_old = [st_ref[:, gss[g]] for g in range(NG)]
        y_off = [jnp.dot(cms[g], st_old[g], preferred_element_type=F32) * ea_x[:, gss[g]] for g in range(NG)]
        st_new = [jnp.dot(bms[g].T, xs[:, gss[g]] * wend_x[:, gss[g]], preferred_element_type=F32)
                  for g in range(NG)]
        for g in range(NG):
            st_ref[:, gss[g]] = st_old[g] * cd_x[:, gss[g]] + st_new[g]
    else:
        rows_g = hpg * HP
        sq = [slice(q * lq, (q + 1) * lq) for q in range(nsq)]
        hs = [[h_io[q][0][g * hpg:(g + 1) * hpg].reshape(rows_g, NS) for g in range(NG)] for q in range(nsq)]
        y_off = [jnp.concatenate([lax.dot_general(cms[g][sq[q]], hs[q][g], (((1,), (1,)), ((), ())),
                                                  preferred_element_type=F32) for q in range(nsq)], axis=0)
                 * ea_x[:, gss[g]] for g in range(NG)]
        xw = [xs[:, gss[g]] * wend_x[:, gss[g]] for g in range(NG)]
        upd = [[lax.dot_general(xw[g][sq[q]], bms[g][sq[q]], (((0,), (0,)), ((), ())),
                                preferred_element_type=F32) for g in range(NG)] for q in range(nsq)]
        seq_decay = jnp.exp(a_tot)
        for q in range(nsq):
            for h in range(NH):
                g, hl = divmod(h, hpg)
                rows = slice(hl * HP, (hl + 1) * HP)
                h_io[q][1][h] = hs[q][g][rows, :] * seq_decay[q * lq:q * lq + 1, h:h + 1] + upd[q][g][rows, :]
    segs = [a_cum[:, h:h + 1] - acum_t[h:h + 1, :] for h in range(NH)]
    decs = [jnp.exp(jnp.where(causal, segs[h], -jnp.inf)) for h in range(NH)]
    ws = [cbms[h // hpg] * decs[h] * dt_t[h:h + 1, :] for h in range(NH)]
    y_diag = []
    for pr in range(NH // 2):
        xp = xs[:, 2 * pr * HP:(2 * pr + 2) * HP]
        rhs = jnp.concatenate([jnp.where(lane_x < HP, xp, 0.0), jnp.where(lane_x >= HP, xp, 0.0)], axis=0)
        y_diag.append(jnp.dot(jnp.concatenate([ws[2 * pr], ws[2 * pr + 1]], axis=1), rhs,
                              preferred_element_type=F32))
    y = jnp.concatenate(y_diag, axis=1) + jnp.concatenate(y_off, axis=1) + dsk_ref[...] * xs
    y = y * _silu(z)
    outs = []
    for g in range(NG):
        gs = slice(g * (DS // NG), (g + 1) * (DS // NG))
        outs.append(_rms(y[:, gs], ng_ref[:, gs]))
    return jnp.concatenate(outs, axis=1)


def _state_out(st_ref, h_ref):
    for j in range(DS // LANE):
        blk = st_ref[:, j * LANE:(j + 1) * LANE].T
        for q in range(LANE // HP):
            h_ref[j * (LANE // HP) + q] = blk[q * HP:(q + 1) * HP, :]


def _ssd_prompt_kernel(xbc_ref, z_ref, dt_ref, cw_ref, cb_ref, dtb_ref, alog_ref, dsk_ref, ng_ref,
                       y_ref, nb_ref, h_ref, xe_ref, st_ref, *, L):
    c = pl.program_id(1)

    @pl.when(c == 0)
    def _():
        xe_ref[:, 0:HALO, :] = jnp.zeros((DX // LANE, HALO, LANE), F32)
        st_ref[...] = jnp.zeros_like(st_ref)

    _to_slabs(xe_ref, HALO, xbc_ref[...])
    y_ref[...] = _ssd_chunk(xe_ref, z_ref[...], dt_ref[...], st_ref, cw_ref, cb_ref, dtb_ref, alog_ref, dsk_ref,
                            ng_ref, L=L, tv=L)
    xe_ref[:, 0:HALO, :] = xe_ref[:, L:L + HALO, :]

    @pl.when(c == pl.num_programs(1) - 1)
    def _():
        nb_ref[...] = _from_slabs(xe_ref, HALO - (KS - 1), KS - 1)
        _state_out(st_ref, h_ref)


def _ssd_params_specs():
    return [_resident((KS, DX)), _resident((1, DX)), _resident((1, LANE)), _resident((1, LANE)),
            _resident((1, DS)), _resident((1, DS))]


def _ssd_prompt(xbc, z, dt, params, *, nb, seq):
    nt = nb * seq
    L = CHUNK
    nper = seq // L
    row = lambda n: pl.BlockSpec((L, n), lambda s, c: (s * nper + c, 0))
    return pl.pallas_call(
        functools.partial(_ssd_prompt_kernel, L=L),
        grid=(nb, nper),
        in_specs=[row(DX), row(DS), row(LANE)] + _ssd_params_specs(),
        out_specs=[row(DS),
                   pl.BlockSpec((None, KS - 1, DX), lambda s, c: (s, 0, 0)),
                   pl.BlockSpec((None, NH, HP, NS), lambda s, c: (s, 0, 0, 0))],
        out_shape=[jax.ShapeDtypeStruct((nt, DS), F32), jax.ShapeDtypeStruct((nb, KS - 1, DX), F32),
                   jax.ShapeDtypeStruct((nb, NH, HP, NS), F32)],
        scratch_shapes=[pltpu.VMEM((DX // LANE, HALO + L, LANE), F32), pltpu.VMEM((NS, DS), F32)],
        compiler_params=_cparams(("parallel", "arbitrary")),
        name="ssd_prompt",
    )(xbc, z, dt, *params)


def _ssd_sample_kernel(xbc_ref, z_ref, dt_ref, hist_ref, h0_ref, cw_ref, cb_ref, dtb_ref, alog_ref, dsk_ref,
                       ng_ref, y_ref, nb_ref, h_ref, xe_ref, zb_ref, dtp_ref, *, lq, tv, nseq):
    assert lq - tv >= KS - 1 and HALO >= KS - 1
    xe_ref[...] = jnp.zeros_like(xe_ref)
    zb_ref[...] = jnp.zeros_like(zb_ref)
    dtp_ref[...] = jnp.zeros_like(dtp_ref)
    for j in range(nseq):
        rows = slice(j * tv, (j + 1) * tv)
        _to_slabs(xe_ref, HALO + j * lq - (KS - 1), hist_ref[j])
        _to_slabs(xe_ref, HALO + j * lq, xbc_ref[rows, :])
        zb_ref[j * lq:j * lq + tv, :] = z_ref[rows, :]
        dtp_ref[j * lq:j * lq + tv, :] = dt_ref[rows, :]
    y = _ssd_chunk(xe_ref, zb_ref[...], dtp_ref[...], None, cw_ref, cb_ref, dtb_ref, alog_ref, dsk_ref, ng_ref,
                   L=nseq * lq, tv=tv, lq=lq, h_io=[(h0_ref.at[j], h_ref.at[j]) for j in range(nseq)])
    for j in range(nseq):
        y_ref[j * tv:(j + 1) * tv, :] = y[j * lq:j * lq + tv, :]
        nb_ref[j] = _from_slabs(xe_ref, HALO + j * lq + tv - (KS - 1), KS - 1)


def _ssd_sample(xbc, z, dt, hist, h0, params, *, row0, tv, nseq=8, lq=SUB):
    nb = hist.shape[0]
    rows = nseq * tv
    L = nseq * lq
    blk0 = row0 // rows
    row = lambda n: pl.BlockSpec((rows, n), lambda i: (blk0 + i, 0))
    return pl.pallas_call(
        functools.partial(_ssd_sample_kernel, lq=lq, tv=tv, nseq=nseq),
        grid=(nb // nseq,),
        in_specs=[row(DX), row(DS), row(LANE),
                  pl.BlockSpec((nseq, KS - 1, DX), lambda i: (i, 0, 0)),
                  pl.BlockSpec((nseq, NH, HP, NS), lambda i: (i, 0, 0, 0))] + _ssd_params_specs(),
        out_specs=[pl.BlockSpec((rows, DS), lambda i: (i, 0)),
                   pl.BlockSpec((nseq, KS - 1, DX), lambda i: (i, 0, 0)),
                   pl.BlockSpec((nseq, NH, HP, NS), lambda i: (i, 0, 0, 0))],
        out_shape=[jax.ShapeDtypeStruct((nb * tv, DS), F32), jax.ShapeDtypeStruct((nb, KS - 1, DX), F32),
                   jax.ShapeDtypeStruct((nb, NH, HP, NS), F32)],
        scratch_shapes=[pltpu.VMEM((DX // LANE, HALO + L, LANE), F32), pltpu.VMEM((L, DS), F32),
                        pltpu.VMEM((L, LANE), F32)],
        compiler_params=_cparams(("arbitrary",)),
        name="ssd_sample",
    )(xbc, z, dt, hist, h0, *params)


def _proj_out_kernel(*refs, n, n_main, side_plan):
    lhs = refs[0:2 * n]
    ws = refs[2 * n:3 * n]
    x_ref, g_ref = refs[3 * n:3 * n + 2]
    ns = len(side_plan)
    side_in = refs[3 * n + 2:3 * n + 2 + ns]
    o_ref = refs[3 * n + 2 + ns]
    side_out = refs[3 * n + 3 + ns:]
    _side_cast(pl.program_id(0), side_plan, side_in, side_out)
    is_main = pl.program_id(0) < n_main
    m = None
    for k, w_ref in enumerate(ws):
        a = jnp.where(is_main, lhs[2 * k][...], lhs[2 * k + 1][...]).astype(BF16)
        p = jnp.dot(a, w_ref[...].astype(BF16), preferred_element_type=F32)
        m = p if m is None else m + p
    o_ref[...] = x_ref[...] + _rms(m, g_ref[...])


def _proj_out(lhs_pairs, ws, x, g, *, tm=512, side=()):
    nt = x.shape[0]
    side_plan, side_steps = _side_plan(side)
    assert side_steps <= nt // tm
    n = len(lhs_pairs)
    n_main = nt // tm - 1
    lhs_specs = []
    for a_main, a_extra in lhs_pairs:
        assert a_main.shape[0] == n_main * tm and a_extra.shape[0] == tm
        lhs_specs.append(pl.BlockSpec((tm, a_main.shape[1]), lambda i: (jnp.minimum(i, n_main - 1), 0)))
        lhs_specs.append(pl.BlockSpec((tm, a_extra.shape[1]), lambda i: (0, 0)))
    res = pl.pallas_call(
        functools.partial(_proj_out_kernel, n=n, n_main=n_main, side_plan=side_plan),
        grid=(nt // tm,),
        in_specs=lhs_specs + [_resident(w.shape) for w in ws]
                 + [pl.BlockSpec((tm, D), lambda i: (i, 0)), _resident((1, D))] + _side_specs(side, lambda i: i),
        out_specs=[pl.BlockSpec((tm, D), lambda i: (i, 0))] + _side_specs(side, lambda i: i),
        out_shape=[jax.ShapeDtypeStruct((nt, D), F32)] + _side_shapes(side),
        compiler_params=_cparams(("arbitrary",)),
        name="proj_out",
    )(*[a for pair in lhs_pairs for a in pair], *ws, x, g, *[w for w, _, _ in side])
    return tuple(res) if side else res[0]


def _norm_proj_kernel(*refs, n, heads, side_plan):
    x_ref, g_ref = refs[0:2]
    ws = refs[2:2 + n]
    ns = len(side_plan)
    side_in = refs[2 + n:2 + n + ns]
    outs = refs[2 + n + ns:2 + 2 * n + ns]
    side_out = refs[2 + 2 * n + ns:]
    _side_cast(pl.program_id(0), side_plan, side_in, side_out)
    hn = _rms(x_ref[...], g_ref[...]).astype(BF16)
    for w_ref, o_ref in zip(ws, outs):
        for h in range(XH):
            sl = slice(h * XD, (h + 1) * XD)
            r = jnp.dot(hn, w_ref[:, sl].astype(BF16), preferred_element_type=F32).astype(o_ref.dtype)
            if heads:
                o_ref[:, h, :] = r
            else:
                o_ref[:, sl] = r


def _norm_proj(x, g, ws, out_dtype, *, heads=False, tm=512, side=()):
    nt = x.shape[0]
    n = len(ws)
    side_plan, side_steps = _side_plan(side)
    if heads:
        tm = NM
        out_specs = [pl.BlockSpec((None, NM, XH, XD), lambda i: (i, 0, 0, 0))] * n
        out_shape = [jax.ShapeDtypeStruct((nt // NM, NM, XH, XD), out_dtype)] * n
    else:
        out_specs = [pl.BlockSpec((tm, D), lambda i: (i, 0))] * n
        out_shape = [jax.ShapeDtypeStruct((nt, D), out_dtype)] * n
    assert side_steps <= nt // tm
    return pl.pallas_call(
        functools.partial(_norm_proj_kernel, n=n, heads=heads, side_plan=side_plan),
        grid=(nt // tm,),
        in_specs=[pl.BlockSpec((tm, D), lambda i: (i, 0)), _resident((1, D))] + [_resident((D, D))] * n
                 + _side_specs(side, lambda i: i),
        out_specs=out_specs + _side_specs(side, lambda i: i),
        out_shape=out_shape + _side_shapes(side),
        compiler_params=_cparams(("arbitrary",)),
        name="norm_proj",
    )(x, g, *ws, *[w for w, _, _ in side])


NLT = XD // LANE
LT_STRIDE = NLT * XH
KV_ROWS = NM * LT_STRIDE


def _kv_tiles(x):
    nb = x.shape[0]
    return x.reshape(nb, NM, XH, NLT, LANE).transpose(0, 1, 3, 2, 4).reshape(nb, KV_ROWS, LANE)


def _attn_prompt_kernel(q_ref, k_ref, v_ref, o_ref, kh_ref, vh_ref):
    @pl.when(pl.program_id(1) == 0)
    def _():
        for h in range(XH):
            for lt in range(NLT):
                rows = pl.ds(lt * XH + h, NM, stride=LT_STRIDE)
                kh_ref[h, :, lt * LANE:(lt + 1) * LANE] = k_ref[0, rows, :].astype(BF16)
                vh_ref[h, :, lt * LANE:(lt + 1) * LANE] = v_ref[0, rows, :].astype(BF16)

    scale = XD ** -0.5
    nt_dims = (((1,), (1,)), ((), ()))
    heads = range(XH)
    s = [lax.dot_general(q_ref[:, h * XD:(h + 1) * XD], kh_ref[h], nt_dims, preferred_element_type=F32) * scale
         for h in heads]
    e = [jnp.exp(s[h] - jnp.max(s[h], axis=-1, keepdims=True)) for h in heads]
    p = [(e[h] / jnp.sum(e[h], axis=-1, keepdims=True)).astype(BF16) for h in heads]
    for h in heads:
        o_ref[:, h * XD:(h + 1) * XD] = jnp.dot(p[h], vh_ref[h], preferred_element_type=F32).astype(o_ref.dtype)


def _group_sum(x, col_lt):
    n = x.shape[1]
    a = x + jnp.where(col_lt % 2 == 0, pltpu.roll(x, n - 1, axis=1), pltpu.roll(x, 1, axis=1))
    return a + jnp.where(col_lt < 2, pltpu.roll(a, n - 2, axis=1), pltpu.roll(a, 2, axis=1))


def _attn_sample_kernel(q_ref, k_ref, v_ref, o_ref, *, nseq, tv):
    assert NLT == 4
    tq = q_ref.shape[0]
    ncol = NM * NLT
    scale = XD ** -0.5
    nt_dims = (((1,), (1,)), ((), ()))
    col_lt = lax.broadcasted_iota(jnp.int32, (tq, ncol), 1) % NLT
    rowi = lax.broadcasted_iota(jnp.int32, (tq, LANE), 0)
    for h in range(XH):
        qp = jnp.concatenate([q_ref[:, h * XD + lt * LANE:h * XD + (lt + 1) * LANE].astype(F32)
                              for lt in range(NLT)], axis=0)
        seqs = range(nseq)
        g = [lax.dot_general(qp, k_ref[j, pl.ds(h, ncol, stride=XH), :], nt_dims, preferred_element_type=F32)
             for j in seqs]
        s4 = [sum(jnp.where(col_lt == lt, g[j][lt * tq:(lt + 1) * tq], 0.0) for lt in range(NLT)) for j in seqs]
        s = [_group_sum(s4[j], col_lt) * scale for j in seqs]
        e = [jnp.exp(s[j] - jnp.max(s[j], axis=-1, keepdims=True)) for j in seqs]
        p = [e[j] / (jnp.sum(e[j], axis=-1, keepdims=True) * (1.0 / NLT)) for j in seqs]
        o = [jnp.dot(jnp.concatenate([jnp.where(col_lt == lt, p[j], 0.0) for lt in range(NLT)], axis=0),
                     v_ref[j, pl.ds(h, ncol, stride=XH), :], preferred_element_type=F32) for j in seqs]
        acc = [None] * NLT
        for j in seqs:
            mine = (rowi >= j * tv) & (rowi < (j + 1) * tv)
            for lt in range(NLT):
                acc[lt] = jnp.where(mine, o[j][lt * tq:(lt + 1) * tq], 0.0 if acc[lt] is None else acc[lt])
        for lt in range(NLT):
            o_ref[:, h * XD + lt * LANE:h * XD + (lt + 1) * LANE] = acc[lt].astype(o_ref.dtype)


def _attn_prompt(q, k, v, *, nb, seq, tq=512):
    nt = nb * seq
    nper = seq // tq
    kv_spec = pl.BlockSpec((1, KV_ROWS, LANE), lambda s, t: (s, 0, 0))
    return pl.pallas_call(
        _attn_prompt_kernel,
        grid=(nb, nper),
        in_specs=[pl.BlockSpec((tq, D), lambda s, t: (s * nper + t, 0)), kv_spec, kv_spec],
        out_specs=pl.BlockSpec((tq, D), lambda s, t: (s * nper + t, 0)),
        out_shape=jax.ShapeDtypeStruct((nt, D), BF16),
        scratch_shapes=[pltpu.VMEM((XH, NM, XD), BF16), pltpu.VMEM((XH, NM, XD), BF16)],
        compiler_params=_cparams(("arbitrary", "arbitrary")),
        name="attn_prompt",
    )(q, _kv_tiles(k), _kv_tiles(v))


def _attn_sample(q, k, v, *, row0, tv, nseq=4):
    nb = k.shape[0]
    rows = nseq * tv
    blk0 = row0 // rows
    kv_spec = pl.BlockSpec((nseq, KV_ROWS, LANE), lambda i: (i, 0, 0))
    return pl.pallas_call(
        functools.partial(_attn_sample_kernel, nseq=nseq, tv=tv),
        grid=(nb // nseq,),
        in_specs=[pl.BlockSpec((rows, D), lambda i: (blk0 + i, 0)), kv_spec, kv_spec],
        out_specs=pl.BlockSpec((rows, D), lambda i: (i, 0)),
        out_shape=jax.ShapeDtypeStruct((nb * tv, D), BF16),
        compiler_params=_cparams(("arbitrary",)),
        name="attn_sample",
    )(q, _kv_tiles(k), _kv_tiles(v))


def _row(v):
    return v.reshape(1, -1).astype(F32)


def _pad_lanes(v, n):
    return jnp.pad(v.reshape(1, -1).astype(F32), ((0, 0), (0, n - v.size)))


def _layer(xs, mem, cache_k, cache_v, st_conv, st_sconv, st_ssm, p, *, nbp, seq, nbs, tv, split_out):
    npr = nbp * seq
    bf = lambda w: w.astype(BF16)

    side = [(p['w_in'].T, 0, LANE),
            (p['w_out'], 0, LANE)]
    x, w_in, w_out = _ffn(xs, _row(p['ffn1_pre_g']), bf(p['ffn1_w_gate']), bf(p['ffn1_w_up']),
                          bf(p['ffn1_w_down']), _row(p['ffn1_post_g']), split_out=False, tf=FF_TILE // 2,
                          side=side)

    wt = jnp.pad(w_in[2 * DC + DS + DX:], ((0, LANE - NH), (0, 0)))
    u, z, xbc, dtr = _mix_in(x, _row(p['mix_pre_g']), w_in, wt)

    cpar = (p['conv_w'].astype(F32), _row(p['conv_b']), _row(p['conv_ln_g']), _row(p['conv_ln_b']))
    a_p, conv_p = _conv_prompt(u, *cpar, nb=nbp, seq=seq)
    a_s, conv_s = _conv_sample(u, jnp.transpose(st_conv, (1, 0, 2)), *cpar, row0=npr, tv=tv)
    conv_s = jnp.transpose(conv_s, (1, 0, 2))

    spar = (p['ssm_conv_w'].astype(F32), _row(p['ssm_conv_b']), _pad_lanes(p['dt_bias'], LANE),
            _pad_lanes(p['a_log'], LANE), jnp.repeat(p['d_skip'].astype(F32), HP).reshape(1, DS),
            _row(p['ssm_norm_g']))
    y_p, sconv_p, ssm_p = _ssd_prompt(xbc, z, dtr, spar, nb=nbp, seq=seq)
    y_s, sconv_s, ssm_s = _ssd_sample(xbc, z, dtr, st_sconv, st_ssm, spar, row0=npr, tv=tv)

    x, wg2 = _proj_out([(a_p, a_s), (y_p, y_s)], [w_out[:DC], w_out[DC:]], x, _row(p['mix_post_g']),
                       side=[(p['ffn2_w_gate'], 1, SIDE_WIDE)])

    q, wu2 = _norm_proj(x, _row(p['xattn_pre_g']), [p['w_xq']], BF16, side=[(p['ffn2_w_up'], 1, SIDE_WIDE)])
    mk, mv = _norm_proj(mem, _row(p['mem_norm_g']), [p['w_xk'], p['w_xv']], F32, heads=True)
    o_p = _attn_prompt(q, mk, mv, nb=nbp, seq=seq)
    o_s = _attn_sample(q, cache_k, cache_v, row0=npr, tv=tv)
    x, wd2 = _proj_out([(o_p, o_s)], [p['w_xo']], x, _row(p['xattn_post_g']),
                       side=[(p['ffn2_w_down'], 0, SIDE_WIDE)])

    x = _ffn([x], _row(p['ffn2_pre_g']), wg2, wu2, wd2, _row(p['ffn2_post_g']), split_out=split_out)
    return x, (mk, mv, conv_p, sconv_p, ssm_p, conv_s, sconv_s, ssm_s)


def kernel(x_prompt, x_sample, mem_prompt, cache_mem_k, cache_mem_v, state_conv, state_ssm_conv, state_ssm, ffn1_pre_g, ffn1_w_gate, ffn1_w_up, ffn1_w_down, ffn1_post_g, mix_pre_g, w_in, conv_w, conv_b, conv_ln_g, conv_ln_b, ssm_conv_w, ssm_conv_b, dt_bias, a_log, d_skip, ssm_norm_g, w_out, mix_post_g, xattn_pre_g, mem_norm_g, w_xq, w_xk, w_xv, w_xo, xattn_post_g, ffn2_pre_g, ffn2_w_gate, ffn2_w_up, ffn2_w_down, ffn2_post_g):
    params = dict(ffn1_pre_g=ffn1_pre_g, ffn1_w_gate=ffn1_w_gate, ffn1_w_up=ffn1_w_up, ffn1_w_down=ffn1_w_down,
                  ffn1_post_g=ffn1_post_g, mix_pre_g=mix_pre_g, w_in=w_in, conv_w=conv_w, conv_b=conv_b,
                  conv_ln_g=conv_ln_g, conv_ln_b=conv_ln_b, ssm_conv_w=ssm_conv_w, ssm_conv_b=ssm_conv_b,
                  dt_bias=dt_bias, a_log=a_log, d_skip=d_skip, ssm_norm_g=ssm_norm_g, w_out=w_out,
                  mix_post_g=mix_post_g, xattn_pre_g=xattn_pre_g, mem_norm_g=mem_norm_g, w_xq=w_xq,
                  w_xk=w_xk, w_xv=w_xv, w_xo=w_xo, xattn_post_g=xattn_post_g, ffn2_pre_g=ffn2_pre_g,
                  ffn2_w_gate=ffn2_w_gate, ffn2_w_up=ffn2_w_up, ffn2_w_down=ffn2_w_down, ffn2_post_g=ffn2_post_g)
    depth = ffn1_pre_g.shape[0]
    nbp, seq, _ = x_prompt.shape
    nbs, tv, _ = x_sample.shape
    npr = nbp * seq
    xs = [x_prompt.reshape(npr, D), x_sample.reshape(nbs * tv, D)]
    mem = mem_prompt.reshape(nbp * NM, D)
    per_layer = []
    for layer in range(depth):
        p = {name: w[layer] for name, w in params.items()}
        x, states = _layer(xs, mem, cache_mem_k[layer], cache_mem_v[layer],
                           state_conv[layer], state_ssm_conv[layer], state_ssm[layer], p,
                           nbp=nbp, seq=seq, nbs=nbs, tv=tv, split_out=layer == depth - 1)
        xs = [x]
        per_layer.append(states)
    mk, mv, conv_p, sconv_p, ssm_p, conv_s, sconv_s, ssm_s = [jnp.stack(t) for t in zip(*per_layer)]
    yp, ys = x
    return (yp.reshape(nbp, seq, D), ys.reshape(nbs, tv, D), mk, mv, conv_p, sconv_p, ssm_p, conv_s, sconv_s, ssm_s)
```

```python
import functools

import jax
import jax.numpy as jnp
from jax import lax
from jax.experimental import pallas as pl
from jax.experimental.pallas import tpu as pltpu

F32 = jnp.float32
BF16 = jnp.bfloat16

D = 2048
FF = 5504
DC = 1024
DS = 1024
KC = 31
NH = 16
HP = 64
NG = 2
NS = 128
KS = 4
DX = DS + 2 * NG * NS
CHUNK = 128
NM = 256
XH = 4
XD = D // XH
EPS = 1e-6

LANE = 128
SUB = 8
HIST = 32
HALO = 8

FF_TILE = 1024
SIDE_WIDE = 3 * LANE

VMEM_LIMIT = 60 * 1024 * 1024


def _cparams(sem):
    return pltpu.CompilerParams(dimension_semantics=sem, vmem_limit_bytes=VMEM_LIMIT)


def _rms(x, g):
    return x * lax.rsqrt(jnp.mean(x * x, axis=-1, keepdims=True) + EPS) * g


def _silu(x):
    return x * jax.nn.sigmoid(x)


def _resident(shape):
    return pl.BlockSpec(shape, lambda *_: (0,) * len(shape), pipeline_mode=pl.Buffered(1))


def _ffn_kernel(*refs, n_in, n_out, n_main, tf, side_plan):
    n_side = len(side_plan)
    x_refs = refs[:n_in]
    pg_ref, wg_ref, wu_ref, wd_ref, qg_ref = refs[n_in:n_in + 5]
    side_in = refs[n_in + 5:n_in + 5 + n_side]
    k = n_in + 5 + n_side
    o_refs = refs[k:k + n_out]
    side_out = refs[k + n_out:k + n_out + n_side]
    xn_ref, acc_ref = refs[k + n_out + n_side:]
    i = pl.program_id(0)
    f = pl.program_id(1)
    last = pl.num_programs(1) - 1

    def x_tile():
        if n_in == 1:
            return x_refs[0][...]
        return jnp.where(i < n_main, x_refs[0][...], x_refs[1][...])

    @pl.when(f == 0)
    def _():
        xn_ref[...] = _rms(x_tile(), pg_ref[...]).astype(BF16)
        acc_ref[...] = jnp.zeros_like(acc_ref)

    def hidden_tile(width):
        xn = xn_ref[...]
        h = jnp.dot(xn, wg_ref[:, 0:width], preferred_element_type=F32)
        u = jnp.dot(xn, wu_ref[:, 0:width], preferred_element_type=F32)
        a = (_silu(h) * u).astype(BF16)
        acc_ref[...] += jnp.dot(a, wd_ref[0:width, :], preferred_element_type=F32)

    @pl.when(f < last)
    def _():
        hidden_tile(tf)

    @pl.when(f == last)
    def _():
        hidden_tile(FF - (FF // tf) * tf)
        res = x_tile() + 0.5 * _rms(acc_ref[...], qg_ref[...])
        if n_out == 1:
            o_refs[0][...] = res
        else:
            @pl.when(i < n_main)
            def _():
                o_refs[0][...] = res

            @pl.when(i >= n_main)
            def _():
                o_refs[1][...] = res

    _side_cast(i * pl.num_programs(1) + f, side_plan, side_in, side_out)


def _side_plan(side):
    plan, start = [], 0
    for w, axis, width in side:
        n = pl.cdiv(w.shape[axis], width)
        plan.append((start, n))
        start += n
    return tuple(plan), start


def _side_specs(side, step_of):
    specs = []
    for (w, axis, width), (start, n) in zip(side, _side_plan(side)[0]):
        strip = lambda *idx, start=start, n=n: jnp.clip(step_of(*idx) - start, 0, n - 1)
        if axis == 0:
            specs.append(pl.BlockSpec((width, w.shape[1]), lambda *idx, strip=strip: (strip(*idx), 0)))
        else:
            specs.append(pl.BlockSpec((w.shape[0], width), lambda *idx, strip=strip: (0, strip(*idx))))
    return specs


def _side_cast(step, side_plan, side_in, side_out):
    for m, (start, n) in enumerate(side_plan):
        @pl.when((step >= start) & (step < start + n))
        def _(m=m):
            side_out[m][...] = side_in[m][...].astype(BF16)


def _side_shapes(side):
    return [jax.ShapeDtypeStruct(w.shape, BF16) for w, _, _ in side]


def _ffn(xs, pre_g, wg, wu, wd, post_g, *, split_out, tm=512, tf=FF_TILE, side=()):
    n_in = len(xs)
    nt = sum(x.shape[0] for x in xs)
    n_main = (nt - tm) // tm
    nf = pl.cdiv(FF, tf)
    side_plan, side_steps = _side_plan(side)
    assert side_steps <= (nt // tm) * nf
    main = lambda i, f: (jnp.minimum(i, n_main - 1), 0)
    extra = lambda i, f: (0, 0)
    whole = lambda i, f: (i, 0)
    if n_in == 1:
        x_specs = [pl.BlockSpec((tm, D), whole)]
    else:
        x_specs = [pl.BlockSpec((tm, D), main), pl.BlockSpec((tm, D), extra)]
    if split_out:
        out_specs = [pl.BlockSpec((tm, D), main), pl.BlockSpec((tm, D), extra)]
        out_shape = [jax.ShapeDtypeStruct((n_main * tm, D), F32), jax.ShapeDtypeStruct((tm, D), F32)]
    else:
        out_specs = [pl.BlockSpec((tm, D), whole)]
        out_shape = [jax.ShapeDtypeStruct((nt, D), F32)]
    res = pl.pallas_call(
        functools.partial(_ffn_kernel, n_in=n_in, n_out=len(out_specs), n_main=n_main, tf=tf, side_plan=side_plan),
        grid=(nt // tm, nf),
        in_specs=x_specs + [
            pl.BlockSpec((1, D), lambda i, f: (0, 0)),
            pl.BlockSpec((D, tf), lambda i, f: (0, f)),
            pl.BlockSpec((D, tf), lambda i, f: (0, f)),
            pl.BlockSpec((tf, D), lambda i, f: (f, 0)),
            pl.BlockSpec((1, D), lambda i, f: (0, 0)),
        ] + _side_specs(side, lambda i, f: i * nf + f),
        out_specs=out_specs + _side_specs(side, lambda i, f: i * nf + f),
        out_shape=out_shape + _side_shapes(side),
        scratch_shapes=[pltpu.VMEM((tm, D), BF16), pltpu.VMEM((tm, D), F32)],
        compiler_params=_cparams(("arbitrary", "arbitrary")),
        name="ffn",
    )(*xs, pre_g, wg, wu, wd, post_g, *[w for w, _, _ in side])
    main_res = res[0] if len(out_specs) == 1 else tuple(res[:2])
    return (main_res, *res[len(out_specs):]) if side else main_res


def _mix_in_kernel(x_ref, g_ref, w_ref, wt_ref, u_ref, z_ref, xbc_ref, dt_ref, *, tn):
    hn = _rms(x_ref[...], g_ref[...]).astype(BF16)
    nt_dims = (((1,), (1,)), ((), ()))

    def cols(start, c):
        return lax.dot_general(hn, w_ref[start + c * tn:start + (c + 1) * tn, :], nt_dims,
                               preferred_element_type=F32)

    for c in range(DC // tn):
        u_ref[:, c * tn:(c + 1) * tn] = cols(0, c) * jax.nn.sigmoid(cols(DC, c))
    for c in range(DS // tn):
        z_ref[:, c * tn:(c + 1) * tn] = cols(2 * DC, c)
    for c in range(DX // tn):
        xbc_ref[:, c * tn:(c + 1) * tn] = cols(2 * DC + DS, c)
    dt_ref[...] = lax.dot_general(hn, wt_ref[...], nt_dims, preferred_element_type=F32)


def _mix_in(x, g, w, wt, *, tm=512, tn=512):
    nt = x.shape[0]
    row = lambda n: pl.BlockSpec((tm, n), lambda i: (i, 0))
    return pl.pallas_call(
        functools.partial(_mix_in_kernel, tn=tn),
        grid=(nt // tm,),
        in_specs=[row(D), _resident((1, D)), _resident(w.shape), _resident((LANE, D))],
        out_specs=[row(DC), row(DS), row(DX), row(LANE)],
        out_shape=[jax.ShapeDtypeStruct((nt, n), F32) for n in (DC, DS, DX, LANE)],
        compiler_params=_cparams(("parallel",)),
        name="mix_in",
    )(x, g, w, wt)


def _conv_prompt_kernel(u_ref, w_ref, b_ref, lg_ref, lb_ref, a_ref, nb_ref, xe_ref, y_ref, *, tt):
    t = pl.program_id(1)
    off = HIST - (KC - 1)

    @pl.when(t == 0)
    def _():
        xe_ref[:, 0:HIST, :] = jnp.zeros((DC // LANE, HIST, LANE), F32)

    for j in range(DC // LANE):
        xe_ref[j, HIST:HIST + tt, :] = u_ref[:, j * LANE:(j + 1) * LANE]
    for j in range(DC // LANE):
        sl = slice(j * LANE, (j + 1) * LANE)
        acc = jnp.broadcast_to(b_ref[:, sl], (tt, LANE))
        for k in range(KC):
            acc = acc + w_ref[k:k + 1, sl] * xe_ref[j, off + k:off + k + tt, :]
        y_ref[:, sl] = acc
    for j in range(DC // LANE):
        nb_ref[:, j * LANE:(j + 1) * LANE] = xe_ref[j, HIST + tt - (KC - 1):HIST + tt, :]
        xe_ref[j, 0:HIST, :] = xe_ref[j, tt:tt + HIST, :]
    @pl.when(t >= 0)
    def _():
        y = y_ref[...]
        yc = y - jnp.mean(y, axis=-1, keepdims=True)
        a_ref[...] = _silu(yc * lax.rsqrt(jnp.mean(yc * yc, axis=-1, keepdims=True) + EPS) * lg_ref[...]
                           + lb_ref[...])


def _conv_prompt(u, w, b, lg, lb, *, nb, seq, tt=256):
    nt = nb * seq
    nper = seq // tt
    par = lambda r: _resident((r, DC))
    return pl.pallas_call(
        functools.partial(_conv_prompt_kernel, tt=tt),
        grid=(nb, nper),
        in_specs=[pl.BlockSpec((tt, DC), lambda s, t: (s * nper + t, 0)), par(KC), par(1), par(1), par(1)],
        out_specs=[pl.BlockSpec((tt, DC), lambda s, t: (s * nper + t, 0)),
                   pl.BlockSpec((None, KC - 1, DC), lambda s, t: (s, 0, 0))],
        out_shape=[jax.ShapeDtypeStruct((nt, DC), F32), jax.ShapeDtypeStruct((nb, KC - 1, DC), F32)],
        scratch_shapes=[pltpu.VMEM((DC // LANE, HIST + tt, LANE), F32), pltpu.VMEM((tt, DC), F32)],
        compiler_params=_cparams(("parallel", "arbitrary")),
        name="conv_prompt",
    )(u, w, b, lg, lb)


def _conv_sample_kernel(u_ref, hist_ref, w_ref, b_ref, lg_ref, lb_ref, a_ref, nh_ref, us_ref, y_ref, as_ref,
                        *, sb, tv):
    nl = DC // LANE
    for j in range(nl):
        us_ref[j] = u_ref[:, j * LANE:(j + 1) * LANE]
    for j in range(nl):
        sl = slice(j * LANE, (j + 1) * LANE)
        accs = [jnp.broadcast_to(b_ref[:, sl], (sb, LANE)) for _ in range(tv)]
        for m in range(KC - 1 + tv):
            if m < KC - 1:
                xm = hist_ref[m, :, sl]
            else:
                xm = us_ref[j, pl.ds(m - (KC - 1), sb, stride=tv), :]
            for t in range(tv):
                if 0 <= m - t < KC:
                    accs[t] = accs[t] + w_ref[m - t:m - t + 1, sl] * xm
            if m >= tv:
                nh_ref[m - tv, :, sl] = xm
        for t in range(tv):
            y_ref[t, :, sl] = accs[t]
    for t in range(tv):
        y = y_ref[t]
        yc = y - jnp.mean(y, axis=-1, keepdims=True)
        a = _silu(yc * lax.rsqrt(jnp.mean(yc * yc, axis=-1, keepdims=True) + EPS) * lg_ref[...] + lb_ref[...])
        for j in range(nl):
            as_ref[j, pl.ds(t, sb, stride=tv), :] = a[:, j * LANE:(j + 1) * LANE]
    for j in range(nl):
        a_ref[:, j * LANE:(j + 1) * LANE] = as_ref[j]


def _conv_sample(u, hist, w, b, lg, lb, *, row0, tv, sb=32):
    nb = hist.shape[1]
    rows = sb * tv
    par = lambda r: _resident((r, DC))
    blk0 = row0 // rows
    hist_spec = pl.BlockSpec((KC - 1, sb, DC), lambda i: (0, i, 0))
    return pl.pallas_call(
        functools.partial(_conv_sample_kernel, sb=sb, tv=tv),
        grid=(nb // sb,),
        in_specs=[pl.BlockSpec((rows, DC), lambda i: (blk0 + i, 0)), hist_spec, par(KC), par(1), par(1), par(1)],
        out_specs=[pl.BlockSpec((rows, DC), lambda i: (i, 0)), hist_spec],
        out_shape=[jax.ShapeDtypeStruct((nb * tv, DC), F32), jax.ShapeDtypeStruct((KC - 1, nb, DC), F32)],
        scratch_shapes=[pltpu.VMEM((DC // LANE, rows, LANE), F32), pltpu.VMEM((tv, sb, DC), F32),
                        pltpu.VMEM((DC // LANE, rows, LANE), F32)],
        compiler_params=_cparams(("arbitrary",)),
        name="conv_sample",
    )(u, hist, w, b, lg, lb)


def _dot01(a, b, dims, *, data):
    x = b if data else a
    one = (a if data else b).astype(BF16)
    t0 = x.astype(BF16)
    r1 = x - t0.astype(F32)
    t1 = r1.astype(BF16)
    t2 = (r1 - t1.astype(F32)).astype(BF16)
    acc = None
    for t in (t0, t1, t2):
        lhs, rhs = (one, t) if data else (t, one)
        p = lax.dot_general(lhs, rhs, dims, preferred_element_type=F32)
        acc = p if acc is None else acc + p
    return acc


def _to_slabs(xe_ref, r0, x):
    for j in range(x.shape[1] // LANE):
        xe_ref[j, r0:r0 + x.shape[0], :] = x[:, j * LANE:(j + 1) * LANE]


def _from_slabs(xe_ref, r0, rows):
    return jnp.concatenate([xe_ref[j, r0:r0 + rows, :] for j in range(xe_ref.shape[0])], axis=1)


def _ssd_chunk(xe_ref, z, dt_raw, st_ref, cw_ref, cb_ref, dtb_ref, alog_ref, dsk_ref, ng_ref, *, L, tv, lq=None,
               h_io=None):
    mm_dims = (((1,), (0,)), ((), ()))
    lq = L if lq is None else lq
    nsq = L // lq
    off = HALO - (KS - 1)
    cols = []
    for j in range(DX // LANE):
        sl = slice(j * LANE, (j + 1) * LANE)
        acc = jnp.broadcast_to(cb_ref[:, sl], (L, LANE))
        for k in range(KS):
            acc = acc + cw_ref[k:k + 1, sl] * xe_ref[j, off + k:off + k + L, :]
        cols.append(acc)
    xc = _silu(jnp.concatenate(cols, axis=1))
    xs = xc[:, 0:DS]

    lane = lax.broadcasted_iota(jnp.int32, (L, LANE), 1)
    rowi = lax.broadcasted_iota(jnp.int32, (L, LANE), 0)
    xdt = dt_raw + dtb_ref[...]
    dt = jnp.maximum(xdt, 0.0) + jnp.log1p(jnp.exp(-jnp.abs(xdt)))
    dt = jnp.where((lane < NH) & (rowi % lq < tv), dt, 0.0)
    da = dt * (-jnp.exp(alog_ref[...]))

    r2 = lax.broadcasted_iota(jnp.int32, (L, L), 0)
    c2 = lax.broadcasted_iota(jnp.int32, (L, L), 1)
    same = r2 // lq == c2 // lq
    causal = (r2 >= c2) & same
    if nsq == 1:
        a_cum = _dot01(causal.astype(F32), da, mm_dims, data=1)
        a_tot = jnp.broadcast_to(a_cum[L - 1:L, :], (L, LANE))
    else:
        cums = _dot01(jnp.concatenate([causal.astype(F32), same.astype(F32)], axis=0), da, mm_dims, data=1)
        a_cum = cums[0:L]
        a_tot = cums[L:2 * L]

    er = lax.broadcasted_iota(jnp.int32, (LANE, DS), 0)
    ec = lax.broadcasted_iota(jnp.int32, (LANE, DS), 1)
    expand = (ec // HP == er).astype(F32)
    stack = jnp.concatenate([jnp.exp(a_cum), jnp.exp(a_tot - a_cum) * dt, jnp.exp(a_tot[0:SUB])], axis=0)
    stack_x = _dot01(stack, expand, mm_dims, data=0)
    ea_x = stack_x[0:L]
    wend_x = stack_x[L:2 * L]
    cd_x = stack_x[2 * L:2 * L + 1]

    ir = lax.broadcasted_iota(jnp.int32, (LANE, LANE), 0)
    ic = lax.broadcasted_iota(jnp.int32, (LANE, LANE), 1)
    ident = (ir == ic).astype(F32)
    tr = _dot01(ident, jnp.concatenate([dt, a_cum], axis=0), (((1,), (1,)), ((), ())), data=1)
    dt_t = tr[:, 0:L]
    acum_t = tr[:, L:2 * L]

    lane_x = lax.broadcasted_iota(jnp.int32, (L, LANE), 1)
    hpg = NH // NG
    bms = [xc[:, DS + g * NS:DS + (g + 1) * NS] for g in range(NG)]
    cms = [xc[:, DS + NG * NS + g * NS:DS + NG * NS + (g + 1) * NS] for g in range(NG)]
    gss = [slice(g * (DS // NG), (g + 1) * (DS // NG)) for g in range(NG)]
    cbms = [lax.dot_general(cms[g], bms[g], (((1,), (1,)), ((), ())), preferred_element_type=F32)
            for g in range(NG)]
    if h_io is None:
        st_old = [st_ref[:, gss[g]] for g in range(NG)]
        y_off = [jnp.dot(cms[g], st_old[g], preferred_element_type=F32) * ea_x[:, gss[g]] for g in range(NG)]
        st_new = [jnp.dot(bms[g].T, xs[:, gss[g]] * wend_x[:, gss[g]], preferred_element_type=F32)
                  for g in range(NG)]
        for g in range(NG):
            st_ref[:, gss[g]] = st_old[g] * cd_x[:, gss[g]] + st_new[g]
    else:
        rows_g = hpg * HP
        sq = [slice(q * lq, (q + 1) * lq) for q in range(nsq)]
        hs = [[h_io[q][0][g * hpg:(g + 1) * hpg].reshape(rows_g, NS) for g in range(NG)] for q in range(nsq)]
        y_off = [jnp.concatenate([lax.dot_general(cms[g][sq[q]], hs[q][g], (((1,), (1,)), ((), ())),
                                                  preferred_element_type=F32) for q in range(nsq)], axis=0)
                 * ea_x[:, gss[g]] for g in range(NG)]
        xw = [xs[:, gss[g]] * wend_x[:, gss[g]] for g in range(NG)]
        upd = [[lax.dot_general(xw[g][sq[q]], bms[g][sq[q]], (((0,), (0,)), ((), ())),
                                preferred_element_type=F32) for g in range(NG)] for q in range(nsq)]
        seq_decay = jnp.exp(a_tot)
        for q in range(nsq):
            for h in range(NH):
                g, hl = divmod(h, hpg)
                rows = slice(hl * HP, (hl + 1) * HP)
                h_io[q][1][h] = hs[q][g][rows, :] * seq_decay[q * lq:q * lq + 1, h:h + 1] + upd[q][g][rows, :]
    segs = [a_cum[:, h:h + 1] - acum_t[h:h + 1, :] for h in range(NH)]
    decs = [jnp.exp(jnp.where(causal, segs[h], -jnp.inf)) for h in range(NH)]
    ws = [cbms[h // hpg] * decs[h] * dt_t[h:h + 1, :] for h in range(NH)]
    y_diag = []
    for pr in range(NH // 2):
        xp = xs[:, 2 * pr * HP:(2 * pr + 2) * HP]
        rhs = jnp.concatenate([jnp.where(lane_x < HP, xp, 0.0), jnp.where(lane_x >= HP, xp, 0.0)], axis=0)
        y_diag.append(jnp.dot(jnp.concatenate([ws[2 * pr], ws[2 * pr + 1]], axis=1), rhs,
                              preferred_element_type=F32))
    y = jnp.concatenate(y_diag, axis=1) + jnp.concatenate(y_off, axis=1) + dsk_ref[...] * xs
    y = y * _silu(z)
    outs = []
    for g in range(NG):
        gs = slice(g * (DS // NG), (g + 1) * (DS // NG))
        outs.append(_rms(y[:, gs], ng_ref[:, gs]))
    return jnp.concatenate(outs, axis=1)


def _state_out(st_ref, h_ref):
    for j in range(DS // LANE):
        blk = st_ref[:, j * LANE:(j + 1) * LANE].T
        for q in range(LANE // HP):
            h_ref[j * (LANE // HP) + q] = blk[q * HP:(q + 1) * HP, :]


def _ssd_prompt_kernel(xbc_ref, z_ref, dt_ref, cw_ref, cb_ref, dtb_ref, alog_ref, dsk_ref, ng_ref,
                       y_ref, nb_ref, h_ref, xe_ref, st_ref, *, L):
    c = pl.program_id(1)

    @pl.when(c == 0)
    def _():
        xe_ref[:, 0:HALO, :] = jnp.zeros((DX // LANE, HALO, LANE), F32)
        st_ref[...] = jnp.zeros_like(st_ref)

    _to_slabs(xe_ref, HALO, xbc_ref[...])
    y_ref[...] = _ssd_chunk(xe_ref, z_ref[...], dt_ref[...], st_ref, cw_ref, cb_ref, dtb_ref, alog_ref, dsk_ref,
                            ng_ref, L=L, tv=L)
    xe_ref[:, 0:HALO, :] = xe_ref[:, L:L + HALO, :]

    @pl.when(c == pl.num_programs(1) - 1)
    def _():
        nb_ref[...] = _from_slabs(xe_ref, HALO - (KS - 1), KS - 1)
        _state_out(st_ref, h_ref)


def _ssd_params_specs():
    return [_resident((KS, DX)), _resident((1, DX)), _resident((1, LANE)), _resident((1, LANE)),
            _resident((1, DS)), _resident((1, DS))]


def _ssd_prompt(xbc, z, dt, params, *, nb, seq):
    nt = nb * seq
    L = CHUNK
    nper = seq // L
    row = lambda n: pl.BlockSpec((L, n), lambda s, c: (s * nper + c, 0))
    return pl.pallas_call(
        functools.partial(_ssd_prompt_kernel, L=L),
        grid=(nb, nper),
        in_specs=[row(DX), row(DS), row(LANE)] + _ssd_params_specs(),
        out_specs=[row(DS),
                   pl.BlockSpec((None, KS - 1, DX), lambda s, c: (s, 0, 0)),
                   pl.BlockSpec((None, NH, HP, NS), lambda s, c: (s, 0, 0, 0))],
        out_shape=[jax.ShapeDtypeStruct((nt, DS), F32), jax.ShapeDtypeStruct((nb, KS - 1, DX), F32),
                   jax.ShapeDtypeStruct((nb, NH, HP, NS), F32)],
        scratch_shapes=[pltpu.VMEM((DX // LANE, HALO + L, LANE), F32), pltpu.VMEM((NS, DS), F32)],
        compiler_params=_cparams(("parallel", "arbitrary")),
        name="ssd_prompt",
    )(xbc, z, dt, *params)


def _ssd_sample_kernel(xbc_ref, z_ref, dt_ref, hist_ref, h0_ref, cw_ref, cb_ref, dtb_ref, alog_ref, dsk_ref,
                       ng_ref, y_ref, nb_ref, h_ref, xe_ref, zb_ref, dtp_ref, *, lq, tv, nseq):
    assert lq - tv >= KS - 1 and HALO >= KS - 1
    xe_ref[...] = jnp.zeros_like(xe_ref)
    zb_ref[...] = jnp.zeros_like(zb_ref)
    dtp_ref[...] = jnp.zeros_like(dtp_ref)
    for j in range(nseq):
        rows = slice(j * tv, (j + 1) * tv)
        _to_slabs(xe_ref, HALO + j * lq - (KS - 1), hist_ref[j])
        _to_slabs(xe_ref, HALO + j * lq, xbc_ref[rows, :])
        zb_ref[j * lq:j * lq + tv, :] = z_ref[rows, :]
        dtp_ref[j * lq:j * lq + tv, :] = dt_ref[rows, :]
    y = _ssd_chunk(xe_ref, zb_ref[...], dtp_ref[...], None, cw_ref, cb_ref, dtb_ref, alog_ref, dsk_ref, ng_ref,
                   L=nseq * lq, tv=tv, lq=lq, h_io=[(h0_ref.at[j], h_ref.at[j]) for j in range(nseq)])
    for j in range(nseq):
        y_ref[j * tv:(j + 1) * tv, :] = y[j * lq:j * lq + tv, :]
        nb_ref[j] = _from_slabs(xe_ref, HALO + j * lq + tv - (KS - 1), KS - 1)


def _ssd_sample(xbc, z, dt, hist, h0, params, *, row0, tv, nseq=8, lq=SUB):
    nb = hist.shape[0]
    rows = nseq * tv
    L = nseq * lq
    blk0 = row0 // rows
    row = lambda n: pl.BlockSpec((rows, n), lambda i: (blk0 + i, 0))
    return pl.pallas_call(
        functools.partial(_ssd_sample_kernel, lq=lq, tv=tv, nseq=nseq),
        grid=(nb // nseq,),
        in_specs=[row(DX), row(DS), row(LANE),
                  pl.BlockSpec((nseq, KS - 1, DX), lambda i: (i, 0, 0)),
                  pl.BlockSpec((nseq, NH, HP, NS), lambda i: (i, 0, 0, 0))] + _ssd_params_specs(),
        out_specs=[pl.BlockSpec((rows, DS), lambda i: (i, 0)),
                   pl.BlockSpec((nseq, KS - 1, DX), lambda i: (i, 0, 0)),
                   pl.BlockSpec((nseq, NH, HP, NS), lambda i: (i, 0, 0, 0))],
        out_shape=[jax.ShapeDtypeStruct((nb * tv, DS), F32), jax.ShapeDtypeStruct((nb, KS - 1, DX), F32),
                   jax.ShapeDtypeStruct((nb, NH, HP, NS), F32)],
        scratch_shapes=[pltpu.VMEM((DX // LANE, HALO + L, LANE), F32), pltpu.VMEM((L, DS), F32),
                        pltpu.VMEM((L, LANE), F32)],
        compiler_params=_cparams(("arbitrary",)),
        name="ssd_sample",
    )(xbc, z, dt, hist, h0, *params)


def _proj_out_kernel(*refs, n, n_main, side_plan):
    lhs = refs[0:2 * n]
    ws = refs[2 * n:3 * n]
    x_ref, g_ref = refs[3 * n:3 * n + 2]
    ns = len(side_plan)
    side_in = refs[3 * n + 2:3 * n + 2 + ns]
    o_ref = refs[3 * n + 2 + ns]
    side_out = refs[3 * n + 3 + ns:]
    _side_cast(pl.program_id(0), side_plan, side_in, side_out)
    is_main = pl.program_id(0) < n_main
    m = None
    for k, w_ref in enumerate(ws):
        a = jnp.where(is_main, lhs[2 * k][...], lhs[2 * k + 1][...]).astype(BF16)
        p = jnp.dot(a, w_ref[...].astype(BF16), preferred_element_type=F32)
        m = p if m is None else m + p
    o_ref[...] = x_ref[...] + _rms(m, g_ref[...])


def _proj_out(lhs_pairs, ws, x, g, *, tm=512, side=()):
    nt = x.shape[0]
    side_plan, side_steps = _side_plan(side)
    assert side_steps <= nt // tm
    n = len(lhs_pairs)
    n_main = nt // tm - 1
    lhs_specs = []
    for a_main, a_extra in lhs_pairs:
        assert a_main.shape[0] == n_main * tm and a_extra.shape[0] == tm
        lhs_specs.append(pl.BlockSpec((tm, a_main.shape[1]), lambda i: (jnp.minimum(i, n_main - 1), 0)))
        lhs_specs.append(pl.BlockSpec((tm, a_extra.shape[1]), lambda i: (0, 0)))
    res = pl.pallas_call(
        functools.partial(_proj_out_kernel, n=n, n_main=n_main, side_plan=side_plan),
        grid=(nt // tm,),
        in_specs=lhs_specs + [_resident(w.shape) for w in ws]
                 + [pl.BlockSpec((tm, D), lambda i: (i, 0)), _resident((1, D))] + _side_specs(side, lambda i: i),
        out_specs=[pl.BlockSpec((tm, D), lambda i: (i, 0))] + _side_specs(side, lambda i: i),
        out_shape=[jax.ShapeDtypeStruct((nt, D), F32)] + _side_shapes(side),
        compiler_params=_cparams(("arbitrary",)),
        name="proj_out",
    )(*[a for pair in lhs_pairs for a in pair], *ws, x, g, *[w for w, _, _ in side])
    return tuple(res) if side else res[0]


def _norm_proj_kernel(*refs, n, heads, side_plan):
    x_ref, g_ref = refs[0:2]
    ws = refs[2:2 + n]
    ns = len(side_plan)
    side_in = refs[2 + n:2 + n + ns]
    outs = refs[2 + n + ns:2 + 2 * n + ns]
    side_out = refs[2 + 2 * n + ns:]
    _side_cast(pl.program_id(0), side_plan, side_in, side_out)
    hn = _rms(x_ref[...], g_ref[...]).astype(BF16)
    for w_ref, o_ref in zip(ws, outs):
        for h in range(XH):
            sl = slice(h * XD, (h + 1) * XD)
            r = jnp.dot(hn, w_ref[:, sl].astype(BF16), preferred_element_type=F32).astype(o_ref.dtype)
            if heads:
                o_ref[:, h, :] = r
            else:
                o_ref[:, sl] = r


def _norm_proj(x, g, ws, out_dtype, *, heads=False, tm=512, side=()):
    nt = x.shape[0]
    n = len(ws)
    side_plan, side_steps = _side_plan(side)
    if heads:
        tm = NM
        out_specs = [pl.BlockSpec((None, NM, XH, XD), lambda i: (i, 0, 0, 0))] * n
        out_shape = [jax.ShapeDtypeStruct((nt // NM, NM, XH, XD), out_dtype)] * n
    else:
        out_specs = [pl.BlockSpec((tm, D), lambda i: (i, 0))] * n
        out_shape = [jax.ShapeDtypeStruct((nt, D), out_dtype)] * n
    assert side_steps <= nt // tm
    return pl.pallas_call(
        functools.partial(_norm_proj_kernel, n=n, heads=heads, side_plan=side_plan),
        grid=(nt // tm,),
        in_specs=[pl.BlockSpec((tm, D), lambda i: (i, 0)), _resident((1, D))] + [_resident((D, D))] * n
                 + _side_specs(side, lambda i: i),
        out_specs=out_specs + _side_specs(side, lambda i: i),
        out_shape=out_shape + _side_shapes(side),
        compiler_params=_cparams(("arbitrary",)),
        name="norm_proj",
    )(x, g, *ws, *[w for w, _, _ in side])


NLT = XD // LANE
LT_STRIDE = NLT * XH
KV_ROWS = NM * LT_STRIDE


def _kv_tiles(x):
    nb = x.shape[0]
    return x.reshape(nb, NM, XH, NLT, LANE).transpose(0, 1, 3, 2, 4).reshape(nb, KV_ROWS, LANE)


def _attn_prompt_kernel(q_ref, k_ref, v_ref, o_ref, kh_ref, vh_ref):
    @pl.when(pl.program_id(1) == 0)
    def _():
        for h in range(XH):
            for lt in range(NLT):
                rows = pl.ds(lt * XH + h, NM, stride=LT_STRIDE)
                kh_ref[h, :, lt * LANE:(lt + 1) * LANE] = k_ref[0, rows, :].astype(BF16)
                vh_ref[h, :, lt * LANE:(lt + 1) * LANE] = v_ref[0, rows, :].astype(BF16)

    scale = XD ** -0.5
    nt_dims = (((1,), (1,)), ((), ()))
    heads = range(XH)
    s = [lax.dot_general(q_ref[:, h * XD:(h + 1) * XD], kh_ref[h], nt_dims, preferred_element_type=F32) * scale
         for h in heads]
    e = [jnp.exp(s[h] - jnp.max(s[h], axis=-1, keepdims=True)) for h in heads]
    p = [(e[h] / jnp.sum(e[h], axis=-1, keepdims=True)).astype(BF16) for h in heads]
    for h in heads:
        o_ref[:, h * XD:(h + 1) * XD] = jnp.dot(p[h], vh_ref[h], preferred_element_type=F32).astype(o_ref.dtype)


def _group_sum(x, col_lt):
    n = x.shape[1]
    a = x + jnp.where(col_lt % 2 == 0, pltpu.roll(x, n - 1, axis=1), pltpu.roll(x, 1, axis=1))
    return a + jnp.where(col_lt < 2, pltpu.roll(a, n - 2, axis=1), pltpu.roll(a, 2, axis=1))


def _attn_sample_kernel(q_ref, k_ref, v_ref, o_ref, *, nseq, tv):
    assert NLT == 4
    tq = q_ref.shape[0]
    ncol = NM * NLT
    scale = XD ** -0.5
    nt_dims = (((1,), (1,)), ((), ()))
    col_lt = lax.broadcasted_iota(jnp.int32, (tq, ncol), 1) % NLT
    rowi = lax.broadcasted_iota(jnp.int32, (tq, LANE), 0)
    for h in range(XH):
        qp = jnp.concatenate([q_ref[:, h * XD + lt * LANE:h * XD + (lt + 1) * LANE].astype(F32)
                              for lt in range(NLT)], axis=0)
        seqs = range(nseq)
        g = [lax.dot_general(qp, k_ref[j, pl.ds(h, ncol, stride=XH), :], nt_dims, preferred_element_type=F32)
             for j in seqs]
        s4 = [sum(jnp.where(col_lt == lt, g[j][lt * tq:(lt + 1) * tq], 0.0) for lt in range(NLT)) for j in seqs]
        s = [_group_sum(s4[j], col_lt) * scale for j in seqs]
        e = [jnp.exp(s[j] - jnp.max(s[j], axis=-1, keepdims=True)) for j in seqs]
        p = [e[j] / (jnp.sum(e[j], axis=-1, keepdims=True) * (1.0 / NLT)) for j in seqs]
        o = [jnp.dot(jnp.concatenate([jnp.where(col_lt == lt, p[j], 0.0) for lt in range(NLT)], axis=0),
                     v_ref[j, pl.ds(h, ncol, stride=XH), :], preferred_element_type=F32) for j in seqs]
        acc = [None] * NLT
        for j in seqs:
            mine = (rowi >= j * tv) & (rowi < (j + 1) * tv)
            for lt in range(NLT):
                acc[lt] = jnp.where(mine, o[j][lt * tq:(lt + 1) * tq], 0.0 if acc[lt] is None else acc[lt])
        for lt in range(NLT):
            o_ref[:, h * XD + lt * LANE:h * XD + (lt + 1) * LANE] = acc[lt].astype(o_ref.dtype)


def _attn_prompt(q, k, v, *, nb, seq, tq=512):
    nt = nb * seq
    nper = seq // tq
    kv_spec = pl.BlockSpec((1, KV_ROWS, LANE), lambda s, t: (s, 0, 0))
    return pl.pallas_call(
        _attn_prompt_kernel,
        grid=(nb, nper),
        in_specs=[pl.BlockSpec((tq, D), lambda s, t: (s * nper + t, 0)), kv_spec, kv_spec],
        out_specs=pl.BlockSpec((tq, D), lambda s, t: (s * nper + t, 0)),
        out_shape=jax.ShapeDtypeStruct((nt, D), BF16),
        scratch_shapes=[pltpu.VMEM((XH, NM, XD), BF16), pltpu.VMEM((XH, NM, XD), BF16)],
        compiler_params=_cparams(("arbitrary", "arbitrary")),
        name="attn_prompt",
    )(q, _kv_tiles(k), _kv_tiles(v))


def _attn_sample(q, k, v, *, row0, tv, nseq=4):
    nb = k.shape[0]
    rows = nseq * tv
    blk0 = row0 // rows
    kv_spec = pl.BlockSpec((nseq, KV_ROWS, LANE), lambda i: (i, 0, 0))
    return pl.pallas_call(
        functools.partial(_attn_sample_kernel, nseq=nseq, tv=tv),
        grid=(nb // nseq,),
        in_specs=[pl.BlockSpec((rows, D), lambda i: (blk0 + i, 0)), kv_spec, kv_spec],
        out_specs=pl.BlockSpec((rows, D), lambda i: (i, 0)),
        out_shape=jax.ShapeDtypeStruct((nb * tv, D), BF16),
        compiler_params=_cparams(("arbitrary",)),
        name="attn_sample",
    )(q, _kv_tiles(k), _kv_tiles(v))


def _row(v):
    return v.reshape(1, -1).astype(F32)


def _pad_lanes(v, n):
    return jnp.pad(v.reshape(1, -1).astype(F32), ((0, 0), (0, n - v.size)))


def _layer(xs, mem, cache_k, cache_v, st_conv, st_sconv, st_ssm, p, *, nbp, seq, nbs, tv, split_out):
    npr = nbp * seq
    bf = lambda w: w.astype(BF16)

    side = [(p['w_in'].T, 0, LANE)]
    x, w_in = _ffn(xs, _row(p['ffn1_pre_g']), bf(p['ffn1_w_gate']), bf(p['ffn1_w_up']), bf(p['ffn1_w_down']),
                   _row(p['ffn1_post_g']), split_out=False, side=side)
    w_out = bf(p['w_out'])

    wt = jnp.pad(w_in[2 * DC + DS + DX:], ((0, LANE - NH), (0, 0)))
    u, z, xbc, dtr = _mix_in(x, _row(p['mix_pre_g']), w_in, wt)

    cpar = (p['conv_w'].astype(F32), _row(p['conv_b']), _row(p['conv_ln_g']), _row(p['conv_ln_b']))
    a_p, conv_p = _conv_prompt(u, *cpar, nb=nbp, seq=seq)
    a_s, conv_s = _conv_sample(u, jnp.transpose(st_conv, (1, 0, 2)), *cpar, row0=npr, tv=tv)
    conv_s = jnp.transpose(conv_s, (1, 0, 2))

    spar = (p['ssm_conv_w'].astype(F32), _row(p['ssm_conv_b']), _pad_lanes(p['dt_bias'], LANE),
            _pad_lanes(p['a_log'], LANE), jnp.repeat(p['d_skip'].astype(F32), HP).reshape(1, DS),
            _row(p['ssm_norm_g']))
    y_p, sconv_p, ssm_p = _ssd_prompt(xbc, z, dtr, spar, nb=nbp, seq=seq)
    y_s, sconv_s, ssm_s = _ssd_sample(xbc, z, dtr, st_sconv, st_ssm, spar, row0=npr, tv=tv)

    x, wg2 = _proj_out([(a_p, a_s), (y_p, y_s)], [w_out[:DC], w_out[DC:]], x, _row(p['mix_post_g']),
                       side=[(p['ffn2_w_gate'], 1, SIDE_WIDE)])

    q, wu2 = _norm_proj(x, _row(p['xattn_pre_g']), [p['w_xq']], BF16, side=[(p['ffn2_w_up'], 1, SIDE_WIDE)])
    mk, mv = _norm_proj(mem, _row(p['mem_norm_g']), [p['w_xk'], p['w_xv']], F32, heads=True)
    o_p = _attn_prompt(q, mk, mv, nb=nbp, seq=seq)
    o_s = _attn_sample(q, cache_k, cache_v, row0=npr, tv=tv)
    x, wd2 = _proj_out([(o_p, o_s)], [p['w_xo']], x, _row(p['xattn_post_g']),
                       side=[(p['ffn2_w_down'], 0, SIDE_WIDE)])

    x = _ffn([x], _row(p['ffn2_pre_g']), wg2, wu2, wd2, _row(p['ffn2_post_g']), split_out=split_out)
    return x, (mk, mv, conv_p, sconv_p, ssm_p, conv_s, sconv_s, ssm_s)


def kernel(x_prompt, x_sample, mem_prompt, cache_mem_k, cache_mem_v, state_conv, state_ssm_conv, state_ssm, ffn1_pre_g, ffn1_w_gate, ffn1_w_up, ffn1_w_down, ffn1_post_g, mix_pre_g, w_in, conv_w, conv_b, conv_ln_g, conv_ln_b, ssm_conv_w, ssm_conv_b, dt_bias, a_log, d_skip, ssm_norm_g, w_out, mix_post_g, xattn_pre_g, mem_norm_g, w_xq, w_xk, w_xv, w_xo, xattn_post_g, ffn2_pre_g, ffn2_w_gate, ffn2_w_up, ffn2_w_down, ffn2_post_g):
    params = dict(ffn1_pre_g=ffn1_pre_g, ffn1_w_gate=ffn1_w_gate, ffn1_w_up=ffn1_w_up, ffn1_w_down=ffn1_w_down,
                  ffn1_post_g=ffn1_post_g, mix_pre_g=mix_pre_g, w_in=w_in, conv_w=conv_w, conv_b=conv_b,
                  conv_ln_g=conv_ln_g, conv_ln_b=conv_ln_b, ssm_conv_w=ssm_conv_w, ssm_conv_b=ssm_conv_b,
                  dt_bias=dt_bias, a_log=a_log, d_skip=d_skip, ssm_norm_g=ssm_norm_g, w_out=w_out,
                  mix_post_g=mix_post_g, xattn_pre_g=xattn_pre_g, mem_norm_g=mem_norm_g, w_xq=w_xq,
                  w_xk=w_xk, w_xv=w_xv, w_xo=w_xo, xattn_post_g=xattn_post_g, ffn2_pre_g=ffn2_pre_g,
                  ffn2_w_gate=ffn2_w_gate, ffn2_w_up=ffn2_w_up, ffn2_w_down=ffn2_w_down, ffn2_post_g=ffn2_post_g)
    depth = ffn1_pre_g.shape[0]
    nbp, seq, _ = x_prompt.shape
    nbs, tv, _ = x_sample.shape
    npr = nbp * seq
    xs = [x_prompt.reshape(npr, D), x_sample.reshape(nbs * tv, D)]
    mem = mem_prompt.reshape(nbp * NM, D)
    per_layer = []
    for layer in range(depth):
        p = {name: w[layer] for name, w in params.items()}
        x, states = _layer(xs, mem, cache_mem_k[layer], cache_mem_v[layer],
                           state_conv[layer], state_ssm_conv[layer], state_ssm[layer], p,
                           nbp=nbp, seq=seq, nbs=nbs, tv=tv, split_out=layer == depth - 1)
        xs = [x]
        per_layer.append(states)
    mk, mv, conv_p, sconv_p, ssm_p, conv_s, sconv_s, ssm_s = [jnp.stack(t) for t in zip(*per_layer)]
    yp, ys = x
    return (yp.reshape(nbp, seq, D), ys.reshape(nbs, tv, D), mk, mv, conv_p, sconv_p, ssm_p, conv_s, sconv_s, ssm_s)
```

```python
import functools

import jax
import jax.numpy as jnp
from jax import lax
from jax.experimental import pallas as pl
from jax.experimental.pallas import tpu as pltpu

F32 = jnp.float32
BF16 = jnp.bfloat16

D = 2048
FF = 5504
DC = 1024
DS = 1024
KC = 31
NH = 16
HP = 64
NG = 2
NS = 128
KS = 4
DX = DS + 2 * NG * NS
CHUNK = 128
NM = 256
XH = 4
XD = D // XH
EPS = 1e-6

LANE = 128
SUB = 8
HIST = 32
HALO = 8

FF_TILE = 1024
SIDE_WIDE = 3 * LANE

VMEM_LIMIT = 60 * 1024 * 1024


def _cparams(sem):
    return pltpu.CompilerParams(dimension_semantics=sem, vmem_limit_bytes=VMEM_LIMIT)


def _rms(x, g):
    return x * lax.rsqrt(jnp.mean(x * x, axis=-1, keepdims=True) + EPS) * g


def _silu(x):
    return x * jax.nn.sigmoid(x)


def _resident(shape):
    return pl.BlockSpec(shape, lambda *_: (0,) * len(shape), pipeline_mode=pl.Buffered(1))


def _ffn_kernel(*refs, n_in, n_out, n_main, tf, side_plan):
    n_side = len(side_plan)
    x_refs = refs[:n_in]
    pg_ref, wg_ref, wu_ref, wd_ref, qg_ref = refs[n_in:n_in + 5]
    side_in = refs[n_in + 5:n_in + 5 + n_side]
    k = n_in + 5 + n_side
    o_refs = refs[k:k + n_out]
    side_out = refs[k + n_out:k + n_out + n_side]
    xn_ref, acc_ref = refs[k + n_out + n_side:]
    i = pl.program_id(0)
    f = pl.program_id(1)
    last = pl.num_programs(1) - 1

    def x_tile():
        if n_in == 1:
            return x_refs[0][...]
        return jnp.where(i < n_main, x_refs[0][...], x_refs[1][...])

    @pl.when(f == 0)
    def _():
        xn_ref[...] = _rms(x_tile(), pg_ref[...]).astype(BF16)
        acc_ref[...] = jnp.zeros_like(acc_ref)

    def hidden_tile(width):
        xn = xn_ref[...]
        h = jnp.dot(xn, wg_ref[:, 0:width], preferred_element_type=F32)
        u = jnp.dot(xn, wu_ref[:, 0:width], preferred_element_type=F32)
        a = (_silu(h) * u).astype(BF16)
        acc_ref[...] += jnp.dot(a, wd_ref[0:width, :], preferred_element_type=F32)

    @pl.when(f < last)
    def _():
        hidden_tile(tf)

    @pl.when(f == last)
    def _():
        hidden_tile(FF - (FF // tf) * tf)
        res = x_tile() + 0.5 * _rms(acc_ref[...], qg_ref[...])
        if n_out == 1:
            o_refs[0][...] = res
        else:
            @pl.when(i < n_main)
            def _():
                o_refs[0][...] = res

            @pl.when(i >= n_main)
            def _():
                o_refs[1][...] = res

    _side_cast(i * pl.num_programs(1) + f, side_plan, side_in, side_out)


def _side_plan(side):
    plan, start = [], 0
    for w, axis, width in side:
        n = pl.cdiv(w.shape[axis], width)
        plan.append((start, n))
        start += n
    return tuple(plan), start


def _side_specs(side, step_of):
    specs = []
    for (w, axis, width), (start, n) in zip(side, _side_plan(side)[0]):
        strip = lambda *idx, start=start, n=n: jnp.clip(step_of(*idx) - start, 0, n - 1)
        if axis == 0:
            specs.append(pl.BlockSpec((width, w.shape[1]), lambda *idx, strip=strip: (strip(*idx), 0)))
        else:
            specs.append(pl.BlockSpec((w.shape[0], width), lambda *idx, strip=strip: (0, strip(*idx))))
    return specs


def _side_cast(step, side_plan, side_in, side_out):
    for m, (start, n) in enumerate(side_plan):
        @pl.when((step >= start) & (step < start + n))
        def _(m=m):
            side_out[m][...] = side_in[m][...].astype(BF16)


def _side_shapes(side):
    return [jax.ShapeDtypeStruct(w.shape, BF16) for w, _, _ in side]


def _ffn(xs, pre_g, wg, wu, wd, post_g, *, split_out, tm=512, tf=FF_TILE, side=()):
    n_in = len(xs)
    nt = sum(x.shape[0] for x in xs)
    n_main = (nt - tm) // tm
    nf = pl.cdiv(FF, tf)
    side_plan, side_steps = _side_plan(side)
    assert side_steps <= (nt // tm) * nf
    main = lambda i, f: (jnp.minimum(i, n_main - 1), 0)
    extra = lambda i, f: (0, 0)
    whole = lambda i, f: (i, 0)
    if n_in == 1:
        x_specs = [pl.BlockSpec((tm, D), whole)]
    else:
        x_specs = [pl.BlockSpec((tm, D), main), pl.BlockSpec((tm, D), extra)]
    if split_out:
        out_specs = [pl.BlockSpec((tm, D), main), pl.BlockSpec((tm, D), extra)]
        out_shape = [jax.ShapeDtypeStruct((n_main * tm, D), F32), jax.ShapeDtypeStruct((tm, D), F32)]
    else:
        out_specs = [pl.BlockSpec((tm, D), whole)]
        out_shape = [jax.ShapeDtypeStruct((nt, D), F32)]
    res = pl.pallas_call(
        functools.partial(_ffn_kernel, n_in=n_in, n_out=len(out_specs), n_main=n_main, tf=tf, side_plan=side_plan),
        grid=(nt // tm, nf),
        in_specs=x_specs + [
            pl.BlockSpec((1, D), lambda i, f: (0, 0)),
            pl.BlockSpec((D, tf), lambda i, f: (0, f)),
            pl.BlockSpec((D, tf), lambda i, f: (0, f)),
            pl.BlockSpec((tf, D), lambda i, f: (f, 0)),
            pl.BlockSpec((1, D), lambda i, f: (0, 0)),
        ] + _side_specs(side, lambda i, f: i * nf + f),
        out_specs=out_specs + _side_specs(side, lambda i, f: i * nf + f),
        out_shape=out_shape + _side_shapes(side),
        scratch_shapes=[pltpu.VMEM((tm, D), BF16), pltpu.VMEM((tm, D), F32)],
        compiler_params=_cparams(("arbitrary", "arbitrary")),
        name="ffn",
    )(*xs, pre_g, wg, wu, wd, post_g, *[w for w, _, _ in side])
    main_res = res[0] if len(out_specs) == 1 else tuple(res[:2])
    return (main_res, *res[len(out_specs):]) if side else main_res


def _mix_in_kernel(x_ref, g_ref, w_ref, wt_ref, u_ref, z_ref, xbc_ref, dt_ref, *, tn):
    hn = _rms(x_ref[...], g_ref[...]).astype(BF16)
    nt_dims = (((1,), (1,)), ((), ()))

    def cols(start, c):
        return lax.dot_general(hn, w_ref[start + c * tn:start + (c + 1) * tn, :], nt_dims,
                               preferred_element_type=F32)

    for c in range(DC // tn):
        u_ref[:, c * tn:(c + 1) * tn] = cols(0, c) * jax.nn.sigmoid(cols(DC, c))
    for c in range(DS // tn):
        z_ref[:, c * tn:(c + 1) * tn] = cols(2 * DC, c)
    for c in range(DX // tn):
        xbc_ref[:, c * tn:(c + 1) * tn] = cols(2 * DC + DS, c)
    dt_ref[...] = lax.dot_general(hn, wt_ref[...], nt_dims, preferred_element_type=F32)


def _mix_in(x, g, w, wt, *, tm=512, tn=512):
    nt = x.shape[0]
    row = lambda n: pl.BlockSpec((tm, n), lambda i: (i, 0))
    return pl.pallas_call(
        functools.partial(_mix_in_kernel, tn=tn),
        grid=(nt // tm,),
        in_specs=[row(D), _resident((1, D)), _resident(w.shape), _resident((LANE, D))],
        out_specs=[row(DC), row(DS), row(DX), row(LANE)],
        out_shape=[jax.ShapeDtypeStruct((nt, n), F32) for n in (DC, DS, DX, LANE)],
        compiler_params=_cparams(("parallel",)),
        name="mix_in",
    )(x, g, w, wt)


def _conv_prompt_kernel(u_ref, w_ref, b_ref, lg_ref, lb_ref, a_ref, nb_ref, xe_ref, y_ref, *, tt):
    t = pl.program_id(1)
    off = HIST - (KC - 1)

    @pl.when(t == 0)
    def _():
        xe_ref[:, 0:HIST, :] = jnp.zeros((DC // LANE, HIST, LANE), F32)

    for j in range(DC // LANE):
        xe_ref[j, HIST:HIST + tt, :] = u_ref[:, j * LANE:(j + 1) * LANE]
    for j in range(DC // LANE):
        sl = slice(j * LANE, (j + 1) * LANE)
        acc = jnp.broadcast_to(b_ref[:, sl], (tt, LANE))
        for k in range(KC):
            acc = acc + w_ref[k:k + 1, sl] * xe_ref[j, off + k:off + k + tt, :]
        y_ref[:, sl] = acc
    for j in range(DC // LANE):
        nb_ref[:, j * LANE:(j + 1) * LANE] = xe_ref[j, HIST + tt - (KC - 1):HIST + tt, :]
        xe_ref[j, 0:HIST, :] = xe_ref[j, tt:tt + HIST, :]
    @pl.when(t >= 0)
    def _():
        y = y_ref[...]
        yc = y - jnp.mean(y, axis=-1, keepdims=True)
        a_ref[...] = _silu(yc * lax.rsqrt(jnp.mean(yc * yc, axis=-1, keepdims=True) + EPS) * lg_ref[...]
                           + lb_ref[...]).astype(a_ref.dtype)


def _conv_prompt(u, w, b, lg, lb, *, nb, seq, tt=256):
    nt = nb * seq
    nper = seq // tt
    par = lambda r: _resident((r, DC))
    return pl.pallas_call(
        functools.partial(_conv_prompt_kernel, tt=tt),
        grid=(nb, nper),
        in_specs=[pl.BlockSpec((tt, DC), lambda s, t: (s * nper + t, 0)), par(KC), par(1), par(1), par(1)],
        out_specs=[pl.BlockSpec((tt, DC), lambda s, t: (s * nper + t, 0)),
                   pl.BlockSpec((None, KC - 1, DC), lambda s, t: (s, 0, 0))],
        out_shape=[jax.ShapeDtypeStruct((nt, DC), BF16), jax.ShapeDtypeStruct((nb, KC - 1, DC), F32)],
        scratch_shapes=[pltpu.VMEM((DC // LANE, HIST + tt, LANE), F32), pltpu.VMEM((tt, DC), F32)],
        compiler_params=_cparams(("parallel", "arbitrary")),
        name="conv_prompt",
    )(u, w, b, lg, lb)


def _conv_sample_kernel(u_ref, hist_ref, w_ref, b_ref, lg_ref, lb_ref, a_ref, nh_ref, us_ref, y_ref, as_ref,
                        *, sb, tv):
    nl = DC // LANE
    for j in range(nl):
        us_ref[j] = u_ref[:, j * LANE:(j + 1) * LANE]
    for j in range(nl):
        sl = slice(j * LANE, (j + 1) * LANE)
        accs = [jnp.broadcast_to(b_ref[:, sl], (sb, LANE)) for _ in range(tv)]
        for m in range(KC - 1 + tv):
            if m < KC - 1:
                xm = hist_ref[m, :, sl]
            else:
                xm = us_ref[j, pl.ds(m - (KC - 1), sb, stride=tv), :]
            for t in range(tv):
                if 0 <= m - t < KC:
                    accs[t] = accs[t] + w_ref[m - t:m - t + 1, sl] * xm
            if m >= tv:
                nh_ref[m - tv, :, sl] = xm
        for t in range(tv):
            y_ref[t, :, sl] = accs[t]
    for t in range(tv):
        y = y_ref[t]
        yc = y - jnp.mean(y, axis=-1, keepdims=True)
        a = _silu(yc * lax.rsqrt(jnp.mean(yc * yc, axis=-1, keepdims=True) + EPS) * lg_ref[...] + lb_ref[...])
        for j in range(nl):
            as_ref[j, pl.ds(t, sb, stride=tv), :] = a[:, j * LANE:(j + 1) * LANE]
    for j in range(nl):
        a_ref[:, j * LANE:(j + 1) * LANE] = as_ref[j].astype(a_ref.dtype)


def _conv_sample(u, hist, w, b, lg, lb, *, row0, tv, sb=32):
    nb = hist.shape[1]
    rows = sb * tv
    par = lambda r: _resident((r, DC))
    blk0 = row0 // rows
    hist_spec = pl.BlockSpec((KC - 1, sb, DC), lambda i: (0, i, 0))
    return pl.pallas_call(
        functools.partial(_conv_sample_kernel, sb=sb, tv=tv),
        grid=(nb // sb,),
        in_specs=[pl.BlockSpec((rows, DC), lambda i: (blk0 + i, 0)), hist_spec, par(KC), par(1), par(1), par(1)],
        out_specs=[pl.BlockSpec((rows, DC), lambda i: (i, 0)), hist_spec],
        out_shape=[jax.ShapeDtypeStruct((nb * tv, DC), BF16), jax.ShapeDtypeStruct((KC - 1, nb, DC), F32)],
        scratch_shapes=[pltpu.VMEM((DC // LANE, rows, LANE), F32), pltpu.VMEM((tv, sb, DC), F32),
                        pltpu.VMEM((DC // LANE, rows, LANE), F32)],
        compiler_params=_cparams(("arbitrary",)),
        name="conv_sample",
    )(u, hist, w, b, lg, lb)


def _dot01(a, b, dims, *, data):
    x = b if data else a
    one = (a if data else b).astype(BF16)
    t0 = x.astype(BF16)
    r1 = x - t0.astype(F32)
    t1 = r1.astype(BF16)
    t2 = (r1 - t1.astype(F32)).astype(BF16)
    acc = None
    for t in (t0, t1, t2):
        lhs, rhs = (one, t) if data else (t, one)
        p = lax.dot_general(lhs, rhs, dims, preferred_element_type=F32)
        acc = p if acc is None else acc + p
    return acc


def _to_slabs(xe_ref, r0, x):
    for j in range(x.shape[1] // LANE):
        xe_ref[j, r0:r0 + x.shape[0], :] = x[:, j * LANE:(j + 1) * LANE]


def _from_slabs(xe_ref, r0, rows):
    return jnp.concatenate([xe_ref[j, r0:r0 + rows, :] for j in range(xe_ref.shape[0])], axis=1)


def _ssd_chunk(xe_ref, z, dt_raw, st_ref, cw_ref, cb_ref, dtb_ref, alog_ref, dsk_ref, ng_ref, *, L, tv, lq=None,
               h_io=None):
    mm_dims = (((1,), (0,)), ((), ()))
    lq = L if lq is None else lq
    nsq = L // lq
    off = HALO - (KS - 1)
    cols = []
    for j in range(DX // LANE):
        sl = slice(j * LANE, (j + 1) * LANE)
        acc = jnp.broadcast_to(cb_ref[:, sl], (L, LANE))
        for k in range(KS):
            acc = acc + cw_ref[k:k + 1, sl] * xe_ref[j, off + k:off + k + L, :]
        cols.append(acc)
    xc = _silu(jnp.concatenate(cols, axis=1))
    xs = xc[:, 0:DS]

    lane = lax.broadcasted_iota(jnp.int32, (L, LANE), 1)
    rowi = lax.broadcasted_iota(jnp.int32, (L, LANE), 0)
    xdt = dt_raw + dtb_ref[...]
    dt = jnp.maximum(xdt, 0.0) + jnp.log1p(jnp.exp(-jnp.abs(xdt)))
    dt = jnp.where((lane < NH) & (rowi % lq < tv), dt, 0.0)
    da = dt * (-jnp.exp(alog_ref[...]))

    r2 = lax.broadcasted_iota(jnp.int32, (L, L), 0)
    c2 = lax.broadcasted_iota(jnp.int32, (L, L), 1)
    same = r2 // lq == c2 // lq
    causal = (r2 >= c2) & same
    if nsq == 1:
        a_cum = _dot01(causal.astype(F32), da, mm_dims, data=1)
        a_tot = jnp.broadcast_to(a_cum[L - 1:L, :], (L, LANE))
    else:
        cums = _dot01(jnp.concatenate([causal.astype(F32), same.astype(F32)], axis=0), da, mm_dims, data=1)
        a_cum = cums[0:L]
        a_tot = cums[L:2 * L]

    er = lax.broadcasted_iota(jnp.int32, (LANE, DS), 0)
    ec = lax.broadcasted_iota(jnp.int32, (LANE, DS), 1)
    expand = (ec // HP == er).astype(F32)
    stack = jnp.concatenate([jnp.exp(a_cum), jnp.exp(a_tot - a_cum) * dt, jnp.exp(a_tot[0:SUB])], axis=0)
    stack_x = _dot01(stack, expand, mm_dims, data=0)
    ea_x = stack_x[0:L]
    wend_x = stack_x[L:2 * L]
    cd_x = stack_x[2 * L:2 * L + 1]

    ir = lax.broadcasted_iota(jnp.int32, (LANE, LANE), 0)
    ic = lax.broadcasted_iota(jnp.int32, (LANE, LANE), 1)
    ident = (ir == ic).astype(F32)
    tr = _dot01(ident, jnp.concatenate([dt, a_cum], axis=0), (((1,), (1,)), ((), ())), data=1)
    dt_t = tr[:, 0:L]
    acum_t = tr[:, L:2 * L]

    lane_x = lax.broadcasted_iota(jnp.int32, (L, LANE), 1)
    hpg = NH // NG
    bms = [xc[:, DS + g * NS:DS + (g + 1) * NS] for g in range(NG)]
    cms = [xc[:, DS + NG * NS + g * NS:DS + NG * NS + (g + 1) * NS] for g in range(NG)]
    gss = [slice(g * (DS // NG), (g + 1) * (DS // NG)) for g in range(NG)]
    cbms = [lax.dot_general(cms[g], bms[g], (((1,), (1,)), ((), ())), preferred_element_type=F32)
            for g in range(NG)]
    if h_io is None:
        st_old = [st_ref[:, gss[g]] for g in range(NG)]
        y_off = [jnp.dot(cms[g], st_old[g], preferred_element_type=F32) * ea_x[:, gss[g]] for g in range(NG)]
        st_new = [jnp.dot(bms[g].T, xs[:, gss[g]] * wend_x[:, gss[g]], preferred_element_type=F32)
                  for g in range(NG)]
        for g in range(NG):
            st_ref[:, gss[g]] = st_old[g] * cd_x[:, gss[g]] + st_new[g]
    else:
        rows_g = hpg * HP
        sq = [slice(q * lq, (q + 1) * lq) for q in range(nsq)]
        hs = [[h_io[q][0][g * hpg:(g + 1) * hpg].reshape(rows_g, NS) for g in range(NG)] for q in range(nsq)]
        y_off = [jnp.concatenate([lax.dot_general(cms[g][sq[q]], hs[q][g], (((1,), (1,)), ((), ())),
                                                  preferred_element_type=F32) for q in range(nsq)], axis=0)
                 * ea_x[:, gss[g]] for g in range(NG)]
        xw = [xs[:, gss[g]] * wend_x[:, gss[g]] for g in range(NG)]
        upd = [[lax.dot_general(xw[g][sq[q]], bms[g][sq[q]], (((0,), (0,)), ((), ())),
                                preferred_element_type=F32) for g in range(NG)] for q in range(nsq)]
        seq_decay = jnp.exp(a_tot)
        for q in range(nsq):
            for h in range(NH):
                g, hl = divmod(h, hpg)
                rows = slice(hl * HP, (hl + 1) * HP)
                h_io[q][1][h] = hs[q][g][rows, :] * seq_decay[q * lq:q * lq + 1, h:h + 1] + upd[q][g][rows, :]
    segs = [a_cum[:, h:h + 1] - acum_t[h:h + 1, :] for h in range(NH)]
    decs = [jnp.exp(jnp.where(causal, segs[h], -jnp.inf)) for h in range(NH)]
    ws = [cbms[h // hpg] * decs[h] * dt_t[h:h + 1, :] for h in range(NH)]
    y_diag = []
    for pr in range(NH // 2):
        xp = xs[:, 2 * pr * HP:(2 * pr + 2) * HP]
        rhs = jnp.concatenate([jnp.where(lane_x < HP, xp, 0.0), jnp.where(lane_x >= HP, xp, 0.0)], axis=0)
        y_diag.append(jnp.dot(jnp.concatenate([ws[2 * pr], ws[2 * pr + 1]], axis=1), rhs,
                              preferred_element_type=F32))
    y = jnp.concatenate(y_diag, axis=1) + jnp.concatenate(y_off, axis=1) + dsk_ref[...] * xs
    y = y * _silu(z)
    outs = []
    for g in range(NG):
        gs = slice(g * (DS // NG), (g + 1) * (DS // NG))
        outs.append(_rms(y[:, gs], ng_ref[:, gs]))
    return jnp.concatenate(outs, axis=1)


def _state_out(st_ref, h_ref):
    for j in range(DS // LANE):
        blk = st_ref[:, j * LANE:(j + 1) * LANE].T
        for q in range(LANE // HP):
            h_ref[j * (LANE // HP) + q] = blk[q * HP:(q + 1) * HP, :]


def _ssd_prompt_kernel(xbc_ref, z_ref, dt_ref, cw_ref, cb_ref, dtb_ref, alog_ref, dsk_ref, ng_ref,
                       y_ref, nb_ref, h_ref, xe_ref, st_ref, *, L):
    c = pl.program_id(1)

    @pl.when(c == 0)
    def _():
        xe_ref[:, 0:HALO, :] = jnp.zeros((DX // LANE, HALO, LANE), F32)
        st_ref[...] = jnp.zeros_like(st_ref)

    _to_slabs(xe_ref, HALO, xbc_ref[...])
    y_ref[...] = _ssd_chunk(xe_ref, z_ref[...], dt_ref[...], st_ref, cw_ref, cb_ref, dtb_ref, alog_ref, dsk_ref,
                            ng_ref, L=L, tv=L).astype(y_ref.dtype)
    xe_ref[:, 0:HALO, :] = xe_ref[:, L:L + HALO, :]

    @pl.when(c == pl.num_programs(1) - 1)
    def _():
        nb_ref[...] = _from_slabs(xe_ref, HALO - (KS - 1), KS - 1)
        _state_out(st_ref, h_ref)


def _ssd_params_specs():
    return [_resident((KS, DX)), _resident((1, DX)), _resident((1, LANE)), _resident((1, LANE)),
            _resident((1, DS)), _resident((1, DS))]


def _ssd_prompt(xbc, z, dt, params, *, nb, seq):
    nt = nb * seq
    L = CHUNK
    nper = seq // L
    row = lambda n: pl.BlockSpec((L, n), lambda s, c: (s * nper + c, 0))
    return pl.pallas_call(
        functools.partial(_ssd_prompt_kernel, L=L),
        grid=(nb, nper),
        in_specs=[row(DX), row(DS), row(LANE)] + _ssd_params_specs(),
        out_specs=[row(DS),
                   pl.BlockSpec((None, KS - 1, DX), lambda s, c: (s, 0, 0)),
                   pl.BlockSpec((None, NH, HP, NS), lambda s, c: (s, 0, 0, 0))],
        out_shape=[jax.ShapeDtypeStruct((nt, DS), BF16), jax.ShapeDtypeStruct((nb, KS - 1, DX), F32),
                   jax.ShapeDtypeStruct((nb, NH, HP, NS), F32)],
        scratch_shapes=[pltpu.VMEM((DX // LANE, HALO + L, LANE), F32), pltpu.VMEM((NS, DS), F32)],
        compiler_params=_cparams(("parallel", "arbitrary")),
        name="ssd_prompt",
    )(xbc, z, dt, *params)


def _ssd_sample_kernel(xbc_ref, z_ref, dt_ref, hist_ref, h0_ref, cw_ref, cb_ref, dtb_ref, alog_ref, dsk_ref,
                       ng_ref, y_ref, nb_ref, h_ref, xe_ref, zb_ref, dtp_ref, *, lq, tv, nseq):
    assert lq - tv >= KS - 1 and HALO >= KS - 1
    xe_ref[...] = jnp.zeros_like(xe_ref)
    zb_ref[...] = jnp.zeros_like(zb_ref)
    dtp_ref[...] = jnp.zeros_like(dtp_ref)
    for j in range(nseq):
        rows = slice(j * tv, (j + 1) * tv)
        _to_slabs(xe_ref, HALO + j * lq - (KS - 1), hist_ref[j])
        _to_slabs(xe_ref, HALO + j * lq, xbc_ref[rows, :])
        zb_ref[j * lq:j * lq + tv, :] = z_ref[rows, :]
        dtp_ref[j * lq:j * lq + tv, :] = dt_ref[rows, :]
    y = _ssd_chunk(xe_ref, zb_ref[...], dtp_ref[...], None, cw_ref, cb_ref, dtb_ref, alog_ref, dsk_ref, ng_ref,
                   L=nseq * lq, tv=tv, lq=lq, h_io=[(h0_ref.at[j], h_ref.at[j]) for j in range(nseq)])
    y_ref[...] = jnp.concatenate([y[j * lq:j * lq + tv, :] for j in range(nseq)], axis=0).astype(y_ref.dtype)
    for j in range(nseq):
        nb_ref[j] = _from_slabs(xe_ref, HALO + j * lq + tv - (KS - 1), KS - 1)


def _ssd_sample(xbc, z, dt, hist, h0, params, *, row0, tv, nseq=8, lq=SUB):
    nb = hist.shape[0]
    rows = nseq * tv
    L = nseq * lq
    blk0 = row0 // rows
    row = lambda n: pl.BlockSpec((rows, n), lambda i: (blk0 + i, 0))
    return pl.pallas_call(
        functools.partial(_ssd_sample_kernel, lq=lq, tv=tv, nseq=nseq),
        grid=(nb // nseq,),
        in_specs=[row(DX), row(DS), row(LANE),
                  pl.BlockSpec((nseq, KS - 1, DX), lambda i: (i, 0, 0)),
                  pl.BlockSpec((nseq, NH, HP, NS), lambda i: (i, 0, 0, 0))] + _ssd_params_specs(),
        out_specs=[pl.BlockSpec((rows, DS), lambda i: (i, 0)),
                   pl.BlockSpec((nseq, KS - 1, DX), lambda i: (i, 0, 0)),
                   pl.BlockSpec((nseq, NH, HP, NS), lambda i: (i, 0, 0, 0))],
        out_shape=[jax.ShapeDtypeStruct((nb * tv, DS), BF16), jax.ShapeDtypeStruct((nb, KS - 1, DX), F32),
                   jax.ShapeDtypeStruct((nb, NH, HP, NS), F32)],
        scratch_shapes=[pltpu.VMEM((DX // LANE, HALO + L, LANE), F32), pltpu.VMEM((L, DS), F32),
                        pltpu.VMEM((L, LANE), F32)],
        compiler_params=_cparams(("arbitrary",)),
        name="ssd_sample",
    )(xbc, z, dt, hist, h0, *params)


def _proj_out_kernel(*refs, n, n_main, side_plan):
    lhs = refs[0:2 * n]
    ws = refs[2 * n:3 * n]
    x_ref, g_ref = refs[3 * n:3 * n + 2]
    ns = len(side_plan)
    side_in = refs[3 * n + 2:3 * n + 2 + ns]
    o_ref = refs[3 * n + 2 + ns]
    side_out = refs[3 * n + 3 + ns:]
    _side_cast(pl.program_id(0), side_plan, side_in, side_out)
    is_main = pl.program_id(0) < n_main
    m = None
    for k, w_ref in enumerate(ws):
        a = jnp.where(is_main, lhs[2 * k][...], lhs[2 * k + 1][...]).astype(BF16)
        p = jnp.dot(a, w_ref[...].astype(BF16), preferred_element_type=F32)
        m = p if m is None else m + p
    o_ref[...] = x_ref[...] + _rms(m, g_ref[...])


def _proj_out(lhs_pairs, ws, x, g, *, tm=512, side=()):
    nt = x.shape[0]
    side_plan, side_steps = _side_plan(side)
    assert side_steps <= nt // tm
    n = len(lhs_pairs)
    n_main = nt // tm - 1
    lhs_specs = []
    for a_main, a_extra in lhs_pairs:
        assert a_main.shape[0] == n_main * tm and a_extra.shape[0] == tm
        lhs_specs.append(pl.BlockSpec((tm, a_main.shape[1]), lambda i: (jnp.minimum(i, n_main - 1), 0)))
        lhs_specs.append(pl.BlockSpec((tm, a_extra.shape[1]), lambda i: (0, 0)))
    res = pl.pallas_call(
        functools.partial(_proj_out_kernel, n=n, n_main=n_main, side_plan=side_plan),
        grid=(nt // tm,),
        in_specs=lhs_specs + [_resident(w.shape) for w in ws]
                 + [pl.BlockSpec((tm, D), lambda i: (i, 0)), _resident((1, D))] + _side_specs(side, lambda i: i),
        out_specs=[pl.BlockSpec((tm, D), lambda i: (i, 0))] + _side_specs(side, lambda i: i),
        out_shape=[jax.ShapeDtypeStruct((nt, D), F32)] + _side_shapes(side),
        compiler_params=_cparams(("arbitrary",)),
        name="proj_out",
    )(*[a for pair in lhs_pairs for a in pair], *ws, x, g, *[w for w, _, _ in side])
    return tuple(res) if side else res[0]


def _norm_proj_kernel(*refs, n, heads, side_plan):
    x_ref, g_ref = refs[0:2]
    ws = refs[2:2 + n]
    ns = len(side_plan)
    side_in = refs[2 + n:2 + n + ns]
    outs = refs[2 + n + ns:2 + 2 * n + ns]
    side_out = refs[2 + 2 * n + ns:]
    _side_cast(pl.program_id(0), side_plan, side_in, side_out)
    hn = _rms(x_ref[...], g_ref[...]).astype(BF16)
    for w_ref, o_ref in zip(ws, outs):
        for h in range(XH):
            sl = slice(h * XD, (h + 1) * XD)
            r = jnp.dot(hn, w_ref[:, sl].astype(BF16), preferred_element_type=F32).astype(o_ref.dtype)
            if heads:
                o_ref[:, h, :] = r
            else:
                o_ref[:, sl] = r


def _norm_proj(x, g, ws, out_dtype, *, heads=False, tm=512, side=()):
    nt = x.shape[0]
    n = len(ws)
    side_plan, side_steps = _side_plan(side)
    if heads:
        tm = NM
        out_specs = [pl.BlockSpec((None, NM, XH, XD), lambda i: (i, 0, 0, 0))] * n
        out_shape = [jax.ShapeDtypeStruct((nt // NM, NM, XH, XD), out_dtype)] * n
    else:
        out_specs = [pl.BlockSpec((tm, D), lambda i: (i, 0))] * n
        out_shape = [jax.ShapeDtypeStruct((nt, D), out_dtype)] * n
    assert side_steps <= nt // tm
    return pl.pallas_call(
        functools.partial(_norm_proj_kernel, n=n, heads=heads, side_plan=side_plan),
        grid=(nt // tm,),
        in_specs=[pl.BlockSpec((tm, D), lambda i: (i, 0)), _resident((1, D))] + [_resident((D, D))] * n
                 + _side_specs(side, lambda i: i),
        out_specs=out_specs + _side_specs(side, lambda i: i),
        out_shape=out_shape + _side_shapes(side),
        compiler_params=_cparams(("arbitrary",)),
        name="norm_proj",
    )(x, g, *ws, *[w for w, _, _ in side])


NLT = XD // LANE
LT_STRIDE = NLT * XH
KV_ROWS = NM * LT_STRIDE


def _kv_tiles(x):
    nb = x.shape[0]
    return x.reshape(nb, NM, XH, NLT, LANE).transpose(0, 1, 3, 2, 4).reshape(nb, KV_ROWS, LANE)


def _attn_prompt_kernel(q_ref, k_ref, v_ref, o_ref, kh_ref, vh_ref):
    @pl.when(pl.program_id(1) == 0)
    def _():
        for h in range(XH):
            for lt in range(NLT):
                rows = pl.ds(lt * XH + h, NM, stride=LT_STRIDE)
                kh_ref[h, :, lt * LANE:(lt + 1) * LANE] = k_ref[0, rows, :].astype(BF16)
                vh_ref[h, :, lt * LANE:(lt + 1) * LANE] = v_ref[0, rows, :].astype(BF16)

    scale = XD ** -0.5
    nt_dims = (((1,), (1,)), ((), ()))
    heads = range(XH)
    s = [lax.dot_general(q_ref[:, h * XD:(h + 1) * XD], kh_ref[h], nt_dims, preferred_element_type=F32) * scale
         for h in heads]
    e = [jnp.exp(s[h] - jnp.max(s[h], axis=-1, keepdims=True)) for h in heads]
    p = [(e[h] / jnp.sum(e[h], axis=-1, keepdims=True)).astype(BF16) for h in heads]
    for h in heads:
        o_ref[:, h * XD:(h + 1) * XD] = jnp.dot(p[h], vh_ref[h], preferred_element_type=F32).astype(o_ref.dtype)


def _group_sum(x, col_lt):
    n = x.shape[1]
    a = x + jnp.where(col_lt % 2 == 0, pltpu.roll(x, n - 1, axis=1), pltpu.roll(x, 1, axis=1))
    return a + jnp.where(col_lt < 2, pltpu.roll(a, n - 2, axis=1), pltpu.roll(a, 2, axis=1))


def _attn_sample_kernel(q_ref, k_ref, v_ref, o_ref, *, nseq, tv):
    assert NLT == 4
    tq = q_ref.shape[0]
    ncol = NM * NLT
    scale = XD ** -0.5
    nt_dims = (((1,), (1,)), ((), ()))
    col_lt = lax.broadcasted_iota(jnp.int32, (tq, ncol), 1) % NLT
    rowi = lax.broadcasted_iota(jnp.int32, (tq, LANE), 0)
    for h in range(XH):
        qp = jnp.concatenate([q_ref[:, h * XD + lt * LANE:h * XD + (lt + 1) * LANE].astype(F32)
                              for lt in range(NLT)], axis=0)
        seqs = range(nseq)
        g = [lax.dot_general(qp, k_ref[j, pl.ds(h, ncol, stride=XH), :], nt_dims, preferred_element_type=F32)
             for j in seqs]
        s4 = [sum(jnp.where(col_lt == lt, g[j][lt * tq:(lt + 1) * tq], 0.0) for lt in range(NLT)) for j in seqs]
        s = [_group_sum(s4[j], col_lt) * scale for j in seqs]
        e = [jnp.exp(s[j] - jnp.max(s[j], axis=-1, keepdims=True)) for j in seqs]
        p = [e[j] / (jnp.sum(e[j], axis=-1, keepdims=True) * (1.0 / NLT)) for j in seqs]
        o = [jnp.dot(jnp.concatenate([jnp.where(col_lt == lt, p[j], 0.0) for lt in range(NLT)], axis=0),
                     v_ref[j, pl.ds(h, ncol, stride=XH), :], preferred_element_type=F32) for j in seqs]
        acc = [None] * NLT
        for j in seqs:
            mine = (rowi >= j * tv) & (rowi < (j + 1) * tv)
            for lt in range(NLT):
                acc[lt] = jnp.where(mine, o[j][lt * tq:(lt + 1) * tq], 0.0 if acc[lt] is None else acc[lt])
        for lt in range(NLT):
            o_ref[:, h * XD + lt * LANE:h * XD + (lt + 1) * LANE] = acc[lt].astype(o_ref.dtype)


def _attn_prompt(q, k, v, *, nb, seq, tq=1024):
    nt = nb * seq
    nper = seq // tq
    kv_spec = pl.BlockSpec((1, KV_ROWS, LANE), lambda s, t: (s, 0, 0))
    return pl.pallas_call(
        _attn_prompt_kernel,
        grid=(nb, nper),
        in_specs=[pl.BlockSpec((tq, D), lambda s, t: (s * nper + t, 0)), kv_spec, kv_spec],
        out_specs=pl.BlockSpec((tq, D), lambda s, t: (s * nper + t, 0)),
        out_shape=jax.ShapeDtypeStruct((nt, D), BF16),
        scratch_shapes=[pltpu.VMEM((XH, NM, XD), BF16), pltpu.VMEM((XH, NM, XD), BF16)],
        compiler_params=_cparams(("arbitrary", "arbitrary")),
        name="attn_prompt",
    )(q, _kv_tiles(k), _kv_tiles(v))


def _attn_sample(q, k, v, *, row0, tv, nseq=4):
    nb = k.shape[0]
    rows = nseq * tv
    blk0 = row0 // rows
    kv_spec = pl.BlockSpec((nseq, KV_ROWS, LANE), lambda i: (i, 0, 0))
    return pl.pallas_call(
        functools.partial(_attn_sample_kernel, nseq=nseq, tv=tv),
        grid=(nb // nseq,),
        in_specs=[pl.BlockSpec((rows, D), lambda i: (blk0 + i, 0)), kv_spec, kv_spec],
        out_specs=pl.BlockSpec((rows, D), lambda i: (i, 0)),
        out_shape=jax.ShapeDtypeStruct((nb * tv, D), BF16),
        compiler_params=_cparams(("arbitrary",)),
        name="attn_sample",
    )(q, _kv_tiles(k), _kv_tiles(v))


def _row(v):
    return v.reshape(1, -1).astype(F32)


def _pad_lanes(v, n):
    return jnp.pad(v.reshape(1, -1).astype(F32), ((0, 0), (0, n - v.size)))


def _layer(xs, mem, cache_k, cache_v, st_conv, st_sconv, st_ssm, p, *, nbp, seq, nbs, tv, split_out):
    npr = nbp * seq
    bf = lambda w: w.astype(BF16)

    side = [(p['w_in'].T, 0, LANE)]
    x, w_in = _ffn(xs, _row(p['ffn1_pre_g']), bf(p['ffn1_w_gate']), bf(p['ffn1_w_up']), bf(p['ffn1_w_down']),
                   _row(p['ffn1_post_g']), split_out=False, side=side)
    w_out = bf(p['w_out'])

    wt = jnp.pad(w_in[2 * DC + DS + DX:], ((0, LANE - NH), (0, 0)))
    u, z, xbc, dtr = _mix_in(x, _row(p['mix_pre_g']), w_in, wt)

    cpar = (p['conv_w'].astype(F32), _row(p['conv_b']), _row(p['conv_ln_g']), _row(p['conv_ln_b']))
    a_p, conv_p = _conv_prompt(u, *cpar, nb=nbp, seq=seq)
    a_s, conv_s = _conv_sample(u, jnp.transpose(st_conv, (1, 0, 2)), *cpar, row0=npr, tv=tv)
    conv_s = jnp.transpose(conv_s, (1, 0, 2))

    spar = (p['ssm_conv_w'].astype(F32), _row(p['ssm_conv_b']), _pad_lanes(p['dt_bias'], LANE),
            _pad_lanes(p['a_log'], LANE), jnp.repeat(p['d_skip'].astype(F32), HP).reshape(1, DS),
            _row(p['ssm_norm_g']))
    y_p, sconv_p, ssm_p = _ssd_prompt(xbc, z, dtr, spar, nb=nbp, seq=seq)
    y_s, sconv_s, ssm_s = _ssd_sample(xbc, z, dtr, st_sconv, st_ssm, spar, row0=npr, tv=tv)

    x, wg2 = _proj_out([(a_p, a_s), (y_p, y_s)], [w_out[:DC], w_out[DC:]], x, _row(p['mix_post_g']),
                       side=[(p['ffn2_w_gate'], 1, SIDE_WIDE)])

    q, wu2 = _norm_proj(x, _row(p['xattn_pre_g']), [p['w_xq']], BF16, side=[(p['ffn2_w_up'], 1, SIDE_WIDE)])
    mk, mv = _norm_proj(mem, _row(p['mem_norm_g']), [p['w_xk'], p['w_xv']], F32, heads=True)
    o_p = _attn_prompt(q, mk, mv, nb=nbp, seq=seq)
    o_s = _attn_sample(q, cache_k, cache_v, row0=npr, tv=tv)
    x, wd2 = _proj_out([(o_p, o_s)], [p['w_xo']], x, _row(p['xattn_post_g']),
                       side=[(p['ffn2_w_down'], 0, SIDE_WIDE)])

    x = _ffn([x], _row(p['ffn2_pre_g']), wg2, wu2, wd2, _row(p['ffn2_post_g']), split_out=split_out)
    return x, (mk, mv, conv_p, sconv_p, ssm_p, conv_s, sconv_s, ssm_s)


def kernel(x_prompt, x_sample, mem_prompt, cache_mem_k, cache_mem_v, state_conv, state_ssm_conv, state_ssm, ffn1_pre_g, ffn1_w_gate, ffn1_w_up, ffn1_w_down, ffn1_post_g, mix_pre_g, w_in, conv_w, conv_b, conv_ln_g, conv_ln_b, ssm_conv_w, ssm_conv_b, dt_bias, a_log, d_skip, ssm_norm_g, w_out, mix_post_g, xattn_pre_g, mem_norm_g, w_xq, w_xk, w_xv, w_xo, xattn_post_g, ffn2_pre_g, ffn2_w_gate, ffn2_w_up, ffn2_w_down, ffn2_post_g):
    params = dict(ffn1_pre_g=ffn1_pre_g, ffn1_w_gate=ffn1_w_gate, ffn1_w_up=ffn1_w_up, ffn1_w_down=ffn1_w_down,
                  ffn1_post_g=ffn1_post_g, mix_pre_g=mix_pre_g, w_in=w_in, conv_w=conv_w, conv_b=conv_b,
                  conv_ln_g=conv_ln_g, conv_ln_b=conv_ln_b, ssm_conv_w=ssm_conv_w, ssm_conv_b=ssm_conv_b,
                  dt_bias=dt_bias, a_log=a_log, d_skip=d_skip, ssm_norm_g=ssm_norm_g, w_out=w_out,
                  mix_post_g=mix_post_g, xattn_pre_g=xattn_pre_g, mem_norm_g=mem_norm_g, w_xq=w_xq,
                  w_xk=w_xk, w_xv=w_xv, w_xo=w_xo, xattn_post_g=xattn_post_g, ffn2_pre_g=ffn2_pre_g,
                  ffn2_w_gate=ffn2_w_gate, ffn2_w_up=ffn2_w_up, ffn2_w_down=ffn2_w_down, ffn2_post_g=ffn2_post_g)
    depth = ffn1_pre_g.shape[0]
    nbp, seq, _ = x_prompt.shape
    nbs, tv, _ = x_sample.shape
    npr = nbp * seq
    xs = [x_prompt.reshape(npr, D), x_sample.reshape(nbs * tv, D)]
    mem = mem_prompt.reshape(nbp * NM, D)
    per_layer = []
    for layer in range(depth):
        p = {name: w[layer] for name, w in params.items()}
        x, states = _layer(xs, mem, cache_mem_k[layer], cache_mem_v[layer],
                           state_conv[layer], state_ssm_conv[layer], state_ssm[layer], p,
                           nbp=nbp, seq=seq, nbs=nbs, tv=tv, split_out=layer == depth - 1)
        xs = [x]
        per_layer.append(states)
    mk, mv, conv_p, sconv_p, ssm_p, conv_s, sconv_s, ssm_s = [jnp.stack(t) for t in zip(*per_layer)]
    yp, ys = x
    return (yp.reshape(nbp, seq, D), ys.reshape(nbs, tv, D), mk, mv, conv_p, sconv_p, ssm_p, conv_s, sconv_s, ssm_s)
```

```python
import functools

import jax
import jax.numpy as jnp
from jax import lax
from jax.experimental import pallas as pl
from jax.experimental.pallas import tpu as pltpu

F32 = jnp.float32
BF16 = jnp.bfloat16

D = 2048
FF = 5504
DC = 1024
DS = 1024
KC = 31
NH = 16
HP = 64
NG = 2
NS = 128
KS = 4
DX = DS + 2 * NG * NS
CHUNK = 128
NM = 256
XH = 4
XD = D // XH
EPS = 1e-6

LANE = 128
SUB = 8
HIST = 32
HALO = 8

FF_TILE = 1024
SIDE_WIDE = 3 * LANE

VMEM_LIMIT = 60 * 1024 * 1024


def _cparams(sem):
    return pltpu.CompilerParams(dimension_semantics=sem, vmem_limit_bytes=VMEM_LIMIT)


def _rms(x, g):
    return x * lax.rsqrt(jnp.mean(x * x, axis=-1, keepdims=True) + EPS) * g


def _silu(x):
    return x * jax.nn.sigmoid(x)


def _resident(shape):
    return pl.BlockSpec(shape, lambda *_: (0,) * len(shape), pipeline_mode=pl.Buffered(1))


def _ffn_kernel(*refs, n_in, n_out, n_main, tf, side_plan):
    n_side = len(side_plan)
    x_refs = refs[:n_in]
    pg_ref, wg_ref, wu_ref, wd_ref, qg_ref = refs[n_in:n_in + 5]
    side_in = refs[n_in + 5:n_in + 5 + n_side]
    k = n_in + 5 + n_side
    o_refs = refs[k:k + n_out]
    side_out = refs[k + n_out:k + n_out + n_side]
    xn_ref, acc_ref = refs[k + n_out + n_side:]
    i = pl.program_id(0)
    f = pl.program_id(1)
    last = pl.num_programs(1) - 1

    def per_row_group(fn):
        if n_in == 1 and n_out == 1:
            fn(x_refs[0], o_refs[0])
        else:
            @pl.when(i < n_main)
            def _():
                fn(x_refs[0], o_refs[0])

            @pl.when(i >= n_main)
            def _():
                fn(x_refs[-1], o_refs[-1])

    @pl.when(f == 0)
    def _():
        def pre_norm(x_ref, _):
            xn_ref[...] = _rms(x_ref[...], pg_ref[...]).astype(BF16)

        per_row_group(pre_norm)
        acc_ref[...] = jnp.zeros_like(acc_ref)

    def hidden_tile(width):
        xn = xn_ref[...]
        h = jnp.dot(xn, wg_ref[:, 0:width], preferred_element_type=F32)
        u = jnp.dot(xn, wu_ref[:, 0:width], preferred_element_type=F32)
        a = (_silu(h) * u).astype(BF16)
        acc_ref[...] += jnp.dot(a, wd_ref[0:width, :], preferred_element_type=F32)

    @pl.when(f < last)
    def _():
        hidden_tile(tf)

    @pl.when(f == last)
    def _():
        hidden_tile(FF - (FF // tf) * tf)
        r = 0.5 * _rms(acc_ref[...], qg_ref[...])

        def residual(x_ref, o_ref):
            o_ref[...] = x_ref[...] + r

        per_row_group(residual)

    _side_cast(i * pl.num_programs(1) + f, side_plan, side_in, side_out)


def _side_plan(side):
    plan, start = [], 0
    for w, axis, width in side:
        n = pl.cdiv(w.shape[axis], width)
        plan.append((start, n))
        start += n
    return tuple(plan), start


def _side_specs(side, step_of):
    specs = []
    for (w, axis, width), (start, n) in zip(side, _side_plan(side)[0]):
        strip = lambda *idx, start=start, n=n: jnp.clip(step_of(*idx) - start, 0, n - 1)
        if axis == 0:
            specs.append(pl.BlockSpec((width, w.shape[1]), lambda *idx, strip=strip: (strip(*idx), 0)))
        else:
            specs.append(pl.BlockSpec((w.shape[0], width), lambda *idx, strip=strip: (0, strip(*idx))))
    return specs


def _side_cast(step, side_plan, side_in, side_out):
    for m, (start, n) in enumerate(side_plan):
        @pl.when((step >= start) & (step < start + n))
        def _(m=m):
            side_out[m][...] = side_in[m][...].astype(BF16)


def _side_shapes(side):
    return [jax.ShapeDtypeStruct(w.shape, BF16) for w, _, _ in side]


def _ffn(xs, pre_g, wg, wu, wd, post_g, *, split_out, tm=512, tf=FF_TILE, side=()):
    n_in = len(xs)
    nt = sum(x.shape[0] for x in xs)
    n_main = (nt - tm) // tm
    nf = pl.cdiv(FF, tf)
    side_plan, side_steps = _side_plan(side)
    assert side_steps <= (nt // tm) * nf
    main = lambda i, f: (jnp.minimum(i, n_main - 1), 0)
    extra = lambda i, f: (0, 0)
    whole = lambda i, f: (i, 0)
    if n_in == 1:
        x_specs = [pl.BlockSpec((tm, D), whole)]
    else:
        x_specs = [pl.BlockSpec((tm, D), main), pl.BlockSpec((tm, D), extra)]
    if split_out:
        out_specs = [pl.BlockSpec((tm, D), main), pl.BlockSpec((tm, D), extra)]
        out_shape = [jax.ShapeDtypeStruct((n_main * tm, D), F32), jax.ShapeDtypeStruct((tm, D), F32)]
    else:
        out_specs = [pl.BlockSpec((tm, D), whole)]
        out_shape = [jax.ShapeDtypeStruct((nt, D), F32)]
    res = pl.pallas_call(
        functools.partial(_ffn_kernel, n_in=n_in, n_out=len(out_specs), n_main=n_main, tf=tf, side_plan=side_plan),
        grid=(nt // tm, nf),
        in_specs=x_specs + [
            pl.BlockSpec((1, D), lambda i, f: (0, 0)),
            pl.BlockSpec((D, tf), lambda i, f: (0, f)),
            pl.BlockSpec((D, tf), lambda i, f: (0, f)),
            pl.BlockSpec((tf, D), lambda i, f: (f, 0)),
            pl.BlockSpec((1, D), lambda i, f: (0, 0)),
        ] + _side_specs(side, lambda i, f: i * nf + f),
        out_specs=out_specs + _side_specs(side, lambda i, f: i * nf + f),
        out_shape=out_shape + _side_shapes(side),
        scratch_shapes=[pltpu.VMEM((tm, D), BF16), pltpu.VMEM((tm, D), F32)],
        compiler_params=_cparams(("arbitrary", "arbitrary")),
        name="ffn",
    )(*xs, pre_g, wg, wu, wd, post_g, *[w for w, _, _ in side])
    main_res = res[0] if len(out_specs) == 1 else tuple(res[:2])
    return (main_res, *res[len(out_specs):]) if side else main_res


def _mix_in_kernel(x_ref, g_ref, w_ref, wt_ref, u_ref, z_ref, xbc_ref, dt_ref, *, tn):
    hn = _rms(x_ref[...], g_ref[...]).astype(BF16)
    nt_dims = (((1,), (1,)), ((), ()))

    def cols(start, c):
        return lax.dot_general(hn, w_ref[start + c * tn:start + (c + 1) * tn, :], nt_dims,
                               preferred_element_type=F32)

    for c in range(DC // tn):
        u_ref[:, c * tn:(c + 1) * tn] = cols(0, c) * jax.nn.sigmoid(cols(DC, c))
    for c in range(DS // tn):
        z_ref[:, c * tn:(c + 1) * tn] = cols(2 * DC, c)
    for c in range(DX // tn):
        xbc_ref[:, c * tn:(c + 1) * tn] = cols(2 * DC + DS, c)
    dt_ref[...] = lax.dot_general(hn, wt_ref[...], nt_dims, preferred_element_type=F32)


def _mix_in(x, g, w, wt, *, tm=512, tn=512):
    nt = x.shape[0]
    row = lambda n: pl.BlockSpec((tm, n), lambda i: (i, 0))
    return pl.pallas_call(
        functools.partial(_mix_in_kernel, tn=tn),
        grid=(nt // tm,),
        in_specs=[row(D), _resident((1, D)), _resident(w.shape), _resident((LANE, D))],
        out_specs=[row(DC), row(DS), row(DX), row(LANE)],
        out_shape=[jax.ShapeDtypeStruct((nt, n), F32) for n in (DC, DS, DX, LANE)],
        compiler_params=_cparams(("parallel",)),
        name="mix_in",
    )(x, g, w, wt)


def _conv_prompt_kernel(u_ref, w_ref, b_ref, lg_ref, lb_ref, a_ref, nb_ref, xe_ref, y_ref, *, tt):
    t = pl.program_id(1)
    off = HIST - (KC - 1)

    @pl.when(t == 0)
    def _():
        xe_ref[:, 0:HIST, :] = jnp.zeros((DC // LANE, HIST, LANE), F32)

    for j in range(DC // LANE):
        xe_ref[j, HIST:HIST + tt, :] = u_ref[:, j * LANE:(j + 1) * LANE]
    for j in range(DC // LANE):
        sl = slice(j * LANE, (j + 1) * LANE)
        acc = jnp.broadcast_to(b_ref[:, sl], (tt, LANE))
        for k in range(KC):
            acc = acc + w_ref[k:k + 1, sl] * xe_ref[j, off + k:off + k + tt, :]
        y_ref[:, sl] = acc
    for j in range(DC // LANE):
        nb_ref[:, j * LANE:(j + 1) * LANE] = xe_ref[j, HIST + tt - (KC - 1):HIST + tt, :]
        xe_ref[j, 0:HIST, :] = xe_ref[j, tt:tt + HIST, :]
    @pl.when(t >= 0)
    def _():
        y = y_ref[...]
        yc = y - jnp.mean(y, axis=-1, keepdims=True)
        a_ref[...] = _silu(yc * lax.rsqrt(jnp.mean(yc * yc, axis=-1, keepdims=True) + EPS) * lg_ref[...]
                           + lb_ref[...]).astype(a_ref.dtype)


def _conv_prompt(u, w, b, lg, lb, *, nb, seq, tt=256):
    nt = nb * seq
    nper = seq // tt
    par = lambda r: _resident((r, DC))
    return pl.pallas_call(
        functools.partial(_conv_prompt_kernel, tt=tt),
        grid=(nb, nper),
        in_specs=[pl.BlockSpec((tt, DC), lambda s, t: (s * nper + t, 0)), par(KC), par(1), par(1), par(1)],
        out_specs=[pl.BlockSpec((tt, DC), lambda s, t: (s * nper + t, 0)),
                   pl.BlockSpec((None, KC - 1, DC), lambda s, t: (s, 0, 0))],
        out_shape=[jax.ShapeDtypeStruct((nt, DC), BF16), jax.ShapeDtypeStruct((nb, KC - 1, DC), F32)],
        scratch_shapes=[pltpu.VMEM((DC // LANE, HIST + tt, LANE), F32), pltpu.VMEM((tt, DC), F32)],
        compiler_params=_cparams(("parallel", "arbitrary")),
        name="conv_prompt",
    )(u, w, b, lg, lb)


def _conv_sample_kernel(u_ref, hist_ref, w_ref, b_ref, lg_ref, lb_ref, a_ref, nh_ref, us_ref, y_ref, as_ref,
                        *, sb, tv):
    nl = DC // LANE
    for j in range(nl):
        us_ref[j] = u_ref[:, j * LANE:(j + 1) * LANE]
    for j in range(nl):
        sl = slice(j * LANE, (j + 1) * LANE)
        accs = [jnp.broadcast_to(b_ref[:, sl], (sb, LANE)) for _ in range(tv)]
        for m in range(KC - 1 + tv):
            if m < KC - 1:
                xm = hist_ref[m, :, sl]
            else:
                xm = us_ref[j, pl.ds(m - (KC - 1), sb, stride=tv), :]
            for t in range(tv):
                if 0 <= m - t < KC:
                    accs[t] = accs[t] + w_ref[m - t:m - t + 1, sl] * xm
            if m >= tv:
                nh_ref[m - tv, :, sl] = xm
        for t in range(tv):
            y_ref[t, :, sl] = accs[t]
    for t in range(tv):
        y = y_ref[t]
        yc = y - jnp.mean(y, axis=-1, keepdims=True)
        a = _silu(yc * lax.rsqrt(jnp.mean(yc * yc, axis=-1, keepdims=True) + EPS) * lg_ref[...] + lb_ref[...])
        for j in range(nl):
            as_ref[j, pl.ds(t, sb, stride=tv), :] = a[:, j * LANE:(j + 1) * LANE]
    for j in range(nl):
        a_ref[:, j * LANE:(j + 1) * LANE] = as_ref[j].astype(a_ref.dtype)


def _conv_sample(u, hist, w, b, lg, lb, *, row0, tv, sb=32):
    nb = hist.shape[1]
    rows = sb * tv
    par = lambda r: _resident((r, DC))
    blk0 = row0 // rows
    hist_spec = pl.BlockSpec((KC - 1, sb, DC), lambda i: (0, i, 0))
    return pl.pallas_call(
        functools.partial(_conv_sample_kernel, sb=sb, tv=tv),
        grid=(nb // sb,),
        in_specs=[pl.BlockSpec((rows, DC), lambda i: (blk0 + i, 0)), hist_spec, par(KC), par(1), par(1), par(1)],
        out_specs=[pl.BlockSpec((rows, DC), lambda i: (i, 0)), hist_spec],
        out_shape=[jax.ShapeDtypeStruct((nb * tv, DC), BF16), jax.ShapeDtypeStruct((KC - 1, nb, DC), F32)],
        scratch_shapes=[pltpu.VMEM((DC // LANE, rows, LANE), F32), pltpu.VMEM((tv, sb, DC), F32),
                        pltpu.VMEM((DC // LANE, rows, LANE), F32)],
        compiler_params=_cparams(("arbitrary",)),
        name="conv_sample",
    )(u, hist, w, b, lg, lb)


def _dot01(a, b, dims, *, data):
    x = b if data else a
    one = (a if data else b).astype(BF16)
    t0 = x.astype(BF16)
    r1 = x - t0.astype(F32)
    t1 = r1.astype(BF16)
    t2 = (r1 - t1.astype(F32)).astype(BF16)
    acc = None
    for t in (t0, t1, t2):
        lhs, rhs = (one, t) if data else (t, one)
        p = lax.dot_general(lhs, rhs, dims, preferred_element_type=F32)
        acc = p if acc is None else acc + p
    return acc


def _to_slabs(xe_ref, r0, x):
    for j in range(x.shape[1] // LANE):
        xe_ref[j, r0:r0 + x.shape[0], :] = x[:, j * LANE:(j + 1) * LANE]


def _from_slabs(xe_ref, r0, rows):
    return jnp.concatenate([xe_ref[j, r0:r0 + rows, :] for j in range(xe_ref.shape[0])], axis=1)


def _ssd_chunk(xe_ref, z, dt_raw, st_ref, cw_ref, cb_ref, dtb_ref, alog_ref, dsk_ref, ng_ref, *, L, tv, lq=None,
               h_io=None):
    mm_dims = (((1,), (0,)), ((), ()))
    lq = L if lq is None else lq
    nsq = L // lq
    off = HALO - (KS - 1)
    cols = []
    for j in range(DX // LANE):
        sl = slice(j * LANE, (j + 1) * LANE)
        acc = jnp.broadcast_to(cb_ref[:, sl], (L, LANE))
        for k in range(KS):
            acc = acc + cw_ref[k:k + 1, sl] * xe_ref[j, off + k:off + k + L, :]
        cols.append(acc)
    xc = _silu(jnp.concatenate(cols, axis=1))
    xs = xc[:, 0:DS]

    lane = lax.broadcasted_iota(jnp.int32, (L, LANE), 1)
    rowi = lax.broadcasted_iota(jnp.int32, (L, LANE), 0)
    xdt = dt_raw + dtb_ref[...]
    dt = jnp.maximum(xdt, 0.0) + jnp.log1p(jnp.exp(-jnp.abs(xdt)))
    dt = jnp.where((lane < NH) & (rowi % lq < tv), dt, 0.0)
    da = dt * (-jnp.exp(alog_ref[...]))

    r2 = lax.broadcasted_iota(jnp.int32, (L, L), 0)
    c2 = lax.broadcasted_iota(jnp.int32, (L, L), 1)
    same = r2 // lq == c2 // lq
    causal = (r2 >= c2) & same
    if nsq == 1:
        a_cum = _dot01(causal.astype(F32), da, mm_dims, data=1)
        a_tot = jnp.broadcast_to(a_cum[L - 1:L, :], (L, LANE))
    else:
        cums = _dot01(jnp.concatenate([causal.astype(F32), same.astype(F32)], axis=0), da, mm_dims, data=1)
        a_cum = cums[0:L]
        a_tot = cums[L:2 * L]

    er = lax.broadcasted_iota(jnp.int32, (LANE, DS), 0)
    ec = lax.broadcasted_iota(jnp.int32, (LANE, DS), 1)
    expand = (ec // HP == er).astype(F32)
    stack = jnp.concatenate([jnp.exp(a_cum), jnp.exp(a_tot - a_cum) * dt, jnp.exp(a_tot[0:SUB])], axis=0)
    stack_x = _dot01(stack, expand, mm_dims, data=0)
    ea_x = stack_x[0:L]
    wend_x = stack_x[L:2 * L]
    cd_x = stack_x[2 * L:2 * L + 1]

    ir = lax.broadcasted_iota(jnp.int32, (LANE, LANE), 0)
    ic = lax.broadcasted_iota(jnp.int32, (LANE, LANE), 1)
    ident = (ir == ic).astype(F32)
    tr = _dot01(ident, jnp.concatenate([dt, a_cum], axis=0), (((1,), (1,)), ((), ())), data=1)
    dt_t = tr[:, 0:L]
    acum_t = tr[:, L:2 * L]

    lane_x = lax.broadcasted_iota(jnp.int32, (L, LANE), 1)
    hpg = NH // NG
    bms = [xc[:, DS + g * NS:DS + (g + 1) * NS] for g in range(NG)]
    cms = [xc[:, DS + NG * NS + g * NS:DS + NG * NS + (g + 1) * NS] for g in range(NG)]
    gss = [slice(g * (DS // NG), (g + 1) * (DS // NG)) for g in range(NG)]
    cbms = [lax.dot_general(cms[g], bms[g], (((1,), (1,)), ((), ())), preferred_element_type=F32)
            for g in range(NG)]
    if h_io is None:
        st_old = [st_ref[:, gss[g]] for g in range(NG)]
        y_off = [jnp.dot(cms[g], st_old[g], preferred_element_type=F32) * ea_x[:, gss[g]] for g in range(NG)]
        st_new = [jnp.dot(bms[g].T, xs[:, gss[g]] * wend_x[:, gss[g]], preferred_element_type=F32)
                  for g in range(NG)]
        for g in range(NG):
            st_ref[:, gss[g]] = st_old[g] * cd_x[:, gss[g]] + st_new[g]
    else:
        rows_g = hpg * HP
        sq = [slice(q * lq, (q + 1) * lq) for q in range(nsq)]
        hs = [[h_io[q][0][g * hpg:(g + 1) * hpg].reshape(rows_g, NS) for g in range(NG)] for q in range(nsq)]
        y_off = [jnp.concatenate([lax.dot_general(cms[g][sq[q]], hs[q][g], (((1,), (1,)), ((), ())),
                                                  preferred_element_type=F32) for q in range(nsq)], axis=0)
                 * ea_x[:, gss[g]] for g in range(NG)]
        xw = [xs[:, gss[g]] * wend_x[:, gss[g]] for g in range(NG)]
        upd = [[lax.dot_general(xw[g][sq[q]], bms[g][sq[q]], (((0,), (0,)), ((), ())),
                                preferred_element_type=F32) for g in range(NG)] for q in range(nsq)]
        seq_decay = jnp.exp(a_tot)
        for q in range(nsq):
            for h in range(NH):
                g, hl = divmod(h, hpg)
                rows = slice(hl * HP, (hl + 1) * HP)
                h_io[q][1][h] = hs[q][g][rows, :] * seq_decay[q * lq:q * lq + 1, h:h + 1] + upd[q][g][rows, :]
    segs = [a_cum[:, h:h + 1] - acum_t[h:h + 1, :] for h in range(NH)]
    decs = [jnp.exp(jnp.where(causal, segs[h], -jnp.inf)) for h in range(NH)]
    ws = [cbms[h // hpg] * decs[h] * dt_t[h:h + 1, :] for h in range(NH)]
    y_diag = []
    for pr in range(NH // 2):
        xp = xs[:, 2 * pr * HP:(2 * pr + 2) * HP]
        rhs = jnp.concatenate([jnp.where(lane_x < HP, xp, 0.0), jnp.where(lane_x >= HP, xp, 0.0)], axis=0)
        y_diag.append(jnp.dot(jnp.concatenate([ws[2 * pr], ws[2 * pr + 1]], axis=1), rhs,
                              preferred_element_type=F32))
    y = jnp.concatenate(y_diag, axis=1) + jnp.concatenate(y_off, axis=1) + dsk_ref[...] * xs
    y = y * _silu(z)
    outs = []
    for g in range(NG):
        gs = slice(g * (DS // NG), (g + 1) * (DS // NG))
        outs.append(_rms(y[:, gs], ng_ref[:, gs]))
    return jnp.concatenate(outs, axis=1)


def _state_out(st_ref, h_ref):
    for j in range(DS // LANE):
        blk = st_ref[:, j * LANE:(j + 1) * LANE].T
        for q in range(LANE // HP):
            h_ref[j * (LANE // HP) + q] = blk[q * HP:(q + 1) * HP, :]


def _ssd_prompt_kernel(xbc_ref, z_ref, dt_ref, cw_ref, cb_ref, dtb_ref, alog_ref, dsk_ref, ng_ref,
                       y_ref, nb_ref, h_ref, xe_ref, st_ref, *, L):
    c = pl.program_id(1)

    @pl.when(c == 0)
    def _():
        xe_ref[:, 0:HALO, :] = jnp.zeros((DX // LANE, HALO, LANE), F32)
        st_ref[...] = jnp.zeros_like(st_ref)

    _to_slabs(xe_ref, HALO, xbc_ref[...])
    y_ref[...] = _ssd_chunk(xe_ref, z_ref[...], dt_ref[...], st_ref, cw_ref, cb_ref, dtb_ref, alog_ref, dsk_ref,
                            ng_ref, L=L, tv=L).astype(y_ref.dtype)
    xe_ref[:, 0:HALO, :] = xe_ref[:, L:L + HALO, :]

    @pl.when(c == pl.num_programs(1) - 1)
    def _():
        nb_ref[...] = _from_slabs(xe_ref, HALO - (KS - 1), KS - 1)
        _state_out(st_ref, h_ref)


def _ssd_params_specs():
    return [_resident((KS, DX)), _resident((1, DX)), _resident((1, LANE)), _resident((1, LANE)),
            _resident((1, DS)), _resident((1, DS))]


def _ssd_prompt(xbc, z, dt, params, *, nb, seq):
    nt = nb * seq
    L = CHUNK
    nper = seq // L
    row = lambda n: pl.BlockSpec((L, n), lambda s, c: (s * nper + c, 0))
    return pl.pallas_call(
        functools.partial(_ssd_prompt_kernel, L=L),
        grid=(nb, nper),
        in_specs=[row(DX), row(DS), row(LANE)] + _ssd_params_specs(),
        out_specs=[row(DS),
                   pl.BlockSpec((None, KS - 1, DX), lambda s, c: (s, 0, 0)),
                   pl.BlockSpec((None, NH, HP, NS), lambda s, c: (s, 0, 0, 0))],
        out_shape=[jax.ShapeDtypeStruct((nt, DS), BF16), jax.ShapeDtypeStruct((nb, KS - 1, DX), F32),
                   jax.ShapeDtypeStruct((nb, NH, HP, NS), F32)],
        scratch_shapes=[pltpu.VMEM((DX // LANE, HALO + L, LANE), F32), pltpu.VMEM((NS, DS), F32)],
        compiler_params=_cparams(("parallel", "arbitrary")),
        name="ssd_prompt",
    )(xbc, z, dt, *params)


def _ssd_sample_kernel(xbc_ref, z_ref, dt_ref, hist_ref, h0_ref, cw_ref, cb_ref, dtb_ref, alog_ref, dsk_ref,
                       ng_ref, y_ref, nb_ref, h_ref, xe_ref, zb_ref, dtp_ref, *, lq, tv, nseq):
    assert lq - tv >= KS - 1 and HALO >= KS - 1
    xe_ref[...] = jnp.zeros_like(xe_ref)
    zb_ref[...] = jnp.zeros_like(zb_ref)
    dtp_ref[...] = jnp.zeros_like(dtp_ref)
    for j in range(nseq):
        rows = slice(j * tv, (j + 1) * tv)
        _to_slabs(xe_ref, HALO + j * lq - (KS - 1), hist_ref[j])
        _to_slabs(xe_ref, HALO + j * lq, xbc_ref[rows, :])
        zb_ref[j * lq:j * lq + tv, :] = z_ref[rows, :]
        dtp_ref[j * lq:j * lq + tv, :] = dt_ref[rows, :]
    y = _ssd_chunk(xe_ref, zb_ref[...], dtp_ref[...], None, cw_ref, cb_ref, dtb_ref, alog_ref, dsk_ref, ng_ref,
                   L=nseq * lq, tv=tv, lq=lq, h_io=[(h0_ref.at[j], h_ref.at[j]) for j in range(nseq)])
    y_ref[...] = jnp.concatenate([y[j * lq:j * lq + tv, :] for j in range(nseq)], axis=0).astype(y_ref.dtype)
    for j in range(nseq):
        nb_ref[j] = _from_slabs(xe_ref, HALO + j * lq + tv - (KS - 1), KS - 1)


def _ssd_sample(xbc, z, dt, hist, h0, params, *, row0, tv, nseq=8, lq=SUB):
    nb = hist.shape[0]
    rows = nseq * tv
    L = nseq * lq
    blk0 = row0 // rows
    row = lambda n: pl.BlockSpec((rows, n), lambda i: (blk0 + i, 0))
    return pl.pallas_call(
        functools.partial(_ssd_sample_kernel, lq=lq, tv=tv, nseq=nseq),
        grid=(nb // nseq,),
        in_specs=[row(DX), row(DS), row(LANE),
                  pl.BlockSpec((nseq, KS - 1, DX), lambda i: (i, 0, 0)),
                  pl.BlockSpec((nseq, NH, HP, NS), lambda i: (i, 0, 0, 0))] + _ssd_params_specs(),
        out_specs=[pl.BlockSpec((rows, DS), lambda i: (i, 0)),
                   pl.BlockSpec((nseq, KS - 1, DX), lambda i: (i, 0, 0)),
                   pl.BlockSpec((nseq, NH, HP, NS), lambda i: (i, 0, 0, 0))],
        out_shape=[jax.ShapeDtypeStruct((nb * tv, DS), BF16), jax.ShapeDtypeStruct((nb, KS - 1, DX), F32),
                   jax.ShapeDtypeStruct((nb, NH, HP, NS), F32)],
        scratch_shapes=[pltpu.VMEM((DX // LANE, HALO + L, LANE), F32), pltpu.VMEM((L, DS), F32),
                        pltpu.VMEM((L, LANE), F32)],
        compiler_params=_cparams(("arbitrary",)),
        name="ssd_sample",
    )(xbc, z, dt, hist, h0, *params)


def _proj_out_kernel(*refs, n, n_main, side_plan):
    lhs = refs[0:2 * n]
    ws = refs[2 * n:3 * n]
    x_ref, g_ref = refs[3 * n:3 * n + 2]
    ns = len(side_plan)
    side_in = refs[3 * n + 2:3 * n + 2 + ns]
    o_ref = refs[3 * n + 2 + ns]
    side_out = refs[3 * n + 3 + ns:]
    _side_cast(pl.program_id(0), side_plan, side_in, side_out)
    is_main = pl.program_id(0) < n_main

    def tile(which):
        m = None
        for k, w_ref in enumerate(ws):
            a = lhs[2 * k + which][...].astype(BF16)
            p = jnp.dot(a, w_ref[...].astype(BF16), preferred_element_type=F32)
            m = p if m is None else m + p
        o_ref[...] = x_ref[...] + _rms(m, g_ref[...])

    @pl.when(is_main)
    def _():
        tile(0)

    @pl.when(jnp.logical_not(is_main))
    def _():
        tile(1)


def _proj_out(lhs_pairs, ws, x, g, *, tm=512, side=()):
    nt = x.shape[0]
    side_plan, side_steps = _side_plan(side)
    assert side_steps <= nt // tm
    n = len(lhs_pairs)
    n_main = nt // tm - 1
    lhs_specs = []
    for a_main, a_extra in lhs_pairs:
        assert a_main.shape[0] == n_main * tm and a_extra.shape[0] == tm
        lhs_specs.append(pl.BlockSpec((tm, a_main.shape[1]), lambda i: (jnp.minimum(i, n_main - 1), 0)))
        lhs_specs.append(pl.BlockSpec((tm, a_extra.shape[1]), lambda i: (0, 0)))
    res = pl.pallas_call(
        functools.partial(_proj_out_kernel, n=n, n_main=n_main, side_plan=side_plan),
        grid=(nt // tm,),
        in_specs=lhs_specs + [_resident(w.shape) for w in ws]
                 + [pl.BlockSpec((tm, D), lambda i: (i, 0)), _resident((1, D))] + _side_specs(side, lambda i: i),
        out_specs=[pl.BlockSpec((tm, D), lambda i: (i, 0))] + _side_specs(side, lambda i: i),
        out_shape=[jax.ShapeDtypeStruct((nt, D), F32)] + _side_shapes(side),
        compiler_params=_cparams(("arbitrary",)),
        name="proj_out",
    )(*[a for pair in lhs_pairs for a in pair], *ws, x, g, *[w for w, _, _ in side])
    return tuple(res) if side else res[0]


def _norm_proj_kernel(*refs, n, heads, side_plan):
    x_ref, g_ref = refs[0:2]
    ws = refs[2:2 + n]
    ns = len(side_plan)
    side_in = refs[2 + n:2 + n + ns]
    outs = refs[2 + n + ns:2 + 2 * n + ns]
    side_out = refs[2 + 2 * n + ns:]
    _side_cast(pl.program_id(0), side_plan, side_in, side_out)
    hn = _rms(x_ref[...], g_ref[...]).astype(BF16)
    for w_ref, o_ref in zip(ws, outs):
        for h in range(XH):
            sl = slice(h * XD, (h + 1) * XD)
            r = jnp.dot(hn, w_ref[:, sl].astype(BF16), preferred_element_type=F32).astype(o_ref.dtype)
            if heads:
                o_ref[:, h, :] = r
            else:
                o_ref[:, sl] = r


def _norm_proj(x, g, ws, out_dtype, *, heads=False, tm=512, side=()):
    nt = x.shape[0]
    n = len(ws)
    side_plan, side_steps = _side_plan(side)
    if heads:
        tm = NM
        out_specs = [pl.BlockSpec((None, NM, XH, XD), lambda i: (i, 0, 0, 0))] * n
        out_shape = [jax.ShapeDtypeStruct((nt // NM, NM, XH, XD), out_dtype)] * n
    else:
        out_specs = [pl.BlockSpec((tm, D), lambda i: (i, 0))] * n
        out_shape = [jax.ShapeDtypeStruct((nt, D), out_dtype)] * n
    assert side_steps <= nt // tm
    return pl.pallas_call(
        functools.partial(_norm_proj_kernel, n=n, heads=heads, side_plan=side_plan),
        grid=(nt // tm,),
        in_specs=[pl.BlockSpec((tm, D), lambda i: (i, 0)), _resident((1, D))] + [_resident((D, D))] * n
                 + _side_specs(side, lambda i: i),
        out_specs=out_specs + _side_specs(side, lambda i: i),
        out_shape=out_shape + _side_shapes(side),
        compiler_params=_cparams(("arbitrary",)),
        name="norm_proj",
    )(x, g, *ws, *[w for w, _, _ in side])


NLT = XD // LANE
LT_STRIDE = NLT * XH
KV_ROWS = NM * LT_STRIDE


def _kv_tiles(x):
    nb = x.shape[0]
    return x.reshape(nb, NM, XH, NLT, LANE).transpose(0, 1, 3, 2, 4).reshape(nb, KV_ROWS, LANE)


def _attn_prompt_kernel(q_ref, k_ref, v_ref, o_ref, kh_ref, vh_ref):
    @pl.when(pl.program_id(1) == 0)
    def _():
        for h in range(XH):
            for lt in range(NLT):
                rows = pl.ds(lt * XH + h, NM, stride=LT_STRIDE)
                kh_ref[h, :, lt * LANE:(lt + 1) * LANE] = k_ref[0, rows, :].astype(BF16)
                vh_ref[h, :, lt * LANE:(lt + 1) * LANE] = v_ref[0, rows, :].astype(BF16)

    scale = XD ** -0.5
    nt_dims = (((1,), (1,)), ((), ()))
    heads = range(XH)
    s = [lax.dot_general(q_ref[:, h * XD:(h + 1) * XD], kh_ref[h], nt_dims, preferred_element_type=F32) * scale
         for h in heads]
    e = [jnp.exp(s[h] - jnp.max(s[h], axis=-1, keepdims=True)) for h in heads]
    p = [(e[h] / jnp.sum(e[h], axis=-1, keepdims=True)).astype(BF16) for h in heads]
    for h in heads:
        o_ref[:, h * XD:(h + 1) * XD] = jnp.dot(p[h], vh_ref[h], preferred_element_type=F32).astype(o_ref.dtype)


def _group_sum(x, col_lt):
    n = x.shape[1]
    a = x + jnp.where(col_lt % 2 == 0, pltpu.roll(x, n - 1, axis=1), pltpu.roll(x, 1, axis=1))
    return a + jnp.where(col_lt < 2, pltpu.roll(a, n - 2, axis=1), pltpu.roll(a, 2, axis=1))


def _attn_sample_kernel(q_ref, k_ref, v_ref, o_ref, *, nseq, tv):
    assert NLT == 4
    tq = q_ref.shape[0]
    ncol = NM * NLT
    scale = XD ** -0.5
    nt_dims = (((1,), (1,)), ((), ()))
    col_lt = lax.broadcasted_iota(jnp.int32, (tq, ncol), 1) % NLT
    rowi = lax.broadcasted_iota(jnp.int32, (tq, LANE), 0)
    for h in range(XH):
        qp = jnp.concatenate([q_ref[:, h * XD + lt * LANE:h * XD + (lt + 1) * LANE].astype(F32)
                              for lt in range(NLT)], axis=0)
        seqs = range(nseq)
        g = [lax.dot_general(qp, k_ref[j, pl.ds(h, ncol, stride=XH), :], nt_dims, preferred_element_type=F32)
             for j in seqs]
        s4 = [sum(jnp.where(col_lt == lt, g[j][lt * tq:(lt + 1) * tq], 0.0) for lt in range(NLT)) for j in seqs]
        s = [_group_sum(s4[j], col_lt) * scale for j in seqs]
        e = [jnp.exp(s[j] - jnp.max(s[j], axis=-1, keepdims=True)) for j in seqs]
        p = [e[j] / (jnp.sum(e[j], axis=-1, keepdims=True) * (1.0 / NLT)) for j in seqs]
        o = [jnp.dot(jnp.concatenate([jnp.where(col_lt == lt, p[j], 0.0) for lt in range(NLT)], axis=0),
                     v_ref[j, pl.ds(h, ncol, stride=XH), :], preferred_element_type=F32) for j in seqs]
        acc = [None] * NLT
        for j in seqs:
            mine = (rowi >= j * tv) & (rowi < (j + 1) * tv)
            for lt in range(NLT):
                acc[lt] = jnp.where(mine, o[j][lt * tq:(lt + 1) * tq], 0.0 if acc[lt] is None else acc[lt])
        for lt in range(NLT):
            o_ref[:, h * XD + lt * LANE:h * XD + (lt + 1) * LANE] = acc[lt].astype(o_ref.dtype)


def _attn_prompt(q, k, v, *, nb, seq, tq=1024):
    nt = nb * seq
    nper = seq // tq
    kv_spec = pl.BlockSpec((1, KV_ROWS, LANE), lambda s, t: (s, 0, 0))
    return pl.pallas_call(
        _attn_prompt_kernel,
        grid=(nb, nper),
        in_specs=[pl.BlockSpec((tq, D), lambda s, t: (s * nper + t, 0)), kv_spec, kv_spec],
        out_specs=pl.BlockSpec((tq, D), lambda s, t: (s * nper + t, 0)),
        out_shape=jax.ShapeDtypeStruct((nt, D), BF16),
        scratch_shapes=[pltpu.VMEM((XH, NM, XD), BF16), pltpu.VMEM((XH, NM, XD), BF16)],
        compiler_params=_cparams(("arbitrary", "arbitrary")),
        name="attn_prompt",
    )(q, _kv_tiles(k), _kv_tiles(v))


def _attn_sample(q, k, v, *, row0, tv, nseq=4):
    nb = k.shape[0]
    rows = nseq * tv
    blk0 = row0 // rows
    kv_spec = pl.BlockSpec((nseq, KV_ROWS, LANE), lambda i: (i, 0, 0))
    return pl.pallas_call(
        functools.partial(_attn_sample_kernel, nseq=nseq, tv=tv),
        grid=(nb // nseq,),
        in_specs=[pl.BlockSpec((rows, D), lambda i: (blk0 + i, 0)), kv_spec, kv_spec],
        out_specs=pl.BlockSpec((rows, D), lambda i: (i, 0)),
        out_shape=jax.ShapeDtypeStruct((nb * tv, D), BF16),
        compiler_params=_cparams(("arbitrary",)),
        name="attn_sample",
    )(q, _kv_tiles(k), _kv_tiles(v))


def _row(v):
    return v.reshape(1, -1).astype(F32)


def _pad_lanes(v, n):
    return jnp.pad(v.reshape(1, -1).astype(F32), ((0, 0), (0, n - v.size)))


def _layer(xs, mem, cache_k, cache_v, st_conv, st_sconv, st_ssm, p, *, nbp, seq, nbs, tv, split_out):
    npr = nbp * seq
    bf = lambda w: w.astype(BF16)

    side = [(p['w_in'].T, 0, LANE)]
    x, w_in = _ffn(xs, _row(p['ffn1_pre_g']), bf(p['ffn1_w_gate']), bf(p['ffn1_w_up']), bf(p['ffn1_w_down']),
                   _row(p['ffn1_post_g']), split_out=False, side=side)
    w_out = bf(p['w_out'])

    wt = jnp.pad(w_in[2 * DC + DS + DX:], ((0, LANE - NH), (0, 0)))
    u, z, xbc, dtr = _mix_in(x, _row(p['mix_pre_g']), w_in, wt)

    cpar = (p['conv_w'].astype(F32), _row(p['conv_b']), _row(p['conv_ln_g']), _row(p['conv_ln_b']))
    a_p, conv_p = _conv_prompt(u, *cpar, nb=nbp, seq=seq)
    a_s, conv_s = _conv_sample(u, jnp.transpose(st_conv, (1, 0, 2)), *cpar, row0=npr, tv=tv)
    conv_s = jnp.transpose(conv_s, (1, 0, 2))

    spar = (p['ssm_conv_w'].astype(F32), _row(p['ssm_conv_b']), _pad_lanes(p['dt_bias'], LANE),
            _pad_lanes(p['a_log'], LANE), jnp.repeat(p['d_skip'].astype(F32), HP).reshape(1, DS),
            _row(p['ssm_norm_g']))
    y_p, sconv_p, ssm_p = _ssd_prompt(xbc, z, dtr, spar, nb=nbp, seq=seq)
    y_s, sconv_s, ssm_s = _ssd_sample(xbc, z, dtr, st_sconv, st_ssm, spar, row0=npr, tv=tv)

    x, wg2 = _proj_out([(a_p, a_s), (y_p, y_s)], [w_out[:DC], w_out[DC:]], x, _row(p['mix_post_g']),
                       side=[(p['ffn2_w_gate'], 1, SIDE_WIDE)])

    q, wu2 = _norm_proj(x, _row(p['xattn_pre_g']), [p['w_xq']], BF16, side=[(p['ffn2_w_up'], 1, SIDE_WIDE)])
    mk, mv = _norm_proj(mem, _row(p['mem_norm_g']), [p['w_xk'], p['w_xv']], F32, heads=True)
    o_p = _attn_prompt(q, mk, mv, nb=nbp, seq=seq)
    o_s = _attn_sample(q, cache_k, cache_v, row0=npr, tv=tv)
    x, wd2 = _proj_out([(o_p, o_s)], [p['w_xo']], x, _row(p['xattn_post_g']),
                       side=[(p['ffn2_w_down'], 0, SIDE_WIDE)])

    x = _ffn([x], _row(p['ffn2_pre_g']), wg2, wu2, wd2, _row(p['ffn2_post_g']), split_out=split_out)
    return x, (mk, mv, conv_p, sconv_p, ssm_p, conv_s, sconv_s, ssm_s)


def kernel(x_prompt, x_sample, mem_prompt, cache_mem_k, cache_mem_v, state_conv, state_ssm_conv, state_ssm, ffn1_pre_g, ffn1_w_gate, ffn1_w_up, ffn1_w_down, ffn1_post_g, mix_pre_g, w_in, conv_w, conv_b, conv_ln_g, conv_ln_b, ssm_conv_w, ssm_conv_b, dt_bias, a_log, d_skip, ssm_norm_g, w_out, mix_post_g, xattn_pre_g, mem_norm_g, w_xq, w_xk, w_xv, w_xo, xattn_post_g, ffn2_pre_g, ffn2_w_gate, ffn2_w_up, ffn2_w_down, ffn2_post_g):
    params = dict(ffn1_pre_g=ffn1_pre_g, ffn1_w_gate=ffn1_w_gate, ffn1_w_up=ffn1_w_up, ffn1_w_down=ffn1_w_down,
                  ffn1_post_g=ffn1_post_g, mix_pre_g=mix_pre_g, w_in=w_in, conv_w=conv_w, conv_b=conv_b,
                  conv_ln_g=conv_ln_g, conv_ln_b=conv_ln_b, ssm_conv_w=ssm_conv_w, ssm_conv_b=ssm_conv_b,
                  dt_bias=dt_bias, a_log=a_log, d_skip=d_skip, ssm_norm_g=ssm_norm_g, w_out=w_out,
                  mix_post_g=mix_post_g, xattn_pre_g=xattn_pre_g, mem_norm_g=mem_norm_g, w_xq=w_xq,
                  w_xk=w_xk, w_xv=w_xv, w_xo=w_xo, xattn_post_g=xattn_post_g, ffn2_pre_g=ffn2_pre_g,
                  ffn2_w_gate=ffn2_w_gate, ffn2_w_up=ffn2_w_up, ffn2_w_down=ffn2_w_down, ffn2_post_g=ffn2_post_g)
    depth = ffn1_pre_g.shape[0]
    nbp, seq, _ = x_prompt.shape
    nbs, tv, _ = x_sample.shape
    npr = nbp * seq
    xs = [x_prompt.reshape(npr, D), x_sample.reshape(nbs * tv, D)]
    mem = mem_prompt.reshape(nbp * NM, D)
    per_layer = []
    for layer in range(depth):
        p = {name: w[layer] for name, w in params.items()}
        x, states = _layer(xs, mem, cache_mem_k[layer], cache_mem_v[layer],
                           state_conv[layer], state_ssm_conv[layer], state_ssm[layer], p,
                           nbp=nbp, seq=seq, nbs=nbs, tv=tv, split_out=layer == depth - 1)
        xs = [x]
        per_layer.append(states)
    mk, mv, conv_p, sconv_p, ssm_p, conv_s, sconv_s, ssm_s = [jnp.stack(t) for t in zip(*per_layer)]
    yp, ys = x
    return (yp.reshape(nbp, seq, D), ys.reshape(nbs, tv, D), mk, mv, conv_p, sconv_p, ssm_p, conv_s, sconv_s, ssm_s)
```

```python
import functools

import jax
import jax.numpy as jnp
from jax import lax
from jax.experimental import pallas as pl
from jax.experimental.pallas import tpu as pltpu

F32 = jnp.float32
BF16 = jnp.bfloat16

D = 2048
FF = 5504
DC = 1024
DS = 1024
KC = 31
NH = 16
HP = 64
NG = 2
NS = 128
KS = 4
DX = DS + 2 * NG * NS
CHUNK = 128
NM = 256
XH = 4
XD = D // XH
EPS = 1e-6

LANE = 128
SUB = 8
HIST = 32
HALO = 8

FF_TILE = 1024
SIDE_WIDE = 3 * LANE

VMEM_LIMIT = 60 * 1024 * 1024


def _cparams(sem):
    return pltpu.CompilerParams(dimension_semantics=sem, vmem_limit_bytes=VMEM_LIMIT)


def _rms(x, g):
    return x * lax.rsqrt(jnp.mean(x * x, axis=-1, keepdims=True) + EPS) * g


def _silu(x):
    return x * jax.nn.sigmoid(x)


def _resident(shape):
    return pl.BlockSpec(shape, lambda *_: (0,) * len(shape), pipeline_mode=pl.Buffered(1))


def _ffn_kernel(*refs, n_in, n_out, n_main, tf, side_plan):
    n_side = len(side_plan)
    x_refs = refs[:n_in]
    pg_ref, wg_ref, wu_ref, wd_ref, qg_ref = refs[n_in:n_in + 5]
    side_in = refs[n_in + 5:n_in + 5 + n_side]
    k = n_in + 5 + n_side
    o_refs = refs[k:k + n_out]
    side_out = refs[k + n_out:k + n_out + n_side]
    xn_ref, acc_ref = refs[k + n_out + n_side:]
    i = pl.program_id(0)
    f = pl.program_id(1)
    last = pl.num_programs(1) - 1

    def per_row_group(fn):
        if n_in == 1 and n_out == 1:
            fn(x_refs[0], o_refs[0])
        else:
            @pl.when(i < n_main)
            def _():
                fn(x_refs[0], o_refs[0])

            @pl.when(i >= n_main)
            def _():
                fn(x_refs[-1], o_refs[-1])

    @pl.when(f == 0)
    def _():
        def pre_norm(x_ref, _):
            xn_ref[...] = _rms(x_ref[...], pg_ref[...]).astype(BF16)

        per_row_group(pre_norm)
        acc_ref[...] = jnp.zeros_like(acc_ref)

    def hidden_tile(width):
        xn = xn_ref[...]
        h = jnp.dot(xn, wg_ref[:, 0:width], preferred_element_type=F32)
        u = jnp.dot(xn, wu_ref[:, 0:width], preferred_element_type=F32)
        a = (_silu(h) * u).astype(BF16)
        acc_ref[...] += jnp.dot(a, wd_ref[0:width, :], preferred_element_type=F32)

    @pl.when(f < last)
    def _():
        hidden_tile(tf)

    @pl.when(f == last)
    def _():
        hidden_tile(FF - (FF // tf) * tf)
        r = _rms(acc_ref[...], qg_ref[...])

        def residual(x_ref, o_ref):
            o_ref[...] = x_ref[...] + r

        per_row_group(residual)

    _side_cast(i * pl.num_programs(1) + f, side_plan, side_in, side_out)


def _side_plan(side):
    plan, start = [], 0
    for w, axis, width in side:
        n = pl.cdiv(w.shape[axis], width)
        plan.append((start, n))
        start += n
    return tuple(plan), start


def _side_specs(side, step_of):
    specs = []
    for (w, axis, width), (start, n) in zip(side, _side_plan(side)[0]):
        strip = lambda *idx, start=start, n=n: jnp.clip(step_of(*idx) - start, 0, n - 1)
        if axis == 0:
            specs.append(pl.BlockSpec((width, w.shape[1]), lambda *idx, strip=strip: (strip(*idx), 0)))
        else:
            specs.append(pl.BlockSpec((w.shape[0], width), lambda *idx, strip=strip: (0, strip(*idx))))
    return specs


def _side_cast(step, side_plan, side_in, side_out):
    for m, (start, n) in enumerate(side_plan):
        @pl.when((step >= start) & (step < start + n))
        def _(m=m):
            side_out[m][...] = side_in[m][...].astype(BF16)


def _side_shapes(side):
    return [jax.ShapeDtypeStruct(w.shape, BF16) for w, _, _ in side]


def _ffn(xs, pre_g, wg, wu, wd, post_g, *, split_out, tm=512, tf=FF_TILE, side=()):
    n_in = len(xs)
    nt = sum(x.shape[0] for x in xs)
    n_main = (nt - tm) // tm
    nf = pl.cdiv(FF, tf)
    side_plan, side_steps = _side_plan(side)
    assert side_steps <= (nt // tm) * nf
    main = lambda i, f: (jnp.minimum(i, n_main - 1), 0)
    extra = lambda i, f: (0, 0)
    whole = lambda i, f: (i, 0)
    if n_in == 1:
        x_specs = [pl.BlockSpec((tm, D), whole)]
    else:
        x_specs = [pl.BlockSpec((tm, D), main), pl.BlockSpec((tm, D), extra)]
    if split_out:
        out_specs = [pl.BlockSpec((tm, D), main), pl.BlockSpec((tm, D), extra)]
        out_shape = [jax.ShapeDtypeStruct((n_main * tm, D), F32), jax.ShapeDtypeStruct((tm, D), F32)]
    else:
        out_specs = [pl.BlockSpec((tm, D), whole)]
        out_shape = [jax.ShapeDtypeStruct((nt, D), F32)]
    res = pl.pallas_call(
        functools.partial(_ffn_kernel, n_in=n_in, n_out=len(out_specs), n_main=n_main, tf=tf, side_plan=side_plan),
        grid=(nt // tm, nf),
        in_specs=x_specs + [
            pl.BlockSpec((1, D), lambda i, f: (0, 0)),
            pl.BlockSpec((D, tf), lambda i, f: (0, f)),
            pl.BlockSpec((D, tf), lambda i, f: (0, f)),
            pl.BlockSpec((tf, D), lambda i, f: (f, 0)),
            pl.BlockSpec((1, D), lambda i, f: (0, 0)),
        ] + _side_specs(side, lambda i, f: i * nf + f),
        out_specs=out_specs + _side_specs(side, lambda i, f: i * nf + f),
        out_shape=out_shape + _side_shapes(side),
        scratch_shapes=[pltpu.VMEM((tm, D), BF16), pltpu.VMEM((tm, D), F32)],
        compiler_params=_cparams(("arbitrary", "arbitrary")),
        name="ffn",
    )(*xs, pre_g, wg, wu, wd, 0.5 * post_g, *[w for w, _, _ in side])
    main_res = res[0] if len(out_specs) == 1 else tuple(res[:2])
    return (main_res, *res[len(out_specs):]) if side else main_res


def _mix_in_kernel(x_ref, g_ref, w_ref, wt_ref, u_ref, z_ref, xbc_ref, dt_ref, *, tn):
    hn = _rms(x_ref[...], g_ref[...]).astype(BF16)
    nt_dims = (((1,), (1,)), ((), ()))

    def cols(start, c):
        return lax.dot_general(hn, w_ref[start + c * tn:start + (c + 1) * tn, :], nt_dims,
                               preferred_element_type=F32)

    for c in range(DC // tn):
        u_ref[:, c * tn:(c + 1) * tn] = cols(0, c) * jax.nn.sigmoid(cols(DC, c))
    for c in range(DS // tn):
        z_ref[:, c * tn:(c + 1) * tn] = cols(2 * DC, c)
    for c in range(DX // tn):
        xbc_ref[:, c * tn:(c + 1) * tn] = cols(2 * DC + DS, c)
    dt_ref[...] = lax.dot_general(hn, wt_ref[...], nt_dims, preferred_element_type=F32)


def _mix_in(x, g, w, wt, *, tm=512, tn=512):
    nt = x.shape[0]
    row = lambda n: pl.BlockSpec((tm, n), lambda i: (i, 0))
    return pl.pallas_call(
        functools.partial(_mix_in_kernel, tn=tn),
        grid=(nt // tm,),
        in_specs=[row(D), _resident((1, D)), _resident(w.shape), _resident((LANE, D))],
        out_specs=[row(DC), row(DS), row(DX), row(LANE)],
        out_shape=[jax.ShapeDtypeStruct((nt, n), F32) for n in (DC, DS, DX, LANE)],
        compiler_params=_cparams(("parallel",)),
        name="mix_in",
    )(x, g, w, wt)


def _conv_prompt_kernel(u_ref, w_ref, b_ref, lg_ref, lb_ref, a_ref, nb_ref, xe_ref, y_ref, *, tt):
    t = pl.program_id(1)
    off = HIST - (KC - 1)

    @pl.when(t == 0)
    def _():
        xe_ref[:, 0:HIST, :] = jnp.zeros((DC // LANE, HIST, LANE), F32)

    for j in range(DC // LANE):
        xe_ref[j, HIST:HIST + tt, :] = u_ref[:, j * LANE:(j + 1) * LANE]
    for j in range(DC // LANE):
        sl = slice(j * LANE, (j + 1) * LANE)
        acc = jnp.broadcast_to(b_ref[:, sl], (tt, LANE))
        for k in range(KC):
            acc = acc + w_ref[k:k + 1, sl] * xe_ref[j, off + k:off + k + tt, :]
        y_ref[:, sl] = acc
    for j in range(DC // LANE):
        nb_ref[:, j * LANE:(j + 1) * LANE] = xe_ref[j, HIST + tt - (KC - 1):HIST + tt, :]
        xe_ref[j, 0:HIST, :] = xe_ref[j, tt:tt + HIST, :]
    @pl.when(t >= 0)
    def _():
        y = y_ref[...]
        yc = y - jnp.mean(y, axis=-1, keepdims=True)
        a_ref[...] = _silu(yc * lax.rsqrt(jnp.mean(yc * yc, axis=-1, keepdims=True) + EPS) * lg_ref[...]
                           + lb_ref[...]).astype(a_ref.dtype)


def _conv_prompt(u, w, b, lg, lb, *, nb, seq, tt=256):
    nt = nb * seq
    nper = seq // tt
    par = lambda r: _resident((r, DC))
    return pl.pallas_call(
        functools.partial(_conv_prompt_kernel, tt=tt),
        grid=(nb, nper),
        in_specs=[pl.BlockSpec((tt, DC), lambda s, t: (s * nper + t, 0)), par(KC), par(1), par(1), par(1)],
        out_specs=[pl.BlockSpec((tt, DC), lambda s, t: (s * nper + t, 0)),
                   pl.BlockSpec((None, KC - 1, DC), lambda s, t: (s, 0, 0))],
        out_shape=[jax.ShapeDtypeStruct((nt, DC), BF16), jax.ShapeDtypeStruct((nb, KC - 1, DC), F32)],
        scratch_shapes=[pltpu.VMEM((DC // LANE, HIST + tt, LANE), F32), pltpu.VMEM((tt, DC), F32)],
        compiler_params=_cparams(("parallel", "arbitrary")),
        name="conv_prompt",
    )(u, w, b, lg, lb)


def _conv_sample_kernel(u_ref, hist_ref, w_ref, b_ref, lg_ref, lb_ref, a_ref, nh_ref, us_ref, y_ref, as_ref,
                        *, sb, tv):
    nl = DC // LANE
    for j in range(nl):
        us_ref[j] = u_ref[:, j * LANE:(j + 1) * LANE]
    for j in range(nl):
        sl = slice(j * LANE, (j + 1) * LANE)
        accs = [jnp.broadcast_to(b_ref[:, sl], (sb, LANE)) for _ in range(tv)]
        for m in range(KC - 1 + tv):
            if m < KC - 1:
                xm = hist_ref[m, :, sl]
            else:
                xm = us_ref[j, pl.ds(m - (KC - 1), sb, stride=tv), :]
            for t in range(tv):
                if 0 <= m - t < KC:
                    accs[t] = accs[t] + w_ref[m - t:m - t + 1, sl] * xm
            if m >= tv:
                nh_ref[m - tv, :, sl] = xm
        for t in range(tv):
            y_ref[t, :, sl] = accs[t]
    for t in range(tv):
        y = y_ref[t]
        yc = y - jnp.mean(y, axis=-1, keepdims=True)
        a = _silu(yc * lax.rsqrt(jnp.mean(yc * yc, axis=-1, keepdims=True) + EPS) * lg_ref[...] + lb_ref[...])
        for j in range(nl):
            as_ref[j, pl.ds(t, sb, stride=tv), :] = a[:, j * LANE:(j + 1) * LANE]
    for j in range(nl):
        a_ref[:, j * LANE:(j + 1) * LANE] = as_ref[j].astype(a_ref.dtype)


def _conv_sample(u, hist, w, b, lg, lb, *, row0, tv, sb=32):
    nb = hist.shape[1]
    rows = sb * tv
    par = lambda r: _resident((r, DC))
    blk0 = row0 // rows
    hist_spec = pl.BlockSpec((KC - 1, sb, DC), lambda i: (0, i, 0))
    return pl.pallas_call(
        functools.partial(_conv_sample_kernel, sb=sb, tv=tv),
        grid=(nb // sb,),
        in_specs=[pl.BlockSpec((rows, DC), lambda i: (blk0 + i, 0)), hist_spec, par(KC), par(1), par(1), par(1)],
        out_specs=[pl.BlockSpec((rows, DC), lambda i: (i, 0)), hist_spec],
        out_shape=[jax.ShapeDtypeStruct((nb * tv, DC), BF16), jax.ShapeDtypeStruct((KC - 1, nb, DC), F32)],
        scratch_shapes=[pltpu.VMEM((DC // LANE, rows, LANE), F32), pltpu.VMEM((tv, sb, DC), F32),
                        pltpu.VMEM((DC // LANE, rows, LANE), F32)],
        compiler_params=_cparams(("arbitrary",)),
        name="conv_sample",
    )(u, hist, w, b, lg, lb)


def _dot01(a, b, dims, *, data):
    x = b if data else a
    one = (a if data else b).astype(BF16)
    t0 = x.astype(BF16)
    r1 = x - t0.astype(F32)
    t1 = r1.astype(BF16)
    t2 = (r1 - t1.astype(F32)).astype(BF16)
    acc = None
    for t in (t0, t1, t2):
        lhs, rhs = (one, t) if data else (t, one)
        p = lax.dot_general(lhs, rhs, dims, preferred_element_type=F32)
        acc = p if acc is None else acc + p
    return acc


def _to_slabs(xe_ref, r0, x):
    for j in range(x.shape[1] // LANE):
        xe_ref[j, r0:r0 + x.shape[0], :] = x[:, j * LANE:(j + 1) * LANE]


def _from_slabs(xe_ref, r0, rows):
    return jnp.concatenate([xe_ref[j, r0:r0 + rows, :] for j in range(xe_ref.shape[0])], axis=1)


def _ssd_chunk(xe_ref, z, dt_raw, st_ref, cw_ref, cb_ref, dtb_ref, alog_ref, dsk_ref, ng_ref, *, L, tv, lq=None,
               h_io=None):
    mm_dims = (((1,), (0,)), ((), ()))
    lq = L if lq is None else lq
    nsq = L // lq
    off = HALO - (KS - 1)
    cols = []
    for j in range(DX // LANE):
        sl = slice(j * LANE, (j + 1) * LANE)
        acc = jnp.broadcast_to(cb_ref[:, sl], (L, LANE))
        for k in range(KS):
            acc = acc + cw_ref[k:k + 1, sl] * xe_ref[j, off + k:off + k + L, :]
        cols.append(acc)
    xc = _silu(jnp.concatenate(cols, axis=1))
    xs = xc[:, 0:DS]

    lane = lax.broadcasted_iota(jnp.int32, (L, LANE), 1)
    rowi = lax.broadcasted_iota(jnp.int32, (L, LANE), 0)
    xdt = dt_raw + dtb_ref[...]
    dt = jnp.maximum(xdt, 0.0) + jnp.log1p(jnp.exp(-jnp.abs(xdt)))
    dt = jnp.where((lane < NH) & (rowi % lq < tv), dt, 0.0)
    da = dt * (-jnp.exp(alog_ref[...]))

    r2 = lax.broadcasted_iota(jnp.int32, (L, L), 0)
    c2 = lax.broadcasted_iota(jnp.int32, (L, L), 1)
    same = r2 // lq == c2 // lq
    causal = (r2 >= c2) & same
    if nsq == 1:
        a_cum = _dot01(causal.astype(F32), da, mm_dims, data=1)
        a_tot = jnp.broadcast_to(a_cum[L - 1:L, :], (L, LANE))
    else:
        cums = _dot01(jnp.concatenate([causal.astype(F32), same.astype(F32)], axis=0), da, mm_dims, data=1)
        a_cum = cums[0:L]
        a_tot = cums[L:2 * L]

    er = lax.broadcasted_iota(jnp.int32, (LANE, DS), 0)
    ec = lax.broadcasted_iota(jnp.int32, (LANE, DS), 1)
    expand = (ec // HP == er).astype(F32)
    stack = jnp.concatenate([jnp.exp(a_cum), jnp.exp(a_tot - a_cum) * dt, jnp.exp(a_tot[0:SUB])], axis=0)
    stack_x = _dot01(stack, expand, mm_dims, data=0)
    ea_x = stack_x[0:L]
    wend_x = stack_x[L:2 * L]
    cd_x = stack_x[2 * L:2 * L + 1]

    ir = lax.broadcasted_iota(jnp.int32, (LANE, LANE), 0)
    ic = lax.broadcasted_iota(jnp.int32, (LANE, LANE), 1)
    ident = (ir == ic).astype(F32)
    tr = _dot01(ident, jnp.concatenate([dt, a_cum], axis=0), (((1,), (1,)), ((), ())), data=1)
    dt_t = tr[:, 0:L]
    acum_t = tr[:, L:2 * L]

    lane_x = lax.broadcasted_iota(jnp.int32, (L, LANE), 1)
    hpg = NH // NG
    bms = [xc[:, DS + g * NS:DS + (g + 1) * NS] for g in range(NG)]
    cms = [xc[:, DS + NG * NS + g * NS:DS + NG * NS + (g + 1) * NS] for g in range(NG)]
    gss = [slice(g * (DS // NG), (g + 1) * (DS // NG)) for g in range(NG)]
    cbms = [lax.dot_general(cms[g], bms[g], (((1,), (1,)), ((), ())), preferred_element_type=F32)
            for g in range(NG)]
    if h_io is None:
        st_old = [st_ref[:, gss[g]] for g in range(NG)]
        y_off = [jnp.dot(cms[g], st_old[g], preferred_element_type=F32) * ea_x[:, gss[g]] for g in range(NG)]
        st_new = [jnp.dot(bms[g].T, xs[:, gss[g]] * wend_x[:, gss[g]], preferred_element_type=F32)
                  for g in range(NG)]
        for g in range(NG):
            st_ref[:, gss[g]] = st_old[g] * cd_x[:, gss[g]] + st_new[g]
    else:
        rows_g = hpg * HP
        sq = [slice(q * lq, (q + 1) * lq) for q in range(nsq)]
        hs = [[h_io[q][0][g * hpg:(g + 1) * hpg].reshape(rows_g, NS) for g in range(NG)] for q in range(nsq)]
        y_off = [jnp.concatenate([lax.dot_general(cms[g][sq[q]], hs[q][g], (((1,), (1,)), ((), ())),
                                                  preferred_element_type=F32) for q in range(nsq)], axis=0)
                 * ea_x[:, gss[g]] for g in range(NG)]
        xw = [xs[:, gss[g]] * wend_x[:, gss[g]] for g in range(NG)]
        upd = [[lax.dot_general(xw[g][sq[q]], bms[g][sq[q]], (((0,), (0,)), ((), ())),
                                preferred_element_type=F32) for g in range(NG)] for q in range(nsq)]
        seq_decay = jnp.exp(a_tot)
        for q in range(nsq):
            for h in range(NH):
                g, hl = divmod(h, hpg)
                rows = slice(hl * HP, (hl + 1) * HP)
                h_io[q][1][h] = hs[q][g][rows, :] * seq_decay[q * lq:q * lq + 1, h:h + 1] + upd[q][g][rows, :]
    segs = [a_cum[:, h:h + 1] - acum_t[h:h + 1, :] for h in range(NH)]
    decs = [jnp.exp(jnp.where(causal, segs[h], -jnp.inf)) for h in range(NH)]
    ws = [cbms[h // hpg] * decs[h] * dt_t[h:h + 1, :] for h in range(NH)]
    y_diag = []
    for pr in range(NH // 2):
        xp = xs[:, 2 * pr * HP:(2 * pr + 2) * HP]
        rhs = jnp.concatenate([jnp.where(lane_x < HP, xp, 0.0), jnp.where(lane_x >= HP, xp, 0.0)], axis=0)
        y_diag.append(jnp.dot(jnp.concatenate([ws[2 * pr], ws[2 * pr + 1]], axis=1), rhs,
                              preferred_element_type=F32))
    y = jnp.concatenate(y_diag, axis=1) + jnp.concatenate(y_off, axis=1) + dsk_ref[...] * xs
    y = y * _silu(z)
    outs = []
    for g in range(NG):
        gs = slice(g * (DS // NG), (g + 1) * (DS // NG))
        outs.append(_rms(y[:, gs], ng_ref[:, gs]))
    return jnp.concatenate(outs, axis=1)


def _state_out(st_ref, h_ref):
    for j in range(DS // LANE):
        blk = st_ref[:, j * LANE:(j + 1) * LANE].T
        for q in range(LANE // HP):
            h_ref[j * (LANE // HP) + q] = blk[q * HP:(q + 1) * HP, :]


def _ssd_prompt_kernel(xbc_ref, z_ref, dt_ref, cw_ref, cb_ref, dtb_ref, alog_ref, dsk_ref, ng_ref,
                       y_ref, nb_ref, h_ref, xe_ref, st_ref, *, L):
    c = pl.program_id(1)

    @pl.when(c == 0)
    def _():
        xe_ref[:, 0:HALO, :] = jnp.zeros((DX // LANE, HALO, LANE), F32)
        st_ref[...] = jnp.zeros_like(st_ref)

    _to_slabs(xe_ref, HALO, xbc_ref[...])
    y_ref[...] = _ssd_chunk(xe_ref, z_ref[...], dt_ref[...], st_ref, cw_ref, cb_ref, dtb_ref, alog_ref, dsk_ref,
                            ng_ref, L=L, tv=L).astype(y_ref.dtype)
    xe_ref[:, 0:HALO, :] = xe_ref[:, L:L + HALO, :]

    @pl.when(c == pl.num_programs(1) - 1)
    def _():
        nb_ref[...] = _from_slabs(xe_ref, HALO - (KS - 1), KS - 1)
        _state_out(st_ref, h_ref)


def _ssd_params_specs():
    return [_resident((KS, DX)), _resident((1, DX)), _resident((1, LANE)), _resident((1, LANE)),
            _resident((1, DS)), _resident((1, DS))]


def _ssd_prompt(xbc, z, dt, params, *, nb, seq):
    nt = nb * seq
    L = CHUNK
    nper = seq // L
    row = lambda n: pl.BlockSpec((L, n), lambda s, c: (s * nper + c, 0))
    return pl.pallas_call(
        functools.partial(_ssd_prompt_kernel, L=L),
        grid=(nb, nper),
        in_specs=[row(DX), row(DS), row(LANE)] + _ssd_params_specs(),
        out_specs=[row(DS),
                   pl.BlockSpec((None, KS - 1, DX), lambda s, c: (s, 0, 0)),
                   pl.BlockSpec((None, NH, HP, NS), lambda s, c: (s, 0, 0, 0))],
        out_shape=[jax.ShapeDtypeStruct((nt, DS), BF16), jax.ShapeDtypeStruct((nb, KS - 1, DX), F32),
                   jax.ShapeDtypeStruct((nb, NH, HP, NS), F32)],
        scratch_shapes=[pltpu.VMEM((DX // LANE, HALO + L, LANE), F32), pltpu.VMEM((NS, DS), F32)],
        compiler_params=_cparams(("parallel", "arbitrary")),
        name="ssd_prompt",
    )(xbc, z, dt, *params)


def _ssd_sample_kernel(xbc_ref, z_ref, dt_ref, hist_ref, h0_ref, cw_ref, cb_ref, dtb_ref, alog_ref, dsk_ref,
                       ng_ref, y_ref, nb_ref, h_ref, xe_ref, zb_ref, dtp_ref, *, lq, tv, nseq):
    assert lq - tv >= KS - 1 and HALO >= KS - 1
    xe_ref[...] = jnp.zeros_like(xe_ref)
    zb_ref[...] = jnp.zeros_like(zb_ref)
    dtp_ref[...] = jnp.zeros_like(dtp_ref)
    for j in range(nseq):
        rows = slice(j * tv, (j + 1) * tv)
        _to_slabs(xe_ref, HALO + j * lq - (KS - 1), hist_ref[j])
        _to_slabs(xe_ref, HALO + j * lq, xbc_ref[rows, :])
        zb_ref[j * lq:j * lq + tv, :] = z_ref[rows, :]
        dtp_ref[j * lq:j * lq + tv, :] = dt_ref[rows, :]
    y = _ssd_chunk(xe_ref, zb_ref[...], dtp_ref[...], None, cw_ref, cb_ref, dtb_ref, alog_ref, dsk_ref, ng_ref,
                   L=nseq * lq, tv=tv, lq=lq, h_io=[(h0_ref.at[j], h_ref.at[j]) for j in range(nseq)])
    y_ref[...] = jnp.concatenate([y[j * lq:j * lq + tv, :] for j in range(nseq)], axis=0).astype(y_ref.dtype)
    for j in range(nseq):
        nb_ref[j] = _from_slabs(xe_ref, HALO + j * lq + tv - (KS - 1), KS - 1)


def _ssd_sample(xbc, z, dt, hist, h0, params, *, row0, tv, nseq=16, lq=SUB):
    nb = hist.shape[0]
    rows = nseq * tv
    L = nseq * lq
    blk0 = row0 // rows
    row = lambda n: pl.BlockSpec((rows, n), lambda i: (blk0 + i, 0))
    return pl.pallas_call(
        functools.partial(_ssd_sample_kernel, lq=lq, tv=tv, nseq=nseq),
        grid=(nb // nseq,),
        in_specs=[row(DX), row(DS), row(LANE),
                  pl.BlockSpec((nseq, KS - 1, DX), lambda i: (i, 0, 0)),
                  pl.BlockSpec((nseq, NH, HP, NS), lambda i: (i, 0, 0, 0))] + _ssd_params_specs(),
        out_specs=[pl.BlockSpec((rows, DS), lambda i: (i, 0)),
                   pl.BlockSpec((nseq, KS - 1, DX), lambda i: (i, 0, 0)),
                   pl.BlockSpec((nseq, NH, HP, NS), lambda i: (i, 0, 0, 0))],
        out_shape=[jax.ShapeDtypeStruct((nb * tv, DS), BF16), jax.ShapeDtypeStruct((nb, KS - 1, DX), F32),
                   jax.ShapeDtypeStruct((nb, NH, HP, NS), F32)],
        scratch_shapes=[pltpu.VMEM((DX // LANE, HALO + L, LANE), F32), pltpu.VMEM((L, DS), F32),
                        pltpu.VMEM((L, LANE), F32)],
        compiler_params=_cparams(("arbitrary",)),
        name="ssd_sample",
    )(xbc, z, dt, hist, h0, *params)


def _proj_out_kernel(*refs, n, n_main, side_plan):
    lhs = refs[0:2 * n]
    ws = refs[2 * n:3 * n]
    x_ref, g_ref = refs[3 * n:3 * n + 2]
    ns = len(side_plan)
    side_in = refs[3 * n + 2:3 * n + 2 + ns]
    o_ref = refs[3 * n + 2 + ns]
    side_out = refs[3 * n + 3 + ns:]
    _side_cast(pl.program_id(0), side_plan, side_in, side_out)
    is_main = pl.program_id(0) < n_main

    def tile(which):
        m = None
        for k, w_ref in enumerate(ws):
            a = lhs[2 * k + which][...].astype(BF16)
            p = jnp.dot(a, w_ref[...].astype(BF16), preferred_element_type=F32)
            m = p if m is None else m + p
        o_ref[...] = x_ref[...] + _rms(m, g_ref[...])

    @pl.when(is_main)
    def _():
        tile(0)

    @pl.when(jnp.logical_not(is_main))
    def _():
        tile(1)


def _proj_out(lhs_pairs, ws, x, g, *, tm=512, side=()):
    nt = x.shape[0]
    side_plan, side_steps = _side_plan(side)
    assert side_steps <= nt // tm
    n = len(lhs_pairs)
    n_main = nt // tm - 1
    lhs_specs = []
    for a_main, a_extra in lhs_pairs:
        assert a_main.shape[0] == n_main * tm and a_extra.shape[0] == tm
        lhs_specs.append(pl.BlockSpec((tm, a_main.shape[1]), lambda i: (jnp.minimum(i, n_main - 1), 0)))
        lhs_specs.append(pl.BlockSpec((tm, a_extra.shape[1]), lambda i: (0, 0)))
    res = pl.pallas_call(
        functools.partial(_proj_out_kernel, n=n, n_main=n_main, side_plan=side_plan),
        grid=(nt // tm,),
        in_specs=lhs_specs + [_resident(w.shape) for w in ws]
                 + [pl.BlockSpec((tm, D), lambda i: (i, 0)), _resident((1, D))] + _side_specs(side, lambda i: i),
        out_specs=[pl.BlockSpec((tm, D), lambda i: (i, 0))] + _side_specs(side, lambda i: i),
        out_shape=[jax.ShapeDtypeStruct((nt, D), F32)] + _side_shapes(side),
        compiler_params=_cparams(("arbitrary",)),
        name="proj_out",
    )(*[a for pair in lhs_pairs for a in pair], *ws, x, g, *[w for w, _, _ in side])
    return tuple(res) if side else res[0]


def _norm_proj_kernel(*refs, n, heads, side_plan):
    x_ref, g_ref = refs[0:2]
    ws = refs[2:2 + n]
    ns = len(side_plan)
    side_in = refs[2 + n:2 + n + ns]
    outs = refs[2 + n + ns:2 + 2 * n + ns]
    side_out = refs[2 + 2 * n + ns:]
    _side_cast(pl.program_id(0), side_plan, side_in, side_out)
    hn = _rms(x_ref[...], g_ref[...]).astype(BF16)
    for w_ref, o_ref in zip(ws, outs):
        for h in range(XH):
            sl = slice(h * XD, (h + 1) * XD)
            r = jnp.dot(hn, w_ref[:, sl].astype(BF16), preferred_element_type=F32).astype(o_ref.dtype)
            if heads:
                o_ref[:, h, :] = r
            else:
                o_ref[:, sl] = r


def _norm_proj(x, g, ws, out_dtype, *, heads=False, tm=512, side=()):
    nt = x.shape[0]
    n = len(ws)
    side_plan, side_steps = _side_plan(side)
    if heads:
        tm = NM
        out_specs = [pl.BlockSpec((None, NM, XH, XD), lambda i: (i, 0, 0, 0))] * n
        out_shape = [jax.ShapeDtypeStruct((nt // NM, NM, XH, XD), out_dtype)] * n
    else:
        out_specs = [pl.BlockSpec((tm, D), lambda i: (i, 0))] * n
        out_shape = [jax.ShapeDtypeStruct((nt, D), out_dtype)] * n
    assert side_steps <= nt // tm
    return pl.pallas_call(
        functools.partial(_norm_proj_kernel, n=n, heads=heads, side_plan=side_plan),
        grid=(nt // tm,),
        in_specs=[pl.BlockSpec((tm, D), lambda i: (i, 0)), _resident((1, D))] + [_resident((D, D))] * n
                 + _side_specs(side, lambda i: i),
        out_specs=out_specs + _side_specs(side, lambda i: i),
        out_shape=out_shape + _side_shapes(side),
        compiler_params=_cparams(("arbitrary",)),
        name="norm_proj",
    )(x, g, *ws, *[w for w, _, _ in side])


NLT = XD // LANE
LT_STRIDE = NLT * XH
KV_ROWS = NM * LT_STRIDE


def _kv_tiles(x):
    nb = x.shape[0]
    return x.reshape(nb, NM, XH, NLT, LANE).transpose(0, 1, 3, 2, 4).reshape(nb, KV_ROWS, LANE)


def _attn_prompt_kernel(q_ref, k_ref, v_ref, o_ref, kh_ref, vh_ref):
    @pl.when(pl.program_id(1) == 0)
    def _():
        for h in range(XH):
            for lt in range(NLT):
                rows = pl.ds(lt * XH + h, NM, stride=LT_STRIDE)
                kh_ref[h, :, lt * LANE:(lt + 1) * LANE] = k_ref[0, rows, :].astype(BF16)
                vh_ref[h, :, lt * LANE:(lt + 1) * LANE] = v_ref[0, rows, :].astype(BF16)

    scale = XD ** -0.5
    nt_dims = (((1,), (1,)), ((), ()))
    heads = range(XH)
    s = [lax.dot_general(q_ref[:, h * XD:(h + 1) * XD], kh_ref[h], nt_dims, preferred_element_type=F32) * scale
         for h in heads]
    e = [jnp.exp(s[h] - jnp.max(s[h], axis=-1, keepdims=True)) for h in heads]
    p = [(e[h] / jnp.sum(e[h], axis=-1, keepdims=True)).astype(BF16) for h in heads]
    for h in heads:
        o_ref[:, h * XD:(h + 1) * XD] = jnp.dot(p[h], vh_ref[h], preferred_element_type=F32).astype(o_ref.dtype)


def _group_sum(x, col_lt):
    n = x.shape[1]
    a = x + jnp.where(col_lt % 2 == 0, pltpu.roll(x, n - 1, axis=1), pltpu.roll(x, 1, axis=1))
    return a + jnp.where(col_lt < 2, pltpu.roll(a, n - 2, axis=1), pltpu.roll(a, 2, axis=1))


def _attn_sample_kernel(q_ref, k_ref, v_ref, o_ref, *, nseq, tv):
    assert NLT == 4
    tq = q_ref.shape[0]
    ncol = NM * NLT
    scale = XD ** -0.5
    nt_dims = (((1,), (1,)), ((), ()))
    col_lt = lax.broadcasted_iota(jnp.int32, (tq, ncol), 1) % NLT
    rowi = lax.broadcasted_iota(jnp.int32, (tq, LANE), 0)
    for h in range(XH):
        qp = jnp.concatenate([q_ref[:, h * XD + lt * LANE:h * XD + (lt + 1) * LANE].astype(F32)
                              for lt in range(NLT)], axis=0)
        seqs = range(nseq)
        g = [lax.dot_general(qp, k_ref[j, pl.ds(h, ncol, stride=XH), :], nt_dims, preferred_element_type=F32)
             for j in seqs]
        s4 = [sum(jnp.where(col_lt == lt, g[j][lt * tq:(lt + 1) * tq], 0.0) for lt in range(NLT)) for j in seqs]
        s = [_group_sum(s4[j], col_lt) * scale for j in seqs]
        e = [jnp.exp(s[j] - jnp.max(s[j], axis=-1, keepdims=True)) for j in seqs]
        p = [e[j] / (jnp.sum(e[j], axis=-1, keepdims=True) * (1.0 / NLT)) for j in seqs]
        o = [jnp.dot(jnp.concatenate([jnp.where(col_lt == lt, p[j], 0.0) for lt in range(NLT)], axis=0),
                     v_ref[j, pl.ds(h, ncol, stride=XH), :], preferred_element_type=F32) for j in seqs]
        acc = [None] * NLT
        for j in seqs:
            mine = (rowi >= j * tv) & (rowi < (j + 1) * tv)
            for lt in range(NLT):
                acc[lt] = jnp.where(mine, o[j][lt * tq:(lt + 1) * tq], 0.0 if acc[lt] is None else acc[lt])
        for lt in range(NLT):
            o_ref[:, h * XD + lt * LANE:h * XD + (lt + 1) * LANE] = acc[lt].astype(o_ref.dtype)


def _attn_prompt(q, k, v, *, nb, seq, tq=1024):
    nt = nb * seq
    nper = seq // tq
    kv_spec = pl.BlockSpec((1, KV_ROWS, LANE), lambda s, t: (s, 0, 0))
    return pl.pallas_call(
        _attn_prompt_kernel,
        grid=(nb, nper),
        in_specs=[pl.BlockSpec((tq, D), lambda s, t: (s * nper + t, 0)), kv_spec, kv_spec],
        out_specs=pl.BlockSpec((tq, D), lambda s, t: (s * nper + t, 0)),
        out_shape=jax.ShapeDtypeStruct((nt, D), BF16),
        scratch_shapes=[pltpu.VMEM((XH, NM, XD), BF16), pltpu.VMEM((XH, NM, XD), BF16)],
        compiler_params=_cparams(("arbitrary", "arbitrary")),
        name="attn_prompt",
    )(q, _kv_tiles(k), _kv_tiles(v))


def _attn_sample(q, k, v, *, row0, tv, nseq=4):
    nb = k.shape[0]
    rows = nseq * tv
    blk0 = row0 // rows
    kv_spec = pl.BlockSpec((nseq, KV_ROWS, LANE), lambda i: (i, 0, 0))
    return pl.pallas_call(
        functools.partial(_attn_sample_kernel, nseq=nseq, tv=tv),
        grid=(nb // nseq,),
        in_specs=[pl.BlockSpec((rows, D), lambda i: (blk0 + i, 0)), kv_spec, kv_spec],
        out_specs=pl.BlockSpec((rows, D), lambda i: (i, 0)),
        out_shape=jax.ShapeDtypeStruct((nb * tv, D), BF16),
        compiler_params=_cparams(("arbitrary",)),
        name="attn_sample",
    )(q, _kv_tiles(k), _kv_tiles(v))


def _row(v):
    return v.reshape(1, -1).astype(F32)


def _pad_lanes(v, n):
    return jnp.pad(v.reshape(1, -1).astype(F32), ((0, 0), (0, n - v.size)))


def _layer(xs, mem, cache_k, cache_v, st_conv, st_sconv, st_ssm, p, *, nbp, seq, nbs, tv, split_out):
    npr = nbp * seq
    bf = lambda w: w.astype(BF16)

    side = [(p['w_in'].T, 0, LANE)]
    x, w_in = _ffn(xs, _row(p['ffn1_pre_g']), bf(p['ffn1_w_gate']), bf(p['ffn1_w_up']), bf(p['ffn1_w_down']),
                   _row(p['ffn1_post_g']), split_out=False, side=side)
    w_out = bf(p['w_out'])

    wt = jnp.pad(w_in[2 * DC + DS + DX:], ((0, LANE - NH), (0, 0)))
    u, z, xbc, dtr = _mix_in(x, _row(p['mix_pre_g']), w_in, wt)

    cpar = (p['conv_w'].astype(F32), _row(p['conv_b']), _row(p['conv_ln_g']), _row(p['conv_ln_b']))
    a_p, conv_p = _conv_prompt(u, *cpar, nb=nbp, seq=seq)
    a_s, conv_s = _conv_sample(u, jnp.transpose(st_conv, (1, 0, 2)), *cpar, row0=npr, tv=tv)
    conv_s = jnp.transpose(conv_s, (1, 0, 2))

    spar = (p['ssm_conv_w'].astype(F32), _row(p['ssm_conv_b']), _pad_lanes(p['dt_bias'], LANE),
            _pad_lanes(p['a_log'], LANE), jnp.repeat(p['d_skip'].astype(F32), HP).reshape(1, DS),
            _row(p['ssm_norm_g']))
    y_p, sconv_p, ssm_p = _ssd_prompt(xbc, z, dtr, spar, nb=nbp, seq=seq)
    y_s, sconv_s, ssm_s = _ssd_sample(xbc, z, dtr, st_sconv, st_ssm, spar, row0=npr, tv=tv)

    x, wg2 = _proj_out([(a_p, a_s), (y_p, y_s)], [w_out[:DC], w_out[DC:]], x, _row(p['mix_post_g']),
                       side=[(p['ffn2_w_gate'], 1, SIDE_WIDE)])

    q, wu2 = _norm_proj(x, _row(p['xattn_pre_g']), [p['w_xq']], BF16, side=[(p['ffn2_w_up'], 1, SIDE_WIDE)])
    mk, mv = _norm_proj(mem, _row(p['mem_norm_g']), [p['w_xk'], p['w_xv']], F32, heads=True)
    o_p = _attn_prompt(q, mk, mv, nb=nbp, seq=seq)
    o_s = _attn_sample(q, cache_k, cache_v, row0=npr, tv=tv)
    x, wd2 = _proj_out([(o_p, o_s)], [p['w_xo']], x, _row(p['xattn_post_g']),
                       side=[(p['ffn2_w_down'], 0, SIDE_WIDE)])

    x = _ffn([x], _row(p['ffn2_pre_g']), wg2, wu2, wd2, _row(p['ffn2_post_g']), split_out=split_out)
    return x, (mk, mv, conv_p, sconv_p, ssm_p, conv_s, sconv_s, ssm_s)


def kernel(x_prompt, x_sample, mem_prompt, cache_mem_k, cache_mem_v, state_conv, state_ssm_conv, state_ssm, ffn1_pre_g, ffn1_w_gate, ffn1_w_up, ffn1_w_down, ffn1_post_g, mix_pre_g, w_in, conv_w, conv_b, conv_ln_g, conv_ln_b, ssm_conv_w, ssm_conv_b, dt_bias, a_log, d_skip, ssm_norm_g, w_out, mix_post_g, xattn_pre_g, mem_norm_g, w_xq, w_xk, w_xv, w_xo, xattn_post_g, ffn2_pre_g, ffn2_w_gate, ffn2_w_up, ffn2_w_down, ffn2_post_g):
    params = dict(ffn1_pre_g=ffn1_pre_g, ffn1_w_gate=ffn1_w_gate, ffn1_w_up=ffn1_w_up, ffn1_w_down=ffn1_w_down,
                  ffn1_post_g=ffn1_post_g, mix_pre_g=mix_pre_g, w_in=w_in, conv_w=conv_w, conv_b=conv_b,
                  conv_ln_g=conv_ln_g, conv_ln_b=conv_ln_b, ssm_conv_w=ssm_conv_w, ssm_conv_b=ssm_conv_b,
                  dt_bias=dt_bias, a_log=a_log, d_skip=d_skip, ssm_norm_g=ssm_norm_g, w_out=w_out,
                  mix_post_g=mix_post_g, xattn_pre_g=xattn_pre_g, mem_norm_g=mem_norm_g, w_xq=w_xq,
                  w_xk=w_xk, w_xv=w_xv, w_xo=w_xo, xattn_post_g=xattn_post_g, ffn2_pre_g=ffn2_pre_g,
                  ffn2_w_gate=ffn2_w_gate, ffn2_w_up=ffn2_w_up, ffn2_w_down=ffn2_w_down, ffn2_post_g=ffn2_post_g)
    depth = ffn1_pre_g.shape[0]
    nbp, seq, _ = x_prompt.shape
    nbs, tv, _ = x_sample.shape
    npr = nbp * seq
    xs = [x_prompt.reshape(npr, D), x_sample.reshape(nbs * tv, D)]
    mem = mem_prompt.reshape(nbp * NM, D)
    per_layer = []
    for layer in range(depth):
        p = {name: w[layer] for name, w in params.items()}
        x, states = _layer(xs, mem, cache_mem_k[layer], cache_mem_v[layer],
                           state_conv[layer], state_ssm_conv[layer], state_ssm[layer], p,
                           nbp=nbp, seq=seq, nbs=nbs, tv=tv, split_out=layer == depth - 1)
        xs = [x]
        per_layer.append(states)
    mk, mv, conv_p, sconv_p, ssm_p, conv_s, sconv_s, ssm_s = [jnp.stack(t) for t in zip(*per_layer)]
    yp, ys = x
    return (yp.reshape(nbp, seq, D), ys.reshape(nbs, tv, D), mk, mv, conv_p, sconv_p, ssm_p, conv_s, sconv_s, ssm_s)
```

```python
import functools

import jax
import jax.numpy as jnp
from jax import lax
from jax.experimental import pallas as pl
from jax.experimental.pallas import tpu as pltpu

F32 = jnp.float32
BF16 = jnp.bfloat16

D = 2048
FF = 5504
DC = 1024
DS = 1024
KC = 31
NH = 16
HP = 64
NG = 2
NS = 128
KS = 4
DX = DS + 2 * NG * NS
CHUNK = 128
NM = 256
XH = 4
XD = D // XH
EPS = 1e-6

LANE = 128
SUB = 8
HIST = 32
HALO = 8

FF_TILE = 1024
SIDE_WIDE = 3 * LANE

VMEM_LIMIT = 60 * 1024 * 1024


def _cparams(sem):
    return pltpu.CompilerParams(dimension_semantics=sem, vmem_limit_bytes=VMEM_LIMIT)


def _rms(x, g):
    return x * lax.rsqrt(jnp.mean(x * x, axis=-1, keepdims=True) + EPS) * g


def _silu(x):
    return x * jax.nn.sigmoid(x)


def _resident(shape):
    return pl.BlockSpec(shape, lambda *_: (0,) * len(shape), pipeline_mode=pl.Buffered(1))


def _ffn_kernel(*refs, n_in, n_out, n_main, tf, side_plan):
    n_side = len(side_plan)
    x_refs = refs[:n_in]
    pg_ref, wg_ref, wu_ref, wd_ref, qg_ref = refs[n_in:n_in + 5]
    side_in = refs[n_in + 5:n_in + 5 + n_side]
    k = n_in + 5 + n_side
    o_refs = refs[k:k + n_out]
    side_out = refs[k + n_out:k + n_out + n_side]
    xn_ref, acc_ref = refs[k + n_out + n_side:]
    i = pl.program_id(0)
    f = pl.program_id(1)
    last = pl.num_programs(1) - 1

    def per_row_group(fn):
        if n_in == 1 and n_out == 1:
            fn(x_refs[0], o_refs[0])
        else:
            @pl.when(i < n_main)
            def _():
                fn(x_refs[0], o_refs[0])

            @pl.when(i >= n_main)
            def _():
                fn(x_refs[-1], o_refs[-1])

    @pl.when(f == 0)
    def _():
        def pre_norm(x_ref, _):
            xn_ref[...] = _rms(x_ref[...], pg_ref[...]).astype(BF16)

        per_row_group(pre_norm)
        acc_ref[...] = jnp.zeros_like(acc_ref)

    def hidden_tile(width):
        xn = xn_ref[...]
        h = jnp.dot(xn, wg_ref[:, 0:width], preferred_element_type=F32)
        u = jnp.dot(xn, wu_ref[:, 0:width], preferred_element_type=F32)
        a = (_silu(h) * u).astype(BF16)
        acc_ref[...] += jnp.dot(a, wd_ref[0:width, :], preferred_element_type=F32)

    @pl.when(f < last)
    def _():
        hidden_tile(tf)

    @pl.when(f == last)
    def _():
        hidden_tile(FF - (FF // tf) * tf)
        r = 0.5 * _rms(acc_ref[...], qg_ref[...])

        def residual(x_ref, o_ref):
            o_ref[...] = x_ref[...] + r

        per_row_group(residual)

    _side_cast(i * pl.num_programs(1) + f, side_plan, side_in, side_out)


def _side_plan(side):
    plan, start = [], 0
    for w, axis, width in side:
        n = pl.cdiv(w.shape[axis], width)
        plan.append((start, n))
        start += n
    return tuple(plan), start


def _side_specs(side, step_of):
    specs = []
    for (w, axis, width), (start, n) in zip(side, _side_plan(side)[0]):
        strip = lambda *idx, start=start, n=n: jnp.clip(step_of(*idx) - start, 0, n - 1)
        if axis == 0:
            specs.append(pl.BlockSpec((width, w.shape[1]), lambda *idx, strip=strip: (strip(*idx), 0)))
        else:
            specs.append(pl.BlockSpec((w.shape[0], width), lambda *idx, strip=strip: (0, strip(*idx))))
    return specs


def _side_cast(step, side_plan, side_in, side_out):
    for m, (start, n) in enumerate(side_plan):
        @pl.when((step >= start) & (step < start + n))
        def _(m=m):
            side_out[m][...] = side_in[m][...].astype(BF16)


def _side_shapes(side):
    return [jax.ShapeDtypeStruct(w.shape, BF16) for w, _, _ in side]


def _ffn(xs, pre_g, wg, wu, wd, post_g, *, split_out, tm=512, tf=FF_TILE, side=()):
    n_in = len(xs)
    nt = sum(x.shape[0] for x in xs)
    n_main = (nt - tm) // tm
    nf = pl.cdiv(FF, tf)
    side_plan, side_steps = _side_plan(side)
    assert side_steps <= (nt // tm) * nf
    main = lambda i, f: (jnp.minimum(i, n_main - 1), 0)
    extra = lambda i, f: (0, 0)
    whole = lambda i, f: (i, 0)
    if n_in == 1:
        x_specs = [pl.BlockSpec((tm, D), whole)]
    else:
        x_specs = [pl.BlockSpec((tm, D), main), pl.BlockSpec((tm, D), extra)]
    if split_out:
        out_specs = [pl.BlockSpec((tm, D), main), pl.BlockSpec((tm, D), extra)]
        out_shape = [jax.ShapeDtypeStruct((n_main * tm, D), F32), jax.ShapeDtypeStruct((tm, D), F32)]
    else:
        out_specs = [pl.BlockSpec((tm, D), whole)]
        out_shape = [jax.ShapeDtypeStruct((nt, D), F32)]
    res = pl.pallas_call(
        functools.partial(_ffn_kernel, n_in=n_in, n_out=len(out_specs), n_main=n_main, tf=tf, side_plan=side_plan),
        grid=(nt // tm, nf),
        in_specs=x_specs + [
            pl.BlockSpec((1, D), lambda i, f: (0, 0)),
            pl.BlockSpec((D, tf), lambda i, f: (0, f)),
            pl.BlockSpec((D, tf), lambda i, f: (0, f)),
            pl.BlockSpec((tf, D), lambda i, f: (f, 0)),
            pl.BlockSpec((1, D), lambda i, f: (0, 0)),
        ] + _side_specs(side, lambda i, f: i * nf + f),
        out_specs=out_specs + _side_specs(side, lambda i, f: i * nf + f),
        out_shape=out_shape + _side_shapes(side),
        scratch_shapes=[pltpu.VMEM((tm, D), BF16), pltpu.VMEM((tm, D), F32)],
        compiler_params=_cparams(("arbitrary", "arbitrary")),
        name="ffn",
    )(*xs, pre_g, wg, wu, wd, post_g, *[w for w, _, _ in side])
    main_res = res[0] if len(out_specs) == 1 else tuple(res[:2])
    return (main_res, *res[len(out_specs):]) if side else main_res


def _mix_in_kernel(x_ref, g_ref, w_ref, wt_ref, u_ref, z_ref, xbc_ref, dt_ref, *, tn):
    hn = _rms(x_ref[...], g_ref[...]).astype(BF16)
    nt_dims = (((1,), (1,)), ((), ()))

    def cols(start, c):
        return lax.dot_general(hn, w_ref[start + c * tn:start + (c + 1) * tn, :], nt_dims,
                               preferred_element_type=F32)

    for c in range(DC // tn):
        u_ref[:, c * tn:(c + 1) * tn] = cols(0, c) * jax.nn.sigmoid(cols(DC, c))
    for c in range(DS // tn):
        z_ref[:, c * tn:(c + 1) * tn] = cols(2 * DC, c)
    for c in range(DX // tn):
        xbc_ref[:, c * tn:(c + 1) * tn] = cols(2 * DC + DS, c)
    dt_ref[...] = lax.dot_general(hn, wt_ref[...], nt_dims, preferred_element_type=F32)


def _mix_in(x, g, w, wt, *, tm=512, tn=512):
    nt = x.shape[0]
    row = lambda n: pl.BlockSpec((tm, n), lambda i: (i, 0))
    return pl.pallas_call(
        functools.partial(_mix_in_kernel, tn=tn),
        grid=(nt // tm,),
        in_specs=[row(D), _resident((1, D)), _resident(w.shape), _resident((LANE, D))],
        out_specs=[row(DC), row(DS), row(DX), row(LANE)],
        out_shape=[jax.ShapeDtypeStruct((nt, n), F32) for n in (DC, DS, DX, LANE)],
        compiler_params=_cparams(("parallel",)),
        name="mix_in",
    )(x, g, w, wt)


def _conv_prompt_kernel(u_ref, w_ref, b_ref, lg_ref, lb_ref, a_ref, nb_ref, xe_ref, y_ref, *, tt):
    t = pl.program_id(1)
    off = HIST - (KC - 1)

    @pl.when(t == 0)
    def _():
        xe_ref[:, 0:HIST, :] = jnp.zeros((DC // LANE, HIST, LANE), F32)

    for j in range(DC // LANE):
        xe_ref[j, HIST:HIST + tt, :] = u_ref[:, j * LANE:(j + 1) * LANE]
    for j in range(DC // LANE):
        sl = slice(j * LANE, (j + 1) * LANE)
        acc = jnp.broadcast_to(b_ref[:, sl], (tt, LANE))
        for k in range(KC):
            acc = acc + w_ref[k:k + 1, sl] * xe_ref[j, off + k:off + k + tt, :]
        y_ref[:, sl] = acc
    for j in range(DC // LANE):
        nb_ref[:, j * LANE:(j + 1) * LANE] = xe_ref[j, HIST + tt - (KC - 1):HIST + tt, :]
        xe_ref[j, 0:HIST, :] = xe_ref[j, tt:tt + HIST, :]
    @pl.when(t >= 0)
    def _():
        y = y_ref[...]
        yc = y - jnp.mean(y, axis=-1, keepdims=True)
        a_ref[...] = _silu(yc * lax.rsqrt(jnp.mean(yc * yc, axis=-1, keepdims=True) + EPS) * lg_ref[...]
                           + lb_ref[...]).astype(a_ref.dtype)


def _conv_prompt(u, w, b, lg, lb, *, nb, seq, tt=256):
    nt = nb * seq
    nper = seq // tt
    par = lambda r: _resident((r, DC))
    return pl.pallas_call(
        functools.partial(_conv_prompt_kernel, tt=tt),
        grid=(nb, nper),
        in_specs=[pl.BlockSpec((tt, DC), lambda s, t: (s * nper + t, 0)), par(KC), par(1), par(1), par(1)],
        out_specs=[pl.BlockSpec((tt, DC), lambda s, t: (s * nper + t, 0)),
                   pl.BlockSpec((None, KC - 1, DC), lambda s, t: (s, 0, 0))],
        out_shape=[jax.ShapeDtypeStruct((nt, DC), BF16), jax.ShapeDtypeStruct((nb, KC - 1, DC), F32)],
        scratch_shapes=[pltpu.VMEM((DC // LANE, HIST + tt, LANE), F32), pltpu.VMEM((tt, DC), F32)],
        compiler_params=_cparams(("parallel", "arbitrary")),
        name="conv_prompt",
    )(u, w, b, lg, lb)


def _conv_sample_kernel(u_ref, hist_ref, w_ref, b_ref, lg_ref, lb_ref, a_ref, nh_ref, us_ref, y_ref, as_ref,
                        *, sb, tv):
    nl = DC // LANE
    for j in range(nl):
        us_ref[j] = u_ref[:, j * LANE:(j + 1) * LANE]
    for j in range(nl):
        sl = slice(j * LANE, (j + 1) * LANE)
        accs = [jnp.broadcast_to(b_ref[:, sl], (sb, LANE)) for _ in range(tv)]
        for m in range(KC - 1 + tv):
            if m < KC - 1:
                xm = hist_ref[m, :, sl]
            else:
                xm = us_ref[j, pl.ds(m - (KC - 1), sb, stride=tv), :]
            for t in range(tv):
                if 0 <= m - t < KC:
                    accs[t] = accs[t] + w_ref[m - t:m - t + 1, sl] * xm
            if m >= tv:
                nh_ref[m - tv, :, sl] = xm
        for t in range(tv):
            y_ref[t, :, sl] = accs[t]
    for t in range(tv):
        y = y_ref[t]
        yc = y - jnp.mean(y, axis=-1, keepdims=True)
        a = _silu(yc * lax.rsqrt(jnp.mean(yc * yc, axis=-1, keepdims=True) + EPS) * lg_ref[...] + lb_ref[...])
        for j in range(nl):
            as_ref[j, pl.ds(t, sb, stride=tv), :] = a[:, j * LANE:(j + 1) * LANE]
    for j in range(nl):
        a_ref[:, j * LANE:(j + 1) * LANE] = as_ref[j].astype(a_ref.dtype)


def _conv_sample(u, hist, w, b, lg, lb, *, row0, tv, sb=32):
    nb = hist.shape[1]
    rows = sb * tv
    par = lambda r: _resident((r, DC))
    blk0 = row0 // rows
    hist_spec = pl.BlockSpec((KC - 1, sb, DC), lambda i: (0, i, 0))
    return pl.pallas_call(
        functools.partial(_conv_sample_kernel, sb=sb, tv=tv),
        grid=(nb // sb,),
        in_specs=[pl.BlockSpec((rows, DC), lambda i: (blk0 + i, 0)), hist_spec, par(KC), par(1), par(1), par(1)],
        out_specs=[pl.BlockSpec((rows, DC), lambda i: (i, 0)), hist_spec],
        out_shape=[jax.ShapeDtypeStruct((nb * tv, DC), BF16), jax.ShapeDtypeStruct((KC - 1, nb, DC), F32)],
        scratch_shapes=[pltpu.VMEM((DC // LANE, rows, LANE), F32), pltpu.VMEM((tv, sb, DC), F32),
                        pltpu.VMEM((DC // LANE, rows, LANE), F32)],
        compiler_params=_cparams(("arbitrary",)),
        name="conv_sample",
    )(u, hist, w, b, lg, lb)


def _dot01(a, b, dims, *, data):
    x = b if data else a
    one = (a if data else b).astype(BF16)
    t0 = x.astype(BF16)
    r1 = x - t0.astype(F32)
    t1 = r1.astype(BF16)
    t2 = (r1 - t1.astype(F32)).astype(BF16)
    acc = None
    for t in (t0, t1, t2):
        lhs, rhs = (one, t) if data else (t, one)
        p = lax.dot_general(lhs, rhs, dims, preferred_element_type=F32)
        acc = p if acc is None else acc + p
    return acc


def _to_slabs(xe_ref, r0, x):
    for j in range(x.shape[1] // LANE):
        xe_ref[j, r0:r0 + x.shape[0], :] = x[:, j * LANE:(j + 1) * LANE]


def _from_slabs(xe_ref, r0, rows):
    return jnp.concatenate([xe_ref[j, r0:r0 + rows, :] for j in range(xe_ref.shape[0])], axis=1)


def _ssd_chunk(xe_ref, z, dt_raw, st_ref, cw_ref, cb_ref, dtb_ref, alog_ref, dsk_ref, ng_ref, *, L, tv, lq=None,
               h_io=None):
    mm_dims = (((1,), (0,)), ((), ()))
    lq = L if lq is None else lq
    nsq = L // lq
    off = HALO - (KS - 1)
    cols = []
    for j in range(DX // LANE):
        sl = slice(j * LANE, (j + 1) * LANE)
        acc = jnp.broadcast_to(cb_ref[:, sl], (L, LANE))
        for k in range(KS):
            acc = acc + cw_ref[k:k + 1, sl] * xe_ref[j, off + k:off + k + L, :]
        cols.append(acc)
    xc = _silu(jnp.concatenate(cols, axis=1))
    xs = xc[:, 0:DS]

    lane = lax.broadcasted_iota(jnp.int32, (L, LANE), 1)
    rowi = lax.broadcasted_iota(jnp.int32, (L, LANE), 0)
    xdt = dt_raw + dtb_ref[...]
    dt = jnp.maximum(xdt, 0.0) + jnp.log1p(jnp.exp(-jnp.abs(xdt)))
    dt = jnp.where((lane < NH) & (rowi % lq < tv), dt, 0.0)
    da = dt * (-jnp.exp(alog_ref[...]))

    r2 = lax.broadcasted_iota(jnp.int32, (L, L), 0)
    c2 = lax.broadcasted_iota(jnp.int32, (L, L), 1)
    same = r2 // lq == c2 // lq
    causal = (r2 >= c2) & same
    if nsq == 1:
        a_cum = _dot01(causal.astype(F32), da, mm_dims, data=1)
        a_tot = jnp.broadcast_to(a_cum[L - 1:L, :], (L, LANE))
    else:
        cums = _dot01(jnp.concatenate([causal.astype(F32), same.astype(F32)], axis=0), da, mm_dims, data=1)
        a_cum = cums[0:L]
        a_tot = cums[L:2 * L]

    er = lax.broadcasted_iota(jnp.int32, (LANE, DS), 0)
    ec = lax.broadcasted_iota(jnp.int32, (LANE, DS), 1)
    expand = (ec // HP == er).astype(F32)
    stack = jnp.concatenate([jnp.exp(a_cum), jnp.exp(a_tot - a_cum) * dt, jnp.exp(a_tot[0:SUB])], axis=0)
    stack_x = _dot01(stack, expand, mm_dims, data=0)
    ea_x = stack_x[0:L]
    wend_x = stack_x[L:2 * L]
    cd_x = stack_x[2 * L:2 * L + 1]

    ir = lax.broadcasted_iota(jnp.int32, (LANE, LANE), 0)
    ic = lax.broadcasted_iota(jnp.int32, (LANE, LANE), 1)
    ident = (ir == ic).astype(F32)
    tr = _dot01(ident, jnp.concatenate([dt, a_cum], axis=0), (((1,), (1,)), ((), ())), data=1)
    dt_t = tr[:, 0:L]
    acum_t = tr[:, L:2 * L]

    lane_x = lax.broadcasted_iota(jnp.int32, (L, LANE), 1)
    hpg = NH // NG
    bms = [xc[:, DS + g * NS:DS + (g + 1) * NS] for g in range(NG)]
    cms = [xc[:, DS + NG * NS + g * NS:DS + NG * NS + (g + 1) * NS] for g in range(NG)]
    gss = [slice(g * (DS // NG), (g + 1) * (DS // NG)) for g in range(NG)]
    cbms = [lax.dot_general(cms[g], bms[g], (((1,), (1,)), ((), ())), preferred_element_type=F32)
            for g in range(NG)]
    if h_io is None:
        st_old = [st_ref[:, gss[g]] for g in range(NG)]
        y_off = [jnp.dot(cms[g], st_old[g], preferred_element_type=F32) * ea_x[:, gss[g]] for g in range(NG)]
        st_new = [jnp.dot(bms[g].T, xs[:, gss[g]] * wend_x[:, gss[g]], preferred_element_type=F32)
                  for g in range(NG)]
        for g in range(NG):
            st_ref[:, gss[g]] = st_old[g] * cd_x[:, gss[g]] + st_new[g]
    else:
        rows_g = hpg * HP
        sq = [slice(q * lq, (q + 1) * lq) for q in range(nsq)]
        hs = [[h_io[q][0][g * hpg:(g + 1) * hpg].reshape(rows_g, NS) for g in range(NG)] for q in range(nsq)]
        y_off = [jnp.concatenate([lax.dot_general(cms[g][sq[q]], hs[q][g], (((1,), (1,)), ((), ())),
                                                  preferred_element_type=F32) for q in range(nsq)], axis=0)
                 * ea_x[:, gss[g]] for g in range(NG)]
        xw = [xs[:, gss[g]] * wend_x[:, gss[g]] for g in range(NG)]
        upd = [[lax.dot_general(xw[g][sq[q]], bms[g][sq[q]], (((0,), (0,)), ((), ())),
                                preferred_element_type=F32) for g in range(NG)] for q in range(nsq)]
        seq_decay = jnp.exp(a_tot)
        for q in range(nsq):
            for h in range(NH):
                g, hl = divmod(h, hpg)
                rows = slice(hl * HP, (hl + 1) * HP)
                h_io[q][1][h] = hs[q][g][rows, :] * seq_decay[q * lq:q * lq + 1, h:h + 1] + upd[q][g][rows, :]
    segs = [a_cum[:, h:h + 1] - acum_t[h:h + 1, :] for h in range(NH)]
    decs = [jnp.exp(jnp.where(causal, segs[h], -jnp.inf)) for h in range(NH)]
    ws = [cbms[h // hpg] * decs[h] * dt_t[h:h + 1, :] for h in range(NH)]
    y_diag = []
    for pr in range(NH // 2):
        xp = xs[:, 2 * pr * HP:(2 * pr + 2) * HP]
        rhs = jnp.concatenate([jnp.where(lane_x < HP, xp, 0.0), jnp.where(lane_x >= HP, xp, 0.0)], axis=0)
        y_diag.append(jnp.dot(jnp.concatenate([ws[2 * pr], ws[2 * pr + 1]], axis=1), rhs,
                              preferred_element_type=F32))
    y = jnp.concatenate(y_diag, axis=1) + jnp.concatenate(y_off, axis=1) + dsk_ref[...] * xs
    y = y * _silu(z)
    outs = []
    for g in range(NG):
        gs = slice(g * (DS // NG), (g + 1) * (DS // NG))
        outs.append(_rms(y[:, gs], ng_ref[:, gs]))
    return jnp.concatenate(outs, axis=1)


def _state_out(st_ref, h_ref):
    for j in range(DS // LANE):
        blk = st_ref[:, j * LANE:(j + 1) * LANE].T
        for q in range(LANE // HP):
            h_ref[j * (LANE // HP) + q] = blk[q * HP:(q + 1) * HP, :]


def _ssd_prompt_kernel(xbc_ref, z_ref, dt_ref, cw_ref, cb_ref, dtb_ref, alog_ref, dsk_ref, ng_ref,
                       y_ref, nb_ref, h_ref, xe_ref, st_ref, *, L):
    c = pl.program_id(1)

    @pl.when(c == 0)
    def _():
        xe_ref[:, 0:HALO, :] = jnp.zeros((DX // LANE, HALO, LANE), F32)
        st_ref[...] = jnp.zeros_like(st_ref)

    _to_slabs(xe_ref, HALO, xbc_ref[...])
    y_ref[...] = _ssd_chunk(xe_ref, z_ref[...], dt_ref[...], st_ref, cw_ref, cb_ref, dtb_ref, alog_ref, dsk_ref,
                            ng_ref, L=L, tv=L).astype(y_ref.dtype)
    xe_ref[:, 0:HALO, :] = xe_ref[:, L:L + HALO, :]

    @pl.when(c == pl.num_programs(1) - 1)
    def _():
        nb_ref[...] = _from_slabs(xe_ref, HALO - (KS - 1), KS - 1)
        _state_out(st_ref, h_ref)


def _ssd_params_specs():
    return [_resident((KS, DX)), _resident((1, DX)), _resident((1, LANE)), _resident((1, LANE)),
            _resident((1, DS)), _resident((1, DS))]


def _ssd_prompt(xbc, z, dt, params, *, nb, seq):
    nt = nb * seq
    L = CHUNK
    nper = seq // L
    row = lambda n: pl.BlockSpec((L, n), lambda s, c: (s * nper + c, 0))
    return pl.pallas_call(
        functools.partial(_ssd_prompt_kernel, L=L),
        grid=(nb, nper),
        in_specs=[row(DX), row(DS), row(LANE)] + _ssd_params_specs(),
        out_specs=[row(DS),
                   pl.BlockSpec((None, KS - 1, DX), lambda s, c: (s, 0, 0)),
                   pl.BlockSpec((None, NH, HP, NS), lambda s, c: (s, 0, 0, 0))],
        out_shape=[jax.ShapeDtypeStruct((nt, DS), BF16), jax.ShapeDtypeStruct((nb, KS - 1, DX), F32),
                   jax.ShapeDtypeStruct((nb, NH, HP, NS), F32)],
        scratch_shapes=[pltpu.VMEM((DX // LANE, HALO + L, LANE), F32), pltpu.VMEM((NS, DS), F32)],
        compiler_params=_cparams(("parallel", "arbitrary")),
        name="ssd_prompt",
    )(xbc, z, dt, *params)


def _ssd_sample_kernel(xbc_ref, z_ref, dt_ref, hist_ref, h0_ref, cw_ref, cb_ref, dtb_ref, alog_ref, dsk_ref,
                       ng_ref, y_ref, nb_ref, h_ref, xe_ref, zb_ref, dtp_ref, *, lq, tv, nseq):
    assert lq - tv >= KS - 1 and HALO >= KS - 1
    xe_ref[...] = jnp.zeros_like(xe_ref)
    zb_ref[...] = jnp.zeros_like(zb_ref)
    dtp_ref[...] = jnp.zeros_like(dtp_ref)
    for j in range(nseq):
        rows = slice(j * tv, (j + 1) * tv)
        _to_slabs(xe_ref, HALO + j * lq - (KS - 1), hist_ref[j])
        _to_slabs(xe_ref, HALO + j * lq, xbc_ref[rows, :])
        zb_ref[j * lq:j * lq + tv, :] = z_ref[rows, :]
        dtp_ref[j * lq:j * lq + tv, :] = dt_ref[rows, :]
    y = _ssd_chunk(xe_ref, zb_ref[...], dtp_ref[...], None, cw_ref, cb_ref, dtb_ref, alog_ref, dsk_ref, ng_ref,
                   L=nseq * lq, tv=tv, lq=lq, h_io=[(h0_ref.at[j], h_ref.at[j]) for j in range(nseq)])
    y_ref[...] = jnp.concatenate([y[j * lq:j * lq + tv, :] for j in range(nseq)], axis=0).astype(y_ref.dtype)
    for j in range(nseq):
        nb_ref[j] = _from_slabs(xe_ref, HALO + j * lq + tv - (KS - 1), KS - 1)


def _ssd_sample(xbc, z, dt, hist, h0, params, *, row0, tv, nseq=8, lq=SUB):
    nb = hist.shape[0]
    rows = nseq * tv
    L = nseq * lq
    blk0 = row0 // rows
    row = lambda n: pl.BlockSpec((rows, n), lambda i: (blk0 + i, 0))
    return pl.pallas_call(
        functools.partial(_ssd_sample_kernel, lq=lq, tv=tv, nseq=nseq),
        grid=(nb // nseq,),
        in_specs=[row(DX), row(DS), row(LANE),
                  pl.BlockSpec((nseq, KS - 1, DX), lambda i: (i, 0, 0)),
                  pl.BlockSpec((nseq, NH, HP, NS), lambda i: (i, 0, 0, 0))] + _ssd_params_specs(),
        out_specs=[pl.BlockSpec((rows, DS), lambda i: (i, 0)),
                   pl.BlockSpec((nseq, KS - 1, DX), lambda i: (i, 0, 0)),
                   pl.BlockSpec((nseq, NH, HP, NS), lambda i: (i, 0, 0, 0))],
        out_shape=[jax.ShapeDtypeStruct((nb * tv, DS), BF16), jax.ShapeDtypeStruct((nb, KS - 1, DX), F32),
                   jax.ShapeDtypeStruct((nb, NH, HP, NS), F32)],
        scratch_shapes=[pltpu.VMEM((DX // LANE, HALO + L, LANE), F32), pltpu.VMEM((L, DS), F32),
                        pltpu.VMEM((L, LANE), F32)],
        compiler_params=_cparams(("arbitrary",)),
        name="ssd_sample",
    )(xbc, z, dt, hist, h0, *params)


def _proj_out_kernel(*refs, n, n_main, side_plan):
    lhs = refs[0:2 * n]
    ws = refs[2 * n:3 * n]
    x_ref, g_ref = refs[3 * n:3 * n + 2]
    ns = len(side_plan)
    side_in = refs[3 * n + 2:3 * n + 2 + ns]
    o_ref = refs[3 * n + 2 + ns]
    side_out = refs[3 * n + 3 + ns:]
    _side_cast(pl.program_id(0), side_plan, side_in, side_out)
    is_main = pl.program_id(0) < n_main

    def tile(which):
        m = None
        for k, w_ref in enumerate(ws):
            a = lhs[2 * k + which][...].astype(BF16)
            p = jnp.dot(a, w_ref[...].astype(BF16), preferred_element_type=F32)
            m = p if m is None else m + p
        o_ref[...] = x_ref[...] + _rms(m, g_ref[...])

    @pl.when(is_main)
    def _():
        tile(0)

    @pl.when(jnp.logical_not(is_main))
    def _():
        tile(1)


def _proj_out(lhs_pairs, ws, x, g, *, tm=512, side=()):
    nt = x.shape[0]
    side_plan, side_steps = _side_plan(side)
    assert side_steps <= nt // tm
    n = len(lhs_pairs)
    n_main = nt // tm - 1
    lhs_specs = []
    for a_main, a_extra in lhs_pairs:
        assert a_main.shape[0] == n_main * tm and a_extra.shape[0] == tm
        lhs_specs.append(pl.BlockSpec((tm, a_main.shape[1]), lambda i: (jnp.minimum(i, n_main - 1), 0)))
        lhs_specs.append(pl.BlockSpec((tm, a_extra.shape[1]), lambda i: (0, 0)))
    res = pl.pallas_call(
        functools.partial(_proj_out_kernel, n=n, n_main=n_main, side_plan=side_plan),
        grid=(nt // tm,),
        in_specs=lhs_specs + [_resident(w.shape) for w in ws]
                 + [pl.BlockSpec((tm, D), lambda i: (i, 0)), _resident((1, D))] + _side_specs(side, lambda i: i),
        out_specs=[pl.BlockSpec((tm, D), lambda i: (i, 0))] + _side_specs(side, lambda i: i),
        out_shape=[jax.ShapeDtypeStruct((nt, D), F32)] + _side_shapes(side),
        compiler_params=_cparams(("arbitrary",)),
        name="proj_out",
    )(*[a for pair in lhs_pairs for a in pair], *ws, x, g, *[w for w, _, _ in side])
    return tuple(res) if side else res[0]


def _norm_proj_kernel(*refs, n, heads, side_plan):
    x_ref, g_ref = refs[0:2]
    ws = refs[2:2 + n]
    ns = len(side_plan)
    side_in = refs[2 + n:2 + n + ns]
    outs = refs[2 + n + ns:2 + 2 * n + ns]
    side_out = refs[2 + 2 * n + ns:]
    _side_cast(pl.program_id(0), side_plan, side_in, side_out)
    hn = _rms(x_ref[...], g_ref[...]).astype(BF16)
    for w_ref, o_ref in zip(ws, outs):
        for h in range(XH):
            sl = slice(h * XD, (h + 1) * XD)
            r = jnp.dot(hn, w_ref[:, sl].astype(BF16), preferred_element_type=F32).astype(o_ref.dtype)
            if heads:
                o_ref[:, h, :] = r
            else:
                o_ref[:, sl] = r


def _norm_proj(x, g, ws, out_dtype, *, heads=False, tm=512, side=()):
    nt = x.shape[0]
    n = len(ws)
    side_plan, side_steps = _side_plan(side)
    if heads:
        tm = NM
        out_specs = [pl.BlockSpec((None, NM, XH, XD), lambda i: (i, 0, 0, 0))] * n
        out_shape = [jax.ShapeDtypeStruct((nt // NM, NM, XH, XD), out_dtype)] * n
    else:
        out_specs = [pl.BlockSpec((tm, D), lambda i: (i, 0))] * n
        out_shape = [jax.ShapeDtypeStruct((nt, D), out_dtype)] * n
    assert side_steps <= nt // tm
    return pl.pallas_call(
        functools.partial(_norm_proj_kernel, n=n, heads=heads, side_plan=side_plan),
        grid=(nt // tm,),
        in_specs=[pl.BlockSpec((tm, D), lambda i: (i, 0)), _resident((1, D))] + [_resident((D, D))] * n
                 + _side_specs(side, lambda i: i),
        out_specs=out_specs + _side_specs(side, lambda i: i),
        out_shape=out_shape + _side_shapes(side),
        compiler_params=_cparams(("arbitrary",)),
        name="norm_proj",
    )(x, g, *ws, *[w for w, _, _ in side])


NLT = XD // LANE
LT_STRIDE = NLT * XH
KV_ROWS = NM * LT_STRIDE


def _kv_tiles(x):
    nb = x.shape[0]
    return x.reshape(nb, NM, XH, NLT, LANE).transpose(0, 1, 3, 2, 4).reshape(nb, KV_ROWS, LANE)


def _attn_prompt_kernel(q_ref, k_ref, v_ref, o_ref, kh_ref, vh_ref):
    @pl.when(pl.program_id(1) == 0)
    def _():
        for h in range(XH):
            for lt in range(NLT):
                rows = pl.ds(lt * XH + h, NM, stride=LT_STRIDE)
                kh_ref[h, :, lt * LANE:(lt + 1) * LANE] = k_ref[0, rows, :].astype(BF16)
                vh_ref[h, :, lt * LANE:(lt + 1) * LANE] = v_ref[0, rows, :].astype(BF16)

    scale = XD ** -0.5
    nt_dims = (((1,), (1,)), ((), ()))
    heads = range(XH)
    s = [lax.dot_general(q_ref[:, h * XD:(h + 1) * XD], kh_ref[h], nt_dims, preferred_element_type=F32) * scale
         for h in heads]
    e = [jnp.exp(s[h] - jnp.max(s[h], axis=-1, keepdims=True)) for h in heads]
    p = [(e[h] / jnp.sum(e[h], axis=-1, keepdims=True)).astype(BF16) for h in heads]
    for h in heads:
        o_ref[:, h * XD:(h + 1) * XD] = jnp.dot(p[h], vh_ref[h], preferred_element_type=F32).astype(o_ref.dtype)


def _group_sum(x, col_lt):
    n = x.shape[1]
    a = x + jnp.where(col_lt % 2 == 0, pltpu.roll(x, n - 1, axis=1), pltpu.roll(x, 1, axis=1))
    return a + jnp.where(col_lt < 2, pltpu.roll(a, n - 2, axis=1), pltpu.roll(a, 2, axis=1))


def _attn_sample_kernel(q_ref, k_ref, v_ref, o_ref, *, nseq, tv):
    assert NLT == 4
    tq = q_ref.shape[0]
    ncol = NM * NLT
    scale = XD ** -0.5
    nt_dims = (((1,), (1,)), ((), ()))
    col_lt = lax.broadcasted_iota(jnp.int32, (tq, ncol), 1) % NLT
    rowi = lax.broadcasted_iota(jnp.int32, (tq, LANE), 0)
    for h in range(XH):
        qp = jnp.concatenate([q_ref[:, h * XD + lt * LANE:h * XD + (lt + 1) * LANE].astype(F32)
                              for lt in range(NLT)], axis=0)
        seqs = range(nseq)
        g = [lax.dot_general(qp, k_ref[j, pl.ds(h, ncol, stride=XH), :], nt_dims, preferred_element_type=F32)
             for j in seqs]
        s4 = [sum(jnp.where(col_lt == lt, g[j][lt * tq:(lt + 1) * tq], 0.0) for lt in range(NLT)) for j in seqs]
        s = [_group_sum(s4[j], col_lt) * scale for j in seqs]
        e = [jnp.exp(s[j] - jnp.max(s[j], axis=-1, keepdims=True)) for j in seqs]
        p = [e[j] / (jnp.sum(e[j], axis=-1, keepdims=True) * (1.0 / NLT)) for j in seqs]
        o = [jnp.dot(jnp.concatenate([jnp.where(col_lt == lt, p[j], 0.0) for lt in range(NLT)], axis=0),
                     v_ref[j, pl.ds(h, ncol, stride=XH), :], preferred_element_type=F32) for j in seqs]
        acc = [None] * NLT
        for j in seqs:
            mine = (rowi >= j * tv) & (rowi < (j + 1) * tv)
            for lt in range(NLT):
                acc[lt] = jnp.where(mine, o[j][lt * tq:(lt + 1) * tq], 0.0 if acc[lt] is None else acc[lt])
        for lt in range(NLT):
            o_ref[:, h * XD + lt * LANE:h * XD + (lt + 1) * LANE] = acc[lt].astype(o_ref.dtype)


def _attn_prompt(q, k, v, *, nb, seq, tq=1024):
    nt = nb * seq
    nper = seq // tq
    kv_spec = pl.BlockSpec((1, KV_ROWS, LANE), lambda s, t: (s, 0, 0))
    return pl.pallas_call(
        _attn_prompt_kernel,
        grid=(nb, nper),
        in_specs=[pl.BlockSpec((tq, D), lambda s, t: (s * nper + t, 0)), kv_spec, kv_spec],
        out_specs=pl.BlockSpec((tq, D), lambda s, t: (s * nper + t, 0)),
        out_shape=jax.ShapeDtypeStruct((nt, D), BF16),
        scratch_shapes=[pltpu.VMEM((XH, NM, XD), BF16), pltpu.VMEM((XH, NM, XD), BF16)],
        compiler_params=_cparams(("arbitrary", "arbitrary")),
        name="attn_prompt",
    )(q, _kv_tiles(k), _kv_tiles(v))


KV_SLOTS = 3


def _attn_sample_ring_kernel(q_ref, k_hbm, v_hbm, o_ref, kbuf, vbuf, ksem, vsem, *, nseq, tv):
    s = pl.program_id(0)
    n = pl.num_programs(0)

    def copies(step):
        slot = step % KV_SLOTS
        src = pl.ds(step * nseq, nseq)
        return (pltpu.make_async_copy(k_hbm.at[src], kbuf.at[slot], ksem.at[slot]),
                pltpu.make_async_copy(v_hbm.at[src], vbuf.at[slot], vsem.at[slot]))

    def start(step):
        for c in copies(step):
            c.start()

    @pl.when(s == 0)
    def _():
        for step in range(KV_SLOTS - 1):
            start(step)

    @pl.when(s + (KV_SLOTS - 1) < n)
    def _():
        start(s + (KV_SLOTS - 1))

    for c in copies(s):
        c.wait()
    slot = s % KV_SLOTS
    _attn_sample_kernel(q_ref, kbuf.at[slot], vbuf.at[slot], o_ref, nseq=nseq, tv=tv)


def _attn_sample(q, k, v, *, row0, tv, nseq=4):
    nb = k.shape[0]
    rows = nseq * tv
    blk0 = row0 // rows
    assert nb // nseq >= KV_SLOTS - 1
    ring = pltpu.VMEM((KV_SLOTS, nseq, KV_ROWS, LANE), F32)
    return pl.pallas_call(
        functools.partial(_attn_sample_ring_kernel, nseq=nseq, tv=tv),
        grid=(nb // nseq,),
        in_specs=[pl.BlockSpec((rows, D), lambda i: (blk0 + i, 0)), pl.BlockSpec(memory_space=pl.ANY),
                  pl.BlockSpec(memory_space=pl.ANY)],
        out_specs=pl.BlockSpec((rows, D), lambda i: (i, 0)),
        out_shape=jax.ShapeDtypeStruct((nb * tv, D), BF16),
        scratch_shapes=[ring, ring, pltpu.SemaphoreType.DMA((KV_SLOTS,)), pltpu.SemaphoreType.DMA((KV_SLOTS,))],
        compiler_params=_cparams(("arbitrary",)),
        name="attn_sample",
    )(q, _kv_tiles(k), _kv_tiles(v))


def _row(v):
    return v.reshape(1, -1).astype(F32)


def _pad_lanes(v, n):
    return jnp.pad(v.reshape(1, -1).astype(F32), ((0, 0), (0, n - v.size)))


def _layer(xs, mem, cache_k, cache_v, st_conv, st_sconv, st_ssm, p, *, nbp, seq, nbs, tv, split_out):
    npr = nbp * seq
    bf = lambda w: w.astype(BF16)

    side = [(p['w_in'].T, 0, LANE)]
    x, w_in = _ffn(xs, _row(p['ffn1_pre_g']), bf(p['ffn1_w_gate']), bf(p['ffn1_w_up']), bf(p['ffn1_w_down']),
                   _row(p['ffn1_post_g']), split_out=False, side=side)
    w_out = bf(p['w_out'])

    wt = jnp.pad(w_in[2 * DC + DS + DX:], ((0, LANE - NH), (0, 0)))
    u, z, xbc, dtr = _mix_in(x, _row(p['mix_pre_g']), w_in, wt)

    cpar = (p['conv_w'].astype(F32), _row(p['conv_b']), _row(p['conv_ln_g']), _row(p['conv_ln_b']))
    a_p, conv_p = _conv_prompt(u, *cpar, nb=nbp, seq=seq)
    a_s, conv_s = _conv_sample(u, jnp.transpose(st_conv, (1, 0, 2)), *cpar, row0=npr, tv=tv)
    conv_s = jnp.transpose(conv_s, (1, 0, 2))

    spar = (p['ssm_conv_w'].astype(F32), _row(p['ssm_conv_b']), _pad_lanes(p['dt_bias'], LANE),
            _pad_lanes(p['a_log'], LANE), jnp.repeat(p['d_skip'].astype(F32), HP).reshape(1, DS),
            _row(p['ssm_norm_g']))
    y_p, sconv_p, ssm_p = _ssd_prompt(xbc, z, dtr, spar, nb=nbp, seq=seq)
    y_s, sconv_s, ssm_s = _ssd_sample(xbc, z, dtr, st_sconv, st_ssm, spar, row0=npr, tv=tv)

    x, wg2 = _proj_out([(a_p, a_s), (y_p, y_s)], [w_out[:DC], w_out[DC:]], x, _row(p['mix_post_g']),
                       side=[(p['ffn2_w_gate'], 1, SIDE_WIDE)])

    q, wu2 = _norm_proj(x, _row(p['xattn_pre_g']), [p['w_xq']], BF16, side=[(p['ffn2_w_up'], 1, SIDE_WIDE)])
    mk, mv = _norm_proj(mem, _row(p['mem_norm_g']), [p['w_xk'], p['w_xv']], F32, heads=True)
    o_p = _attn_prompt(q, mk, mv, nb=nbp, seq=seq)
    o_s = _attn_sample(q, cache_k, cache_v, row0=npr, tv=tv)
    x, wd2 = _proj_out([(o_p, o_s)], [p['w_xo']], x, _row(p['xattn_post_g']),
                       side=[(p['ffn2_w_down'], 0, SIDE_WIDE)])

    x = _ffn([x], _row(p['ffn2_pre_g']), wg2, wu2, wd2, _row(p['ffn2_post_g']), split_out=split_out)
    return x, (mk, mv, conv_p, sconv_p, ssm_p, conv_s, sconv_s, ssm_s)


def kernel(x_prompt, x_sample, mem_prompt, cache_mem_k, cache_mem_v, state_conv, state_ssm_conv, state_ssm, ffn1_pre_g, ffn1_w_gate, ffn1_w_up, ffn1_w_down, ffn1_post_g, mix_pre_g, w_in, conv_w, conv_b, conv_ln_g, conv_ln_b, ssm_conv_w, ssm_conv_b, dt_bias, a_log, d_skip, ssm_norm_g, w_out, mix_post_g, xattn_pre_g, mem_norm_g, w_xq, w_xk, w_xv, w_xo, xattn_post_g, ffn2_pre_g, ffn2_w_gate, ffn2_w_up, ffn2_w_down, ffn2_post_g):
    params = dict(ffn1_pre_g=ffn1_pre_g, ffn1_w_gate=ffn1_w_gate, ffn1_w_up=ffn1_w_up, ffn1_w_down=ffn1_w_down,
                  ffn1_post_g=ffn1_post_g, mix_pre_g=mix_pre_g, w_in=w_in, conv_w=conv_w, conv_b=conv_b,
                  conv_ln_g=conv_ln_g, conv_ln_b=conv_ln_b, ssm_conv_w=ssm_conv_w, ssm_conv_b=ssm_conv_b,
                  dt_bias=dt_bias, a_log=a_log, d_skip=d_skip, ssm_norm_g=ssm_norm_g, w_out=w_out,
                  mix_post_g=mix_post_g, xattn_pre_g=xattn_pre_g, mem_norm_g=mem_norm_g, w_xq=w_xq,
                  w_xk=w_xk, w_xv=w_xv, w_xo=w_xo, xattn_post_g=xattn_post_g, ffn2_pre_g=ffn2_pre_g,
                  ffn2_w_gate=ffn2_w_gate, ffn2_w_up=ffn2_w_up, ffn2_w_down=ffn2_w_down, ffn2_post_g=ffn2_post_g)
    depth = ffn1_pre_g.shape[0]
    nbp, seq, _ = x_prompt.shape
    nbs, tv, _ = x_sample.shape
    npr = nbp * seq
    xs = [x_prompt.reshape(npr, D), x_sample.reshape(nbs * tv, D)]
    mem = mem_prompt.reshape(nbp * NM, D)
    per_layer = []
    for layer in range(depth):
        p = {name: w[layer] for name, w in params.items()}
        x, states = _layer(xs, mem, cache_mem_k[layer], cache_mem_v[layer],
                           state_conv[layer], state_ssm_conv[layer], state_ssm[layer], p,
                           nbp=nbp, seq=seq, nbs=nbs, tv=tv, split_out=layer == depth - 1)
        xs = [x]
        per_layer.append(states)
    mk, mv, conv_p, sconv_p, ssm_p, conv_s, sconv_s, ssm_s = [jnp.stack(t) for t in zip(*per_layer)]
    yp, ys = x
    return (yp.reshape(nbp, seq, D), ys.reshape(nbs, tv, D), mk, mv, conv_p, sconv_p, ssm_p, conv_s, sconv_s, ssm_s)
```

```python
import functools

import jax
import jax.numpy as jnp
from jax import lax
from jax.experimental import pallas as pl
from jax.experimental.pallas import tpu as pltpu

F32 = jnp.float32
BF16 = jnp.bfloat16

D = 2048
FF = 5504
DC = 1024
DS = 1024
KC = 31
NH = 16
HP = 64
NG = 2
NS = 128
KS = 4
DX = DS + 2 * NG * NS
CHUNK = 128
NM = 256
XH = 4
XD = D // XH
EPS = 1e-6

LANE = 128
SUB = 8
HIST = 32
HALO = 8

FF_TILE = 1024
SIDE_WIDE = 3 * LANE

VMEM_LIMIT = 60 * 1024 * 1024


def _cparams(sem):
    return pltpu.CompilerParams(dimension_semantics=sem, vmem_limit_bytes=VMEM_LIMIT)


def _rms(x, g):
    return x * lax.rsqrt(jnp.mean(x * x, axis=-1, keepdims=True) + EPS) * g


def _silu(x):
    return x * jax.nn.sigmoid(x)


def _resident(shape):
    return pl.BlockSpec(shape, lambda *_: (0,) * len(shape), pipeline_mode=pl.Buffered(1))


def _ffn_kernel(*refs, n_in, n_out, n_main, tf, side_plan):
    n_side = len(side_plan)
    x_refs = refs[:n_in]
    pg_ref, wg_ref, wu_ref, wd_ref, qg_ref = refs[n_in:n_in + 5]
    side_in = refs[n_in + 5:n_in + 5 + n_side]
    k = n_in + 5 + n_side
    o_refs = refs[k:k + n_out]
    side_out = refs[k + n_out:k + n_out + n_side]
    xn_ref, acc_ref = refs[k + n_out + n_side:]
    i = pl.program_id(0)
    f = pl.program_id(1)
    last = pl.num_programs(1) - 1

    def per_row_group(fn):
        if n_in == 1 and n_out == 1:
            fn(x_refs[0], o_refs[0])
        else:
            @pl.when(i < n_main)
            def _():
                fn(x_refs[0], o_refs[0])

            @pl.when(i >= n_main)
            def _():
                fn(x_refs[-1], o_refs[-1])

    @pl.when(f == 0)
    def _():
        def pre_norm(x_ref, _):
            xn_ref[...] = _rms(x_ref[...], pg_ref[...]).astype(BF16)

        per_row_group(pre_norm)
        acc_ref[...] = jnp.zeros_like(acc_ref)

    def hidden_tile(width):
        xn = xn_ref[...]
        h = jnp.dot(xn, wg_ref[:, 0:width], preferred_element_type=F32)
        u = jnp.dot(xn, wu_ref[:, 0:width], preferred_element_type=F32)
        a = (_silu(h) * u).astype(BF16)
        acc_ref[...] += jnp.dot(a, wd_ref[0:width, :], preferred_element_type=F32)

    @pl.when(f < last)
    def _():
        hidden_tile(tf)

    @pl.when(f == last)
    def _():
        hidden_tile(FF - (FF // tf) * tf)
        r = 0.5 * _rms(acc_ref[...], qg_ref[...])

        def residual(x_ref, o_ref):
            o_ref[...] = x_ref[...] + r

        per_row_group(residual)

    _side_cast(i * pl.num_programs(1) + f, side_plan, side_in, side_out)


def _side_plan(side):
    plan, start = [], 0
    for w, axis, width in side:
        n = pl.cdiv(w.shape[axis], width)
        plan.append((start, n))
        start += n
    return tuple(plan), start


def _side_specs(side, step_of):
    specs = []
    for (w, axis, width), (start, n) in zip(side, _side_plan(side)[0]):
        strip = lambda *idx, start=start, n=n: jnp.clip(step_of(*idx) - start, 0, n - 1)
        if axis == 0:
            specs.append(pl.BlockSpec((width, w.shape[1]), lambda *idx, strip=strip: (strip(*idx), 0)))
        else:
            specs.append(pl.BlockSpec((w.shape[0], width), lambda *idx, strip=strip: (0, strip(*idx))))
    return specs


def _side_cast(step, side_plan, side_in, side_out):
    for m, (start, n) in enumerate(side_plan):
        @pl.when((step >= start) & (step < start + n))
        def _(m=m):
            side_out[m][...] = side_in[m][...].astype(BF16)


def _side_shapes(side):
    return [jax.ShapeDtypeStruct(w.shape, BF16) for w, _, _ in side]


def _ffn(xs, pre_g, wg, wu, wd, post_g, *, split_out, tm=512, tf=FF_TILE, side=()):
    n_in = len(xs)
    nt = sum(x.shape[0] for x in xs)
    n_main = (nt - tm) // tm
    nf = pl.cdiv(FF, tf)
    side_plan, side_steps = _side_plan(side)
    assert side_steps <= (nt // tm) * nf
    main = lambda i, f: (jnp.minimum(i, n_main - 1), 0)
    extra = lambda i, f: (0, 0)
    whole = lambda i, f: (i, 0)
    if n_in == 1:
        x_specs = [pl.BlockSpec((tm, D), whole)]
    else:
        x_specs = [pl.BlockSpec((tm, D), main), pl.BlockSpec((tm, D), extra)]
    if split_out:
        out_specs = [pl.BlockSpec((tm, D), main), pl.BlockSpec((tm, D), extra)]
        out_shape = [jax.ShapeDtypeStruct((n_main * tm, D), F32), jax.ShapeDtypeStruct((tm, D), F32)]
    else:
        out_specs = [pl.BlockSpec((tm, D), whole)]
        out_shape = [jax.ShapeDtypeStruct((nt, D), F32)]
    res = pl.pallas_call(
        functools.partial(_ffn_kernel, n_in=n_in, n_out=len(out_specs), n_main=n_main, tf=tf, side_plan=side_plan),
        grid=(nt // tm, nf),
        in_specs=x_specs + [
            pl.BlockSpec((1, D), lambda i, f: (0, 0)),
            pl.BlockSpec((D, tf), lambda i, f: (0, f)),
            pl.BlockSpec((D, tf), lambda i, f: (0, f)),
            pl.BlockSpec((tf, D), lambda i, f: (f, 0)),
            pl.BlockSpec((1, D), lambda i, f: (0, 0)),
        ] + _side_specs(side, lambda i, f: i * nf + f),
        out_specs=out_specs + _side_specs(side, lambda i, f: i * nf + f),
        out_shape=out_shape + _side_shapes(side),
        scratch_shapes=[pltpu.VMEM((tm, D), BF16), pltpu.VMEM((tm, D), F32)],
        compiler_params=_cparams(("arbitrary", "arbitrary")),
        name="ffn",
    )(*xs, pre_g, wg, wu, wd, post_g, *[w for w, _, _ in side])
    main_res = res[0] if len(out_specs) == 1 else tuple(res[:2])
    return (main_res, *res[len(out_specs):]) if side else main_res


def _mix_in_kernel(x_ref, g_ref, w_ref, wt_ref, u_ref, z_ref, xbc_ref, dt_ref, *, tn):
    hn = _rms(x_ref[...], g_ref[...]).astype(BF16)
    nt_dims = (((1,), (1,)), ((), ()))

    def cols(start, c):
        return lax.dot_general(hn, w_ref[start + c * tn:start + (c + 1) * tn, :], nt_dims,
                               preferred_element_type=F32)

    for c in range(DC // tn):
        u_ref[:, c * tn:(c + 1) * tn] = cols(0, c) * jax.nn.sigmoid(cols(DC, c))
    for c in range(DS // tn):
        z_ref[:, c * tn:(c + 1) * tn] = cols(2 * DC, c)
    for c in range(DX // tn):
        xbc_ref[:, c * tn:(c + 1) * tn] = cols(2 * DC + DS, c)
    dt_ref[...] = lax.dot_general(hn, wt_ref[...], nt_dims, preferred_element_type=F32)


def _mix_in(x, g, w, wt, *, tm=512, tn=512):
    nt = x.shape[0]
    row = lambda n: pl.BlockSpec((tm, n), lambda i: (i, 0))
    return pl.pallas_call(
        functools.partial(_mix_in_kernel, tn=tn),
        grid=(nt // tm,),
        in_specs=[row(D), _resident((1, D)), _resident(w.shape), _resident((LANE, D))],
        out_specs=[row(DC), row(DS), row(DX), row(LANE)],
        out_shape=[jax.ShapeDtypeStruct((nt, n), F32) for n in (DC, DS, DX, LANE)],
        compiler_params=_cparams(("parallel",)),
        name="mix_in",
    )(x, g, w, wt)


def _conv_prompt_kernel(u_ref, w_ref, b_ref, lg_ref, lb_ref, a_ref, nb_ref, xe_ref, y_ref, *, tt):
    t = pl.program_id(1)
    off = HIST - (KC - 1)

    @pl.when(t == 0)
    def _():
        xe_ref[:, 0:HIST, :] = jnp.zeros((DC // LANE, HIST, LANE), F32)

    for j in range(DC // LANE):
        xe_ref[j, HIST:HIST + tt, :] = u_ref[:, j * LANE:(j + 1) * LANE]
    for j in range(DC // LANE):
        sl = slice(j * LANE, (j + 1) * LANE)
        acc = jnp.broadcast_to(b_ref[:, sl], (tt, LANE))
        for k in range(KC):
            acc = acc + w_ref[k:k + 1, sl] * xe_ref[j, off + k:off + k + tt, :]
        y_ref[:, sl] = acc
    for j in range(DC // LANE):
        nb_ref[:, j * LANE:(j + 1) * LANE] = xe_ref[j, HIST + tt - (KC - 1):HIST + tt, :]
        xe_ref[j, 0:HIST, :] = xe_ref[j, tt:tt + HIST, :]
    @pl.when(t >= 0)
    def _():
        y = y_ref[...]
        yc = y - jnp.mean(y, axis=-1, keepdims=True)
        a_ref[...] = _silu(yc * lax.rsqrt(jnp.mean(yc * yc, axis=-1, keepdims=True) + EPS) * lg_ref[...]
                           + lb_ref[...]).astype(a_ref.dtype)


def _conv_prompt(u, w, b, lg, lb, *, nb, seq, tt=256):
    nt = nb * seq
    nper = seq // tt
    par = lambda r: _resident((r, DC))
    return pl.pallas_call(
        functools.partial(_conv_prompt_kernel, tt=tt),
        grid=(nb, nper),
        in_specs=[pl.BlockSpec((tt, DC), lambda s, t: (s * nper + t, 0)), par(KC), par(1), par(1), par(1)],
        out_specs=[pl.BlockSpec((tt, DC), lambda s, t: (s * nper + t, 0)),
                   pl.BlockSpec((None, KC - 1, DC), lambda s, t: (s, 0, 0))],
        out_shape=[jax.ShapeDtypeStruct((nt, DC), BF16), jax.ShapeDtypeStruct((nb, KC - 1, DC), F32)],
        scratch_shapes=[pltpu.VMEM((DC // LANE, HIST + tt, LANE), F32), pltpu.VMEM((tt, DC), F32)],
        compiler_params=_cparams(("parallel", "arbitrary")),
        name="conv_prompt",
    )(u, w, b, lg, lb)


def _conv_sample_kernel(u_ref, hist_ref, w_ref, b_ref, lg_ref, lb_ref, a_ref, nh_ref, us_ref, y_ref, as_ref,
                        *, sb, tv):
    nl = DC // LANE
    for j in range(nl):
        us_ref[j] = u_ref[:, j * LANE:(j + 1) * LANE]
    for j in range(nl):
        sl = slice(j * LANE, (j + 1) * LANE)
        accs = [jnp.broadcast_to(b_ref[:, sl], (sb, LANE)) for _ in range(tv)]
        for m in range(KC - 1 + tv):
            if m < KC - 1:
                xm = hist_ref[m, :, sl]
            else:
                xm = us_ref[j, pl.ds(m - (KC - 1), sb, stride=tv), :]
            for t in range(tv):
                if 0 <= m - t < KC:
                    accs[t] = accs[t] + w_ref[m - t:m - t + 1, sl] * xm
            if m >= tv:
                nh_ref[m - tv, :, sl] = xm
        for t in range(tv):
            y_ref[t, :, sl] = accs[t]
    for t in range(tv):
        y = y_ref[t]
        yc = y - jnp.mean(y, axis=-1, keepdims=True)
        a = _silu(yc * lax.rsqrt(jnp.mean(yc * yc, axis=-1, keepdims=True) + EPS) * lg_ref[...] + lb_ref[...])
        for j in range(nl):
            as_ref[j, pl.ds(t, sb, stride=tv), :] = a[:, j * LANE:(j + 1) * LANE]
    for j in range(nl):
        a_ref[:, j * LANE:(j + 1) * LANE] = as_ref[j].astype(a_ref.dtype)


def _conv_sample(u, hist, w, b, lg, lb, *, row0, tv, sb=32):
    nb = hist.shape[1]
    rows = sb * tv
    par = lambda r: _resident((r, DC))
    blk0 = row0 // rows
    hist_spec = pl.BlockSpec((KC - 1, sb, DC), lambda i: (0, i, 0))
    return pl.pallas_call(
        functools.partial(_conv_sample_kernel, sb=sb, tv=tv),
        grid=(nb // sb,),
        in_specs=[pl.BlockSpec((rows, DC), lambda i: (blk0 + i, 0)), hist_spec, par(KC), par(1), par(1), par(1)],
        out_specs=[pl.BlockSpec((rows, DC), lambda i: (i, 0)), hist_spec],
        out_shape=[jax.ShapeDtypeStruct((nb * tv, DC), BF16), jax.ShapeDtypeStruct((KC - 1, nb, DC), F32)],
        scratch_shapes=[pltpu.VMEM((DC // LANE, rows, LANE), F32), pltpu.VMEM((tv, sb, DC), F32),
                        pltpu.VMEM((DC // LANE, rows, LANE), F32)],
        compiler_params=_cparams(("arbitrary",)),
        name="conv_sample",
    )(u, hist, w, b, lg, lb)


def _dot01(a, b, dims, *, data):
    x = b if data else a
    one = (a if data else b).astype(BF16)
    t0 = x.astype(BF16)
    r1 = x - t0.astype(F32)
    t1 = r1.astype(BF16)
    t2 = (r1 - t1.astype(F32)).astype(BF16)
    acc = None
    for t in (t0, t1, t2):
        lhs, rhs = (one, t) if data else (t, one)
        p = lax.dot_general(lhs, rhs, dims, preferred_element_type=F32)
        acc = p if acc is None else acc + p
    return acc


def _to_slabs(xe_ref, r0, x):
    for j in range(x.shape[1] // LANE):
        xe_ref[j, r0:r0 + x.shape[0], :] = x[:, j * LANE:(j + 1) * LANE]


def _from_slabs(xe_ref, r0, rows):
    return jnp.concatenate([xe_ref[j, r0:r0 + rows, :] for j in range(xe_ref.shape[0])], axis=1)


def _ssd_chunk(xe_ref, z, dt_raw, st_ref, cw_ref, cb_ref, dtb_ref, alog_ref, dsk_ref, ng_ref, *, L, tv, lq=None,
               h_io=None):
    mm_dims = (((1,), (0,)), ((), ()))
    lq = L if lq is None else lq
    nsq = L // lq
    off = HALO - (KS - 1)
    cols = []
    for j in range(DX // LANE):
        sl = slice(j * LANE, (j + 1) * LANE)
        acc = jnp.broadcast_to(cb_ref[:, sl], (L, LANE))
        for k in range(KS):
            acc = acc + cw_ref[k:k + 1, sl] * xe_ref[j, off + k:off + k + L, :]
        cols.append(acc)
    xc = _silu(jnp.concatenate(cols, axis=1))
    xs = xc[:, 0:DS]

    lane = lax.broadcasted_iota(jnp.int32, (L, LANE), 1)
    rowi = lax.broadcasted_iota(jnp.int32, (L, LANE), 0)
    xdt = dt_raw + dtb_ref[...]
    dt = jnp.maximum(xdt, 0.0) + jnp.log1p(jnp.exp(-jnp.abs(xdt)))
    dt = jnp.where((lane < NH) & (rowi % lq < tv), dt, 0.0)
    da = dt * (-jnp.exp(alog_ref[...]))

    r2 = lax.broadcasted_iota(jnp.int32, (L, L), 0)
    c2 = lax.broadcasted_iota(jnp.int32, (L, L), 1)
    same = r2 // lq == c2 // lq
    causal = (r2 >= c2) & same
    if nsq == 1:
        a_cum = _dot01(causal.astype(F32), da, mm_dims, data=1)
        a_tot = jnp.broadcast_to(a_cum[L - 1:L, :], (L, LANE))
    else:
        cums = _dot01(jnp.concatenate([causal.astype(F32), same.astype(F32)], axis=0), da, mm_dims, data=1)
        a_cum = cums[0:L]
        a_tot = cums[L:2 * L]

    er = lax.broadcasted_iota(jnp.int32, (LANE, DS), 0)
    ec = lax.broadcasted_iota(jnp.int32, (LANE, DS), 1)
    expand = (ec // HP == er).astype(F32)
    stack = jnp.concatenate([jnp.exp(a_cum), jnp.exp(a_tot - a_cum) * dt, jnp.exp(a_tot[0:SUB])], axis=0)
    stack_x = _dot01(stack, expand, mm_dims, data=0)
    ea_x = stack_x[0:L]
    wend_x = stack_x[L:2 * L]
    cd_x = stack_x[2 * L:2 * L + 1]

    ir = lax.broadcasted_iota(jnp.int32, (LANE, LANE), 0)
    ic = lax.broadcasted_iota(jnp.int32, (LANE, LANE), 1)
    ident = (ir == ic).astype(F32)
    tr = _dot01(ident, jnp.concatenate([dt, a_cum], axis=0), (((1,), (1,)), ((), ())), data=1)
    dt_t = tr[:, 0:L]
    acum_t = tr[:, L:2 * L]

    lane_x = lax.broadcasted_iota(jnp.int32, (L, LANE), 1)
    hpg = NH // NG
    bms = [xc[:, DS + g * NS:DS + (g + 1) * NS] for g in range(NG)]
    cms = [xc[:, DS + NG * NS + g * NS:DS + NG * NS + (g + 1) * NS] for g in range(NG)]
    gss = [slice(g * (DS // NG), (g + 1) * (DS // NG)) for g in range(NG)]
    cbms = [lax.dot_general(cms[g], bms[g], (((1,), (1,)), ((), ())), preferred_element_type=F32)
            for g in range(NG)]
    if h_io is None:
        st_old = [st_ref[:, gss[g]] for g in range(NG)]
        y_off = [jnp.dot(cms[g], st_old[g], preferred_element_type=F32) * ea_x[:, gss[g]] for g in range(NG)]
        st_new = [jnp.dot(bms[g].T, xs[:, gss[g]] * wend_x[:, gss[g]], preferred_element_type=F32)
                  for g in range(NG)]
        for g in range(NG):
            st_ref[:, gss[g]] = st_old[g] * cd_x[:, gss[g]] + st_new[g]
    else:
        rows_g = hpg * HP
        sq = [slice(q * lq, (q + 1) * lq) for q in range(nsq)]
        hs = [[h_io[q][0][g * hpg:(g + 1) * hpg].reshape(rows_g, NS) for g in range(NG)] for q in range(nsq)]
        y_off = [jnp.concatenate([lax.dot_general(cms[g][sq[q]], hs[q][g], (((1,), (1,)), ((), ())),
                                                  preferred_element_type=F32) for q in range(nsq)], axis=0)
                 * ea_x[:, gss[g]] for g in range(NG)]
        xw = [xs[:, gss[g]] * wend_x[:, gss[g]] for g in range(NG)]
        upd = [[lax.dot_general(xw[g][sq[q]], bms[g][sq[q]], (((0,), (0,)), ((), ())),
                                preferred_element_type=F32) for g in range(NG)] for q in range(nsq)]
        seq_decay = jnp.exp(a_tot)
        for q in range(nsq):
            for h in range(NH):
                g, hl = divmod(h, hpg)
                rows = slice(hl * HP, (hl + 1) * HP)
                h_io[q][1][h] = hs[q][g][rows, :] * seq_decay[q * lq:q * lq + 1, h:h + 1] + upd[q][g][rows, :]
    segs = [a_cum[:, h:h + 1] - acum_t[h:h + 1, :] for h in range(NH)]
    decs = [jnp.exp(jnp.where(causal, segs[h], -jnp.inf)) for h in range(NH)]
    ws = [cbms[h // hpg] * decs[h] * dt_t[h:h + 1, :] for h in range(NH)]
    y_diag = []
    for pr in range(NH // 2):
        xp = xs[:, 2 * pr * HP:(2 * pr + 2) * HP]
        rhs = jnp.concatenate([jnp.where(lane_x < HP, xp, 0.0), jnp.where(lane_x >= HP, xp, 0.0)], axis=0)
        y_diag.append(jnp.dot(jnp.concatenate([ws[2 * pr], ws[2 * pr + 1]], axis=1), rhs,
                              preferred_element_type=F32))
    y = jnp.concatenate(y_diag, axis=1) + jnp.concatenate(y_off, axis=1) + dsk_ref[...] * xs
    y = y * _silu(z)
    outs = []
    for g in range(NG):
        gs = slice(g * (DS // NG), (g + 1) * (DS // NG))
        outs.append(_rms(y[:, gs], ng_ref[:, gs]))
    return jnp.concatenate(outs, axis=1)


def _state_out(st_ref, h_ref):
    for j in range(DS // LANE):
        blk = st_ref[:, j * LANE:(j + 1) * LANE].T
        for q in range(LANE // HP):
            h_ref[j * (LANE // HP) + q] = blk[q * HP:(q + 1) * HP, :]


def _ssd_prompt_kernel(xbc_ref, z_ref, dt_ref, cw_ref, cb_ref, dtb_ref, alog_ref, dsk_ref, ng_ref,
                       y_ref, nb_ref, h_ref, xe_ref, st_ref, *, L):
    c = pl.program_id(1)

    @pl.when(c == 0)
    def _():
        xe_ref[:, 0:HALO, :] = jnp.zeros((DX // LANE, HALO, LANE), F32)
        st_ref[...] = jnp.zeros_like(st_ref)

    _to_slabs(xe_ref, HALO, xbc_ref[...])
    y_ref[...] = _ssd_chunk(xe_ref, z_ref[...], dt_ref[...], st_ref, cw_ref, cb_ref, dtb_ref, alog_ref, dsk_ref,
                            ng_ref, L=L, tv=L).astype(y_ref.dtype)
    xe_ref[:, 0:HALO, :] = xe_ref[:, L:L + HALO, :]

    @pl.when(c == pl.num_programs(1) - 1)
    def _():
        nb_ref[...] = _from_slabs(xe_ref, HALO - (KS - 1), KS - 1)
        _state_out(st_ref, h_ref)


def _ssd_params_specs():
    return [_resident((KS, DX)), _resident((1, DX)), _resident((1, LANE)), _resident((1, LANE)),
            _resident((1, DS)), _resident((1, DS))]


def _ssd_prompt(xbc, z, dt, params, *, nb, seq):
    nt = nb * seq
    L = CHUNK
    nper = seq // L
    row = lambda n: pl.BlockSpec((L, n), lambda s, c: (s * nper + c, 0))
    return pl.pallas_call(
        functools.partial(_ssd_prompt_kernel, L=L),
        grid=(nb, nper),
        in_specs=[row(DX), row(DS), row(LANE)] + _ssd_params_specs(),
        out_specs=[row(DS),
                   pl.BlockSpec((None, KS - 1, DX), lambda s, c: (s, 0, 0)),
                   pl.BlockSpec((None, NH, HP, NS), lambda s, c: (s, 0, 0, 0))],
        out_shape=[jax.ShapeDtypeStruct((nt, DS), BF16), jax.ShapeDtypeStruct((nb, KS - 1, DX), F32),
                   jax.ShapeDtypeStruct((nb, NH, HP, NS), F32)],
        scratch_shapes=[pltpu.VMEM((DX // LANE, HALO + L, LANE), F32), pltpu.VMEM((NS, DS), F32)],
        compiler_params=_cparams(("parallel", "arbitrary")),
        name="ssd_prompt",
    )(xbc, z, dt, *params)


def _ssd_sample_kernel(xbc_ref, z_ref, dt_ref, hist_ref, h0_hbm, cw_ref, cb_ref, dtb_ref, alog_ref, dsk_ref,
                       ng_ref, y_ref, nb_ref, h_ref, xe_ref, zb_ref, dtp_ref, hbuf, hsem, *, lq, tv, nseq):
    assert lq - tv >= KS - 1 and HALO >= KS - 1
    s = pl.program_id(0)

    def state_copy(step):
        slot = step % KV_SLOTS
        return pltpu.make_async_copy(h0_hbm.at[pl.ds(step * nseq, nseq)], hbuf.at[slot], hsem.at[slot])

    @pl.when(s == 0)
    def _():
        for step in range(KV_SLOTS - 1):
            state_copy(step).start()

    @pl.when(s + (KV_SLOTS - 1) < pl.num_programs(0))
    def _():
        state_copy(s + (KV_SLOTS - 1)).start()

    state_copy(s).wait()
    h0_ref = hbuf.at[s % KV_SLOTS]
    xe_ref[...] = jnp.zeros_like(xe_ref)
    zb_ref[...] = jnp.zeros_like(zb_ref)
    dtp_ref[...] = jnp.zeros_like(dtp_ref)
    for j in range(nseq):
        rows = slice(j * tv, (j + 1) * tv)
        _to_slabs(xe_ref, HALO + j * lq - (KS - 1), hist_ref[j])
        _to_slabs(xe_ref, HALO + j * lq, xbc_ref[rows, :])
        zb_ref[j * lq:j * lq + tv, :] = z_ref[rows, :]
        dtp_ref[j * lq:j * lq + tv, :] = dt_ref[rows, :]
    y = _ssd_chunk(xe_ref, zb_ref[...], dtp_ref[...], None, cw_ref, cb_ref, dtb_ref, alog_ref, dsk_ref, ng_ref,
                   L=nseq * lq, tv=tv, lq=lq, h_io=[(h0_ref.at[j], h_ref.at[j]) for j in range(nseq)])
    y_ref[...] = jnp.concatenate([y[j * lq:j * lq + tv, :] for j in range(nseq)], axis=0).astype(y_ref.dtype)
    for j in range(nseq):
        nb_ref[j] = _from_slabs(xe_ref, HALO + j * lq + tv - (KS - 1), KS - 1)


def _ssd_sample(xbc, z, dt, hist, h0, params, *, row0, tv, nseq=8, lq=SUB):
    nb = hist.shape[0]
    rows = nseq * tv
    L = nseq * lq
    blk0 = row0 // rows
    row = lambda n: pl.BlockSpec((rows, n), lambda i: (blk0 + i, 0))
    return pl.pallas_call(
        functools.partial(_ssd_sample_kernel, lq=lq, tv=tv, nseq=nseq),
        grid=(nb // nseq,),
        in_specs=[row(DX), row(DS), row(LANE),
                  pl.BlockSpec((nseq, KS - 1, DX), lambda i: (i, 0, 0)),
                  pl.BlockSpec(memory_space=pl.ANY)] + _ssd_params_specs(),
        out_specs=[pl.BlockSpec((rows, DS), lambda i: (i, 0)),
                   pl.BlockSpec((nseq, KS - 1, DX), lambda i: (i, 0, 0)),
                   pl.BlockSpec((nseq, NH, HP, NS), lambda i: (i, 0, 0, 0))],
        out_shape=[jax.ShapeDtypeStruct((nb * tv, DS), BF16), jax.ShapeDtypeStruct((nb, KS - 1, DX), F32),
                   jax.ShapeDtypeStruct((nb, NH, HP, NS), F32)],
        scratch_shapes=[pltpu.VMEM((DX // LANE, HALO + L, LANE), F32), pltpu.VMEM((L, DS), F32),
                        pltpu.VMEM((L, LANE), F32), pltpu.VMEM((KV_SLOTS, nseq, NH, HP, NS), F32),
                        pltpu.SemaphoreType.DMA((KV_SLOTS,))],
        compiler_params=_cparams(("arbitrary",)),
        name="ssd_sample",
    )(xbc, z, dt, hist, h0, *params)


def _proj_out_kernel(*refs, n, n_main, side_plan):
    lhs = refs[0:2 * n]
    ws = refs[2 * n:3 * n]
    x_ref, g_ref = refs[3 * n:3 * n + 2]
    ns = len(side_plan)
    side_in = refs[3 * n + 2:3 * n + 2 + ns]
    o_ref = refs[3 * n + 2 + ns]
    side_out = refs[3 * n + 3 + ns:]
    _side_cast(pl.program_id(0), side_plan, side_in, side_out)
    is_main = pl.program_id(0) < n_main

    def tile(which):
        m = None
        for k, w_ref in enumerate(ws):
            a = lhs[2 * k + which][...].astype(BF16)
            p = jnp.dot(a, w_ref[...].astype(BF16), preferred_element_type=F32)
            m = p if m is None else m + p
        o_ref[...] = x_ref[...] + _rms(m, g_ref[...])

    @pl.when(is_main)
    def _():
        tile(0)

    @pl.when(jnp.logical_not(is_main))
    def _():
        tile(1)


def _proj_out(lhs_pairs, ws, x, g, *, tm=512, side=()):
    nt = x.shape[0]
    side_plan, side_steps = _side_plan(side)
    assert side_steps <= nt // tm
    n = len(lhs_pairs)
    n_main = nt // tm - 1
    lhs_specs = []
    for a_main, a_extra in lhs_pairs:
        assert a_main.shape[0] == n_main * tm and a_extra.shape[0] == tm
        lhs_specs.append(pl.BlockSpec((tm, a_main.shape[1]), lambda i: (jnp.minimum(i, n_main - 1), 0)))
        lhs_specs.append(pl.BlockSpec((tm, a_extra.shape[1]), lambda i: (0, 0)))
    res = pl.pallas_call(
        functools.partial(_proj_out_kernel, n=n, n_main=n_main, side_plan=side_plan),
        grid=(nt // tm,),
        in_specs=lhs_specs + [_resident(w.shape) for w in ws]
                 + [pl.BlockSpec((tm, D), lambda i: (i, 0)), _resident((1, D))] + _side_specs(side, lambda i: i),
        out_specs=[pl.BlockSpec((tm, D), lambda i: (i, 0))] + _side_specs(side, lambda i: i),
        out_shape=[jax.ShapeDtypeStruct((nt, D), F32)] + _side_shapes(side),
        compiler_params=_cparams(("arbitrary",)),
        name="proj_out",
    )(*[a for pair in lhs_pairs for a in pair], *ws, x, g, *[w for w, _, _ in side])
    return tuple(res) if side else res[0]


def _norm_proj_kernel(*refs, n, heads, side_plan):
    x_ref, g_ref = refs[0:2]
    ws = refs[2:2 + n]
    ns = len(side_plan)
    side_in = refs[2 + n:2 + n + ns]
    outs = refs[2 + n + ns:2 + 2 * n + ns]
    side_out = refs[2 + 2 * n + ns:]
    _side_cast(pl.program_id(0), side_plan, side_in, side_out)
    hn = _rms(x_ref[...], g_ref[...]).astype(BF16)
    for w_ref, o_ref in zip(ws, outs):
        for h in range(XH):
            sl = slice(h * XD, (h + 1) * XD)
            r = jnp.dot(hn, w_ref[:, sl].astype(BF16), preferred_element_type=F32).astype(o_ref.dtype)
            if heads:
                o_ref[:, h, :] = r
            else:
                o_ref[:, sl] = r


def _norm_proj(x, g, ws, out_dtype, *, heads=False, tm=512, side=()):
    nt = x.shape[0]
    n = len(ws)
    side_plan, side_steps = _side_plan(side)
    if heads:
        tm = NM
        out_specs = [pl.BlockSpec((None, NM, XH, XD), lambda i: (i, 0, 0, 0))] * n
        out_shape = [jax.ShapeDtypeStruct((nt // NM, NM, XH, XD), out_dtype)] * n
    else:
        out_specs = [pl.BlockSpec((tm, D), lambda i: (i, 0))] * n
        out_shape = [jax.ShapeDtypeStruct((nt, D), out_dtype)] * n
    assert side_steps <= nt // tm
    return pl.pallas_call(
        functools.partial(_norm_proj_kernel, n=n, heads=heads, side_plan=side_plan),
        grid=(nt // tm,),
        in_specs=[pl.BlockSpec((tm, D), lambda i: (i, 0)), _resident((1, D))] + [_resident((D, D))] * n
                 + _side_specs(side, lambda i: i),
        out_specs=out_specs + _side_specs(side, lambda i: i),
        out_shape=out_shape + _side_shapes(side),
        compiler_params=_cparams(("arbitrary",)),
        name="norm_proj",
    )(x, g, *ws, *[w for w, _, _ in side])


NLT = XD // LANE
LT_STRIDE = NLT * XH
KV_ROWS = NM * LT_STRIDE


def _kv_tiles(x):
    nb = x.shape[0]
    return x.reshape(nb, NM, XH, NLT, LANE).transpose(0, 1, 3, 2, 4).reshape(nb, KV_ROWS, LANE)


def _attn_prompt_kernel(q_ref, k_ref, v_ref, o_ref, kh_ref, vh_ref):
    @pl.when(pl.program_id(1) == 0)
    def _():
        for h in range(XH):
            for lt in range(NLT):
                rows = pl.ds(lt * XH + h, NM, stride=LT_STRIDE)
                kh_ref[h, :, lt * LANE:(lt + 1) * LANE] = k_ref[0, rows, :].astype(BF16)
                vh_ref[h, :, lt * LANE:(lt + 1) * LANE] = v_ref[0, rows, :].astype(BF16)

    scale = XD ** -0.5
    nt_dims = (((1,), (1,)), ((), ()))
    heads = range(XH)
    s = [lax.dot_general(q_ref[:, h * XD:(h + 1) * XD], kh_ref[h], nt_dims, preferred_element_type=F32) * scale
         for h in heads]
    e = [jnp.exp(s[h] - jnp.max(s[h], axis=-1, keepdims=True)) for h in heads]
    p = [(e[h] / jnp.sum(e[h], axis=-1, keepdims=True)).astype(BF16) for h in heads]
    for h in heads:
        o_ref[:, h * XD:(h + 1) * XD] = jnp.dot(p[h], vh_ref[h], preferred_element_type=F32).astype(o_ref.dtype)


def _group_sum(x, col_lt):
    n = x.shape[1]
    a = x + jnp.where(col_lt % 2 == 0, pltpu.roll(x, n - 1, axis=1), pltpu.roll(x, 1, axis=1))
    return a + jnp.where(col_lt < 2, pltpu.roll(a, n - 2, axis=1), pltpu.roll(a, 2, axis=1))


def _attn_sample_kernel(q_ref, k_ref, v_ref, o_ref, *, nseq, tv):
    assert NLT == 4
    tq = q_ref.shape[0]
    ncol = NM * NLT
    scale = XD ** -0.5
    nt_dims = (((1,), (1,)), ((), ()))
    col_lt = lax.broadcasted_iota(jnp.int32, (tq, ncol), 1) % NLT
    rowi = lax.broadcasted_iota(jnp.int32, (tq, LANE), 0)
    for h in range(XH):
        qp = jnp.concatenate([q_ref[:, h * XD + lt * LANE:h * XD + (lt + 1) * LANE].astype(F32)
                              for lt in range(NLT)], axis=0)
        seqs = range(nseq)
        g = [lax.dot_general(qp, k_ref[j, pl.ds(h, ncol, stride=XH), :], nt_dims, preferred_element_type=F32)
             for j in seqs]
        s4 = [sum(jnp.where(col_lt == lt, g[j][lt * tq:(lt + 1) * tq], 0.0) for lt in range(NLT)) for j in seqs]
        s = [_group_sum(s4[j], col_lt) * scale for j in seqs]
        e = [jnp.exp(s[j] - jnp.max(s[j], axis=-1, keepdims=True)) for j in seqs]
        p = [e[j] / (jnp.sum(e[j], axis=-1, keepdims=True) * (1.0 / NLT)) for j in seqs]
        o = [jnp.dot(jnp.concatenate([jnp.where(col_lt == lt, p[j], 0.0) for lt in range(NLT)], axis=0),
                     v_ref[j, pl.ds(h, ncol, stride=XH), :], preferred_element_type=F32) for j in seqs]
        acc = [None] * NLT
        for j in seqs:
            mine = (rowi >= j * tv) & (rowi < (j + 1) * tv)
            for lt in range(NLT):
                acc[lt] = jnp.where(mine, o[j][lt * tq:(lt + 1) * tq], 0.0 if acc[lt] is None else acc[lt])
        for lt in range(NLT):
            o_ref[:, h * XD + lt * LANE:h * XD + (lt + 1) * LANE] = acc[lt].astype(o_ref.dtype)


def _attn_prompt(q, k, v, *, nb, seq, tq=1024):
    nt = nb * seq
    nper = seq // tq
    kv_spec = pl.BlockSpec((1, KV_ROWS, LANE), lambda s, t: (s, 0, 0))
    return pl.pallas_call(
        _attn_prompt_kernel,
        grid=(nb, nper),
        in_specs=[pl.BlockSpec((tq, D), lambda s, t: (s * nper + t, 0)), kv_spec, kv_spec],
        out_specs=pl.BlockSpec((tq, D), lambda s, t: (s * nper + t, 0)),
        out_shape=jax.ShapeDtypeStruct((nt, D), BF16),
        scratch_shapes=[pltpu.VMEM((XH, NM, XD), BF16), pltpu.VMEM((XH, NM, XD), BF16)],
        compiler_params=_cparams(("arbitrary", "arbitrary")),
        name="attn_prompt",
    )(q, _kv_tiles(k), _kv_tiles(v))


KV_SLOTS = 3


def _attn_sample_ring_kernel(q_ref, k_hbm, v_hbm, o_ref, kbuf, vbuf, ksem, vsem, *, nseq, tv):
    s = pl.program_id(0)
    n = pl.num_programs(0)

    def copies(step):
        slot = step % KV_SLOTS
        src = pl.ds(step * nseq, nseq)
        return (pltpu.make_async_copy(k_hbm.at[src], kbuf.at[slot], ksem.at[slot]),
                pltpu.make_async_copy(v_hbm.at[src], vbuf.at[slot], vsem.at[slot]))

    def start(step):
        for c in copies(step):
            c.start()

    @pl.when(s == 0)
    def _():
        for step in range(KV_SLOTS - 1):
            start(step)

    @pl.when(s + (KV_SLOTS - 1) < n)
    def _():
        start(s + (KV_SLOTS - 1))

    for c in copies(s):
        c.wait()
    slot = s % KV_SLOTS
    _attn_sample_kernel(q_ref, kbuf.at[slot], vbuf.at[slot], o_ref, nseq=nseq, tv=tv)


def _attn_sample(q, k, v, *, row0, tv, nseq=4):
    nb = k.shape[0]
    rows = nseq * tv
    blk0 = row0 // rows
    assert nb // nseq >= KV_SLOTS - 1
    ring = pltpu.VMEM((KV_SLOTS, nseq, KV_ROWS, LANE), F32)
    return pl.pallas_call(
        functools.partial(_attn_sample_ring_kernel, nseq=nseq, tv=tv),
        grid=(nb // nseq,),
        in_specs=[pl.BlockSpec((rows, D), lambda i: (blk0 + i, 0)), pl.BlockSpec(memory_space=pl.ANY),
                  pl.BlockSpec(memory_space=pl.ANY)],
        out_specs=pl.BlockSpec((rows, D), lambda i: (i, 0)),
        out_shape=jax.ShapeDtypeStruct((nb * tv, D), BF16),
        scratch_shapes=[ring, ring, pltpu.SemaphoreType.DMA((KV_SLOTS,)), pltpu.SemaphoreType.DMA((KV_SLOTS,))],
        compiler_params=_cparams(("arbitrary",)),
        name="attn_sample",
    )(q, _kv_tiles(k), _kv_tiles(v))


def _row(v):
    return v.reshape(1, -1).astype(F32)


def _pad_lanes(v, n):
    return jnp.pad(v.reshape(1, -1).astype(F32), ((0, 0), (0, n - v.size)))


def _layer(xs, mem, cache_k, cache_v, st_conv, st_sconv, st_ssm, p, *, nbp, seq, nbs, tv, split_out):
    npr = nbp * seq
    bf = lambda w: w.astype(BF16)

    side = [(p['w_in'].T, 0, LANE)]
    x, w_in = _ffn(xs, _row(p['ffn1_pre_g']), bf(p['ffn1_w_gate']), bf(p['ffn1_w_up']), bf(p['ffn1_w_down']),
                   _row(p['ffn1_post_g']), split_out=False, side=side)
    w_out = bf(p['w_out'])

    wt = jnp.pad(w_in[2 * DC + DS + DX:], ((0, LANE - NH), (0, 0)))
    u, z, xbc, dtr = _mix_in(x, _row(p['mix_pre_g']), w_in, wt)

    cpar = (p['conv_w'].astype(F32), _row(p['conv_b']), _row(p['conv_ln_g']), _row(p['conv_ln_b']))
    a_p, conv_p = _conv_prompt(u, *cpar, nb=nbp, seq=seq)
    a_s, conv_s = _conv_sample(u, jnp.transpose(st_conv, (1, 0, 2)), *cpar, row0=npr, tv=tv)
    conv_s = jnp.transpose(conv_s, (1, 0, 2))

    spar = (p['ssm_conv_w'].astype(F32), _row(p['ssm_conv_b']), _pad_lanes(p['dt_bias'], LANE),
            _pad_lanes(p['a_log'], LANE), jnp.repeat(p['d_skip'].astype(F32), HP).reshape(1, DS),
            _row(p['ssm_norm_g']))
    y_p, sconv_p, ssm_p = _ssd_prompt(xbc, z, dtr, spar, nb=nbp, seq=seq)
    y_s, sconv_s, ssm_s = _ssd_sample(xbc, z, dtr, st_sconv, st_ssm, spar, row0=npr, tv=tv)

    x, wg2 = _proj_out([(a_p, a_s), (y_p, y_s)], [w_out[:DC], w_out[DC:]], x, _row(p['mix_post_g']),
                       side=[(p['ffn2_w_gate'], 1, SIDE_WIDE)])

    q, wu2 = _norm_proj(x, _row(p['xattn_pre_g']), [p['w_xq']], BF16, side=[(p['ffn2_w_up'], 1, SIDE_WIDE)])
    mk, mv = _norm_proj(mem, _row(p['mem_norm_g']), [p['w_xk'], p['w_xv']], F32, heads=True)
    o_p = _attn_prompt(q, mk, mv, nb=nbp, seq=seq)
    o_s = _attn_sample(q, cache_k, cache_v, row0=npr, tv=tv)
    x, wd2 = _proj_out([(o_p, o_s)], [p['w_xo']], x, _row(p['xattn_post_g']),
                       side=[(p['ffn2_w_down'], 0, SIDE_WIDE)])

    x = _ffn([x], _row(p['ffn2_pre_g']), wg2, wu2, wd2, _row(p['ffn2_post_g']), split_out=split_out)
    return x, (mk, mv, conv_p, sconv_p, ssm_p, conv_s, sconv_s, ssm_s)


def kernel(x_prompt, x_sample, mem_prompt, cache_mem_k, cache_mem_v, state_conv, state_ssm_conv, state_ssm, ffn1_pre_g, ffn1_w_gate, ffn1_w_up, ffn1_w_down, ffn1_post_g, mix_pre_g, w_in, conv_w, conv_b, conv_ln_g, conv_ln_b, ssm_conv_w, ssm_conv_b, dt_bias, a_log, d_skip, ssm_norm_g, w_out, mix_post_g, xattn_pre_g, mem_norm_g, w_xq, w_xk, w_xv, w_xo, xattn_post_g, ffn2_pre_g, ffn2_w_gate, ffn2_w_up, ffn2_w_down, ffn2_post_g):
    params = dict(ffn1_pre_g=ffn1_pre_g, ffn1_w_gate=ffn1_w_gate, ffn1_w_up=ffn1_w_up, ffn1_w_down=ffn1_w_down,
                  ffn1_post_g=ffn1_post_g, mix_pre_g=mix_pre_g, w_in=w_in, conv_w=conv_w, conv_b=conv_b,
                  conv_ln_g=conv_ln_g, conv_ln_b=conv_ln_b, ssm_conv_w=ssm_conv_w, ssm_conv_b=ssm_conv_b,
                  dt_bias=dt_bias, a_log=a_log, d_skip=d_skip, ssm_norm_g=ssm_norm_g, w_out=w_out,
                  mix_post_g=mix_post_g, xattn_pre_g=xattn_pre_g, mem_norm_g=mem_norm_g, w_xq=w_xq,
                  w_xk=w_xk, w_xv=w_xv, w_xo=w_xo, xattn_post_g=xattn_post_g, ffn2_pre_g=ffn2_pre_g,
                  ffn2_w_gate=ffn2_w_gate, ffn2_w_up=ffn2_w_up, ffn2_w_down=ffn2_w_down, ffn2_post_g=ffn2_post_g)
    depth = ffn1_pre_g.shape[0]
    nbp, seq, _ = x_prompt.shape
    nbs, tv, _ = x_sample.shape
    npr = nbp * seq
    xs = [x_prompt.reshape(npr, D), x_sample.reshape(nbs * tv, D)]
    mem = mem_prompt.reshape(nbp * NM, D)
    per_layer = []
    for layer in range(depth):
        p = {name: w[layer] for name, w in params.items()}
        x, states = _layer(xs, mem, cache_mem_k[layer], cache_mem_v[layer],
                           state_conv[layer], state_ssm_conv[layer], state_ssm[layer], p,
                           nbp=nbp, seq=seq, nbs=nbs, tv=tv, split_out=layer == depth - 1)
        xs = [x]
        per_layer.append(states)
    mk, mv, conv_p, sconv_p, ssm_p, conv_s, sconv_s, ssm_s = [jnp.stack(t) for t in zip(*per_layer)]
    yp, ys = x
    return (yp.reshape(nbp, seq, D), ys.reshape(nbs, tv, D), mk, mv, conv_p, sconv_p, ssm_p, conv_s, sconv_s, ssm_s)
```
